```python
import jax, jax.numpy as jnp
from jax import lax
import numpy as np

D_MODEL = 1024
BATCH = 8
SEQ = 4096
DEPTH = 1

MEM_LEN = 256
MIX_WIDTH = D_MODEL
CONV_CH = MIX_WIDTH // 2
CONV_WIDTH = 31
CONV_PAD = CONV_WIDTH // 2
HEAD_DIM = 64
ATTN_CH = MIX_WIDTH - CONV_CH
N_Q_HEADS = ATTN_CH // HEAD_DIM
N_KV_HEADS = max(N_Q_HEADS // 4, 1)
KV_CH = N_KV_HEADS * HEAD_DIM
WINDOW = 128
BLOCK = 128
ROPE_THETA = 10000.0
MEM_HEADS = 4
MEM_HEAD_DIM = D_MODEL // MEM_HEADS
D_FF = -(-8 * D_MODEL // (3 * 256)) * 256
IN_COLS = 2 * CONV_CH + ATTN_CH + 2 * KV_CH
EPS = 1e-6

kernel_name = "hybrid_conformer_swa_encoder_block"


def rms_norm(x, g):
    xf = x.astype(jnp.float32)
    y = xf * lax.rsqrt(jnp.mean(xf * xf, axis=-1, keepdims=True) + EPS)
    return (y * g.astype(jnp.float32)).astype(x.dtype)


def layer_norm(x, g, b):
    xf = x.astype(jnp.float32)
    mu = jnp.mean(xf, axis=-1, keepdims=True)
    var = jnp.mean(jnp.square(xf - mu), axis=-1, keepdims=True)
    y = (xf - mu) * lax.rsqrt(var + EPS)
    return (y * g.astype(jnp.float32) + b.astype(jnp.float32)).astype(x.dtype)


def rope_tables(seq):
    pos = jnp.arange(seq, dtype=jnp.float32)
    inv_freq = ROPE_THETA ** (-jnp.arange(0, HEAD_DIM, 2, dtype=jnp.float32) / HEAD_DIM)
    ang = pos[:, None] * inv_freq[None, :]
    return jnp.cos(ang), jnp.sin(ang)


def apply_rope(t, cos, sin):
    t1, t2 = jnp.split(t.astype(jnp.float32), 2, axis=-1)
    c = cos[None, :, None, :]
    s = sin[None, :, None, :]
    return jnp.concatenate([t1 * c - t2 * s, t1 * s + t2 * c], axis=-1).astype(t.dtype)


def conformer_conv_group(u_glu, w_dw, b_dw, ln_g, ln_b):
    a, gate = jnp.split(u_glu, 2, axis=-1)
    v = a * jax.nn.sigmoid(gate)
    y = lax.conv_general_dilated(
        v, w_dw[:, None, :].astype(v.dtype), window_strides=(1,),
        padding=[(CONV_PAD, CONV_PAD)], dimension_numbers=("NWC", "WIO", "NWC"),
        feature_group_count=CONV_CH) + b_dw
    return jax.nn.silu(layer_norm(y, ln_g, ln_b))


def windowed_gqa_with_sink(q, k, v, sink):
    B, S = q.shape[0], q.shape[1]
    nb = S // BLOCK
    G = N_Q_HEADS // N_KV_HEADS
    qb = q.reshape(B, nb, BLOCK, N_KV_HEADS, G, HEAD_DIM)
    pad = ((0, 0), (BLOCK, BLOCK), (0, 0), (0, 0))
    kp = jnp.pad(k, pad).reshape(B, nb + 2, BLOCK, N_KV_HEADS, HEAD_DIM)
    vp = jnp.pad(v, pad).reshape(B, nb + 2, BLOCK, N_KV_HEADS, HEAD_DIM)
    kb = jnp.concatenate([kp[:, :-2], kp[:, 1:-1], kp[:, 2:]], axis=2)
    vb = jnp.concatenate([vp[:, :-2], vp[:, 1:-1], vp[:, 2:]], axis=2)
    a_idx = jnp.arange(BLOCK)[:, None]
    c_idx = jnp.arange(3 * BLOCK)[None, :]
    rel = c_idx - BLOCK - a_idx
    kpos = jnp.arange(nb)[:, None] * BLOCK - BLOCK + jnp.arange(3 * BLOCK)[None, :]
    in_seq = (kpos >= 0) & (kpos < S)
    mask = (jnp.abs(rel) <= WINDOW)[None] & in_seq[:, None, :]
    scale = HEAD_DIM ** -0.5
    s = jnp.einsum("bnqhgd,bnkhd->bnhgqk", qb, kb,
                   preferred_element_type=jnp.float32) * scale
    s = jnp.where(mask[None, :, None, None], s, -jnp.inf)
    sk = sink.astype(jnp.float32).reshape(N_KV_HEADS, G)[None, None, :, :, None, None]
    m = jnp.maximum(jnp.max(s, axis=-1, keepdims=True), sk)
    p = jnp.exp(s - m)
    denom = jnp.sum(p, axis=-1, keepdims=True) + jnp.exp(sk - m)
    o = jnp.einsum("bnhgqk,bnkhd->bnqhgd", (p / denom).astype(v.dtype), vb)
    return o.reshape(B, S, N_Q_HEADS * HEAD_DIM)


def memory_cross_attention(h, mem_n, w_q, w_kv, w_o):
    B, S = h.shape[0], h.shape[1]
    M = mem_n.shape[1]
    q = (h @ w_q).reshape(B, S, MEM_HEADS, MEM_HEAD_DIM)
    km, vm = jnp.split(mem_n @ w_kv, 2, axis=-1)
    km = km.reshape(B, M, MEM_HEADS, MEM_HEAD_DIM)
    vm = vm.reshape(B, M, MEM_HEADS, MEM_HEAD_DIM)
    s = jnp.einsum("bshd,bmhd->bhsm", q, km,
                   preferred_element_type=jnp.float32) * (MEM_HEAD_DIM ** -0.5)
    p = jax.nn.softmax(s, axis=-1)
    o = jnp.einsum("bhsm,bmhd->bshd", p.astype(vm.dtype), vm).reshape(B, S, D_MODEL)
    return o @ w_o


def swiglu(h, w_gate, w_up, w_down):
    return (jax.nn.silu(h @ w_gate) * (h @ w_up)) @ w_down


def _fwd_setup_inputs(seed: int = 0) -> dict:
    key = jax.random.key(seed)
    ks = jax.random.split(key, 24)
    L = DEPTH

    def nrm(k, shape, scale):
        return jax.random.normal(k, shape, jnp.float32) * scale

    def gain(k, shape):
        return 1.0 + 0.05 * jax.random.normal(k, shape, jnp.float32)

    return {
        "x": nrm(ks[0], (BATCH, SEQ, D_MODEL), 1.0),
        "mem": nrm(ks[1], (BATCH, MEM_LEN, D_MODEL), 1.0),
        "g_mix": gain(ks[2], (L, D_MODEL)),
        "w_in": nrm(ks[3], (L, D_MODEL, IN_COLS), D_MODEL ** -0.5),
        "b_in": nrm(ks[4], (L, IN_COLS), 0.02),
        "w_dw": nrm(ks[5], (L, CONV_WIDTH, CONV_CH), CONV_WIDTH ** -0.5),
        "b_dw": nrm(ks[6], (L, CONV_CH), 0.02),
        "g_conv_ln": gain(ks[7], (L, CONV_CH)),
        "b_conv_ln": nrm(ks[8], (L, CONV_CH), 0.02),
        "attn_sink": nrm(ks[9], (L, N_Q_HEADS), 0.5),
        "w_out": nrm(ks[10], (L, MIX_WIDTH, D_MODEL), MIX_WIDTH ** -0.5),
        "b_out": nrm(ks[11], (L, D_MODEL), 0.02),
        "g_mem_q": gain(ks[12], (L, D_MODEL)),
        "g_mem_kv": gain(ks[13], (L, D_MODEL)),
        "w_mem_q": nrm(ks[14], (L, D_MODEL, D_MODEL), D_MODEL ** -0.5),
        "w_mem_kv": nrm(ks[15], (L, D_MODEL, 2 * D_MODEL), D_MODEL ** -0.5),
        "w_mem_o": nrm(ks[16], (L, D_MODEL, D_MODEL), D_MODEL ** -0.5),
        "g_ffn": gain(ks[17], (L, D_MODEL)),
        "w_gate": nrm(ks[18], (L, D_MODEL, D_FF), D_MODEL ** -0.5),
        "w_up": nrm(ks[19], (L, D_MODEL, D_FF), D_MODEL ** -0.5),
        "w_down": nrm(ks[20], (L, D_FF, D_MODEL), D_FF ** -0.5),
        "g_final": gain(ks[21], (D_MODEL,)),
    }


def _fwd_reference(x, mem, g_mix, w_in, b_in, w_dw, b_dw, g_conv_ln, b_conv_ln, attn_sink,
              w_out, b_out, g_mem_q, g_mem_kv, w_mem_q, w_mem_kv, w_mem_o,
              g_ffn, w_gate, w_up, w_down, g_final):
    B, S = x.shape[0], x.shape[1]
    cos, sin = rope_tables(S)
    split_pts = [2 * CONV_CH, 2 * CONV_CH + ATTN_CH, 2 * CONV_CH + ATTN_CH + KV_CH]
    for l in range(DEPTH):
        h = rms_norm(x, g_mix[l])
        u = h @ w_in[l] + b_in[l]
        u_glu, q, k, v = jnp.split(u, split_pts, axis=-1)
        y_conv = conformer_conv_group(u_glu, w_dw[l], b_dw[l], g_conv_ln[l], b_conv_ln[l])
        q = apply_rope(q.reshape(B, S, N_Q_HEADS, HEAD_DIM), cos, sin)
        k = apply_rope(k.reshape(B, S, N_KV_HEADS, HEAD_DIM), cos, sin)
        v = v.reshape(B, S, N_KV_HEADS, HEAD_DIM)
        y_attn = windowed_gqa_with_sink(q, k, v, attn_sink[l])
        y_mix = jnp.concatenate([y_conv, y_attn], axis=-1)
        x = x + (y_mix @ w_out[l] + b_out[l])
        x = x + memory_cross_attention(rms_norm(x, g_mem_q[l]), rms_norm(mem, g_mem_kv[l]),
                                       w_mem_q[l], w_mem_kv[l], w_mem_o[l])
        x = x + swiglu(rms_norm(x, g_ffn[l]), w_gate[l], w_up[l], w_down[l])
    return rms_norm(x, g_final)


import jax as _jax
import jax.numpy as _jnp

TWIN_FORMAT = 'train_step'
FWD_PARAMS = ['x', 'mem', 'g_mix', 'w_in', 'b_in', 'w_dw', 'b_dw', 'g_conv_ln', 'b_conv_ln', 'attn_sink', 'w_out', 'b_out', 'g_mem_q', 'g_mem_kv', 'w_mem_q', 'w_mem_kv', 'w_mem_o', 'g_ffn', 'w_gate', 'w_up', 'w_down', 'g_final']
TWIN_WEIGHTS = ['g_mix', 'w_in', 'b_in', 'w_dw', 'b_dw', 'g_conv_ln', 'b_conv_ln', 'attn_sink', 'w_out', 'b_out', 'g_mem_q', 'g_mem_kv', 'w_mem_q', 'w_mem_kv', 'w_mem_o', 'g_ffn', 'w_gate', 'w_up', 'w_down', 'g_final']
TWIN_DIFF_INPUT = 'x'
TWIN_INPUTS = ['x', 'mem', 'g_mix', 'w_in', 'b_in', 'w_dw', 'b_dw', 'g_conv_ln', 'b_conv_ln', 'attn_sink', 'w_out', 'b_out', 'g_mem_q', 'g_mem_kv', 'w_mem_q', 'w_mem_kv', 'w_mem_o', 'g_ffn', 'w_gate', 'w_up', 'w_down', 'g_final', 'loss_target', 'm_g_mix', 'm_w_in', 'm_b_in', 'm_w_dw', 'm_b_dw', 'm_g_conv_ln', 'm_b_conv_ln', 'm_attn_sink', 'm_w_out', 'm_b_out', 'm_g_mem_q', 'm_g_mem_kv', 'm_w_mem_q', 'm_w_mem_kv', 'm_w_mem_o', 'm_g_ffn', 'm_w_gate', 'm_w_up', 'm_w_down', 'm_g_final', 'v_g_mix', 'v_w_in', 'v_b_in', 'v_w_dw', 'v_b_dw', 'v_g_conv_ln', 'v_b_conv_ln', 'v_attn_sink', 'v_w_out', 'v_b_out', 'v_g_mem_q', 'v_g_mem_kv', 'v_w_mem_q', 'v_w_mem_kv', 'v_w_mem_o', 'v_g_ffn', 'v_w_gate', 'v_w_up', 'v_w_down', 'v_g_final']
TWIN_OUTPUTS = ['loss', 'grad_x', 'grad_g_mix', 'grad_w_in', 'grad_b_in', 'grad_w_dw', 'grad_b_dw', 'grad_g_conv_ln', 'grad_b_conv_ln', 'grad_attn_sink', 'grad_w_out', 'grad_b_out', 'grad_g_mem_q', 'grad_g_mem_kv', 'grad_w_mem_q', 'grad_w_mem_kv', 'grad_w_mem_o', 'grad_g_ffn', 'grad_w_gate', 'grad_w_up', 'grad_w_down', 'grad_g_final', 'delta_g_mix', 'delta_w_in', 'delta_b_in', 'delta_w_dw', 'delta_b_dw', 'delta_g_conv_ln', 'delta_b_conv_ln', 'delta_attn_sink', 'delta_w_out', 'delta_b_out', 'delta_g_mem_q', 'delta_g_mem_kv', 'delta_w_mem_q', 'delta_w_mem_kv', 'delta_w_mem_o', 'delta_g_ffn', 'delta_w_gate', 'delta_w_up', 'delta_w_down', 'delta_g_final', 'new_m_g_mix', 'new_m_w_in', 'new_m_b_in', 'new_m_w_dw', 'new_m_b_dw', 'new_m_g_conv_ln', 'new_m_b_conv_ln', 'new_m_attn_sink', 'new_m_w_out', 'new_m_b_out', 'new_m_g_mem_q', 'new_m_g_mem_kv', 'new_m_w_mem_q', 'new_m_w_mem_kv', 'new_m_w_mem_o', 'new_m_g_ffn', 'new_m_w_gate', 'new_m_w_up', 'new_m_w_down', 'new_m_g_final', 'new_v_g_mix', 'new_v_w_in', 'new_v_b_in', 'new_v_w_dw', 'new_v_b_dw', 'new_v_g_conv_ln', 'new_v_b_conv_ln', 'new_v_attn_sink', 'new_v_w_out', 'new_v_b_out', 'new_v_g_mem_q', 'new_v_g_mem_kv', 'new_v_w_mem_q', 'new_v_w_mem_kv', 'new_v_w_mem_o', 'new_v_g_ffn', 'new_v_w_gate', 'new_v_w_up', 'new_v_w_down', 'new_v_g_final']
TWIN_LEAF_KINDS = {'loss': 'loss', 'grad_x': 'grad_x', 'grad_g_mix': 'grad_w', 'grad_w_in': 'grad_w', 'grad_b_in': 'grad_w', 'grad_w_dw': 'grad_w', 'grad_b_dw': 'grad_w', 'grad_g_conv_ln': 'grad_w', 'grad_b_conv_ln': 'grad_w', 'grad_attn_sink': 'grad_w', 'grad_w_out': 'grad_w', 'grad_b_out': 'grad_w', 'grad_g_mem_q': 'grad_w', 'grad_g_mem_kv': 'grad_w', 'grad_w_mem_q': 'grad_w', 'grad_w_mem_kv': 'grad_w', 'grad_w_mem_o': 'grad_w', 'grad_g_ffn': 'grad_w', 'grad_w_gate': 'grad_w', 'grad_w_up': 'grad_w', 'grad_w_down': 'grad_w', 'grad_g_final': 'grad_w', 'delta_g_mix': 'delta_w', 'delta_w_in': 'delta_w', 'delta_b_in': 'delta_w', 'delta_w_dw': 'delta_w', 'delta_b_dw': 'delta_w', 'delta_g_conv_ln': 'delta_w', 'delta_b_conv_ln': 'delta_w', 'delta_attn_sink': 'delta_w', 'delta_w_out': 'delta_w', 'delta_b_out': 'delta_w', 'delta_g_mem_q': 'delta_w', 'delta_g_mem_kv': 'delta_w', 'delta_w_mem_q': 'delta_w', 'delta_w_mem_kv': 'delta_w', 'delta_w_mem_o': 'delta_w', 'delta_g_ffn': 'delta_w', 'delta_w_gate': 'delta_w', 'delta_w_up': 'delta_w', 'delta_w_down': 'delta_w', 'delta_g_final': 'delta_w', 'new_m_g_mix': 'new_m', 'new_m_w_in': 'new_m', 'new_m_b_in': 'new_m', 'new_m_w_dw': 'new_m', 'new_m_b_dw': 'new_m', 'new_m_g_conv_ln': 'new_m', 'new_m_b_conv_ln': 'new_m', 'new_m_attn_sink': 'new_m', 'new_m_w_out': 'new_m', 'new_m_b_out': 'new_m', 'new_m_g_mem_q': 'new_m', 'new_m_g_mem_kv': 'new_m', 'new_m_w_mem_q': 'new_m', 'new_m_w_mem_kv': 'new_m', 'new_m_w_mem_o': 'new_m', 'new_m_g_ffn': 'new_m', 'new_m_w_gate': 'new_m', 'new_m_w_up': 'new_m', 'new_m_w_down': 'new_m', 'new_m_g_final': 'new_m', 'new_v_g_mix': 'new_v', 'new_v_w_in': 'new_v', 'new_v_b_in': 'new_v', 'new_v_w_dw': 'new_v', 'new_v_b_dw': 'new_v', 'new_v_g_conv_ln': 'new_v', 'new_v_b_conv_ln': 'new_v', 'new_v_attn_sink': 'new_v', 'new_v_w_out': 'new_v', 'new_v_b_out': 'new_v', 'new_v_g_mem_q': 'new_v', 'new_v_g_mem_kv': 'new_v', 'new_v_w_mem_q': 'new_v', 'new_v_w_mem_kv': 'new_v', 'new_v_w_mem_o': 'new_v', 'new_v_g_ffn': 'new_v', 'new_v_w_gate': 'new_v', 'new_v_w_up': 'new_v', 'new_v_w_down': 'new_v', 'new_v_g_final': 'new_v'}


def _forward(args):
    return _fwd_reference(*[args[k] for k in FWD_PARAMS])


def _output_shape():
    out = _jax.eval_shape(lambda: _forward(_fwd_setup_inputs(0)))
    return out.shape, out.dtype

N_MICROBATCH = 1
ADAM_LR = 0.001
ADAM_B1 = 0.9
ADAM_B2 = 0.999
ADAM_EPS = 1e-08
ADAM_WD = 0.01
ADAM_STEP = 10
PER_EXAMPLE_BATCH_AXIS = {'x': 0, 'mem': 0, 'loss_target': 0}
SHARED_INPUTS = []
_WEIGHT_DTYPES = {'g_mix': _jnp.float32, 'w_in': _jnp.float32, 'b_in': _jnp.float32, 'w_dw': _jnp.float32, 'b_dw': _jnp.float32, 'g_conv_ln': _jnp.float32, 'b_conv_ln': _jnp.float32, 'attn_sink': _jnp.float32, 'w_out': _jnp.float32, 'b_out': _jnp.float32, 'g_mem_q': _jnp.float32, 'g_mem_kv': _jnp.float32, 'w_mem_q': _jnp.float32, 'w_mem_kv': _jnp.float32, 'w_mem_o': _jnp.float32, 'g_ffn': _jnp.float32, 'w_gate': _jnp.float32, 'w_up': _jnp.float32, 'w_down': _jnp.float32, 'g_final': _jnp.float32}
MOMENT_SCALE = {'g_mix': 9.354267e-02, 'w_in': 6.816760e-02, 'b_in': 1.630390e-01, 'w_dw': 1.166826e-01, 'b_dw': 2.657669e-01, 'g_conv_ln': 1.560295e-01, 'b_conv_ln': 1.449131e-01, 'attn_sink': 6.424660e-04, 'w_out': 8.312438e-02, 'b_out': 2.597330e-01, 'g_mem_q': 2.025541e-02, 'g_mem_kv': 3.171994e-02, 'w_mem_q': 1.946508e-02, 'w_mem_kv': 1.962106e-02, 'w_mem_o': 2.057205e-02, 'g_ffn': 1.281025e-01, 'w_gate': 5.377787e-02, 'w_up': 5.252472e-02, 'w_down': 8.694453e-02, 'g_final': 3.211712e+01}


def _to_microbatches(a, axis):
    t = _jnp.moveaxis(a, axis, 0)
    t = t.reshape((N_MICROBATCH, t.shape[0] // N_MICROBATCH) + t.shape[1:])
    return _jnp.moveaxis(t, 1, axis + 1)


def setup_inputs(seed: int = 0) -> dict:
    inp = _fwd_setup_inputs(seed)
    key = _jax.random.fold_in(_jax.random.key(seed), 7919)
    shape, _ = _output_shape()
    out = dict(inp)
    out["loss_target"] = _jax.random.normal(_jax.random.fold_in(key, 0), shape, _jnp.float32)
    for i, name in enumerate(TWIN_WEIGHTS):
        w = inp[name].astype(_jnp.float32)
        if MOMENT_SCALE is None:
            s = _jnp.sqrt(_jnp.mean(_jnp.square(w)) + 1e-30)
        else:
            s = MOMENT_SCALE[name]
        km, kv = _jax.random.split(_jax.random.fold_in(key, i + 1))
        out[name] = w
        out["m_" + name] = s * _jax.random.normal(km, w.shape, _jnp.float32)
        out["v_" + name] = (s * s) * _jax.random.uniform(kv, w.shape, _jnp.float32, 0.5, 1.5)
    if N_MICROBATCH > 1:
        for name, axis in PER_EXAMPLE_BATCH_AXIS.items():
            out[name] = _to_microbatches(out[name], axis)
    return {'x': out['x'], 'mem': out['mem'], 'g_mix': out['g_mix'], 'w_in': out['w_in'], 'b_in': out['b_in'], 'w_dw': out['w_dw'], 'b_dw': out['b_dw'], 'g_conv_ln': out['g_conv_ln'], 'b_conv_ln': out['b_conv_ln'], 'attn_sink': out['attn_sink'], 'w_out': out['w_out'], 'b_out': out['b_out'], 'g_mem_q': out['g_mem_q'], 'g_mem_kv': out['g_mem_kv'], 'w_mem_q': out['w_mem_q'], 'w_mem_kv': out['w_mem_kv'], 'w_mem_o': out['w_mem_o'], 'g_ffn': out['g_ffn'], 'w_gate': out['w_gate'], 'w_up': out['w_up'], 'w_down': out['w_down'], 'g_final': out['g_final'], 'loss_target': out['loss_target'], 'm_g_mix': out['m_g_mix'], 'm_w_in': out['m_w_in'], 'm_b_in': out['m_b_in'], 'm_w_dw': out['m_w_dw'], 'm_b_dw': out['m_b_dw'], 'm_g_conv_ln': out['m_g_conv_ln'], 'm_b_conv_ln': out['m_b_conv_ln'], 'm_attn_sink': out['m_attn_sink'], 'm_w_out': out['m_w_out'], 'm_b_out': out['m_b_out'], 'm_g_mem_q': out['m_g_mem_q'], 'm_g_mem_kv': out['m_g_mem_kv'], 'm_w_mem_q': out['m_w_mem_q'], 'm_w_mem_kv': out['m_w_mem_kv'], 'm_w_mem_o': out['m_w_mem_o'], 'm_g_ffn': out['m_g_ffn'], 'm_w_gate': out['m_w_gate'], 'm_w_up': out['m_w_up'], 'm_w_down': out['m_w_down'], 'm_g_final': out['m_g_final'], 'v_g_mix': out['v_g_mix'], 'v_w_in': out['v_w_in'], 'v_b_in': out['v_b_in'], 'v_w_dw': out['v_w_dw'], 'v_b_dw': out['v_b_dw'], 'v_g_conv_ln': out['v_g_conv_ln'], 'v_b_conv_ln': out['v_b_conv_ln'], 'v_attn_sink': out['v_attn_sink'], 'v_w_out': out['v_w_out'], 'v_b_out': out['v_b_out'], 'v_g_mem_q': out['v_g_mem_q'], 'v_g_mem_kv': out['v_g_mem_kv'], 'v_w_mem_q': out['v_w_mem_q'], 'v_w_mem_kv': out['v_w_mem_kv'], 'v_w_mem_o': out['v_w_mem_o'], 'v_g_ffn': out['v_g_ffn'], 'v_w_gate': out['v_w_gate'], 'v_w_up': out['v_w_up'], 'v_w_down': out['v_w_down'], 'v_g_final': out['v_g_final']}


def _loss(weights, diff, rest, loss_target):
    with _jax.named_scope("forward"):
        args = {**rest, TWIN_DIFF_INPUT: diff, **{k: w.astype(_WEIGHT_DTYPES[k]) for k, w in weights.items()}}
        y = _forward(args)
    with _jax.named_scope("loss_head"):
        err = _jnp.square(y.astype(_jnp.float32) - loss_target)
        return 0.5 * _jnp.sum(_jnp.mean(err, axis=-1)) if err.ndim else 0.5 * err


def _adamw(w, g, m, v):
    m = ADAM_B1 * m + (1.0 - ADAM_B1) * g
    v = ADAM_B2 * v + (1.0 - ADAM_B2) * _jnp.square(g)
    m_hat = m / (1.0 - ADAM_B1 ** ADAM_STEP)
    v_hat = v / (1.0 - ADAM_B2 ** ADAM_STEP)
    delta = -ADAM_LR * (m_hat / (_jnp.sqrt(v_hat) + ADAM_EPS) + ADAM_WD * w)
    return delta, m, v


def reference(x, mem, g_mix, w_in, b_in, w_dw, b_dw, g_conv_ln, b_conv_ln, attn_sink, w_out, b_out, g_mem_q, g_mem_kv, w_mem_q, w_mem_kv, w_mem_o, g_ffn, w_gate, w_up, w_down, g_final, loss_target, m_g_mix, m_w_in, m_b_in, m_w_dw, m_b_dw, m_g_conv_ln, m_b_conv_ln, m_attn_sink, m_w_out, m_b_out, m_g_mem_q, m_g_mem_kv, m_w_mem_q, m_w_mem_kv, m_w_mem_o, m_g_ffn, m_w_gate, m_w_up, m_w_down, m_g_final, v_g_mix, v_w_in, v_b_in, v_w_dw, v_b_dw, v_g_conv_ln, v_b_conv_ln, v_attn_sink, v_w_out, v_b_out, v_g_mem_q, v_g_mem_kv, v_w_mem_q, v_w_mem_kv, v_w_mem_o, v_g_ffn, v_w_gate, v_w_up, v_w_down, v_g_final):
    given = dict(x=x, mem=mem, g_mix=g_mix, w_in=w_in, b_in=b_in, w_dw=w_dw, b_dw=b_dw, g_conv_ln=g_conv_ln, b_conv_ln=b_conv_ln, attn_sink=attn_sink, w_out=w_out, b_out=b_out, g_mem_q=g_mem_q, g_mem_kv=g_mem_kv, w_mem_q=w_mem_q, w_mem_kv=w_mem_kv, w_mem_o=w_mem_o, g_ffn=g_ffn, w_gate=w_gate, w_up=w_up, w_down=w_down, g_final=g_final, loss_target=loss_target, m_g_mix=m_g_mix, m_w_in=m_w_in, m_b_in=m_b_in, m_w_dw=m_w_dw, m_b_dw=m_b_dw, m_g_conv_ln=m_g_conv_ln, m_b_conv_ln=m_b_conv_ln, m_attn_sink=m_attn_sink, m_w_out=m_w_out, m_b_out=m_b_out, m_g_mem_q=m_g_mem_q, m_g_mem_kv=m_g_mem_kv, m_w_mem_q=m_w_mem_q, m_w_mem_kv=m_w_mem_kv, m_w_mem_o=m_w_mem_o, m_g_ffn=m_g_ffn, m_w_gate=m_w_gate, m_w_up=m_w_up, m_w_down=m_w_down, m_g_final=m_g_final, v_g_mix=v_g_mix, v_w_in=v_w_in, v_b_in=v_b_in, v_w_dw=v_w_dw, v_b_dw=v_b_dw, v_g_conv_ln=v_g_conv_ln, v_b_conv_ln=v_b_conv_ln, v_attn_sink=v_attn_sink, v_w_out=v_w_out, v_b_out=v_b_out, v_g_mem_q=v_g_mem_q, v_g_mem_kv=v_g_mem_kv, v_w_mem_q=v_w_mem_q, v_w_mem_kv=v_w_mem_kv, v_w_mem_o=v_w_mem_o, v_g_ffn=v_g_ffn, v_w_gate=v_w_gate, v_w_up=v_w_up, v_w_down=v_w_down, v_g_final=v_g_final)
    weights = {n: given[n] for n in TWIN_WEIGHTS}
    shared = {n: given[n] for n in SHARED_INPUTS}
    per_example = {n: given[n] for n in ['x', 'mem']}
    grad_fn = _jax.value_and_grad(_loss, argnums=(0, 1))

    def one_microbatch(ex, loss_target):
        ex = dict(ex)
        diff = ex.pop(TWIN_DIFF_INPUT)
        return grad_fn(weights, diff, {**shared, **ex}, loss_target)

    if N_MICROBATCH == 1:
        loss, (grad_w, grad_x) = one_microbatch(per_example, given["loss_target"])
    else:
        def body(carry, xs):
            loss_sum, grad_sum = carry
            l_k, (gw_k, gx_k) = one_microbatch(xs[0], xs[1])
            with _jax.named_scope("update"):
                return (loss_sum + l_k, _jax.tree.map(_jnp.add, grad_sum, gw_k)), gx_k

        init = (_jnp.zeros((), _jnp.float32), _jax.tree.map(_jnp.zeros_like, weights))
        (loss, grad_w), grad_x = _jax.lax.scan(body, init, (per_example, given["loss_target"]))
    with _jax.named_scope("update"):
        delta_w, new_m, new_v = {}, {}, {}
        for n in TWIN_WEIGHTS:
            delta_w[n], new_m[n], new_v[n] = _adamw(weights[n], grad_w[n], given["m_" + n], given["v_" + n])
    return (loss, grad_x, *[grad_w[n] for n in TWIN_WEIGHTS], *[delta_w[n] for n in TWIN_WEIGHTS],
            *[new_m[n] for n in TWIN_WEIGHTS], *[new_v[n] for n in TWIN_WEIGHTS])
```

```python
import jax
import jax.numpy as jnp
from jax import lax
from jax.experimental import pallas as pl
from jax.experimental.pallas import tpu as pltpu

F32 = jnp.float32
BF16 = jnp.bfloat16
EPS = 1e-6
NEG = -1e30

D_MODEL = 1024
CONV_CH = 512
CONV_W = 31
HEAD_DIM = 64
BLK = 128
MEM_HEADS = 4
MEM_HD = 256
N_CHIPS = 4
N_DEV = 8
ATT_SCALE = HEAD_DIM ** -0.5
MEM_SCALE = MEM_HD ** -0.5
ROPE_THETA = 10000.0

ADAM_LR = 0.001
ADAM_B1 = 0.9
ADAM_B2 = 0.999
ADAM_EPS = 1e-08
ADAM_WD = 0.01
ADAM_STEP = 10

VMEM_LIMIT_BYTES = 56 * 1024 * 1024
MESH = pl.DeviceIdType.MESH


def _call(body, **kw):
    return pl.pallas_call(body, **kw)


def _cp(n_grid):
    return pltpu.CompilerParams(dimension_semantics=("arbitrary",) * n_grid, vmem_limit_bytes=VMEM_LIMIT_BYTES)


def _res(shape):
    nd = len(shape)
    return pl.BlockSpec(shape, lambda *_: (0,) * nd, pipeline_mode=pl.Buffered(1))


def _rows(tm, n):
    return pl.BlockSpec((tm, n), lambda i: (i, 0))


def _div_tile(n, target):
    best = None
    for d in range(16, min(n, target) + 1, 16):
        if n % d == 0:
            best = d
    assert best is not None, (n, target)
    return best


def _dot(a, b):
    return jnp.dot(a, b, preferred_element_type=F32)


def _dot_nt(a, b):
    return lax.dot_general(a, b, (((1,), (1,)), ((), ())), preferred_element_type=F32)


def _dot_tn(a, b):
    return lax.dot_general(a, b, (((0,), (0,)), ((), ())), preferred_element_type=F32)


def _bf(x):
    return x.astype(BF16)


def _sigmoid(x):
    return 1.0 / (1.0 + jnp.exp(-x))


def _rms_fwd(x, g):
    r = lax.rsqrt(jnp.mean(x * x, axis=-1, keepdims=True) + EPS)
    xh = x * r
    return xh * g, xh, r


def _rms_bwd(dh, xh, r, g):
    dxh = dh * g
    return r * (dxh - xh * jnp.mean(dxh * xh, axis=-1, keepdims=True))


def _colsum(x):
    return jnp.sum(x, axis=0, keepdims=True)


def _rope(x, cos, sin, sign):
    n = x.shape[1] // 128
    c = jnp.tile(cos, (1, n)) if n > 1 else cos
    s = jnp.tile(sin, (1, n)) if n > 1 else sin
    lane = lax.broadcasted_iota(jnp.int32, x.shape, 1)
    first = (lane & 63) < 32
    partner = jnp.where(first, pltpu.roll(x, x.shape[1] - 32, 1), pltpu.roll(x, 32, 1))
    return x * c + sign * (partner * s)


def _lo_lanes(shape):
    return lax.broadcasted_iota(jnp.int32, shape, 1) < 64


def _stack_heads(t):
    t0, t1 = t[:, 0:128], t[:, 128:256]
    lo = _lo_lanes(t0.shape)
    z = jnp.zeros_like(t0)
    return jnp.concatenate([jnp.where(lo, t0, z), jnp.where(lo, z, t0), jnp.where(lo, t1, z), jnp.where(lo, z, t1)], axis=0)


def _unstack_heads(o):
    lo = _lo_lanes((BLK, 128))
    return jnp.concatenate([jnp.where(lo, o[0:128], o[128:256]), jnp.where(lo, o[256:384], o[384:512])], axis=1)


def _sink_col(sk_ref, g):
    return jnp.concatenate([jnp.broadcast_to(sk_ref[4 * g + h:4 * g + h + 1, :], (BLK, 128)) for h in range(4)], axis=0)


def _tile3(x):
    return jnp.concatenate([x, x, x], axis=1)


def _mem_kv_fwd(mem, g_kv, w_kv):
    m_len = mem.shape[0]
    cols = w_kv.shape[2]

    def body(mem_ref, g_ref, w_ref, memn_ref, kv_ref):
        h, _, _ = _rms_fwd(mem_ref[...], g_ref[...])
        hb = _bf(h)
        memn_ref[...] = hb
        for s in range(N_CHIPS):
            kv_ref[s] = _bf(_dot(hb, w_ref[s]))

    return _call(
        body, name="mem_kv_fwd",
        out_shape=(jax.ShapeDtypeStruct((m_len, D_MODEL), BF16), jax.ShapeDtypeStruct((N_CHIPS, m_len, cols), BF16)),
        compiler_params=pltpu.CompilerParams(vmem_limit_bytes=VMEM_LIMIT_BYTES),
    )(mem, g_kv, w_kv)


def _in_proj_fwd(x, g_mix, w_ext, b_ext, cos_t, sin_t, tm):
    t_len = x.shape[0]

    def body(x_ref, g_ref, w_ref, b_ref, c_ref, s_ref, ug_ref, q_ref, k_ref, v_ref, h_ref):
        h, _, _ = _rms_fwd(x_ref[...], g_ref[...])
        hb = _bf(h)
        h_ref[...] = hb
        ug_ref[...] = _dot(hb, w_ref[:, 0:1024]) + b_ref[:, 0:1024]
        c, s = c_ref[...], s_ref[...]
        q_ref[...] = _bf(_rope(_dot(hb, w_ref[:, 1024:1536]) + b_ref[:, 1024:1536], c, s, 1.0))
        k_ref[...] = _bf(_rope(_dot(hb, w_ref[:, 1536:1792]) + b_ref[:, 1536:1792], c, s, 1.0))
        v_ref[...] = _bf(_dot(hb, w_ref[:, 1792:2048]) + b_ref[:, 1792:2048])

    return _call(
        body, name="in_proj_fwd", grid=(t_len // tm,),
        in_specs=[_rows(tm, D_MODEL), _res((1, D_MODEL)), _res(w_ext.shape), _res(b_ext.shape), _rows(tm, 128), _rows(tm, 128)],
        out_specs=[_rows(tm, 1024), _rows(tm, 512), _rows(tm, 256), _rows(tm, 256), _rows(tm, D_MODEL)],
        out_shape=(jax.ShapeDtypeStruct((t_len, 1024), F32), jax.ShapeDtypeStruct((t_len, 512), BF16),
                   jax.ShapeDtypeStruct((t_len, 256), BF16), jax.ShapeDtypeStruct((t_len, 256), BF16),
                   jax.ShapeDtypeStruct((t_len, D_MODEL), BF16)),
        compiler_params=_cp(1),
    )(x, g_mix, w_ext, b_ext, cos_t, sin_t)


def _halo_specs(tc, n, t_len):
    per = tc // 16
    last = t_len // 16 - 1
    return [pl.BlockSpec((16, n), lambda i: (jnp.maximum(i * per - 1, 0), 0)),
            pl.BlockSpec((tc, n), lambda i: (i, 0)),
            pl.BlockSpec((16, n), lambda i: (jnp.minimum((i + 1) * per, last), 0))]


def _glu(z):
    return z[:, 0:CONV_CH] * _sigmoid(z[:, CONV_CH:2 * CONV_CH])


def _fill_halo_buf(buf, prev, main, nxt, i, n_tiles, tc):
    buf[0:16, :] = jnp.where(i > 0, prev, jnp.zeros_like(prev))
    buf[16:16 + tc, :] = main
    buf[16 + tc:32 + tc, :] = jnp.where(i < n_tiles - 1, nxt, jnp.zeros_like(nxt))


def _conv_fwd(ug, w_dw, b_dw, g_ln, b_ln, tc):
    t_len = ug.shape[0]
    n_tiles = t_len // tc

    def body(up_ref, um_ref, un_ref, w_ref, bdw_ref, g_ref, b_ref, y_ref, pre_ref, buf):
        i = pl.program_id(0)
        _fill_halo_buf(buf, _glu(up_ref[...]), _glu(um_ref[...]), _glu(un_ref[...]), i, n_tiles, tc)
        acc = jnp.zeros((tc, CONV_CH), F32)
        for k in range(CONV_W):
            acc = acc + w_ref[k:k + 1, :] * buf[k + 1:k + 1 + tc, :]
        pre = acc + bdw_ref[...]
        pre_ref[...] = pre
        mu = jnp.mean(pre, axis=-1, keepdims=True)
        d = pre - mu
        rstd = lax.rsqrt(jnp.mean(d * d, axis=-1, keepdims=True) + EPS)
        ln = d * rstd * g_ref[...] + b_ref[...]
        y_ref[...] = _bf(ln * _sigmoid(ln))

    return _call(
        body, name="conv_fwd", grid=(n_tiles,),
        in_specs=_halo_specs(tc, 1024, t_len) + [_res((32, CONV_CH)), _res((1, CONV_CH)), _res((1, CONV_CH)), _res((1, CONV_CH))],
        out_specs=[_rows(tc, CONV_CH), _rows(tc, CONV_CH)],
        out_shape=(jax.ShapeDtypeStruct((t_len, CONV_CH), BF16), jax.ShapeDtypeStruct((t_len, CONV_CH), F32)),
        scratch_shapes=[pltpu.VMEM((tc + 32, CONV_CH), F32)],
        compiler_params=_cp(1),
    )(ug, ug, ug, w_dw, b_dw, g_ln, b_ln)


def _nbr_specs(n, nb):
    return [pl.BlockSpec((BLK, n), lambda i: (jnp.maximum(i - 1, 0), 0)),
            pl.BlockSpec((BLK, n), lambda i: (i, 0)),
            pl.BlockSpec((BLK, n), lambda i: (jnp.minimum(i + 1, nb - 1), 0))]


def _nbr_specs4(nb):
    return [pl.BlockSpec((1, 2, 4 * BLK, 128), lambda i: (jnp.maximum(i - 1, 0), 0, 0, 0)),
            pl.BlockSpec((1, 2, 4 * BLK, 128), lambda i: (i, 0, 0, 0)),
            pl.BlockSpec((1, 2, 4 * BLK, 128), lambda i: (jnp.minimum(i + 1, nb - 1), 0, 0, 0))]


def _band_mask_q(i, t_len):
    a = lax.broadcasted_iota(jnp.int32, (4 * BLK, 3 * BLK), 0) & (BLK - 1)
    c = lax.broadcasted_iota(jnp.int32, (4 * BLK, 3 * BLK), 1)
    kpos = (i - 1) * BLK + c
    rel = c - BLK - a
    return (jnp.abs(rel) <= BLK) & (kpos >= 0) & (kpos < t_len)


def _attn_fwd(q, kd, vd, sink_b):
    t_len = q.shape[0]
    nb = t_len // BLK

    def body(q_ref, kp_ref, kc_ref, kn_ref, vp_ref, vc_ref, vn_ref, sk_ref, y_ref, lse_ref):
        i = pl.program_id(0)
        kcat = jnp.concatenate([kp_ref[...], kc_ref[...], kn_ref[...]], axis=0)
        vcat = jnp.concatenate([vp_ref[...], vc_ref[...], vn_ref[...]], axis=0)
        valid = _band_mask_q(i, t_len)
        ys = []
        for g in range(2):
            qs = _stack_heads(q_ref[:, 256 * g:256 * g + 256])
            s = _dot_nt(qs, kcat[:, 128 * g:128 * g + 128]) * ATT_SCALE
            s = jnp.where(valid, s, NEG)
            skc = _sink_col(sk_ref, g)
            m_b = jnp.maximum(jnp.max(s, axis=-1, keepdims=True), skc)
            p = jnp.exp(s - _tile3(m_b))
            den_b = jnp.sum(p, axis=-1, keepdims=True) + jnp.exp(skc - m_b)
            pn = p * _tile3(1.0 / den_b)
            o = _dot(_bf(pn), vcat[:, 128 * g:128 * g + 128])
            ys.append(_unstack_heads(o))
            lse_ref[0, g] = m_b + jnp.log(den_b)
        y_ref[...] = _bf(jnp.concatenate(ys, axis=1))

    return _call(
        body, name="attn_fwd", grid=(nb,),
        in_specs=[_rows(BLK, 512)] + _nbr_specs(256, nb) + _nbr_specs(256, nb) + [_res((8, 128))],
        out_specs=[_rows(BLK, 512), pl.BlockSpec((1, 2, 4 * BLK, 128), lambda i: (i, 0, 0, 0))],
        out_shape=(jax.ShapeDtypeStruct((t_len, 512), BF16), jax.ShapeDtypeStruct((nb, 2, 4 * BLK, 128), F32)),
        compiler_params=_cp(1),
    )(q, kd, kd, kd, vd, vd, vd, sink_b)


def _mem_heads(kv_ref, h):
    lo = MEM_HD * (h % 2)
    return kv_ref[h // 2, :, lo:lo + MEM_HD], kv_ref[2 + h // 2, :, lo:lo + MEM_HD]


def _mix_mem_fwd(x, yc, ya, w_out, b_out, g_q, w_q, kv, w_o, tm):
    t_len = x.shape[0]

    def body(x_ref, yc_ref, ya_ref, wout_ref, bout_ref, g_ref, wq_ref, kv_ref, wo_ref,
             ymix_ref, x1_ref, hq_ref, qm_ref, om_ref, x2_ref):
        ymix = jnp.concatenate([yc_ref[...], ya_ref[...]], axis=1)
        ymix_ref[...] = ymix
        x1 = x_ref[...] + _dot(ymix, wout_ref[...]) + bout_ref[...]
        x1_ref[...] = x1
        hq, _, _ = _rms_fwd(x1, g_ref[...])
        hqb = _bf(hq)
        hq_ref[...] = hqb
        qm = _bf(_dot(hqb, wq_ref[...]))
        qm_ref[...] = qm
        outs = []
        for h in range(MEM_HEADS):
            kh, vh = _mem_heads(kv_ref, h)
            s = _dot_nt(qm[:, MEM_HD * h:MEM_HD * (h + 1)], kh) * MEM_SCALE
            p = jnp.exp(s - jnp.max(s, axis=-1, keepdims=True))
            p = p * (1.0 / jnp.sum(p, axis=-1, keepdims=True))
            outs.append(_dot(_bf(p), vh))
        om = _bf(jnp.concatenate(outs, axis=1))
        om_ref[...] = om
        x2_ref[...] = x1 + _dot(om, wo_ref[...])

    act_b = jax.ShapeDtypeStruct((t_len, D_MODEL), BF16)
    act_f = jax.ShapeDtypeStruct((t_len, D_MODEL), F32)
    return _call(
        body, name="mix_mem_fwd", grid=(t_len // tm,),
        in_specs=[_rows(tm, D_MODEL), _rows(tm, 512), _rows(tm, 512), _res(w_out.shape), _res((1, D_MODEL)), _res((1, D_MODEL)),
                  _res(w_q.shape), _res(kv.shape), _res(w_o.shape)],
        out_specs=[_rows(tm, D_MODEL)] * 6,
        out_shape=(act_b, act_f, act_b, act_b, act_b, act_f),
        compiler_params=_cp(1),
    )(x, yc, ya, w_out, b_out, g_q, w_q, kv, w_o)


def _ffn_loss(x2, g_ffn, w_gate, w_up, w_down, g_final, target, tm):
    t_len = x2.shape[0]
    ff = w_gate.shape[2]
    n_tiles = t_len // tm

    def body(x2_ref, g_ref, wg_ref, wu_ref, wd_ref, gf_ref, tgt_ref,
             hf_ref, gate_ref, up_ref, act_ref, dx3_ref, loss_ref, dgf_ref):
        i = pl.program_id(0)
        x2v = x2_ref[...]
        hf, _, _ = _rms_fwd(x2v, g_ref[...])
        hfb = _bf(hf)
        hf_ref[...] = hfb
        acc = jnp.zeros((tm, D_MODEL), F32)
        for s in range(N_CHIPS):
            gate = _dot(hfb, wg_ref[s])
            up = _dot(hfb, wu_ref[s])
            act = _bf(gate * _sigmoid(gate) * up)
            gate_ref[s] = _bf(gate)
            up_ref[s] = _bf(up)
            act_ref[s] = act
            acc = acc + _dot(act, wd_ref[s])
        x3 = x2v + acc
        gf = gf_ref[...]
        y, xh, r = _rms_fwd(x3, gf)
        err = y - tgt_ref[...]
        part = 0.5 * jnp.sum(jnp.mean(err * err, axis=-1, keepdims=True), axis=0, keepdims=True)
        dy = err * (1.0 / D_MODEL)
        dx3_ref[...] = _rms_bwd(dy, xh, r, gf)

        @pl.when(i == 0)
        def _():
            loss_ref[...] = jnp.zeros_like(loss_ref)
            dgf_ref[...] = jnp.zeros_like(dgf_ref)

        loss_ref[...] += jnp.broadcast_to(part, loss_ref.shape)
        dgf_ref[...] += _colsum(dy * xh)

    hid = jax.ShapeDtypeStruct((N_CHIPS, t_len, ff), BF16)
    hid_spec = pl.BlockSpec((N_CHIPS, tm, ff), lambda i: (0, i, 0))
    return _call(
        body, name="ffn_loss", grid=(n_tiles,),
        in_specs=[_rows(tm, D_MODEL), _res((1, D_MODEL)), _res(w_gate.shape), _res(w_up.shape), _res(w_down.shape),
                  _res((1, D_MODEL)), _rows(tm, D_MODEL)],
        out_specs=[_rows(tm, D_MODEL), hid_spec, hid_spec, hid_spec, _rows(tm, D_MODEL),
                   pl.BlockSpec((1, D_MODEL), lambda i: (0, 0)), pl.BlockSpec((1, D_MODEL), lambda i: (0, 0))],
        out_shape=(jax.ShapeDtypeStruct((t_len, D_MODEL), BF16), hid, hid, hid, jax.ShapeDtypeStruct((t_len, D_MODEL), F32),
                   jax.ShapeDtypeStruct((1, D_MODEL), F32), jax.ShapeDtypeStruct((1, D_MODEL), F32)),
        compiler_params=_cp(1),
    )(x2, g_ffn, w_gate, w_up, w_down, g_final, target)


def _ffn_bwd(dx3, x2, gate, up, g_ffn, w_gate, w_up, w_down, tm):
    t_len = x2.shape[0]
    ff = w_gate.shape[2]

    def body(dx3_ref, x2_ref, gate_ref, up_ref, g_ref, wg_ref, wu_ref, wd_ref, dx2_ref, dgate_ref, dup_ref, dg_ref):
        i = pl.program_id(0)
        dx3 = dx3_ref[...]
        d3b = _bf(dx3)
        dh = jnp.zeros((tm, D_MODEL), F32)
        for s in range(N_CHIPS):
            dact = _dot_nt(d3b, wd_ref[s])
            gt = gate_ref[s].astype(F32)
            u = up_ref[s].astype(F32)
            sg = _sigmoid(gt)
            dup = _bf(dact * (gt * sg))
            dgate = _bf(dact * u * (sg * (1.0 + gt * (1.0 - sg))))
            dup_ref[s] = dup
            dgate_ref[s] = dgate
            dh = dh + _dot_nt(dgate, wg_ref[s]) + _dot_nt(dup, wu_ref[s])
        g = g_ref[...]
        _, xh, r = _rms_fwd(x2_ref[...], g)
        dx2_ref[...] = dx3 + _rms_bwd(dh, xh, r, g)

        @pl.when(i == 0)
        def _():
            dg_ref[...] = jnp.zeros_like(dg_ref)

        dg_ref[...] += _colsum(dh * xh)

    hid = jax.ShapeDtypeStruct((N_CHIPS, t_len, ff), BF16)
    hid_spec = pl.BlockSpec((N_CHIPS, tm, ff), lambda i: (0, i, 0))
    return _call(
        body, name="ffn_bwd", grid=(t_len // tm,),
        in_specs=[_rows(tm, D_MODEL), _rows(tm, D_MODEL), hid_spec, hid_spec, _res((1, D_MODEL)),
                  _res(w_gate.shape), _res(w_up.shape), _res(w_down.shape)],
        out_specs=[_rows(tm, D_MODEL), hid_spec, hid_spec, pl.BlockSpec((1, D_MODEL), lambda i: (0, 0))],
        out_shape=(jax.ShapeDtypeStruct((t_len, D_MODEL), F32), hid, hid, jax.ShapeDtypeStruct((1, D_MODEL), F32)),
        compiler_params=_cp(1),
    )(dx3, x2, gate, up, g_ffn, w_gate, w_up, w_down)


def _mix_mem_bwd(dx2, x1, qm, kv, g_q, w_q, w_o, w_out, tm):
    t_len = x1.shape[0]
    m_len = kv.shape[1]

    def body(dx2_ref, x1_ref, qm_ref, kv_ref, g_ref, wq_ref, wo_ref, wout_ref,
             dx1_ref, dqm_ref, dyc_ref, dya_ref, dkv_ref, dgq_ref, dbout_ref):
        i = pl.program_id(0)

        @pl.when(i == 0)
        def _():
            dkv_ref[...] = jnp.zeros_like(dkv_ref)
            dgq_ref[...] = jnp.zeros_like(dgq_ref)
            dbout_ref[...] = jnp.zeros_like(dbout_ref)

        dx2 = dx2_ref[...]
        dom = _dot_nt(_bf(dx2), wo_ref[...])
        dqs = []
        for h in range(MEM_HEADS):
            kh, vh = _mem_heads(kv_ref, h)
            qh = qm_ref[:, MEM_HD * h:MEM_HD * (h + 1)]
            s = _dot_nt(qh, kh) * MEM_SCALE
            p = jnp.exp(s - jnp.max(s, axis=-1, keepdims=True))
            p = p * (1.0 / jnp.sum(p, axis=-1, keepdims=True))
            domh = _bf(dom[:, MEM_HD * h:MEM_HD * (h + 1)])
            dp = _dot_nt(domh, vh)
            ds = _bf(p * (dp - jnp.sum(p * dp, axis=-1, keepdims=True)) * MEM_SCALE)
            dqs.append(_dot(ds, kh))
            lo = MEM_HD * (h % 2)
            dkv_ref[h // 2, :, lo:lo + MEM_HD] += _dot_tn(ds, qh)
            dkv_ref[2 + h // 2, :, lo:lo + MEM_HD] += _dot_tn(_bf(p), domh)
        dqm = _bf(jnp.concatenate(dqs, axis=1))
        dqm_ref[...] = dqm
        dhq = _dot_nt(dqm, wq_ref[...])
        g = g_ref[...]
        _, xh, r = _rms_fwd(x1_ref[...], g)
        dx1 = dx2 + _rms_bwd(dhq, xh, r, g)
        dx1_ref[...] = dx1
        dgq_ref[...] += _colsum(dhq * xh)
        dbout_ref[...] += _colsum(dx1)
        dymix = _dot_nt(_bf(dx1), wout_ref[...])
        dyc_ref[...] = dymix[:, 0:CONV_CH]
        dya_ref[...] = _bf(dymix[:, CONV_CH:2 * CONV_CH])

    vec = pl.BlockSpec((1, D_MODEL), lambda i: (0, 0))
    return _call(
        body, name="mix_mem_bwd", grid=(t_len // tm,),
        in_specs=[_rows(tm, D_MODEL), _rows(tm, D_MODEL), _rows(tm, D_MODEL), _res(kv.shape), _res((1, D_MODEL)),
                  _res(w_q.shape), _res(w_o.shape), _res(w_out.shape)],
        out_specs=[_rows(tm, D_MODEL), _rows(tm, D_MODEL), _rows(tm, CONV_CH), _rows(tm, CONV_CH),
                   pl.BlockSpec(kv.shape, lambda i: (0, 0, 0)), vec, vec],
        out_shape=(jax.ShapeDtypeStruct((t_len, D_MODEL), F32), jax.ShapeDtypeStruct((t_len, D_MODEL), BF16),
                   jax.ShapeDtypeStruct((t_len, CONV_CH), F32), jax.ShapeDtypeStruct((t_len, CONV_CH), BF16),
                   jax.ShapeDtypeStruct((N_CHIPS, m_len, kv.shape[2]), F32),
                   jax.ShapeDtypeStruct((1, D_MODEL), F32), jax.ShapeDtypeStruct((1, D_MODEL), F32)),
        compiler_params=_cp(1),
    )(dx2, x1, qm, kv, g_q, w_q, w_o, w_out)


def _mem_kv_bwd(dkv, memn, mem, g_kv, w_kv):
    m_len = mem.shape[0]

    def body(dkv_ref, memn_ref, mem_ref, g_ref, w_ref, dw_ref, dg_ref):
        hb = memn_ref[...]
        dmn = jnp.zeros((m_len, D_MODEL), F32)
        for s in range(N_CHIPS):
            d = _bf(dkv_ref[s])
            dw_ref[s] = _dot_tn(hb, d)
            dmn = dmn + _dot_nt(d, w_ref[s])
        _, xh, _ = _rms_fwd(mem_ref[...], g_ref[...])
        dg_ref[...] = _colsum(dmn * xh)

    return _call(
        body, name="mem_kv_bwd",
        out_shape=(jax.ShapeDtypeStruct(w_kv.shape, F32), jax.ShapeDtypeStruct((1, D_MODEL), F32)),
        compiler_params=pltpu.CompilerParams(vmem_limit_bytes=VMEM_LIMIT_BYTES),
    )(dkv, memn, mem, g_kv, w_kv)


def _attn_bwd_q(q, kd, vd, dya, lse, sink_b, cos_t, sin_t):
    t_len = q.shape[0]
    nb = t_len // BLK

    def body(q_ref, kp_ref, kc_ref, kn_ref, vp_ref, vc_ref, vn_ref, do_ref, lse_ref, sk_ref, c_ref, s_ref,
             dq_ref, dd_ref, dsk_ref):
        i = pl.program_id(0)
        kcat = jnp.concatenate([kp_ref[...], kc_ref[...], kn_ref[...]], axis=0)
        vcat = jnp.concatenate([vp_ref[...], vc_ref[...], vn_ref[...]], axis=0)
        valid = _band_mask_q(i, t_len)
        dqs, dsks = [], []
        for g in range(2):
            qs = _stack_heads(q_ref[:, 256 * g:256 * g + 256])
            dos = _stack_heads(do_ref[:, 256 * g:256 * g + 256])
            kk = kcat[:, 128 * g:128 * g + 128]
            s = jnp.where(valid, _dot_nt(qs, kk) * ATT_SCALE, NEG)
            lse_b = lse_ref[0, g]
            p = jnp.exp(s - _tile3(lse_b))
            dp = _dot_nt(dos, vcat[:, 128 * g:128 * g + 128])
            drow = jnp.sum(p * dp, axis=-1, keepdims=True)
            ds = _bf(p * (dp - drow) * ATT_SCALE)
            dqs.append(_unstack_heads(_dot(ds, kk)))
            d_b = jnp.broadcast_to(drow, (4 * BLK, 128))
            dd_ref[0, g] = d_b
            contrib = -(jnp.exp(_sink_col(sk_ref, g) - lse_b) * d_b)
            dsks.append(jnp.sum(contrib.reshape(4, BLK, 128), axis=1))
        dq = jnp.concatenate(dqs, axis=1)
        dq_ref[...] = _bf(_rope(dq, c_ref[...], s_ref[...], -1.0))
        dsk_ref[0] = jnp.concatenate(dsks, axis=0)

    stat = pl.BlockSpec((1, 2, 4 * BLK, 128), lambda i: (i, 0, 0, 0))
    return _call(
        body, name="attn_bwd_q", grid=(nb,),
        in_specs=[_rows(BLK, 512)] + _nbr_specs(256, nb) + _nbr_specs(256, nb) + [_rows(BLK, 512), stat, _res((8, 128)),
                                                                                 _rows(BLK, 128), _rows(BLK, 128)],
        out_specs=[_rows(BLK, 512), stat, pl.BlockSpec((1, 8, 128), lambda i: (i, 0, 0))],
        out_shape=(jax.ShapeDtypeStruct((t_len, 512), BF16), jax.ShapeDtypeStruct((nb, 2, 4 * BLK, 128), F32),
                   jax.ShapeDtypeStruct((nb, 8, 128), F32)),
        compiler_params=_cp(1),
    )(q, kd, kd, kd, vd, vd, vd, dya, lse, sink_b, cos_t, sin_t)


def _attn_bwd_kv(q, kd, vd, dya, lse, dd, cos_t, sin_t):
    t_len = q.shape[0]
    nb = t_len // BLK

    def body(kc_ref, vc_ref, qp_ref, qc_ref, qn_ref, dop_ref, doc_ref, don_ref, lp_ref, lc_ref, ln_ref,
             dp_ref, dc_ref, dn_ref, c_ref, s_ref, dk_ref, dv_ref):
        j = pl.program_id(0)
        row = lax.broadcasted_iota(jnp.int32, (12 * BLK, BLK), 0)
        col = lax.broadcasted_iota(jnp.int32, (12 * BLK, BLK), 1)
        qblk = j - 1 + row // (4 * BLK)
        rel = col - (row & (BLK - 1)) + (j - qblk) * BLK
        valid = (jnp.abs(rel) <= BLK) & (qblk >= 0) & (qblk < nb)
        dks, dvs = [], []
        for g in range(2):
            cols = slice(256 * g, 256 * g + 256)
            qs = jnp.concatenate([_stack_heads(r[:, cols]) for r in (qp_ref, qc_ref, qn_ref)], axis=0)
            dos = jnp.concatenate([_stack_heads(r[:, cols]) for r in (dop_ref, doc_ref, don_ref)], axis=0)
            lse_b = jnp.concatenate([r[0, g] for r in (lp_ref, lc_ref, ln_ref)], axis=0)
            d_b = jnp.concatenate([r[0, g] for r in (dp_ref, dc_ref, dn_ref)], axis=0)
            kk = kc_ref[:, 128 * g:128 * g + 128]
            s = jnp.where(valid, _dot_nt(qs, kk) * ATT_SCALE, NEG)
            p = jnp.exp(s - lse_b)
            dp = _dot_nt(dos, vc_ref[:, 128 * g:128 * g + 128])
            ds = _bf(p * (dp - d_b) * ATT_SCALE)
            dvs.append(_dot_tn(_bf(p), dos))
            dks.append(_dot_tn(ds, qs))
        dk = jnp.concatenate(dks, axis=1)
        dk_ref[...] = _bf(_rope(dk, c_ref[...], s_ref[...], -1.0))
        dv_ref[...] = _bf(jnp.concatenate(dvs, axis=1))

    return _call(
        body, name="attn_bwd_kv", grid=(nb,),
        in_specs=[_rows(BLK, 256), _rows(BLK, 256)] + _nbr_specs(512, nb) + _nbr_specs(512, nb) + _nbr_specs4(nb) + _nbr_specs4(nb)
        + [_rows(BLK, 128), _rows(BLK, 128)],
        out_specs=[_rows(BLK, 256), _rows(BLK, 256)],
        out_shape=(jax.ShapeDtypeStruct((t_len, 256), BF16), jax.ShapeDtypeStruct((t_len, 256), BF16)),
        compiler_params=_cp(1),
    )(kd, vd, q, q, q, dya, dya, dya, lse, lse, lse, dd, dd, dd, cos_t, sin_t)


def _conv_norm_bwd(pre, dyc, g_ln, b_ln, tc):
    t_len = pre.shape[0]

    def body(pre_ref, dy_ref, g_ref, b_ref, dpre_ref, stats_ref):
        i = pl.program_id(0)
        pre_v = pre_ref[...]
        mu = jnp.mean(pre_v, axis=-1, keepdims=True)
        d = pre_v - mu
        rstd = lax.rsqrt(jnp.mean(d * d, axis=-1, keepdims=True) + EPS)
        xh = d * rstd
        g = g_ref[...]
        ln = xh * g + b_ref[...]
        sg = _sigmoid(ln)
        dln = dy_ref[...] * (sg * (1.0 + ln * (1.0 - sg)))
        dxh = dln * g
        dpre = rstd * (dxh - jnp.mean(dxh, axis=-1, keepdims=True) - xh * jnp.mean(dxh * xh, axis=-1, keepdims=True))
        dpre_ref[...] = dpre

        @pl.when(i == 0)
        def _():
            stats_ref[...] = jnp.zeros_like(stats_ref)

        stats_ref[0:1, :] += _colsum(dln * xh)
        stats_ref[1:2, :] += _colsum(dln)
        stats_ref[2:3, :] += _colsum(dpre)

    return _call(
        body, name="conv_norm_bwd", grid=(t_len // tc,),
        in_specs=[_rows(tc, CONV_CH), _rows(tc, CONV_CH), _res((1, CONV_CH)), _res((1, CONV_CH))],
        out_specs=[_rows(tc, CONV_CH), pl.BlockSpec((8, CONV_CH), lambda i: (0, 0))],
        out_shape=(jax.ShapeDtypeStruct((t_len, CONV_CH), F32), jax.ShapeDtypeStruct((8, CONV_CH), F32)),
        compiler_params=_cp(1),
    )(pre, dyc, g_ln, b_ln)


def _conv_bwd(dpre, ug, w_dw, tc):
    t_len = ug.shape[0]
    n_tiles = t_len // tc

    def body(dp_ref, dm_ref, dn_ref, up_ref, um_ref, un_ref, w_ref, du_ref, dw_ref, dbuf, vbuf):
        i = pl.program_id(0)
        _fill_halo_buf(dbuf, dp_ref[...], dm_ref[...], dn_ref[...], i, n_tiles, tc)
        _fill_halo_buf(vbuf, _glu(up_ref[...]), _glu(um_ref[...]), _glu(un_ref[...]), i, n_tiles, tc)
        dmain = dm_ref[...]
        dv = jnp.zeros((tc, CONV_CH), F32)
        taps = []
        for k in range(CONV_W):
            dv = dv + w_ref[k:k + 1, :] * dbuf[31 - k:31 - k + tc, :]
            taps.append(_colsum(dmain * vbuf[k + 1:k + 1 + tc, :]))
        taps.append(jnp.zeros((1, CONV_CH), F32))
        um = um_ref[...]
        a, gt = um[:, 0:CONV_CH], um[:, CONV_CH:2 * CONV_CH]
        sg = _sigmoid(gt)
        du_ref[...] = _bf(jnp.concatenate([dv * sg, dv * a * (sg * (1.0 - sg))], axis=1))

        @pl.when(i == 0)
        def _():
            dw_ref[...] = jnp.zeros_like(dw_ref)

        dw_ref[...] += jnp.concatenate(taps, axis=0)

    return _call(
        body, name="conv_bwd", grid=(n_tiles,),
        in_specs=_halo_specs(tc, CONV_CH, t_len) + _halo_specs(tc, 1024, t_len) + [_res((32, CONV_CH))],
        out_specs=[_rows(tc, 1024), pl.BlockSpec((32, CONV_CH), lambda i: (0, 0))],
        out_shape=(jax.ShapeDtypeStruct((t_len, 1024), BF16), jax.ShapeDtypeStruct((32, CONV_CH), F32)),
        scratch_shapes=[pltpu.VMEM((tc + 32, CONV_CH), F32), pltpu.VMEM((tc + 32, CONV_CH), F32)],
        compiler_params=_cp(1),
    )(dpre, dpre, dpre, ug, ug, ug, w_dw)


def _in_proj_bwd(du_glu, dq, dk, dv, dx1, x, g_mix, w_ext, tm):
    t_len = x.shape[0]
    n_ext = w_ext.shape[1]

    def body(dg_ref, dq_ref, dk_ref, dv_ref, dx1_ref, x_ref, g_ref, w_ref, dx_ref, du_ref, db_ref, dgm_ref):
        i = pl.program_id(0)
        du = jnp.concatenate([dg_ref[...], dq_ref[...], dk_ref[...], dv_ref[...]], axis=1)
        du_ref[...] = du
        dh = _dot_nt(du, w_ref[...])
        g = g_ref[...]
        _, xh, r = _rms_fwd(x_ref[...], g)
        dx_ref[...] = dx1_ref[...] + _rms_bwd(dh, xh, r, g)

        @pl.when(i == 0)
        def _():
            db_ref[...] = jnp.zeros_like(db_ref)
            dgm_ref[...] = jnp.zeros_like(dgm_ref)

        db_ref[...] += _colsum(du.astype(F32))
        dgm_ref[...] += _colsum(dh * xh)

    return _call(
        body, name="in_proj_bwd", grid=(t_len // tm,),
        in_specs=[_rows(tm, 1024), _rows(tm, 512), _rows(tm, 256), _rows(tm, 256), _rows(tm, D_MODEL), _rows(tm, D_MODEL),
                  _res((1, D_MODEL)), _res(w_ext.shape)],
        out_specs=[_rows(tm, D_MODEL), _rows(tm, n_ext), pl.BlockSpec((1, n_ext), lambda i: (0, 0)),
                   pl.BlockSpec((1, D_MODEL), lambda i: (0, 0))],
        out_shape=(jax.ShapeDtypeStruct((t_len, D_MODEL), F32), jax.ShapeDtypeStruct((t_len, n_ext), BF16),
                   jax.ShapeDtypeStruct((1, n_ext), F32), jax.ShapeDtypeStruct((1, D_MODEL), F32)),
        compiler_params=_cp(1),
    )(du_glu, dq, dk, dv, dx1, x, g_mix, w_ext)


def _weight_grad(a, d, name, tt):
    sa, t_len, k_dim = a.shape
    sd, _, n_dim = d.shape
    n_s = max(sa, sd)
    tn = n_dim if n_dim <= 1024 else 1024
    tt = min(tt, t_len)

    def body(a_ref, d_ref, o_ref):
        t = pl.program_id(2)

        @pl.when(t == 0)
        def _():
            o_ref[...] = jnp.zeros_like(o_ref)

        o_ref[0] += _dot_tn(_bf(a_ref[0]), _bf(d_ref[0]))

    return _call(
        body, name=name, grid=(n_s, n_dim // tn, t_len // tt),
        in_specs=[pl.BlockSpec((1, tt, k_dim), (lambda s, n, t: (s, t, 0)) if sa > 1 else (lambda s, n, t: (0, t, 0))),
                  pl.BlockSpec((1, tt, tn), (lambda s, n, t: (s, t, n)) if sd > 1 else (lambda s, n, t: (0, t, n)))],
        out_specs=pl.BlockSpec((1, k_dim, tn), lambda s, n, t: (s, 0, n)),
        out_shape=jax.ShapeDtypeStruct((n_s, k_dim, n_dim), F32),
        compiler_params=_cp(3),
    )(a, d)


ANY = pl.BlockSpec(memory_space=pl.ANY)


def _place():
    x, y, c = lax.axis_index("x"), lax.axis_index("y"), lax.axis_index("c")
    chips = [(1 - x, y), (x, 1 - y), (1 - x, 1 - y)]
    return x, y, c, chips


def _remote(src, dst, send_sems, recv_sems, k, to):
    return pltpu.make_async_remote_copy(src_ref=src, dst_ref=dst, send_sem=send_sems.at[k], recv_sem=recv_sems.at[k],
                                        device_id=to, device_id_type=MESH)


def _gather_weights(shards, dw_shard):
    n = len(shards)

    def body(*refs):
        ins, dw_in = refs[0:n], refs[n]
        outs, dw_out = refs[n + 1:2 * n + 1], refs[2 * n + 1]
        send_sems, recv_sems, loc_sems = refs[2 * n + 2:]
        x, y, c, chips = _place()
        me = 2 * x + y
        sib = (x, y, 1 - c)
        local = [pltpu.make_async_copy(ins[i], outs[i].at[me], loc_sems.at[i]) for i in range(n)]
        local.append(pltpu.make_async_copy(dw_in, dw_out.at[me], loc_sems.at[n]))
        for cp in local:
            cp.start()
        sent = []
        for i in range(n):
            for j, chip in enumerate(chips):
                cp = _remote(ins[i].at[c], outs[i].at[me, c], send_sems, recv_sems, 3 * i + j, (*chip, c))
                cp.start()
                sent.append(cp)
        for j, chip in enumerate(chips):
            cp = _remote(dw_in, dw_out.at[me], send_sems, recv_sems, 3 * n + j, (*chip, c))
            cp.start()
            sent.append(cp)
        base = 3 * n + 3
        for i in range(n):
            for j, (cx, cy) in enumerate(chips):
                slab = outs[i].at[2 * cx + cy, c]
                _remote(slab, slab, send_sems, recv_sems, 3 * i + j, sib).wait_recv()
                cp = _remote(slab, slab, send_sems, recv_sems, base + 3 * i + j, sib)
                cp.start()
                sent.append(cp)
        for j, (cx, cy) in enumerate(chips):
            slab = dw_out.at[2 * cx + cy]
            _remote(slab, slab, send_sems, recv_sems, 3 * n + j, sib).wait_recv()
        for i in range(n):
            for j, (cx, cy) in enumerate(chips):
                slab = outs[i].at[2 * cx + cy, 1 - c]
                _remote(slab, slab, send_sems, recv_sems, base + 3 * i + j, sib).wait_recv()
        for cp in sent:
            cp.wait_send()
        for cp in local:
            cp.wait()

    n_sem = 6 * n + 3
    out_shape = [jax.ShapeDtypeStruct((N_CHIPS,) + s.shape, s.dtype) for s in shards]
    out_shape.append(jax.ShapeDtypeStruct((N_CHIPS,) + dw_shard.shape, dw_shard.dtype))
    return _call(
        body, name="gather_weights", in_specs=[ANY] * (n + 1), out_specs=[ANY] * (n + 1), out_shape=out_shape,
        scratch_shapes=[pltpu.SemaphoreType.DMA((n_sem,)), pltpu.SemaphoreType.DMA((n_sem,)), pltpu.SemaphoreType.DMA((n + 1,))],
    )(*shards, dw_shard)


def _sibling_swap(grads):
    n = len(grads)

    def body(*refs):
        ins, outs = refs[0:n], refs[n:2 * n]
        send_sems, recv_sems = refs[2 * n:]
        x, y, c, _ = _place()
        sib = (x, y, 1 - c)
        sent = []
        for i in range(n):
            cp = _remote(ins[i].at[:, 1 - c], outs[i], send_sems, recv_sems, i, sib)
            cp.start()
            sent.append(cp)
        for cp in sent:
            cp.wait()

    out_shape = [jax.ShapeDtypeStruct((g.shape[0],) + g.shape[2:], g.dtype) for g in grads]
    return _call(
        body, name="sibling_swap", in_specs=[ANY] * n, out_specs=[ANY] * n, out_shape=out_shape,
        scratch_shapes=[pltpu.SemaphoreType.DMA((n,)), pltpu.SemaphoreType.DMA((n,))],
    )(*grads)


def _pair_sum(grad, other, c_idx, tr):
    n_s, _, h, cols = grad.shape
    tr = _div_tile(h, tr)

    def body(c_ref, a_ref, b_ref, o_ref):
        o_ref[...] = _bf(a_ref[0] + b_ref[...])

    return _call(
        body, name="pair_sum",
        grid_spec=pltpu.PrefetchScalarGridSpec(
            num_scalar_prefetch=1, grid=(n_s, h // tr),
            in_specs=[pl.BlockSpec((1, 1, tr, cols), lambda s, r, c_ref: (s, c_ref[0], r, 0)),
                      pl.BlockSpec((1, tr, cols), lambda s, r, c_ref: (s, r, 0))],
            out_specs=pl.BlockSpec((1, tr, cols), lambda s, r, c_ref: (s, r, 0))),
        out_shape=jax.ShapeDtypeStruct((n_s, h, cols), BF16),
        compiler_params=_cp(2),
    )(c_idx, grad, other)


def _chip_exchange(sums):
    n = len(sums)

    def body(*refs):
        ins, outs = refs[0:n], refs[n:2 * n]
        send_sems, recv_sems, loc_sems = refs[2 * n:]
        x, y, c, chips = _place()
        me = 2 * x + y
        local = [pltpu.make_async_copy(ins[i].at[me], outs[i].at[me], loc_sems.at[i]) for i in range(n)]
        for cp in local:
            cp.start()
        sent = []
        for i in range(n):
            for j, (cx, cy) in enumerate(chips):
                cp = _remote(ins[i].at[2 * cx + cy], outs[i].at[me], send_sems, recv_sems, 3 * i + j, (cx, cy, c))
                cp.start()
                sent.append(cp)
        for i in range(n):
            for j, (cx, cy) in enumerate(chips):
                slab = outs[i].at[2 * cx + cy]
                _remote(slab, slab, send_sems, recv_sems, 3 * i + j, (cx, cy, c)).wait_recv()
        for cp in sent:
            cp.wait_send()
        for cp in local:
            cp.wait()

    return _call(
        body, name="chip_exchange", in_specs=[ANY] * n, out_specs=[ANY] * n,
        out_shape=[jax.ShapeDtypeStruct(s.shape, s.dtype) for s in sums],
        scratch_shapes=[pltpu.SemaphoreType.DMA((3 * n,)), pltpu.SemaphoreType.DMA((3 * n,)), pltpu.SemaphoreType.DMA((n,))],
    )(*sums)


def _chip_sum(parts, tr):
    _, h, cols = parts.shape
    tr = _div_tile(h, tr)

    def body(p_ref, o_ref):
        acc = p_ref[0].astype(F32)
        for s in range(1, N_CHIPS):
            acc = acc + p_ref[s].astype(F32)
        o_ref[...] = acc

    return _call(
        body, name="chip_sum", grid=(h // tr,),
        in_specs=[pl.BlockSpec((N_CHIPS, tr, cols), lambda r: (0, r, 0))],
        out_specs=pl.BlockSpec((tr, cols), lambda r: (r, 0)),
        out_shape=jax.ShapeDtypeStruct((h, cols), F32),
        compiler_params=_cp(1),
    )(parts)


def _sibling_share(halves):
    n = len(halves)

    def body(*refs):
        ins, outs = refs[0:n], refs[n:2 * n]
        send_sems, recv_sems, loc_sems = refs[2 * n:]
        x, y, c, _ = _place()
        sib = (x, y, 1 - c)
        local = [pltpu.make_async_copy(ins[i], outs[i].at[c], loc_sems.at[i]) for i in range(n)]
        for cp in local:
            cp.start()
        sent = []
        for i in range(n):
            cp = _remote(ins[i], outs[i].at[c], send_sems, recv_sems, i, sib)
            cp.start()
            sent.append(cp)
        for i in range(n):
            slab = outs[i].at[1 - c]
            _remote(slab, slab, send_sems, recv_sems, i, sib).wait_recv()
        for cp in sent:
            cp.wait_send()
        for cp in local:
            cp.wait()

    return _call(
        body, name="sibling_share", in_specs=[ANY] * n, out_specs=[ANY] * n,
        out_shape=[jax.ShapeDtypeStruct((2,) + s.shape, s.dtype) for s in halves],
        scratch_shapes=[pltpu.SemaphoreType.DMA((n,)), pltpu.SemaphoreType.DMA((n,)), pltpu.SemaphoreType.DMA((n,))],
    )(*halves)


def _small_allreduce(pack):
    rows, cols = pack.shape

    def body(p_ref, o_ref, buf, send_sems, recv_sems):
        x, y, c, _ = _place()
        me = 4 * x + 2 * y + c
        buf[me] = p_ref[...]
        sent = []
        for k in range(1, N_DEV):
            fx, fy, fc = (k >> 2) & 1, (k >> 1) & 1, k & 1
            to = (x ^ fx, y ^ fy, c ^ fc)
            cp = _remote(p_ref, buf.at[me], send_sems, recv_sems, k - 1, to)
            cp.start()
            sent.append(cp)
        for k in range(1, N_DEV):
            fx, fy, fc = (k >> 2) & 1, (k >> 1) & 1, k & 1
            frm = 4 * (x ^ fx) + 2 * (y ^ fy) + (c ^ fc)
            _remote(p_ref, buf.at[frm], send_sems, recv_sems, k - 1, (x, y, c)).wait_recv()
        for cp in sent:
            cp.wait_send()
        acc = buf[0]
        for d in range(1, N_DEV):
            acc = acc + buf[d]
        o_ref[...] = acc

    return _call(
        body, name="small_allreduce",
        in_specs=[pl.BlockSpec(memory_space=pltpu.VMEM)], out_specs=pl.BlockSpec(memory_space=pltpu.VMEM),
        out_shape=jax.ShapeDtypeStruct(pack.shape, F32),
        scratch_shapes=[pltpu.VMEM((N_DEV, rows, cols), F32), pltpu.SemaphoreType.DMA((N_DEV - 1,)),
                        pltpu.SemaphoreType.DMA((N_DEV - 1,))],
    )(pack)


def _adamw_math(w, g, m, v):
    m_new = ADAM_B1 * m + (1.0 - ADAM_B1) * g
    v_new = ADAM_B2 * v + (1.0 - ADAM_B2) * (g * g)
    m_hat = m_new * (1.0 / (1.0 - ADAM_B1 ** ADAM_STEP))
    v_hat = v_new * (1.0 / (1.0 - ADAM_B2 ** ADAM_STEP))
    delta = -ADAM_LR * (m_hat / (jnp.sqrt(v_hat) + ADAM_EPS) + ADAM_WD * w)
    return delta, m_new, v_new


def _adamw(w, g, m, v, tr):
    rows, cols = w.shape
    tr = _div_tile(rows, tr)

    def body(w_ref, g_ref, m_ref, v_ref, d_ref, mo_ref, vo_ref):
        d, mn, vn = _adamw_math(w_ref[...], g_ref[...], m_ref[...], v_ref[...])
        d_ref[...] = d
        mo_ref[...] = mn
        vo_ref[...] = vn

    spec = pl.BlockSpec((tr, cols), lambda r: (r, 0))
    shp = jax.ShapeDtypeStruct(w.shape, F32)
    return _call(body, name="adamw", grid=(rows // tr,), in_specs=[spec] * 4, out_specs=[spec] * 3,
                 out_shape=(shp, shp, shp), compiler_params=_cp(1))(w, g, m, v)


def _adamw_small(ws, gs, ms, vs):
    n = len(ws)

    def body(*refs):
        w_r, g_r, m_r, v_r = refs[0:n], refs[n:2 * n], refs[2 * n:3 * n], refs[3 * n:4 * n]
        d_o, m_o, v_o = refs[4 * n:5 * n], refs[5 * n:6 * n], refs[6 * n:7 * n]
        for i in range(n):
            d, mn, vn = _adamw_math(w_r[i][...], g_r[i][...], m_r[i][...], v_r[i][...])
            d_o[i][...] = d
            m_o[i][...] = mn
            v_o[i][...] = vn

    shp = [jax.ShapeDtypeStruct(w.shape, F32) for w in ws]
    outs = _call(body, name="adamw_small", out_shape=shp * 3)(*ws, *gs, *ms, *vs)
    return outs[0:n], outs[n:2 * n], outs[2 * n:3 * n]


def _rope_tables(t_len):
    pos = jnp.arange(t_len, dtype=F32)
    inv_freq = ROPE_THETA ** (-jnp.arange(0, HEAD_DIM, 2, dtype=F32) / HEAD_DIM)
    ang = pos[:, None] * inv_freq[None, :]
    cos, sin = jnp.cos(ang), jnp.sin(ang)
    return jnp.tile(jnp.concatenate([cos, cos], axis=1), (1, 2)), jnp.tile(jnp.concatenate([-sin, sin], axis=1), (1, 2))


def _dup_heads(a):
    h0, h1 = a[..., 0:64], a[..., 64:128]
    return jnp.concatenate([h0, h0, h1, h1], axis=-1)


def _undup_heads(a):
    return jnp.concatenate([a[..., 0:64] + a[..., 64:128], a[..., 128:192] + a[..., 192:256]], axis=-1)


def _extend_in_cols(a):
    return jnp.concatenate([a[..., 0:1536], _dup_heads(a[..., 1536:1664]), _dup_heads(a[..., 1664:1792])], axis=-1)


def _reduce_in_cols(a):
    return jnp.concatenate([a[..., 0:1536], _undup_heads(a[..., 1536:1792]), _undup_heads(a[..., 1792:2048])], axis=-1)


def _local_step(x, mem, target, small, wg, tm_a=512, tm_b=256, tc=512, tt=1024):
    t_len = x.shape[0]
    cos_t, sin_t = _rope_tables(t_len)
    w_ext = _extend_in_cols(wg["w_in"])
    b_ext = _extend_in_cols(small["b_in"])
    w_dw = jnp.concatenate([wg["w_dw"], jnp.zeros((1, CONV_CH), F32)], axis=0)
    sink_b = jnp.broadcast_to(small["attn_sink"].reshape(8, 1), (8, 128))

    memn, kv = _mem_kv_fwd(mem, small["g_mem_kv"], wg["w_mem_kv"])
    ug, q, kd, vd, h1 = _in_proj_fwd(x, small["g_mix"], w_ext, b_ext, cos_t, sin_t, tm_a)
    yc, pre = _conv_fwd(ug, w_dw, small["b_dw"], small["g_conv_ln"], small["b_conv_ln"], tc)
    ya, lse = _attn_fwd(q, kd, vd, sink_b)
    ymix, x1, hq, qm, om, x2 = _mix_mem_fwd(x, yc, ya, wg["w_out"], small["b_out"], small["g_mem_q"], wg["w_mem_q"], kv,
                                            wg["w_mem_o"], tm_a)
    hf, gate, up, act, dx3, loss, d_g_final = _ffn_loss(x2, small["g_ffn"], wg["w_gate"], wg["w_up"], wg["w_down"],
                                                        small["g_final"], target, tm_b)

    dx2, dgate, dup, d_g_ffn = _ffn_bwd(dx3, x2, gate, up, small["g_ffn"], wg["w_gate"], wg["w_up"], wg["w_down"], tm_b)
    dx1, dqm, dyc, dya, dkv, d_g_mem_q, d_b_out = _mix_mem_bwd(dx2, x1, qm, kv, small["g_mem_q"], wg["w_mem_q"], wg["w_mem_o"],
                                                               wg["w_out"], tm_b)
    d_w_mem_kv, d_g_mem_kv = _mem_kv_bwd(dkv, memn, mem, small["g_mem_kv"], wg["w_mem_kv"])
    dq, dd, dsink = _attn_bwd_q(q, kd, vd, dya, lse, sink_b, cos_t, sin_t)
    dk, dv = _attn_bwd_kv(q, kd, vd, dya, lse, dd, cos_t, sin_t)
    dpre, cstats = _conv_norm_bwd(pre, dyc, small["g_conv_ln"], small["b_conv_ln"], tc)
    du_glu, d_w_dw = _conv_bwd(dpre, ug, w_dw, tc)
    grad_x, du, d_b_ext, d_g_mix = _in_proj_bwd(du_glu, dq, dk, dv, dx1, x, small["g_mix"], w_ext, tm_a)

    d_w_ext = _weight_grad(h1[None], du[None], "dw_in", tt)[0]
    grads = {
        "w_in": _reduce_in_cols(d_w_ext),
        "w_out": _weight_grad(ymix[None], dx1[None], "dw_out", tt)[0],
        "w_mem_q": _weight_grad(hq[None], dqm[None], "dw_mem_q", tt)[0],
        "w_mem_o": _weight_grad(om[None], dx2[None], "dw_mem_o", tt)[0],
        "w_mem_kv": d_w_mem_kv,
        "w_gate": _weight_grad(hf[None], dgate, "dw_gate", tt),
        "w_up": _weight_grad(hf[None], dup, "dw_up", tt),
        "w_down": _weight_grad(act, dx3[None], "dw_down", tt),
        "w_dw": d_w_dw[0:CONV_W],
        "g_mix": d_g_mix, "b_in": _reduce_in_cols(d_b_ext), "b_dw": cstats[2:3], "g_conv_ln": cstats[0:1],
        "b_conv_ln": cstats[1:2], "attn_sink": jnp.sum(dsink[:, :, 0], axis=0)[None, :], "b_out": d_b_out,
        "g_mem_q": d_g_mem_q, "g_mem_kv": d_g_mem_kv, "g_ffn": d_g_ffn, "g_final": d_g_final,
    }
    return loss[0:1, 0:1], grad_x, grads


BIG = ["w_in", "w_out", "w_mem_q", "w_mem_kv", "w_mem_o", "w_gate", "w_up", "w_down"]
KEEP_SLABS = ("w_mem_kv", "w_gate", "w_up", "w_down")
SMALL = ["g_mix", "b_in", "b_dw", "g_conv_ln", "b_conv_ln", "attn_sink", "b_out", "g_mem_q", "g_mem_kv", "g_ffn", "g_final"]
PACK_ROWS = 32


def _pack_small(loss, grads):
    def row(a):
        a = a.reshape(1, -1)
        return jnp.pad(a, ((0, 0), (0, 1024 - a.shape[1])))

    rows = [row(grads[k]) for k in ("g_mix", "b_out", "g_mem_q", "g_mem_kv", "g_ffn", "g_final")]
    rows += [grads["b_in"][:, 0:1024], row(grads["b_in"][:, 1024:1792])]
    rows += [jnp.concatenate([grads["b_dw"], grads["g_conv_ln"]], axis=1), row(grads["b_conv_ln"]), row(grads["attn_sink"]),
             row(loss)]
    dw = jnp.pad(grads["w_dw"], ((0, 1), (0, 0))).reshape(16, 1024)
    pack = jnp.concatenate(rows + [dw], axis=0)
    return jnp.pad(pack, ((0, PACK_ROWS - pack.shape[0]), (0, 0)))


def _unpack_small(pack):
    out = {k: pack[i:i + 1] for i, k in enumerate(("g_mix", "b_out", "g_mem_q", "g_mem_kv", "g_ffn", "g_final"))}
    out["b_in"] = jnp.concatenate([pack[6:7], pack[7:8, 0:768]], axis=1)
    out["b_dw"], out["g_conv_ln"] = pack[8:9, 0:512], pack[8:9, 512:1024]
    out["b_conv_ln"] = pack[9:10, 0:512]
    out["attn_sink"] = pack[10:11, 0:8]
    loss = pack[11, 0]
    dw = pack[12:28].reshape(32, 512)[0:CONV_W]
    return loss, out, dw


def kernel(x, mem, g_mix, w_in, b_in, w_dw, b_dw, g_conv_ln, b_conv_ln, attn_sink, w_out, b_out, g_mem_q, g_mem_kv, w_mem_q, w_mem_kv, w_mem_o, g_ffn, w_gate, w_up, w_down, g_final, loss_target, m_g_mix, m_w_in, m_b_in, m_w_dw, m_b_dw, m_g_conv_ln, m_b_conv_ln, m_attn_sink, m_w_out, m_b_out, m_g_mem_q, m_g_mem_kv, m_w_mem_q, m_w_mem_kv, m_w_mem_o, m_g_ffn, m_w_gate, m_w_up, m_w_down, m_g_final, v_g_mix, v_w_in, v_b_in, v_w_dw, v_b_dw, v_g_conv_ln, v_b_conv_ln, v_attn_sink, v_w_out, v_b_out, v_g_mem_q, v_g_mem_kv, v_w_mem_q, v_w_mem_kv, v_w_mem_o, v_g_ffn, v_w_gate, v_w_up, v_w_down, v_g_final):
    args = dict(locals())
    weight_names = ["g_mix", "w_in", "b_in", "w_dw", "b_dw", "g_conv_ln", "b_conv_ln", "attn_sink", "w_out", "b_out", "g_mem_q",
                    "g_mem_kv", "w_mem_q", "w_mem_kv", "w_mem_o", "g_ffn", "w_gate", "w_up", "w_down", "g_final"]
    chip = 2 * lax.axis_index("x") + lax.axis_index("y")
    core = lax.axis_index("c")

    shards = []
    for k in BIG:
        s = args[k][0].astype(BF16)
        shards.append(s.reshape(2, s.shape[0] // 2, s.shape[1]))
    gathered = _gather_weights(shards, w_dw[0])
    wg = {}
    for k, g in zip(BIG, gathered[:-1]):
        g = g.reshape(N_CHIPS, g.shape[2] * 2, g.shape[3])
        if k == "w_in":
            g = jnp.transpose(g, (1, 0, 2)).reshape(g.shape[1], -1)
        elif k not in KEEP_SLABS:
            g = g.reshape(-1, g.shape[2])
        wg[k] = g
    wg["w_dw"] = jnp.transpose(gathered[-1], (1, 0, 2)).reshape(CONV_W, CONV_CH)

    small = {k: args[k].reshape(1, -1) for k in SMALL}

    loss, grad_x, grads = _local_step(x[0], mem[0], loss_target[0], small, wg)

    parts = []
    for k in BIG:
        g = grads[k]
        if k == "w_in":
            g = jnp.transpose(g.reshape(g.shape[0], N_CHIPS, -1), (1, 0, 2))
        elif g.ndim == 2:
            g = g.reshape(N_CHIPS, g.shape[0] // N_CHIPS, g.shape[1])
        parts.append(g.reshape(N_CHIPS, 2, g.shape[1] // 2, g.shape[2]))
    from_sibling = _sibling_swap(parts)
    c_idx = core.astype(jnp.int32).reshape(1)
    chip_sums = [_pair_sum(p, o, c_idx, 256) for p, o in zip(parts, from_sibling)]
    from_chips = _chip_exchange(chip_sums)
    halves = [_chip_sum(p, 256) for p in from_chips]
    full = _sibling_share(halves)
    big_grads = {k: f.reshape(-1, f.shape[2]) for k, f in zip(BIG, full)}

    loss_sum, small_grads, dw_full = _unpack_small(_small_allreduce(_pack_small(loss, grads)))
    dw_cols = jnp.transpose(dw_full.reshape(CONV_W, N_CHIPS, 128), (1, 0, 2))
    small_grads["w_dw"] = lax.dynamic_index_in_dim(dw_cols, chip, axis=0, keepdims=False)

    out_g, out_d, out_m, out_v = {}, {}, {}, {}
    for k in BIG:
        g = big_grads[k]
        d, mn, vn = _adamw(args[k][0], g, args["m_" + k][0], args["v_" + k][0], 256)
        out_g[k], out_d[k], out_m[k], out_v[k] = g[None], d[None], mn[None], vn[None]
    names = SMALL + ["w_dw"]

    def flat(a):
        return a[0] if a.ndim == 3 else a.reshape(1, -1)

    def pad_lanes(a):
        return jnp.pad(a, ((0, 0), (0, 128 - a.shape[1]))) if a.shape[1] < 128 else a

    ws = [flat(args[k]) for k in names]
    gs = [small_grads[k] for k in names]
    ms = [flat(args["m_" + k]) for k in names]
    vs = [flat(args["v_" + k]) for k in names]
    ds, mns, vns = _adamw_small([pad_lanes(a) for a in ws], [pad_lanes(a) for a in gs], [pad_lanes(a) for a in ms],
                                [pad_lanes(a) for a in vs])
    for i, k in enumerate(names):
        n_lanes = ws[i].shape[1]
        for out, val in ((out_g, gs[i]), (out_d, ds[i]), (out_m, mns[i]), (out_v, vns[i])):
            out[k] = val[:, 0:n_lanes].reshape(args[k].shape)

    return (loss_sum, grad_x[None], *[out_g[k] for k in weight_names], *[out_d[k] for k in weight_names],
            *[out_m[k] for k in weight_names], *[out_v[k] for k in weight_names])
```

```python
import jax
import jax.numpy as jnp
from jax import lax
from jax.experimental import pallas as pl
from jax.experimental.pallas import tpu as pltpu

F32 = jnp.float32
BF16 = jnp.bfloat16
EPS = 1e-6
NEG = -1e30

D_MODEL = 1024
CONV_CH = 512
CONV_W = 31
HEAD_DIM = 64
BLK = 128
MEM_HEADS = 4
MEM_HD = 256
N_CHIPS = 4
N_DEV = 8
ATT_SCALE = HEAD_DIM ** -0.5
MEM_SCALE = MEM_HD ** -0.5
ROPE_THETA = 10000.0

ADAM_LR = 0.001
ADAM_B1 = 0.9
ADAM_B2 = 0.999
ADAM_EPS = 1e-08
ADAM_WD = 0.01
ADAM_STEP = 10

VMEM_LIMIT_BYTES = 56 * 1024 * 1024
MESH = pl.DeviceIdType.MESH


def _call(body, **kw):
    return pl.pallas_call(body, **kw)


def _cp(n_grid):
    return pltpu.CompilerParams(dimension_semantics=("arbitrary",) * n_grid, vmem_limit_bytes=VMEM_LIMIT_BYTES)


def _res(shape):
    nd = len(shape)
    return pl.BlockSpec(shape, lambda *_: (0,) * nd, pipeline_mode=pl.Buffered(1))


def _rows(tm, n):
    return pl.BlockSpec((tm, n), lambda i: (i, 0))


def _div_tile(n, target):
    best = None
    for d in range(16, min(n, target) + 1, 16):
        if n % d == 0:
            best = d
    assert best is not None, (n, target)
    return best


def _dot(a, b):
    return jnp.dot(a, b, preferred_element_type=F32)


def _dot_nt(a, b):
    return lax.dot_general(a, b, (((1,), (1,)), ((), ())), preferred_element_type=F32)


def _dot_tn(a, b):
    return lax.dot_general(a, b, (((0,), (0,)), ((), ())), preferred_element_type=F32)


def _bf(x):
    return x.astype(BF16)


def _sigmoid(x):
    return 1.0 / (1.0 + jnp.exp(-x))


def _rms_fwd(x, g):
    r = lax.rsqrt(jnp.mean(x * x, axis=-1, keepdims=True) + EPS)
    xh = x * r
    return xh * g, xh, r


def _rms_bwd(dh, xh, r, g):
    dxh = dh * g
    return r * (dxh - xh * jnp.mean(dxh * xh, axis=-1, keepdims=True))


def _colsum(x):
    return jnp.sum(x, axis=0, keepdims=True)


def _rope(x, cos, sin, sign):
    n = x.shape[1] // 128
    c = jnp.tile(cos, (1, n)) if n > 1 else cos
    s = jnp.tile(sin, (1, n)) if n > 1 else sin
    lane = lax.broadcasted_iota(jnp.int32, x.shape, 1)
    first = (lane & 63) < 32
    partner = jnp.where(first, pltpu.roll(x, x.shape[1] - 32, 1), pltpu.roll(x, 32, 1))
    return x * c + sign * (partner * s)


def _lo_lanes(shape):
    return lax.broadcasted_iota(jnp.int32, shape, 1) < 64


def _stack_heads(t):
    t0, t1 = t[:, 0:128], t[:, 128:256]
    lo = _lo_lanes(t0.shape)
    z = jnp.zeros_like(t0)
    return jnp.concatenate([jnp.where(lo, t0, z), jnp.where(lo, z, t0), jnp.where(lo, t1, z), jnp.where(lo, z, t1)], axis=0)


def _unstack_heads(o):
    lo = _lo_lanes((BLK, 128))
    return jnp.concatenate([jnp.where(lo, o[0:128], o[128:256]), jnp.where(lo, o[256:384], o[384:512])], axis=1)


def _fold_heads(parts):
    a, b = (p + pltpu.roll(p, 64, 1) for p in parts)
    return jnp.where(_lo_lanes(a.shape), a, b)


def _sink_col(sk_ref, g):
    return jnp.concatenate([jnp.broadcast_to(sk_ref[4 * g + h:4 * g + h + 1, :], (BLK, 128)) for h in range(4)], axis=0)


def _tile3(x):
    return jnp.concatenate([x, x, x], axis=1)


def _mem_kv_fwd(mem, g_kv, w_kv):
    m_len = mem.shape[0]
    cols = w_kv.shape[2]

    def body(mem_ref, g_ref, w_ref, memn_ref, kv_ref):
        h, _, _ = _rms_fwd(mem_ref[...], g_ref[...])
        hb = _bf(h)
        memn_ref[...] = hb
        for s in range(N_CHIPS):
            kv_ref[s] = _bf(_dot(hb, w_ref[s]))

    return _call(
        body, name="mem_kv_fwd",
        out_shape=(jax.ShapeDtypeStruct((m_len, D_MODEL), BF16), jax.ShapeDtypeStruct((N_CHIPS, m_len, cols), BF16)),
        compiler_params=pltpu.CompilerParams(vmem_limit_bytes=VMEM_LIMIT_BYTES),
    )(mem, g_kv, w_kv)


def _dup_head_rows(w_ref, lo):
    h0, h1 = w_ref[lo:lo + 64, :], w_ref[lo + 64:lo + 128, :]
    return jnp.concatenate([h0, h0, h1, h1], axis=0)


def _in_proj_fwd(x, g_mix, w_t, b_ext, cos_t, sin_t, tm):
    t_len = x.shape[0]

    def body(x_ref, g_ref, w_ref, b_ref, c_ref, s_ref, ug_ref, q_ref, k_ref, v_ref, h_ref):
        h, _, _ = _rms_fwd(x_ref[...], g_ref[...])
        hb = _bf(h)
        h_ref[...] = hb
        ug_ref[...] = _dot_nt(hb, w_ref[0:1024, :]) + b_ref[:, 0:1024]
        c, s = c_ref[...], s_ref[...]
        q_ref[...] = _bf(_rope(_dot_nt(hb, w_ref[1024:1536, :]) + b_ref[:, 1024:1536], c, s, 1.0))
        k_ref[...] = _bf(_rope(_dot_nt(hb, _dup_head_rows(w_ref, 1536)) + b_ref[:, 1536:1792], c, s, 1.0))
        v_ref[...] = _bf(_dot_nt(hb, _dup_head_rows(w_ref, 1664)) + b_ref[:, 1792:2048])

    return _call(
        body, name="in_proj_fwd", grid=(t_len // tm,),
        in_specs=[_rows(tm, D_MODEL), _res((1, D_MODEL)), _res(w_t.shape), _res(b_ext.shape), _rows(tm, 128), _rows(tm, 128)],
        out_specs=[_rows(tm, 1024), _rows(tm, 512), _rows(tm, 256), _rows(tm, 256), _rows(tm, D_MODEL)],
        out_shape=(jax.ShapeDtypeStruct((t_len, 1024), F32), jax.ShapeDtypeStruct((t_len, 512), BF16),
                   jax.ShapeDtypeStruct((t_len, 256), BF16), jax.ShapeDtypeStruct((t_len, 256), BF16),
                   jax.ShapeDtypeStruct((t_len, D_MODEL), BF16)),
        compiler_params=_cp(1),
    )(x, g_mix, w_t, b_ext, cos_t, sin_t)


def _halo_specs(tc, n, t_len):
    per = tc // 16
    last = t_len // 16 - 1
    return [pl.BlockSpec((16, n), lambda i: (jnp.maximum(i * per - 1, 0), 0)),
            pl.BlockSpec((tc, n), lambda i: (i, 0)),
            pl.BlockSpec((16, n), lambda i: (jnp.minimum((i + 1) * per, last), 0))]


def _glu(z):
    return z[:, 0:CONV_CH] * _sigmoid(z[:, CONV_CH:2 * CONV_CH])


def _fill_halo_buf(buf, prev, main, nxt, i, n_tiles, tc):
    buf[0:16, :] = jnp.where(i > 0, prev, jnp.zeros_like(prev))
    buf[16:16 + tc, :] = main
    buf[16 + tc:32 + tc, :] = jnp.where(i < n_tiles - 1, nxt, jnp.zeros_like(nxt))


def _conv_fwd(ug, w_dw, b_dw, g_ln, b_ln, tc):
    t_len = ug.shape[0]
    n_tiles = t_len // tc

    def body(up_ref, um_ref, un_ref, w_ref, bdw_ref, g_ref, b_ref, y_ref, pre_ref, buf):
        i = pl.program_id(0)
        _fill_halo_buf(buf, _glu(up_ref[...]), _glu(um_ref[...]), _glu(un_ref[...]), i, n_tiles, tc)
        acc = jnp.zeros((tc, CONV_CH), F32)
        for k in range(CONV_W):
            acc = acc + w_ref[k:k + 1, :] * buf[k + 1:k + 1 + tc, :]
        pre = acc + bdw_ref[...]
        pre_ref[...] = pre
        mu = jnp.mean(pre, axis=-1, keepdims=True)
        d = pre - mu
        rstd = lax.rsqrt(jnp.mean(d * d, axis=-1, keepdims=True) + EPS)
        ln = d * rstd * g_ref[...] + b_ref[...]
        y_ref[...] = _bf(ln * _sigmoid(ln))

    return _call(
        body, name="conv_fwd", grid=(n_tiles,),
        in_specs=_halo_specs(tc, 1024, t_len) + [_res((32, CONV_CH)), _res((1, CONV_CH)), _res((1, CONV_CH)), _res((1, CONV_CH))],
        out_specs=[_rows(tc, CONV_CH), _rows(tc, CONV_CH)],
        out_shape=(jax.ShapeDtypeStruct((t_len, CONV_CH), BF16), jax.ShapeDtypeStruct((t_len, CONV_CH), F32)),
        scratch_shapes=[pltpu.VMEM((tc + 32, CONV_CH), F32)],
        compiler_params=_cp(1),
    )(ug, ug, ug, w_dw, b_dw, g_ln, b_ln)


def _nbr_specs(n, nb):
    return [pl.BlockSpec((BLK, n), lambda i: (jnp.maximum(i - 1, 0), 0)),
            pl.BlockSpec((BLK, n), lambda i: (i, 0)),
            pl.BlockSpec((BLK, n), lambda i: (jnp.minimum(i + 1, nb - 1), 0))]


def _nbr_specs4(nb):
    return [pl.BlockSpec((1, 2, 4 * BLK, 128), lambda i: (jnp.maximum(i - 1, 0), 0, 0, 0)),
            pl.BlockSpec((1, 2, 4 * BLK, 128), lambda i: (i, 0, 0, 0)),
            pl.BlockSpec((1, 2, 4 * BLK, 128), lambda i: (jnp.minimum(i + 1, nb - 1), 0, 0, 0))]


def _band_mask_q(i, t_len):
    a = lax.broadcasted_iota(jnp.int32, (4 * BLK, 3 * BLK), 0) & (BLK - 1)
    c = lax.broadcasted_iota(jnp.int32, (4 * BLK, 3 * BLK), 1)
    kpos = (i - 1) * BLK + c
    rel = c - BLK - a
    return (jnp.abs(rel) <= BLK) & (kpos >= 0) & (kpos < t_len)


def _attn_fwd(q, kd, vd, sink_b):
    t_len = q.shape[0]
    nb = t_len // BLK

    def body(q_ref, kp_ref, kc_ref, kn_ref, vp_ref, vc_ref, vn_ref, sk_ref, y_ref, lse_ref):
        i = pl.program_id(0)
        kcat = jnp.concatenate([kp_ref[...], kc_ref[...], kn_ref[...]], axis=0)
        vcat = jnp.concatenate([vp_ref[...], vc_ref[...], vn_ref[...]], axis=0)
        valid = _band_mask_q(i, t_len)
        ys = []
        for g in range(2):
            qs = _stack_heads(q_ref[:, 256 * g:256 * g + 256])
            s = _dot_nt(qs, kcat[:, 128 * g:128 * g + 128]) * ATT_SCALE
            s = jnp.where(valid, s, NEG)
            skc = _sink_col(sk_ref, g)
            m_b = jnp.maximum(jnp.max(s, axis=-1, keepdims=True), skc)
            p = jnp.exp(s - _tile3(m_b))
            den_b = jnp.sum(p, axis=-1, keepdims=True) + jnp.exp(skc - m_b)
            pn = p * _tile3(1.0 / den_b)
            o = _dot(_bf(pn), vcat[:, 128 * g:128 * g + 128])
            ys.append(_unstack_heads(o))
            lse_ref[0, g] = m_b + jnp.log(den_b)
        y_ref[...] = _bf(jnp.concatenate(ys, axis=1))

    return _call(
        body, name="attn_fwd", grid=(nb,),
        in_specs=[_rows(BLK, 512)] + _nbr_specs(256, nb) + _nbr_specs(256, nb) + [_res((8, 128))],
        out_specs=[_rows(BLK, 512), pl.BlockSpec((1, 2, 4 * BLK, 128), lambda i: (i, 0, 0, 0))],
        out_shape=(jax.ShapeDtypeStruct((t_len, 512), BF16), jax.ShapeDtypeStruct((nb, 2, 4 * BLK, 128), F32)),
        compiler_params=_cp(1),
    )(q, kd, kd, kd, vd, vd, vd, sink_b)


def _mem_heads(kv_ref, h):
    lo = MEM_HD * (h % 2)
    return kv_ref[h // 2, :, lo:lo + MEM_HD], kv_ref[2 + h // 2, :, lo:lo + MEM_HD]


def _mix_mem_fwd(x, yc, ya, w_out, b_out, g_q, w_q, kv, w_o, tm):
    t_len = x.shape[0]

    def body(x_ref, yc_ref, ya_ref, wout_ref, bout_ref, g_ref, wq_ref, kv_ref, wo_ref,
             ymix_ref, x1_ref, hq_ref, qm_ref, om_ref, x2_ref):
        ymix = jnp.concatenate([yc_ref[...], ya_ref[...]], axis=1)
        ymix_ref[...] = ymix
        x1 = x_ref[...] + _dot(ymix, wout_ref[...]) + bout_ref[...]
        x1_ref[...] = x1
        hq, _, _ = _rms_fwd(x1, g_ref[...])
        hqb = _bf(hq)
        hq_ref[...] = hqb
        qm = _bf(_dot(hqb, wq_ref[...]))
        qm_ref[...] = qm
        outs = []
        for h in range(MEM_HEADS):
            kh, vh = _mem_heads(kv_ref, h)
            s = _dot_nt(qm[:, MEM_HD * h:MEM_HD * (h + 1)], kh) * MEM_SCALE
            p = jnp.exp(s - jnp.max(s, axis=-1, keepdims=True))
            p = p * (1.0 / jnp.sum(p, axis=-1, keepdims=True))
            outs.append(_dot(_bf(p), vh))
        om = _bf(jnp.concatenate(outs, axis=1))
        om_ref[...] = om
        x2_ref[...] = x1 + _dot(om, wo_ref[...])

    act_b = jax.ShapeDtypeStruct((t_len, D_MODEL), BF16)
    act_f = jax.ShapeDtypeStruct((t_len, D_MODEL), F32)
    return _call(
        body, name="mix_mem_fwd", grid=(t_len // tm,),
        in_specs=[_rows(tm, D_MODEL), _rows(tm, 512), _rows(tm, 512), _res(w_out.shape), _res((1, D_MODEL)), _res((1, D_MODEL)),
                  _res(w_q.shape), _res(kv.shape), _res(w_o.shape)],
        out_specs=[_rows(tm, D_MODEL)] * 6,
        out_shape=(act_b, act_f, act_b, act_b, act_b, act_f),
        compiler_params=_cp(1),
    )(x, yc, ya, w_out, b_out, g_q, w_q, kv, w_o)


def _ffn_loss(x2, g_ffn, w_gate, w_up, w_down, g_final, target, tm):
    t_len = x2.shape[0]
    ff = w_gate.shape[1]
    n_tiles = t_len // tm

    def body(x2_ref, g_ref, wg_ref, wu_ref, wd_ref, gf_ref, tgt_ref,
             hf_ref, gate_ref, up_ref, act_ref, dx3_ref, loss_ref, dgf_ref):
        i = pl.program_id(0)
        x2v = x2_ref[...]
        hf, _, _ = _rms_fwd(x2v, g_ref[...])
        hfb = _bf(hf)
        hf_ref[...] = hfb
        acc = jnp.zeros((tm, D_MODEL), F32)
        for s in range(N_CHIPS):
            gate = _dot_nt(hfb, wg_ref[s])
            up = _dot_nt(hfb, wu_ref[s])
            act = _bf(gate * _sigmoid(gate) * up)
            gate_ref[s] = _bf(gate)
            up_ref[s] = _bf(up)
            act_ref[s] = act
            acc = acc + _dot(act, wd_ref[s])
        x3 = x2v + acc
        gf = gf_ref[...]
        y, xh, r = _rms_fwd(x3, gf)
        err = y - tgt_ref[...]
        part = 0.5 * jnp.sum(jnp.mean(err * err, axis=-1, keepdims=True), axis=0, keepdims=True)
        dy = err * (1.0 / D_MODEL)
        dx3_ref[...] = _rms_bwd(dy, xh, r, gf)

        @pl.when(i == 0)
        def _():
            loss_ref[...] = jnp.zeros_like(loss_ref)
            dgf_ref[...] = jnp.zeros_like(dgf_ref)

        loss_ref[...] += jnp.broadcast_to(part, loss_ref.shape)
        dgf_ref[...] += _colsum(dy * xh)

    hid = jax.ShapeDtypeStruct((N_CHIPS, t_len, ff), BF16)
    hid_spec = pl.BlockSpec((N_CHIPS, tm, ff), lambda i: (0, i, 0))
    return _call(
        body, name="ffn_loss", grid=(n_tiles,),
        in_specs=[_rows(tm, D_MODEL), _res((1, D_MODEL)), _res(w_gate.shape), _res(w_up.shape), _res(w_down.shape),
                  _res((1, D_MODEL)), _rows(tm, D_MODEL)],
        out_specs=[_rows(tm, D_MODEL), hid_spec, hid_spec, hid_spec, _rows(tm, D_MODEL),
                   pl.BlockSpec((1, D_MODEL), lambda i: (0, 0)), pl.BlockSpec((1, D_MODEL), lambda i: (0, 0))],
        out_shape=(jax.ShapeDtypeStruct((t_len, D_MODEL), BF16), hid, hid, hid, jax.ShapeDtypeStruct((t_len, D_MODEL), F32),
                   jax.ShapeDtypeStruct((1, D_MODEL), F32), jax.ShapeDtypeStruct((1, D_MODEL), F32)),
        compiler_params=_cp(1),
    )(x2, g_ffn, w_gate, w_up, w_down, g_final, target)


def _ffn_bwd(dx3, x2, gate, up, g_ffn, w_gate, w_up, w_down, tm):
    t_len = x2.shape[0]
    ff = w_gate.shape[1]

    def body(dx3_ref, x2_ref, gate_ref, up_ref, g_ref, wg_ref, wu_ref, wd_ref, dx2_ref, dgate_ref, dup_ref, dg_ref):
        i = pl.program_id(0)
        dx3 = dx3_ref[...]
        d3b = _bf(dx3)
        dh = jnp.zeros((tm, D_MODEL), F32)
        for s in range(N_CHIPS):
            dact = _dot_nt(d3b, wd_ref[s])
            gt = gate_ref[s].astype(F32)
            u = up_ref[s].astype(F32)
            sg = _sigmoid(gt)
            dup = _bf(dact * (gt * sg))
            dgate = _bf(dact * u * (sg * (1.0 + gt * (1.0 - sg))))
            dup_ref[s] = dup
            dgate_ref[s] = dgate
            dh = dh + _dot(dgate, wg_ref[s]) + _dot(dup, wu_ref[s])
        g = g_ref[...]
        _, xh, r = _rms_fwd(x2_ref[...], g)
        dx2_ref[...] = dx3 + _rms_bwd(dh, xh, r, g)

        @pl.when(i == 0)
        def _():
            dg_ref[...] = jnp.zeros_like(dg_ref)

        dg_ref[...] += _colsum(dh * xh)

    hid = jax.ShapeDtypeStruct((N_CHIPS, t_len, ff), BF16)
    hid_spec = pl.BlockSpec((N_CHIPS, tm, ff), lambda i: (0, i, 0))
    return _call(
        body, name="ffn_bwd", grid=(t_len // tm,),
        in_specs=[_rows(tm, D_MODEL), _rows(tm, D_MODEL), hid_spec, hid_spec, _res((1, D_MODEL)),
                  _res(w_gate.shape), _res(w_up.shape), _res(w_down.shape)],
        out_specs=[_rows(tm, D_MODEL), hid_spec, hid_spec, pl.BlockSpec((1, D_MODEL), lambda i: (0, 0))],
        out_shape=(jax.ShapeDtypeStruct((t_len, D_MODEL), F32), hid, hid, jax.ShapeDtypeStruct((1, D_MODEL), F32)),
        compiler_params=_cp(1),
    )(dx3, x2, gate, up, g_ffn, w_gate, w_up, w_down)


def _mix_mem_bwd(dx2, x1, qm, kv, g_q, w_q, w_o, w_out, tm):
    t_len = x1.shape[0]
    m_len = kv.shape[1]

    def body(dx2_ref, x1_ref, qm_ref, kv_ref, g_ref, wq_ref, wo_ref, wout_ref,
             dx1_ref, dqm_ref, dyc_ref, dya_ref, dkv_ref, dgq_ref, dbout_ref):
        i = pl.program_id(0)

        @pl.when(i == 0)
        def _():
            dkv_ref[...] = jnp.zeros_like(dkv_ref)
            dgq_ref[...] = jnp.zeros_like(dgq_ref)
            dbout_ref[...] = jnp.zeros_like(dbout_ref)

        dx2 = dx2_ref[...]
        dom = _dot_nt(_bf(dx2), wo_ref[...])
        dqs = []
        for h in range(MEM_HEADS):
            kh, vh = _mem_heads(kv_ref, h)
            qh = qm_ref[:, MEM_HD * h:MEM_HD * (h + 1)]
            s = _dot_nt(qh, kh) * MEM_SCALE
            p = jnp.exp(s - jnp.max(s, axis=-1, keepdims=True))
            p = p * (1.0 / jnp.sum(p, axis=-1, keepdims=True))
            domh = _bf(dom[:, MEM_HD * h:MEM_HD * (h + 1)])
            dp = _dot_nt(domh, vh)
            ds = _bf(p * (dp - jnp.sum(p * dp, axis=-1, keepdims=True)) * MEM_SCALE)
            dqs.append(_dot(ds, kh))
            lo = MEM_HD * (h % 2)
            dkv_ref[h // 2, :, lo:lo + MEM_HD] += _dot_tn(ds, qh)
            dkv_ref[2 + h // 2, :, lo:lo + MEM_HD] += _dot_tn(_bf(p), domh)
        dqm = _bf(jnp.concatenate(dqs, axis=1))
        dqm_ref[...] = dqm
        dhq = _dot_nt(dqm, wq_ref[...])
        g = g_ref[...]
        _, xh, r = _rms_fwd(x1_ref[...], g)
        dx1 = dx2 + _rms_bwd(dhq, xh, r, g)
        dx1_ref[...] = dx1
        dgq_ref[...] += _colsum(dhq * xh)
        dbout_ref[...] += _colsum(dx1)
        dymix = _dot_nt(_bf(dx1), wout_ref[...])
        dyc_ref[...] = dymix[:, 0:CONV_CH]
        dya_ref[...] = _bf(dymix[:, CONV_CH:2 * CONV_CH])

    vec = pl.BlockSpec((1, D_MODEL), lambda i: (0, 0))
    return _call(
        body, name="mix_mem_bwd", grid=(t_len // tm,),
        in_specs=[_rows(tm, D_MODEL), _rows(tm, D_MODEL), _rows(tm, D_MODEL), _res(kv.shape), _res((1, D_MODEL)),
                  _res(w_q.shape), _res(w_o.shape), _res(w_out.shape)],
        out_specs=[_rows(tm, D_MODEL), _rows(tm, D_MODEL), _rows(tm, CONV_CH), _rows(tm, CONV_CH),
                   pl.BlockSpec(kv.shape, lambda i: (0, 0, 0)), vec, vec],
        out_shape=(jax.ShapeDtypeStruct((t_len, D_MODEL), F32), jax.ShapeDtypeStruct((t_len, D_MODEL), BF16),
                   jax.ShapeDtypeStruct((t_len, CONV_CH), F32), jax.ShapeDtypeStruct((t_len, CONV_CH), BF16),
                   jax.ShapeDtypeStruct((N_CHIPS, m_len, kv.shape[2]), F32),
                   jax.ShapeDtypeStruct((1, D_MODEL), F32), jax.ShapeDtypeStruct((1, D_MODEL), F32)),
        compiler_params=_cp(1),
    )(dx2, x1, qm, kv, g_q, w_q, w_o, w_out)


def _mem_kv_bwd(dkv, memn, mem, g_kv, w_kv):
    m_len = mem.shape[0]

    def body(dkv_ref, memn_ref, mem_ref, g_ref, w_ref, dw_ref, dg_ref):
        hb = memn_ref[...]
        dmn = jnp.zeros((m_len, D_MODEL), F32)
        for s in range(N_CHIPS):
            d = _bf(dkv_ref[s])
            dw_ref[s] = _bf(_dot_tn(hb, d))
            dmn = dmn + _dot_nt(d, w_ref[s])
        _, xh, _ = _rms_fwd(mem_ref[...], g_ref[...])
        dg_ref[...] = _colsum(dmn * xh)

    return _call(
        body, name="mem_kv_bwd",
        out_shape=(jax.ShapeDtypeStruct(w_kv.shape, BF16), jax.ShapeDtypeStruct((1, D_MODEL), F32)),
        compiler_params=pltpu.CompilerParams(vmem_limit_bytes=VMEM_LIMIT_BYTES),
    )(dkv, memn, mem, g_kv, w_kv)


def _attn_bwd_q(q, kd, vd, dya, lse, sink_b, cos_t, sin_t):
    t_len = q.shape[0]
    nb = t_len // BLK

    def body(q_ref, kp_ref, kc_ref, kn_ref, vp_ref, vc_ref, vn_ref, do_ref, lse_ref, sk_ref, c_ref, s_ref,
             dq_ref, dd_ref, dsk_ref):
        i = pl.program_id(0)
        kcat = jnp.concatenate([kp_ref[...], kc_ref[...], kn_ref[...]], axis=0)
        vcat = jnp.concatenate([vp_ref[...], vc_ref[...], vn_ref[...]], axis=0)
        valid = _band_mask_q(i, t_len)
        dqs, dsks = [], []
        for g in range(2):
            qs = _stack_heads(q_ref[:, 256 * g:256 * g + 256])
            dos = _stack_heads(do_ref[:, 256 * g:256 * g + 256])
            kk = kcat[:, 128 * g:128 * g + 128]
            s = jnp.where(valid, _dot_nt(qs, kk) * ATT_SCALE, NEG)
            lse_b = lse_ref[0, g]
            p = jnp.exp(s - _tile3(lse_b))
            dp = _dot_nt(dos, vcat[:, 128 * g:128 * g + 128])
            drow = jnp.sum(p * dp, axis=-1, keepdims=True)
            ds = _bf(p * (dp - drow) * ATT_SCALE)
            dqs.append(_unstack_heads(_dot(ds, kk)))
            d_b = jnp.broadcast_to(drow, (4 * BLK, 128))
            dd_ref[0, g] = d_b
            contrib = -(jnp.exp(_sink_col(sk_ref, g) - lse_b) * d_b)
            dsks.append(jnp.sum(contrib.reshape(4, BLK, 128), axis=1))
        dq = jnp.concatenate(dqs, axis=1)
        dq_ref[...] = _bf(_rope(dq, c_ref[...], s_ref[...], -1.0))
        dsk_ref[0] = jnp.concatenate(dsks, axis=0)

    stat = pl.BlockSpec((1, 2, 4 * BLK, 128), lambda i: (i, 0, 0, 0))
    return _call(
        body, name="attn_bwd_q", grid=(nb,),
        in_specs=[_rows(BLK, 512)] + _nbr_specs(256, nb) + _nbr_specs(256, nb) + [_rows(BLK, 512), stat, _res((8, 128)),
                                                                                 _rows(BLK, 128), _rows(BLK, 128)],
        out_specs=[_rows(BLK, 512), stat, pl.BlockSpec((1, 8, 128), lambda i: (i, 0, 0))],
        out_shape=(jax.ShapeDtypeStruct((t_len, 512), BF16), jax.ShapeDtypeStruct((nb, 2, 4 * BLK, 128), F32),
                   jax.ShapeDtypeStruct((nb, 8, 128), F32)),
        compiler_params=_cp(1),
    )(q, kd, kd, kd, vd, vd, vd, dya, lse, sink_b, cos_t, sin_t)


def _attn_bwd_kv(q, kd, vd, dya, lse, dd, cos_t, sin_t):
    t_len = q.shape[0]
    nb = t_len // BLK

    def body(kc_ref, vc_ref, qp_ref, qc_ref, qn_ref, dop_ref, doc_ref, don_ref, lp_ref, lc_ref, ln_ref,
             dp_ref, dc_ref, dn_ref, c_ref, s_ref, dk_ref, dv_ref):
        j = pl.program_id(0)
        row = lax.broadcasted_iota(jnp.int32, (12 * BLK, BLK), 0)
        col = lax.broadcasted_iota(jnp.int32, (12 * BLK, BLK), 1)
        qblk = j - 1 + row // (4 * BLK)
        rel = col - (row & (BLK - 1)) + (j - qblk) * BLK
        valid = (jnp.abs(rel) <= BLK) & (qblk >= 0) & (qblk < nb)
        dks, dvs = [], []
        for g in range(2):
            cols = slice(256 * g, 256 * g + 256)
            qs = jnp.concatenate([_stack_heads(r[:, cols]) for r in (qp_ref, qc_ref, qn_ref)], axis=0)
            dos = jnp.concatenate([_stack_heads(r[:, cols]) for r in (dop_ref, doc_ref, don_ref)], axis=0)
            lse_b = jnp.concatenate([r[0, g] for r in (lp_ref, lc_ref, ln_ref)], axis=0)
            d_b = jnp.concatenate([r[0, g] for r in (dp_ref, dc_ref, dn_ref)], axis=0)
            kk = kc_ref[:, 128 * g:128 * g + 128]
            s = jnp.where(valid, _dot_nt(qs, kk) * ATT_SCALE, NEG)
            p = jnp.exp(s - lse_b)
            dp = _dot_nt(dos, vc_ref[:, 128 * g:128 * g + 128])
            ds = _bf(p * (dp - d_b) * ATT_SCALE)
            dvs.append(_dot_tn(_bf(p), dos))
            dks.append(_dot_tn(ds, qs))
        dk_ref[...] = _bf(_rope(_fold_heads(dks), c_ref[...], s_ref[...], -1.0))
        dv_ref[...] = _bf(_fold_heads(dvs))

    return _call(
        body, name="attn_bwd_kv", grid=(nb,),
        in_specs=[_rows(BLK, 256), _rows(BLK, 256)] + _nbr_specs(512, nb) + _nbr_specs(512, nb) + _nbr_specs4(nb) + _nbr_specs4(nb)
        + [_rows(BLK, 128), _rows(BLK, 128)],
        out_specs=[_rows(BLK, 128), _rows(BLK, 128)],
        out_shape=(jax.ShapeDtypeStruct((t_len, 128), BF16), jax.ShapeDtypeStruct((t_len, 128), BF16)),
        compiler_params=_cp(1),
    )(kd, vd, q, q, q, dya, dya, dya, lse, lse, lse, dd, dd, dd, cos_t, sin_t)


def _conv_norm_bwd(pre, dyc, g_ln, b_ln, tc):
    t_len = pre.shape[0]

    def body(pre_ref, dy_ref, g_ref, b_ref, dpre_ref, stats_ref):
        i = pl.program_id(0)
        pre_v = pre_ref[...]
        mu = jnp.mean(pre_v, axis=-1, keepdims=True)
        d = pre_v - mu
        rstd = lax.rsqrt(jnp.mean(d * d, axis=-1, keepdims=True) + EPS)
        xh = d * rstd
        g = g_ref[...]
        ln = xh * g + b_ref[...]
        sg = _sigmoid(ln)
        dln = dy_ref[...] * (sg * (1.0 + ln * (1.0 - sg)))
        dxh = dln * g
        dpre = rstd * (dxh - jnp.mean(dxh, axis=-1, keepdims=True) - xh * jnp.mean(dxh * xh, axis=-1, keepdims=True))
        dpre_ref[...] = dpre

        @pl.when(i == 0)
        def _():
            stats_ref[...] = jnp.zeros_like(stats_ref)

        stats_ref[0:1, :] += _colsum(dln * xh)
        stats_ref[1:2, :] += _colsum(dln)
        stats_ref[2:3, :] += _colsum(dpre)

    return _call(
        body, name="conv_norm_bwd", grid=(t_len // tc,),
        in_specs=[_rows(tc, CONV_CH), _rows(tc, CONV_CH), _res((1, CONV_CH)), _res((1, CONV_CH))],
        out_specs=[_rows(tc, CONV_CH), pl.BlockSpec((8, CONV_CH), lambda i: (0, 0))],
        out_shape=(jax.ShapeDtypeStruct((t_len, CONV_CH), F32), jax.ShapeDtypeStruct((8, CONV_CH), F32)),
        compiler_params=_cp(1),
    )(pre, dyc, g_ln, b_ln)


def _conv_bwd(dpre, ug, w_dw, tc):
    t_len = ug.shape[0]
    n_tiles = t_len // tc

    def body(dp_ref, dm_ref, dn_ref, up_ref, um_ref, un_ref, w_ref, du_ref, dw_ref, dbuf, vbuf):
        i = pl.program_id(0)
        _fill_halo_buf(dbuf, dp_ref[...], dm_ref[...], dn_ref[...], i, n_tiles, tc)
        _fill_halo_buf(vbuf, _glu(up_ref[...]), _glu(um_ref[...]), _glu(un_ref[...]), i, n_tiles, tc)
        dmain = dm_ref[...]
        dv = jnp.zeros((tc, CONV_CH), F32)
        taps = []
        for k in range(CONV_W):
            dv = dv + w_ref[k:k + 1, :] * dbuf[31 - k:31 - k + tc, :]
            taps.append(_colsum(dmain * vbuf[k + 1:k + 1 + tc, :]))
        taps.append(jnp.zeros((1, CONV_CH), F32))
        um = um_ref[...]
        a, gt = um[:, 0:CONV_CH], um[:, CONV_CH:2 * CONV_CH]
        sg = _sigmoid(gt)
        du_ref[...] = _bf(jnp.concatenate([dv * sg, dv * a * (sg * (1.0 - sg))], axis=1))

        @pl.when(i == 0)
        def _():
            dw_ref[...] = jnp.zeros_like(dw_ref)

        dw_ref[...] += jnp.concatenate(taps, axis=0)

    return _call(
        body, name="conv_bwd", grid=(n_tiles,),
        in_specs=_halo_specs(tc, CONV_CH, t_len) + _halo_specs(tc, 1024, t_len) + [_res((32, CONV_CH))],
        out_specs=[_rows(tc, 1024), pl.BlockSpec((32, CONV_CH), lambda i: (0, 0))],
        out_shape=(jax.ShapeDtypeStruct((t_len, 1024), BF16), jax.ShapeDtypeStruct((32, CONV_CH), F32)),
        scratch_shapes=[pltpu.VMEM((tc + 32, CONV_CH), F32), pltpu.VMEM((tc + 32, CONV_CH), F32)],
        compiler_params=_cp(1),
    )(dpre, dpre, dpre, ug, ug, ug, w_dw)


def _in_proj_bwd(du_glu, dq, dk, dv, dx1, x, g_mix, w_t, tm):
    t_len = x.shape[0]
    n_ext = w_t.shape[0]

    def body(dg_ref, dq_ref, dk_ref, dv_ref, dx1_ref, x_ref, g_ref, w_ref, dx_ref, du_ref, db_ref, dgm_ref):
        i = pl.program_id(0)
        du = jnp.concatenate([dg_ref[...], dq_ref[...], dk_ref[...], dv_ref[...]], axis=1)
        du_ref[...] = du
        dh = _dot(du, w_ref[...])
        g = g_ref[...]
        _, xh, r = _rms_fwd(x_ref[...], g)
        dx_ref[...] = dx1_ref[...] + _rms_bwd(dh, xh, r, g)

        @pl.when(i == 0)
        def _():
            db_ref[...] = jnp.zeros_like(db_ref)
            dgm_ref[...] = jnp.zeros_like(dgm_ref)

        db_ref[...] += _colsum(du.astype(F32))
        dgm_ref[...] += _colsum(dh * xh)

    return _call(
        body, name="in_proj_bwd", grid=(t_len // tm,),
        in_specs=[_rows(tm, 1024), _rows(tm, 512), _rows(tm, 128), _rows(tm, 128), _rows(tm, D_MODEL), _rows(tm, D_MODEL),
                  _res((1, D_MODEL)), _res(w_t.shape)],
        out_specs=[_rows(tm, D_MODEL), _rows(tm, n_ext), pl.BlockSpec((1, n_ext), lambda i: (0, 0)),
                   pl.BlockSpec((1, D_MODEL), lambda i: (0, 0))],
        out_shape=(jax.ShapeDtypeStruct((t_len, D_MODEL), F32), jax.ShapeDtypeStruct((t_len, n_ext), BF16),
                   jax.ShapeDtypeStruct((1, n_ext), F32), jax.ShapeDtypeStruct((1, D_MODEL), F32)),
        compiler_params=_cp(1),
    )(du_glu, dq, dk, dv, dx1, x, g_mix, w_t)


def _weight_grad(a, d, name, tt):
    sa, t_len, k_dim = a.shape
    sd, _, n_dim = d.shape
    n_s = max(sa, sd)
    tn = n_dim if n_dim <= 1024 else 1024
    tt = min(tt, t_len)

    n_t = t_len // tt

    def body(a_ref, d_ref, o_ref, acc):
        t = pl.program_id(2)

        @pl.when(t == 0)
        def _():
            acc[...] = jnp.zeros_like(acc)

        acc[...] += _dot_tn(_bf(a_ref[0]), _bf(d_ref[0]))

        @pl.when(t == n_t - 1)
        def _():
            o_ref[0] = _bf(acc[...])

    return _call(
        body, name=name, grid=(n_s, n_dim // tn, n_t),
        in_specs=[pl.BlockSpec((1, tt, k_dim), (lambda s, n, t: (s, t, 0)) if sa > 1 else (lambda s, n, t: (0, t, 0))),
                  pl.BlockSpec((1, tt, tn), (lambda s, n, t: (s, t, n)) if sd > 1 else (lambda s, n, t: (0, t, n)))],
        out_specs=pl.BlockSpec((1, k_dim, tn), lambda s, n, t: (s, 0, n)),
        out_shape=jax.ShapeDtypeStruct((n_s, k_dim, n_dim), BF16),
        scratch_shapes=[pltpu.VMEM((k_dim, tn), F32)],
        compiler_params=_cp(3),
    )(a, d)


ANY = pl.BlockSpec(memory_space=pl.ANY)


def _place():
    x, y, c = lax.axis_index("x"), lax.axis_index("y"), lax.axis_index("c")
    chips = [(1 - x, y), (x, 1 - y), (1 - x, 1 - y)]
    return x, y, c, chips


def _remote(src, dst, send_sems, recv_sems, k, to):
    return pltpu.make_async_remote_copy(src_ref=src, dst_ref=dst, send_sem=send_sems.at[k], recv_sem=recv_sems.at[k],
                                        device_id=to, device_id_type=MESH)


def _cast_place(w, chip_idx, tr):
    rows, cols = w.shape
    h = rows // 2
    tr = _div_tile(h, tr)
    per = h // tr

    def body(s_ref, w_ref, o_ref):
        o_ref[0, 0] = _bf(w_ref[...])

    return _call(
        body, name="cast_place",
        grid_spec=pltpu.PrefetchScalarGridSpec(
            num_scalar_prefetch=1, grid=(2, per),
            in_specs=[pl.BlockSpec((tr, cols), lambda hh, r, s_ref: (hh * per + r, 0))],
            out_specs=pl.BlockSpec((1, 1, tr, cols), lambda hh, r, s_ref: (s_ref[0], hh, r, 0))),
        out_shape=jax.ShapeDtypeStruct((N_CHIPS, 2, h, cols), BF16),
        compiler_params=_cp(2),
    )(chip_idx, w)


def _gather_weights(bufs, dw_shard):
    n = len(bufs)

    def body(*refs):
        dw_in = refs[n]
        outs, dw_out = refs[n + 1:2 * n + 1], refs[2 * n + 1]
        send_sems, recv_sems, loc_sems = refs[2 * n + 2:]
        x, y, c, chips = _place()
        me = 2 * x + y
        sib = (x, y, 1 - c)
        local = [pltpu.make_async_copy(dw_in, dw_out.at[me], loc_sems.at[0])]
        for cp in local:
            cp.start()
        sent = []
        for i in range(n):
            for j, chip in enumerate(chips):
                cp = _remote(outs[i].at[me, c], outs[i].at[me, c], send_sems, recv_sems, 3 * i + j, (*chip, c))
                cp.start()
                sent.append(cp)
        for j, chip in enumerate(chips):
            cp = _remote(dw_in, dw_out.at[me], send_sems, recv_sems, 3 * n + j, (*chip, c))
            cp.start()
            sent.append(cp)
        base = 3 * n + 3
        for i in range(n):
            for j, (cx, cy) in enumerate(chips):
                slab = outs[i].at[2 * cx + cy, c]
                _remote(slab, slab, send_sems, recv_sems, 3 * i + j, sib).wait_recv()
                cp = _remote(slab, slab, send_sems, recv_sems, base + 3 * i + j, sib)
                cp.start()
                sent.append(cp)
        for j, (cx, cy) in enumerate(chips):
            slab = dw_out.at[2 * cx + cy]
            _remote(slab, slab, send_sems, recv_sems, 3 * n + j, sib).wait_recv()
        for i in range(n):
            for j, (cx, cy) in enumerate(chips):
                slab = outs[i].at[2 * cx + cy, 1 - c]
                _remote(slab, slab, send_sems, recv_sems, base + 3 * i + j, sib).wait_recv()
        for cp in sent:
            cp.wait_send()
        for cp in local:
            cp.wait()

    n_sem = 6 * n + 3
    out_shape = [jax.ShapeDtypeStruct(b.shape, b.dtype) for b in bufs]
    out_shape.append(jax.ShapeDtypeStruct((N_CHIPS,) + dw_shard.shape, dw_shard.dtype))
    return _call(
        body, name="gather_weights", in_specs=[ANY] * (n + 1), out_specs=[ANY] * (n + 1), out_shape=out_shape,
        input_output_aliases={i: i for i in range(n)},
        scratch_shapes=[pltpu.SemaphoreType.DMA((n_sem,)), pltpu.SemaphoreType.DMA((n_sem,)), pltpu.SemaphoreType.DMA((1,))],
    )(*bufs, dw_shard)


def _sibling_swap(grads):
    n = len(grads)

    def body(*refs):
        ins, outs = refs[0:n], refs[n:2 * n]
        send_sems, recv_sems = refs[2 * n:]
        x, y, c, _ = _place()
        sib = (x, y, 1 - c)
        sent = []
        for i in range(n):
            cp = _remote(ins[i].at[:, 1 - c], outs[i], send_sems, recv_sems, i, sib)
            cp.start()
            sent.append(cp)
        for cp in sent:
            cp.wait()

    out_shape = [jax.ShapeDtypeStruct((g.shape[0],) + g.shape[2:], g.dtype) for g in grads]
    return _call(
        body, name="sibling_swap", in_specs=[ANY] * n, out_specs=[ANY] * n, out_shape=out_shape,
        scratch_shapes=[pltpu.SemaphoreType.DMA((n,)), pltpu.SemaphoreType.DMA((n,))],
    )(*grads)


def _pair_sum(grad, other, c_idx, tr):
    n_s, _, h, cols = grad.shape
    tr = _div_tile(h, tr)

    def body(c_ref, a_ref, b_ref, o_ref):
        o_ref[...] = _bf(a_ref[0].astype(F32) + b_ref[...].astype(F32))

    return _call(
        body, name="pair_sum",
        grid_spec=pltpu.PrefetchScalarGridSpec(
            num_scalar_prefetch=1, grid=(n_s, h // tr),
            in_specs=[pl.BlockSpec((1, 1, tr, cols), lambda s, r, c_ref: (s, c_ref[0], r, 0)),
                      pl.BlockSpec((1, tr, cols), lambda s, r, c_ref: (s, r, 0))],
            out_specs=pl.BlockSpec((1, tr, cols), lambda s, r, c_ref: (s, r, 0))),
        out_shape=jax.ShapeDtypeStruct((n_s, h, cols), BF16),
        compiler_params=_cp(2),
    )(c_idx, grad, other)


def _chip_exchange(sums):
    n = len(sums)

    def body(*refs):
        ins, outs = refs[0:n], refs[n:2 * n]
        send_sems, recv_sems = refs[2 * n:]
        _, _, c, chips = _place()
        sent = []
        for i in range(n):
            for j, (cx, cy) in enumerate(chips):
                cp = _remote(ins[i].at[2 * cx + cy], outs[i].at[j], send_sems, recv_sems, 3 * i + j, (cx, cy, c))
                cp.start()
                sent.append(cp)
        for cp in sent:
            cp.wait()

    return _call(
        body, name="chip_exchange", in_specs=[ANY] * n, out_specs=[ANY] * n,
        out_shape=[jax.ShapeDtypeStruct((3,) + s.shape[1:], s.dtype) for s in sums],
        scratch_shapes=[pltpu.SemaphoreType.DMA((3 * n,)), pltpu.SemaphoreType.DMA((3 * n,))],
    )(*sums)


def _chip_sum(own, others, chip_idx, tr):
    _, h, cols = own.shape
    tr = _div_tile(h, tr)

    def body(s_ref, a_ref, p_ref, o_ref):
        acc = a_ref[0].astype(F32)
        for j in range(N_CHIPS - 1):
            acc = acc + p_ref[j].astype(F32)
        o_ref[...] = acc

    return _call(
        body, name="chip_sum",
        grid_spec=pltpu.PrefetchScalarGridSpec(
            num_scalar_prefetch=1, grid=(h // tr,),
            in_specs=[pl.BlockSpec((1, tr, cols), lambda r, s_ref: (s_ref[0], r, 0)),
                      pl.BlockSpec((N_CHIPS - 1, tr, cols), lambda r, s_ref: (0, r, 0))],
            out_specs=pl.BlockSpec((tr, cols), lambda r, s_ref: (r, 0))),
        out_shape=jax.ShapeDtypeStruct((h, cols), F32),
        compiler_params=_cp(1),
    )(chip_idx, own, others)


def _sibling_share(halves):
    n = len(halves)

    def body(*refs):
        ins, outs = refs[0:n], refs[n:2 * n]
        send_sems, recv_sems = refs[2 * n:]
        x, y, c, _ = _place()
        sent = []
        for i in range(n):
            cp = _remote(ins[i], outs[i], send_sems, recv_sems, i, (x, y, 1 - c))
            cp.start()
            sent.append(cp)
        for cp in sent:
            cp.wait()

    return _call(
        body, name="sibling_share", in_specs=[ANY] * n, out_specs=[ANY] * n,
        out_shape=[jax.ShapeDtypeStruct(s.shape, s.dtype) for s in halves],
        scratch_shapes=[pltpu.SemaphoreType.DMA((n,)), pltpu.SemaphoreType.DMA((n,))],
    )(*halves)


def _small_allreduce(pack):
    rows, cols = pack.shape

    def body(p_ref, o_ref, buf, send_sems, recv_sems):
        x, y, c, _ = _place()
        me = 4 * x + 2 * y + c
        buf[me] = p_ref[...]
        sent = []
        for k in range(1, N_DEV):
            fx, fy, fc = (k >> 2) & 1, (k >> 1) & 1, k & 1
            to = (x ^ fx, y ^ fy, c ^ fc)
            cp = _remote(p_ref, buf.at[me], send_sems, recv_sems, k - 1, to)
            cp.start()
            sent.append(cp)
        for k in range(1, N_DEV):
            fx, fy, fc = (k >> 2) & 1, (k >> 1) & 1, k & 1
            frm = 4 * (x ^ fx) + 2 * (y ^ fy) + (c ^ fc)
            _remote(p_ref, buf.at[frm], send_sems, recv_sems, k - 1, (x, y, c)).wait_recv()
        for cp in sent:
            cp.wait_send()
        acc = buf[0]
        for d in range(1, N_DEV):
            acc = acc + buf[d]
        o_ref[...] = acc

    return _call(
        body, name="small_allreduce",
        in_specs=[pl.BlockSpec(memory_space=pltpu.VMEM)], out_specs=pl.BlockSpec(memory_space=pltpu.VMEM),
        out_shape=jax.ShapeDtypeStruct(pack.shape, F32),
        scratch_shapes=[pltpu.VMEM((N_DEV, rows, cols), F32), pltpu.SemaphoreType.DMA((N_DEV - 1,)),
                        pltpu.SemaphoreType.DMA((N_DEV - 1,))],
    )(pack)


def _adamw_math(w, g, m, v):
    m_new = ADAM_B1 * m + (1.0 - ADAM_B1) * g
    v_new = ADAM_B2 * v + (1.0 - ADAM_B2) * (g * g)
    m_hat = m_new * (1.0 / (1.0 - ADAM_B1 ** ADAM_STEP))
    v_hat = v_new * (1.0 / (1.0 - ADAM_B2 ** ADAM_STEP))
    delta = -ADAM_LR * (m_hat / (jnp.sqrt(v_hat) + ADAM_EPS) + ADAM_WD * w)
    return delta, m_new, v_new


def _adamw(w, g_mine, g_other, m, v, core_idx, tr):
    rows, cols = w.shape
    h = rows // 2
    tr = _div_tile(h, tr)
    per = h // tr

    def body(c_ref, w_ref, ga_ref, gb_ref, m_ref, v_ref, g_ref, d_ref, mo_ref, vo_ref):
        g = jnp.where(pl.program_id(0) == c_ref[0], ga_ref[...], gb_ref[...])
        d, mn, vn = _adamw_math(w_ref[...], g, m_ref[...], v_ref[...])
        g_ref[...] = g
        d_ref[...] = d
        mo_ref[...] = mn
        vo_ref[...] = vn

    full = pl.BlockSpec((tr, cols), lambda hh, r, c_ref: (hh * per + r, 0))
    half = pl.BlockSpec((tr, cols), lambda hh, r, c_ref: (r, 0))
    shp = jax.ShapeDtypeStruct(w.shape, F32)
    return _call(
        body, name="adamw",
        grid_spec=pltpu.PrefetchScalarGridSpec(num_scalar_prefetch=1, grid=(2, per), in_specs=[full, half, half, full, full],
                                               out_specs=[full] * 4),
        out_shape=(shp, shp, shp, shp), compiler_params=_cp(2))(core_idx, w, g_mine, g_other, m, v)


def _adamw_small(ws, gs, ms, vs):
    n = len(ws)

    def body(*refs):
        w_r, g_r, m_r, v_r = refs[0:n], refs[n:2 * n], refs[2 * n:3 * n], refs[3 * n:4 * n]
        d_o, m_o, v_o = refs[4 * n:5 * n], refs[5 * n:6 * n], refs[6 * n:7 * n]
        for i in range(n):
            d, mn, vn = _adamw_math(w_r[i][...], g_r[i][...], m_r[i][...], v_r[i][...])
            d_o[i][...] = d
            m_o[i][...] = mn
            v_o[i][...] = vn

    shp = [jax.ShapeDtypeStruct(w.shape, F32) for w in ws]
    outs = _call(body, name="adamw_small", out_shape=shp * 3)(*ws, *gs, *ms, *vs)
    return outs[0:n], outs[n:2 * n], outs[2 * n:3 * n]


def _rope_tables(t_len):
    pos = jnp.arange(t_len, dtype=F32)
    inv_freq = ROPE_THETA ** (-jnp.arange(0, HEAD_DIM, 2, dtype=F32) / HEAD_DIM)
    ang = pos[:, None] * inv_freq[None, :]
    cos, sin = jnp.cos(ang), jnp.sin(ang)
    return jnp.tile(jnp.concatenate([cos, cos], axis=1), (1, 2)), jnp.tile(jnp.concatenate([-sin, sin], axis=1), (1, 2))


def _dup_heads(a):
    h0, h1 = a[..., 0:64], a[..., 64:128]
    return jnp.concatenate([h0, h0, h1, h1], axis=-1)


def _local_step(x, mem, target, small, wg, tm_a=512, tm_b=256, tc=512, tt=1024):
    t_len = x.shape[0]
    cos_t, sin_t = _rope_tables(t_len)
    b_in = small["b_in"]
    b_ext = jnp.concatenate([b_in[:, 0:1536], _dup_heads(b_in[:, 1536:1664]), _dup_heads(b_in[:, 1664:1792])], axis=1)
    w_dw = jnp.concatenate([wg["w_dw"], jnp.zeros((1, CONV_CH), F32)], axis=0)
    sink_b = jnp.broadcast_to(small["attn_sink"].reshape(8, 1), (8, 128))

    memn, kv = _mem_kv_fwd(mem, small["g_mem_kv"], wg["w_mem_kv"])
    ug, q, kd, vd, h1 = _in_proj_fwd(x, small["g_mix"], wg["w_in"], b_ext, cos_t, sin_t, tm_a)
    yc, pre = _conv_fwd(ug, w_dw, small["b_dw"], small["g_conv_ln"], small["b_conv_ln"], tc)
    ya, lse = _attn_fwd(q, kd, vd, sink_b)
    ymix, x1, hq, qm, om, x2 = _mix_mem_fwd(x, yc, ya, wg["w_out"], small["b_out"], small["g_mem_q"], wg["w_mem_q"], kv,
                                            wg["w_mem_o"], tm_a)
    hf, gate, up, act, dx3, loss, d_g_final = _ffn_loss(x2, small["g_ffn"], wg["w_gate"], wg["w_up"], wg["w_down"],
                                                        small["g_final"], target, tm_b)

    dx2, dgate, dup, d_g_ffn = _ffn_bwd(dx3, x2, gate, up, small["g_ffn"], wg["w_gate"], wg["w_up"], wg["w_down"], tm_b)
    dx1, dqm, dyc, dya, dkv, d_g_mem_q, d_b_out = _mix_mem_bwd(dx2, x1, qm, kv, small["g_mem_q"], wg["w_mem_q"], wg["w_mem_o"],
                                                               wg["w_out"], tm_b)
    d_w_mem_kv, d_g_mem_kv = _mem_kv_bwd(dkv, memn, mem, small["g_mem_kv"], wg["w_mem_kv"])
    dq, dd, dsink = _attn_bwd_q(q, kd, vd, dya, lse, sink_b, cos_t, sin_t)
    dk, dv = _attn_bwd_kv(q, kd, vd, dya, lse, dd, cos_t, sin_t)
    dpre, cstats = _conv_norm_bwd(pre, dyc, small["g_conv_ln"], small["b_conv_ln"], tc)
    du_glu, d_w_dw = _conv_bwd(dpre, ug, w_dw, tc)
    grad_x, du, d_b_in, d_g_mix = _in_proj_bwd(du_glu, dq, dk, dv, dx1, x, small["g_mix"], wg["w_in"], tm_a)

    grads = {
        "w_in": _weight_grad(du[None], h1[None], "dw_in", tt)[0],
        "w_out": _weight_grad(ymix[None], dx1[None], "dw_out", tt)[0],
        "w_mem_q": _weight_grad(hq[None], dqm[None], "dw_mem_q", tt)[0],
        "w_mem_o": _weight_grad(om[None], dx2[None], "dw_mem_o", tt)[0],
        "w_mem_kv": d_w_mem_kv,
        "w_gate": _weight_grad(dgate, hf[None], "dw_gate", tt),
        "w_up": _weight_grad(dup, hf[None], "dw_up", tt),
        "w_down": _weight_grad(act, dx3[None], "dw_down", tt),
        "w_dw": d_w_dw[0:CONV_W],
        "g_mix": d_g_mix, "b_in": d_b_in, "b_dw": cstats[2:3], "g_conv_ln": cstats[0:1],
        "b_conv_ln": cstats[1:2], "attn_sink": jnp.sum(dsink[:, :, 0], axis=0)[None, :], "b_out": d_b_out,
        "g_mem_q": d_g_mem_q, "g_mem_kv": d_g_mem_kv, "g_ffn": d_g_ffn, "g_final": d_g_final,
    }
    return loss[0:1, 0:1], grad_x, grads


BIG = ["w_in", "w_out", "w_mem_q", "w_mem_kv", "w_mem_o", "w_gate", "w_up", "w_down"]
KEEP_SLABS = ("w_mem_kv", "w_gate", "w_up", "w_down")
TRANSPOSED = ("w_in", "w_gate", "w_up")
SMALL = ["g_mix", "b_in", "b_dw", "g_conv_ln", "b_conv_ln", "attn_sink", "b_out", "g_mem_q", "g_mem_kv", "g_ffn", "g_final"]
PACK_ROWS = 32


def _pack_small(loss, grads):
    def row(a):
        a = a.reshape(1, -1)
        return jnp.pad(a, ((0, 0), (0, 1024 - a.shape[1])))

    rows = [row(grads[k]) for k in ("g_mix", "b_out", "g_mem_q", "g_mem_kv", "g_ffn", "g_final")]
    rows += [grads["b_in"][:, 0:1024], row(grads["b_in"][:, 1024:1792])]
    rows += [jnp.concatenate([grads["b_dw"], grads["g_conv_ln"]], axis=1), row(grads["b_conv_ln"]), row(grads["attn_sink"]),
             row(loss)]
    dw = jnp.pad(grads["w_dw"], ((0, 1), (0, 0))).reshape(16, 1024)
    pack = jnp.concatenate(rows + [dw], axis=0)
    return jnp.pad(pack, ((0, PACK_ROWS - pack.shape[0]), (0, 0)))


def _unpack_small(pack):
    out = {k: pack[i:i + 1] for i, k in enumerate(("g_mix", "b_out", "g_mem_q", "g_mem_kv", "g_ffn", "g_final"))}
    out["b_in"] = jnp.concatenate([pack[6:7], pack[7:8, 0:768]], axis=1)
    out["b_dw"], out["g_conv_ln"] = pack[8:9, 0:512], pack[8:9, 512:1024]
    out["b_conv_ln"] = pack[9:10, 0:512]
    out["attn_sink"] = pack[10:11, 0:8]
    loss = pack[11, 0]
    dw = pack[12:28].reshape(32, 512)[0:CONV_W]
    return loss, out, dw


def kernel(x, mem, g_mix, w_in, b_in, w_dw, b_dw, g_conv_ln, b_conv_ln, attn_sink, w_out, b_out, g_mem_q, g_mem_kv, w_mem_q, w_mem_kv, w_mem_o, g_ffn, w_gate, w_up, w_down, g_final, loss_target, m_g_mix, m_w_in, m_b_in, m_w_dw, m_b_dw, m_g_conv_ln, m_b_conv_ln, m_attn_sink, m_w_out, m_b_out, m_g_mem_q, m_g_mem_kv, m_w_mem_q, m_w_mem_kv, m_w_mem_o, m_g_ffn, m_w_gate, m_w_up, m_w_down, m_g_final, v_g_mix, v_w_in, v_b_in, v_w_dw, v_b_dw, v_g_conv_ln, v_b_conv_ln, v_attn_sink, v_w_out, v_b_out, v_g_mem_q, v_g_mem_kv, v_w_mem_q, v_w_mem_kv, v_w_mem_o, v_g_ffn, v_w_gate, v_w_up, v_w_down, v_g_final):
    args = dict(locals())
    weight_names = ["g_mix", "w_in", "b_in", "w_dw", "b_dw", "g_conv_ln", "b_conv_ln", "attn_sink", "w_out", "b_out", "g_mem_q",
                    "g_mem_kv", "w_mem_q", "w_mem_kv", "w_mem_o", "g_ffn", "w_gate", "w_up", "w_down", "g_final"]
    chip = 2 * lax.axis_index("x") + lax.axis_index("y")
    core = lax.axis_index("c")

    chip_idx = chip.astype(jnp.int32).reshape(1)
    core_idx = core.astype(jnp.int32).reshape(1)

    def block(name):
        a = args[name][0]
        weight = name[2:] if name[:2] in ("m_", "v_") else name
        return a.T if weight in TRANSPOSED else a

    gathered = _gather_weights([_cast_place(block(k), chip_idx, 256) for k in BIG], w_dw[0])
    wg = {}
    for k, g in zip(BIG, gathered[:-1]):
        g = g.reshape(N_CHIPS, g.shape[2] * 2, g.shape[3])
        wg[k] = g if k in KEEP_SLABS else g.reshape(-1, g.shape[2])
    wg["w_dw"] = jnp.transpose(gathered[-1], (1, 0, 2)).reshape(CONV_W, CONV_CH)

    small = {k: args[k].reshape(1, -1) for k in SMALL}

    loss, grad_x, grads = _local_step(x[0], mem[0], loss_target[0], small, wg)

    parts = []
    for k in BIG:
        g = grads[k]
        if g.ndim == 2:
            g = g.reshape(N_CHIPS, g.shape[0] // N_CHIPS, g.shape[1])
        parts.append(g.reshape(N_CHIPS, 2, g.shape[1] // 2, g.shape[2]))
    from_sibling = _sibling_swap(parts)
    chip_sums = [_pair_sum(p, o, core_idx, 256) for p, o in zip(parts, from_sibling)]
    from_chips = _chip_exchange(chip_sums)
    halves = [_chip_sum(a, p, chip_idx, 256) for a, p in zip(chip_sums, from_chips)]
    other_halves = _sibling_share(halves)

    loss_sum, small_grads, dw_full = _unpack_small(_small_allreduce(_pack_small(loss, grads)))
    dw_cols = jnp.transpose(dw_full.reshape(CONV_W, N_CHIPS, 128), (1, 0, 2))
    small_grads["w_dw"] = lax.dynamic_index_in_dim(dw_cols, chip, axis=0, keepdims=False)

    out_g, out_d, out_m, out_v = {}, {}, {}, {}
    for k, g_mine, g_other in zip(BIG, halves, other_halves):
        res = _adamw(block(k), g_mine, g_other, block("m_" + k), block("v_" + k), core_idx, 256)
        out_g[k], out_d[k], out_m[k], out_v[k] = [(r.T if k in TRANSPOSED else r)[None] for r in res]
    names = SMALL + ["w_dw"]

    def flat(a):
        return a[0] if a.ndim == 3 else a.reshape(1, -1)

    def pad_lanes(a):
        return jnp.pad(a, ((0, 0), (0, 128 - a.shape[1]))) if a.shape[1] < 128 else a

    ws = [flat(args[k]) for k in names]
    gs = [small_grads[k] for k in names]
    ms = [flat(args["m_" + k]) for k in names]
    vs = [flat(args["v_" + k]) for k in names]
    ds, mns, vns = _adamw_small([pad_lanes(a) for a in ws], [pad_lanes(a) for a in gs], [pad_lanes(a) for a in ms],
                                [pad_lanes(a) for a in vs])
    for i, k in enumerate(names):
        n_lanes = ws[i].shape[1]
        for out, val in ((out_g, gs[i]), (out_d, ds[i]), (out_m, mns[i]), (out_v, vns[i])):
            out[k] = val[:, 0:n_lanes].reshape(args[k].shape)

    return (loss_sum, grad_x[None], *[out_g[k] for k in weight_names], *[out_d[k] for k in weight_names],
            *[out_m[k] for k in weight_names], *[out_v[k] for k in weight_names])
```

```python
import jax
import jax.numpy as jnp
from jax import lax
from jax.experimental import pallas as pl
from jax.experimental.pallas import tpu as pltpu

F32 = jnp.float32
BF16 = jnp.bfloat16
EPS = 1e-6
NEG = -1e30

D_MODEL = 1024
CONV_CH = 512
CONV_W = 31
HEAD_DIM = 64
BLK = 128
MEM_HEADS = 4
MEM_HD = 256
N_CHIPS = 4
N_DEV = 8
ATT_SCALE = HEAD_DIM ** -0.5
MEM_SCALE = MEM_HD ** -0.5
ROPE_THETA = 10000.0

ADAM_LR = 0.001
ADAM_B1 = 0.9
ADAM_B2 = 0.999
ADAM_EPS = 1e-08
ADAM_WD = 0.01
ADAM_STEP = 10

VMEM_LIMIT_BYTES = 56 * 1024 * 1024
MESH = pl.DeviceIdType.MESH


class _Ride:
    def __init__(self, operands, out_shape, n_sem, start, finish, aliases=None):
        self.operands, self.out_shape, self.n_sem = list(operands), list(out_shape), n_sem
        self.start, self.finish, self.aliases = start, finish, dict(aliases or {})


def _call(body, rides=(), **kw):
    if not rides:
        return pl.pallas_call(body, **kw)
    (n_steps,) = kw["grid"]
    n_in, n_out = len(kw["in_specs"]), len(kw["out_specs"])
    scratch = list(kw.get("scratch_shapes", ()))
    k_in = [len(r.operands) for r in rides]
    k_out = [len(r.out_shape) for r in rides]

    def carried(*refs):
        pos = n_in
        r_in, r_out = [], []
        for k in k_in:
            r_in.append(refs[pos:pos + k])
            pos += k
        own_out = refs[pos:pos + n_out]
        pos += n_out
        for k in k_out:
            r_out.append(refs[pos:pos + k])
            pos += k
        own_scratch = refs[pos:pos + len(scratch)]
        sems = refs[pos + len(scratch):]
        step = pl.program_id(0)

        @pl.when(step == 0)
        def _():
            for j, r in enumerate(rides):
                r.start(r_in[j], r_out[j], sems[2 * j], sems[2 * j + 1])

        body(*refs[:n_in], *own_out, *own_scratch)

        @pl.when(step == n_steps - 1)
        def _():
            for j, r in enumerate(rides):
                r.finish(r_in[j], r_out[j], sems[2 * j], sems[2 * j + 1])

    kw = dict(kw)
    kw["in_specs"] = list(kw["in_specs"]) + [ANY] * sum(k_in)
    kw["out_specs"] = list(kw["out_specs"]) + [ANY] * sum(k_out)
    kw["out_shape"] = list(kw["out_shape"]) + [s for r in rides for s in r.out_shape]
    kw["scratch_shapes"] = scratch + [pltpu.SemaphoreType.DMA((r.n_sem,)) for r in rides for _ in range(2)]
    aliases, off_in, off_out = {}, n_in, n_out
    for r, ki, ko in zip(rides, k_in, k_out):
        aliases.update({off_in + a: off_out + b for a, b in r.aliases.items()})
        off_in, off_out = off_in + ki, off_out + ko
    if aliases:
        kw["input_output_aliases"] = aliases
    call = pl.pallas_call(carried, **kw)
    return lambda *args: call(*args, *[op for r in rides for op in r.operands])


def _cp(n_grid):
    return pltpu.CompilerParams(dimension_semantics=("arbitrary",) * n_grid, vmem_limit_bytes=VMEM_LIMIT_BYTES)


def _res(shape):
    nd = len(shape)
    return pl.BlockSpec(shape, lambda *_: (0,) * nd, pipeline_mode=pl.Buffered(1))


def _rows(tm, n):
    return pl.BlockSpec((tm, n), lambda i: (i, 0))


def _div_tile(n, target):
    best = None
    for d in range(16, min(n, target) + 1, 16):
        if n % d == 0:
            best = d
    assert best is not None, (n, target)
    return best


def _dot(a, b):
    return jnp.dot(a, b, preferred_element_type=F32)


def _dot_nt(a, b):
    return lax.dot_general(a, b, (((1,), (1,)), ((), ())), preferred_element_type=F32)


def _dot_tn(a, b):
    return lax.dot_general(a, b, (((0,), (0,)), ((), ())), preferred_element_type=F32)


def _bf(x):
    return x.astype(BF16)


def _sigmoid(x):
    return 1.0 / (1.0 + jnp.exp(-x))


def _rms_fwd(x, g):
    r = lax.rsqrt(jnp.mean(x * x, axis=-1, keepdims=True) + EPS)
    xh = x * r
    return xh * g, xh, r


def _rms_bwd(dh, xh, r, g):
    dxh = dh * g
    return r * (dxh - xh * jnp.mean(dxh * xh, axis=-1, keepdims=True))


def _colsum(x):
    return jnp.sum(x, axis=0, keepdims=True)


def _rope(x, cos, sin, sign):
    n = x.shape[1] // 128
    c = jnp.tile(cos, (1, n)) if n > 1 else cos
    s = jnp.tile(sin, (1, n)) if n > 1 else sin
    lane = lax.broadcasted_iota(jnp.int32, x.shape, 1)
    first = (lane & 63) < 32
    partner = jnp.where(first, pltpu.roll(x, x.shape[1] - 32, 1), pltpu.roll(x, 32, 1))
    return x * c + sign * (partner * s)


def _lo_lanes(shape):
    return lax.broadcasted_iota(jnp.int32, shape, 1) < 64


def _stack_heads(t):
    t0, t1 = t[:, 0:128], t[:, 128:256]
    lo = _lo_lanes(t0.shape)
    z = jnp.zeros_like(t0)
    return jnp.concatenate([jnp.where(lo, t0, z), jnp.where(lo, z, t0), jnp.where(lo, t1, z), jnp.where(lo, z, t1)], axis=0)


def _unstack_heads(o):
    lo = _lo_lanes((BLK, 128))
    return jnp.concatenate([jnp.where(lo, o[0:128], o[128:256]), jnp.where(lo, o[256:384], o[384:512])], axis=1)


def _fold_heads(parts):
    a, b = (p + pltpu.roll(p, 64, 1) for p in parts)
    return jnp.where(_lo_lanes(a.shape), a, b)


def _sink_col(sk_ref, g):
    return jnp.concatenate([jnp.broadcast_to(sk_ref[4 * g + h:4 * g + h + 1, :], (BLK, 128)) for h in range(4)], axis=0)


def _tile3(x):
    return jnp.concatenate([x, x, x], axis=1)


def _mem_kv_fwd(mem, g_kv, w_kv):
    m_len = mem.shape[0]
    cols = w_kv.shape[2]

    def body(mem_ref, g_ref, w_ref, memn_ref, kv_ref):
        h, _, _ = _rms_fwd(mem_ref[...], g_ref[...])
        hb = _bf(h)
        memn_ref[...] = hb
        for s in range(N_CHIPS):
            kv_ref[s] = _bf(_dot(hb, w_ref[s]))

    return _call(
        body, name="mem_kv_fwd",
        out_shape=(jax.ShapeDtypeStruct((m_len, D_MODEL), BF16), jax.ShapeDtypeStruct((N_CHIPS, m_len, cols), BF16)),
        compiler_params=pltpu.CompilerParams(vmem_limit_bytes=VMEM_LIMIT_BYTES),
    )(mem, g_kv, w_kv)


def _dup_head_rows(w_ref, lo):
    h0, h1 = w_ref[lo:lo + 64, :], w_ref[lo + 64:lo + 128, :]
    return jnp.concatenate([h0, h0, h1, h1], axis=0)


def _in_proj_fwd(x, g_mix, w_t, b_ext, cos_t, sin_t, tm, rides=()):
    t_len = x.shape[0]

    def body(x_ref, g_ref, w_ref, b_ref, c_ref, s_ref, ug_ref, q_ref, k_ref, v_ref, h_ref):
        h, _, _ = _rms_fwd(x_ref[...], g_ref[...])
        hb = _bf(h)
        h_ref[...] = hb
        ug_ref[...] = _dot_nt(hb, w_ref[0:1024, :]) + b_ref[:, 0:1024]
        c, s = c_ref[...], s_ref[...]
        q_ref[...] = _bf(_rope(_dot_nt(hb, w_ref[1024:1536, :]) + b_ref[:, 1024:1536], c, s, 1.0))
        k_ref[...] = _bf(_rope(_dot_nt(hb, _dup_head_rows(w_ref, 1536)) + b_ref[:, 1536:1792], c, s, 1.0))
        v_ref[...] = _bf(_dot_nt(hb, _dup_head_rows(w_ref, 1664)) + b_ref[:, 1792:2048])

    return _call(
        body, rides=rides, name="in_proj_fwd", grid=(t_len // tm,),
        in_specs=[_rows(tm, D_MODEL), _res((1, D_MODEL)), _res(w_t.shape), _res(b_ext.shape), _rows(tm, 128), _rows(tm, 128)],
        out_specs=[_rows(tm, 1024), _rows(tm, 512), _rows(tm, 256), _rows(tm, 256), _rows(tm, D_MODEL)],
        out_shape=(jax.ShapeDtypeStruct((t_len, 1024), F32), jax.ShapeDtypeStruct((t_len, 512), BF16),
                   jax.ShapeDtypeStruct((t_len, 256), BF16), jax.ShapeDtypeStruct((t_len, 256), BF16),
                   jax.ShapeDtypeStruct((t_len, D_MODEL), BF16)),
        compiler_params=_cp(1),
    )(x, g_mix, w_t, b_ext, cos_t, sin_t)


def _halo_specs(tc, n, t_len):
    per = tc // 16
    last = t_len // 16 - 1
    return [pl.BlockSpec((16, n), lambda i: (jnp.maximum(i * per - 1, 0), 0)),
            pl.BlockSpec((tc, n), lambda i: (i, 0)),
            pl.BlockSpec((16, n), lambda i: (jnp.minimum((i + 1) * per, last), 0))]


def _glu(z):
    return z[:, 0:CONV_CH] * _sigmoid(z[:, CONV_CH:2 * CONV_CH])


def _fill_halo_buf(buf, prev, main, nxt, i, n_tiles, tc):
    buf[0:16, :] = jnp.where(i > 0, prev, jnp.zeros_like(prev))
    buf[16:16 + tc, :] = main
    buf[16 + tc:32 + tc, :] = jnp.where(i < n_tiles - 1, nxt, jnp.zeros_like(nxt))


def _conv_fwd(ug, w_dw, b_dw, g_ln, b_ln, tc, rides=()):
    t_len = ug.shape[0]
    n_tiles = t_len // tc

    def body(up_ref, um_ref, un_ref, w_ref, bdw_ref, g_ref, b_ref, y_ref, pre_ref, buf):
        i = pl.program_id(0)
        _fill_halo_buf(buf, _glu(up_ref[...]), _glu(um_ref[...]), _glu(un_ref[...]), i, n_tiles, tc)
        acc = jnp.zeros((tc, CONV_CH), F32)
        for k in range(CONV_W):
            acc = acc + w_ref[k:k + 1, :] * buf[k + 1:k + 1 + tc, :]
        pre = acc + bdw_ref[...]
        pre_ref[...] = pre
        mu = jnp.mean(pre, axis=-1, keepdims=True)
        d = pre - mu
        rstd = lax.rsqrt(jnp.mean(d * d, axis=-1, keepdims=True) + EPS)
        ln = d * rstd * g_ref[...] + b_ref[...]
        y_ref[...] = _bf(ln * _sigmoid(ln))

    return _call(
        body, rides=rides, name="conv_fwd", grid=(n_tiles,),
        in_specs=_halo_specs(tc, 1024, t_len) + [_res((32, CONV_CH)), _res((1, CONV_CH)), _res((1, CONV_CH)), _res((1, CONV_CH))],
        out_specs=[_rows(tc, CONV_CH), _rows(tc, CONV_CH)],
        out_shape=(jax.ShapeDtypeStruct((t_len, CONV_CH), BF16), jax.ShapeDtypeStruct((t_len, CONV_CH), F32)),
        scratch_shapes=[pltpu.VMEM((tc + 32, CONV_CH), F32)],
        compiler_params=_cp(1),
    )(ug, ug, ug, w_dw, b_dw, g_ln, b_ln)


def _nbr_specs(n, nb):
    return [pl.BlockSpec((BLK, n), lambda i: (jnp.maximum(i - 1, 0), 0)),
            pl.BlockSpec((BLK, n), lambda i: (i, 0)),
            pl.BlockSpec((BLK, n), lambda i: (jnp.minimum(i + 1, nb - 1), 0))]


def _nbr_specs4(nb):
    return [pl.BlockSpec((1, 2, 4 * BLK, 128), lambda i: (jnp.maximum(i - 1, 0), 0, 0, 0)),
            pl.BlockSpec((1, 2, 4 * BLK, 128), lambda i: (i, 0, 0, 0)),
            pl.BlockSpec((1, 2, 4 * BLK, 128), lambda i: (jnp.minimum(i + 1, nb - 1), 0, 0, 0))]


def _band_mask_q(i, t_len):
    a = lax.broadcasted_iota(jnp.int32, (4 * BLK, 3 * BLK), 0) & (BLK - 1)
    c = lax.broadcasted_iota(jnp.int32, (4 * BLK, 3 * BLK), 1)
    kpos = (i - 1) * BLK + c
    rel = c - BLK - a
    return (jnp.abs(rel) <= BLK) & (kpos >= 0) & (kpos < t_len)


def _attn_fwd(q, kd, vd, sink_b, rides=()):
    t_len = q.shape[0]
    nb = t_len // BLK

    def body(q_ref, kp_ref, kc_ref, kn_ref, vp_ref, vc_ref, vn_ref, sk_ref, y_ref, lse_ref):
        i = pl.program_id(0)
        kcat = jnp.concatenate([kp_ref[...], kc_ref[...], kn_ref[...]], axis=0)
        vcat = jnp.concatenate([vp_ref[...], vc_ref[...], vn_ref[...]], axis=0)
        valid = _band_mask_q(i, t_len)
        ys = []
        for g in range(2):
            qs = _stack_heads(q_ref[:, 256 * g:256 * g + 256])
            s = _dot_nt(qs, kcat[:, 128 * g:128 * g + 128]) * ATT_SCALE
            s = jnp.where(valid, s, NEG)
            skc = _sink_col(sk_ref, g)
            m_b = jnp.maximum(jnp.max(s, axis=-1, keepdims=True), skc)
            p = jnp.exp(s - _tile3(m_b))
            den_b = jnp.sum(p, axis=-1, keepdims=True) + jnp.exp(skc - m_b)
            pn = p * _tile3(1.0 / den_b)
            o = _dot(_bf(pn), vcat[:, 128 * g:128 * g + 128])
            ys.append(_unstack_heads(o))
            lse_ref[0, g] = m_b + jnp.log(den_b)
        y_ref[...] = _bf(jnp.concatenate(ys, axis=1))

    return _call(
        body, rides=rides, name="attn_fwd", grid=(nb,),
        in_specs=[_rows(BLK, 512)] + _nbr_specs(256, nb) + _nbr_specs(256, nb) + [_res((8, 128))],
        out_specs=[_rows(BLK, 512), pl.BlockSpec((1, 2, 4 * BLK, 128), lambda i: (i, 0, 0, 0))],
        out_shape=(jax.ShapeDtypeStruct((t_len, 512), BF16), jax.ShapeDtypeStruct((nb, 2, 4 * BLK, 128), F32)),
        compiler_params=_cp(1),
    )(q, kd, kd, kd, vd, vd, vd, sink_b)


def _mem_heads(kv_ref, h):
    lo = MEM_HD * (h % 2)
    return kv_ref[h // 2, :, lo:lo + MEM_HD], kv_ref[2 + h // 2, :, lo:lo + MEM_HD]


def _mix_mem_fwd(x, yc, ya, w_out, b_out, g_q, w_q, kv, w_o, tm, rides=()):
    t_len = x.shape[0]

    def body(x_ref, yc_ref, ya_ref, wout_ref, bout_ref, g_ref, wq_ref, kv_ref, wo_ref,
             ymix_ref, x1_ref, hq_ref, qm_ref, om_ref, x2_ref):
        ymix = jnp.concatenate([yc_ref[...], ya_ref[...]], axis=1)
        ymix_ref[...] = ymix
        x1 = x_ref[...] + _dot(ymix, wout_ref[...]) + bout_ref[...]
        x1_ref[...] = x1
        hq, _, _ = _rms_fwd(x1, g_ref[...])
        hqb = _bf(hq)
        hq_ref[...] = hqb
        qm = _bf(_dot(hqb, wq_ref[...]))
        qm_ref[...] = qm
        outs = []
        for h in range(MEM_HEADS):
            kh, vh = _mem_heads(kv_ref, h)
            s = _dot_nt(qm[:, MEM_HD * h:MEM_HD * (h + 1)], kh) * MEM_SCALE
            p = jnp.exp(s - jnp.max(s, axis=-1, keepdims=True))
            p = p * (1.0 / jnp.sum(p, axis=-1, keepdims=True))
            outs.append(_dot(_bf(p), vh))
        om = _bf(jnp.concatenate(outs, axis=1))
        om_ref[...] = om
        x2_ref[...] = x1 + _dot(om, wo_ref[...])

    act_b = jax.ShapeDtypeStruct((t_len, D_MODEL), BF16)
    act_f = jax.ShapeDtypeStruct((t_len, D_MODEL), F32)
    return _call(
        body, rides=rides, name="mix_mem_fwd", grid=(t_len // tm,),
        in_specs=[_rows(tm, D_MODEL), _rows(tm, 512), _rows(tm, 512), _res(w_out.shape), _res((1, D_MODEL)), _res((1, D_MODEL)),
                  _res(w_q.shape), _res(kv.shape), _res(w_o.shape)],
        out_specs=[_rows(tm, D_MODEL)] * 6,
        out_shape=(act_b, act_f, act_b, act_b, act_b, act_f),
        compiler_params=_cp(1),
    )(x, yc, ya, w_out, b_out, g_q, w_q, kv, w_o)


def _ffn_loss(x2, g_ffn, w_gate, w_up, w_down, g_final, target, tm):
    t_len = x2.shape[0]
    ff = w_gate.shape[1]
    n_tiles = t_len // tm

    def body(x2_ref, g_ref, wg_ref, wu_ref, wd_ref, gf_ref, tgt_ref,
             hf_ref, gate_ref, up_ref, act_ref, dx3_ref, loss_ref, dgf_ref):
        i = pl.program_id(0)
        x2v = x2_ref[...]
        hf, _, _ = _rms_fwd(x2v, g_ref[...])
        hfb = _bf(hf)
        hf_ref[...] = hfb
        acc = jnp.zeros((tm, D_MODEL), F32)
        for s in range(N_CHIPS):
            gate = _dot_nt(hfb, wg_ref[s])
            up = _dot_nt(hfb, wu_ref[s])
            act = _bf(gate * _sigmoid(gate) * up)
            gate_ref[s] = _bf(gate)
            up_ref[s] = _bf(up)
            act_ref[s] = act
            acc = acc + _dot(act, wd_ref[s])
        x3 = x2v + acc
        gf = gf_ref[...]
        y, xh, r = _rms_fwd(x3, gf)
        err = y - tgt_ref[...]
        part = 0.5 * jnp.sum(jnp.mean(err * err, axis=-1, keepdims=True), axis=0, keepdims=True)
        dy = err * (1.0 / D_MODEL)
        dx3_ref[...] = _rms_bwd(dy, xh, r, gf)

        @pl.when(i == 0)
        def _():
            loss_ref[...] = jnp.zeros_like(loss_ref)
            dgf_ref[...] = jnp.zeros_like(dgf_ref)

        loss_ref[...] += jnp.broadcast_to(part, loss_ref.shape)
        dgf_ref[...] += _colsum(dy * xh)

    hid = jax.ShapeDtypeStruct((N_CHIPS, t_len, ff), BF16)
    hid_spec = pl.BlockSpec((N_CHIPS, tm, ff), lambda i: (0, i, 0))
    return _call(
        body, name="ffn_loss", grid=(n_tiles,),
        in_specs=[_rows(tm, D_MODEL), _res((1, D_MODEL)), _res(w_gate.shape), _res(w_up.shape), _res(w_down.shape),
                  _res((1, D_MODEL)), _rows(tm, D_MODEL)],
        out_specs=[_rows(tm, D_MODEL), hid_spec, hid_spec, hid_spec, _rows(tm, D_MODEL),
                   pl.BlockSpec((1, D_MODEL), lambda i: (0, 0)), pl.BlockSpec((1, D_MODEL), lambda i: (0, 0))],
        out_shape=(jax.ShapeDtypeStruct((t_len, D_MODEL), BF16), hid, hid, hid, jax.ShapeDtypeStruct((t_len, D_MODEL), F32),
                   jax.ShapeDtypeStruct((1, D_MODEL), F32), jax.ShapeDtypeStruct((1, D_MODEL), F32)),
        compiler_params=_cp(1),
    )(x2, g_ffn, w_gate, w_up, w_down, g_final, target)


def _ffn_bwd(dx3, x2, gate, up, g_ffn, w_gate, w_up, w_down, tm):
    t_len = x2.shape[0]
    ff = w_gate.shape[1]

    def body(dx3_ref, x2_ref, gate_ref, up_ref, g_ref, wg_ref, wu_ref, wd_ref, dx2_ref, dgate_ref, dup_ref, dg_ref):
        i = pl.program_id(0)
        dx3 = dx3_ref[...]
        d3b = _bf(dx3)
        dh = jnp.zeros((tm, D_MODEL), F32)
        for s in range(N_CHIPS):
            dact = _dot_nt(d3b, wd_ref[s])
            gt = gate_ref[s].astype(F32)
            u = up_ref[s].astype(F32)
            sg = _sigmoid(gt)
            dup = _bf(dact * (gt * sg))
            dgate = _bf(dact * u * (sg * (1.0 + gt * (1.0 - sg))))
            dup_ref[s] = dup
            dgate_ref[s] = dgate
            dh = dh + _dot(dgate, wg_ref[s]) + _dot(dup, wu_ref[s])
        g = g_ref[...]
        _, xh, r = _rms_fwd(x2_ref[...], g)
        dx2_ref[...] = dx3 + _rms_bwd(dh, xh, r, g)

        @pl.when(i == 0)
        def _():
            dg_ref[...] = jnp.zeros_like(dg_ref)

        dg_ref[...] += _colsum(dh * xh)

    hid = jax.ShapeDtypeStruct((N_CHIPS, t_len, ff), BF16)
    hid_spec = pl.BlockSpec((N_CHIPS, tm, ff), lambda i: (0, i, 0))
    return _call(
        body, name="ffn_bwd", grid=(t_len // tm,),
        in_specs=[_rows(tm, D_MODEL), _rows(tm, D_MODEL), hid_spec, hid_spec, _res((1, D_MODEL)),
                  _res(w_gate.shape), _res(w_up.shape), _res(w_down.shape)],
        out_specs=[_rows(tm, D_MODEL), hid_spec, hid_spec, pl.BlockSpec((1, D_MODEL), lambda i: (0, 0))],
        out_shape=(jax.ShapeDtypeStruct((t_len, D_MODEL), F32), hid, hid, jax.ShapeDtypeStruct((1, D_MODEL), F32)),
        compiler_params=_cp(1),
    )(dx3, x2, gate, up, g_ffn, w_gate, w_up, w_down)


def _mix_mem_bwd(dx2, x1, qm, kv, g_q, w_q, w_o, w_out, tm, rides=()):
    t_len = x1.shape[0]
    m_len = kv.shape[1]

    def body(dx2_ref, x1_ref, qm_ref, kv_ref, g_ref, wq_ref, wo_ref, wout_ref,
             dx1_ref, dqm_ref, dyc_ref, dya_ref, dkv_ref, dgq_ref, dbout_ref):
        i = pl.program_id(0)

        @pl.when(i == 0)
        def _():
            dkv_ref[...] = jnp.zeros_like(dkv_ref)
            dgq_ref[...] = jnp.zeros_like(dgq_ref)
            dbout_ref[...] = jnp.zeros_like(dbout_ref)

        dx2 = dx2_ref[...]
        dom = _dot_nt(_bf(dx2), wo_ref[...])
        dqs = []
        for h in range(MEM_HEADS):
            kh, vh = _mem_heads(kv_ref, h)
            qh = qm_ref[:, MEM_HD * h:MEM_HD * (h + 1)]
            s = _dot_nt(qh, kh) * MEM_SCALE
            p = jnp.exp(s - jnp.max(s, axis=-1, keepdims=True))
            p = p * (1.0 / jnp.sum(p, axis=-1, keepdims=True))
            domh = _bf(dom[:, MEM_HD * h:MEM_HD * (h + 1)])
            dp = _dot_nt(domh, vh)
            ds = _bf(p * (dp - jnp.sum(p * dp, axis=-1, keepdims=True)) * MEM_SCALE)
            dqs.append(_dot(ds, kh))
            lo = MEM_HD * (h % 2)
            dkv_ref[h // 2, :, lo:lo + MEM_HD] += _dot_tn(ds, qh)
            dkv_ref[2 + h // 2, :, lo:lo + MEM_HD] += _dot_tn(_bf(p), domh)
        dqm = _bf(jnp.concatenate(dqs, axis=1))
        dqm_ref[...] = dqm
        dhq = _dot_nt(dqm, wq_ref[...])
        g = g_ref[...]
        _, xh, r = _rms_fwd(x1_ref[...], g)
        dx1 = dx2 + _rms_bwd(dhq, xh, r, g)
        dx1_ref[...] = dx1
        dgq_ref[...] += _colsum(dhq * xh)
        dbout_ref[...] += _colsum(dx1)
        dymix = _dot_nt(_bf(dx1), wout_ref[...])
        dyc_ref[...] = dymix[:, 0:CONV_CH]
        dya_ref[...] = _bf(dymix[:, CONV_CH:2 * CONV_CH])

    vec = pl.BlockSpec((1, D_MODEL), lambda i: (0, 0))
    return _call(
        body, rides=rides, name="mix_mem_bwd", grid=(t_len // tm,),
        in_specs=[_rows(tm, D_MODEL), _rows(tm, D_MODEL), _rows(tm, D_MODEL), _res(kv.shape), _res((1, D_MODEL)),
                  _res(w_q.shape), _res(w_o.shape), _res(w_out.shape)],
        out_specs=[_rows(tm, D_MODEL), _rows(tm, D_MODEL), _rows(tm, CONV_CH), _rows(tm, CONV_CH),
                   pl.BlockSpec(kv.shape, lambda i: (0, 0, 0)), vec, vec],
        out_shape=(jax.ShapeDtypeStruct((t_len, D_MODEL), F32), jax.ShapeDtypeStruct((t_len, D_MODEL), BF16),
                   jax.ShapeDtypeStruct((t_len, CONV_CH), F32), jax.ShapeDtypeStruct((t_len, CONV_CH), BF16),
                   jax.ShapeDtypeStruct((N_CHIPS, m_len, kv.shape[2]), F32),
                   jax.ShapeDtypeStruct((1, D_MODEL), F32), jax.ShapeDtypeStruct((1, D_MODEL), F32)),
        compiler_params=_cp(1),
    )(dx2, x1, qm, kv, g_q, w_q, w_o, w_out)


def _mem_kv_bwd(dkv, memn, mem, g_kv, w_kv):
    m_len = mem.shape[0]

    def body(dkv_ref, memn_ref, mem_ref, g_ref, w_ref, dw_ref, dg_ref):
        hb = memn_ref[...]
        dmn = jnp.zeros((m_len, D_MODEL), F32)
        for s in range(N_CHIPS):
            d = _bf(dkv_ref[s])
            dw_ref[s] = _bf(_dot_tn(hb, d))
            dmn = dmn + _dot_nt(d, w_ref[s])
        _, xh, _ = _rms_fwd(mem_ref[...], g_ref[...])
        dg_ref[...] = _colsum(dmn * xh)

    return _call(
        body, name="mem_kv_bwd",
        out_shape=(jax.ShapeDtypeStruct(w_kv.shape, BF16), jax.ShapeDtypeStruct((1, D_MODEL), F32)),
        compiler_params=pltpu.CompilerParams(vmem_limit_bytes=VMEM_LIMIT_BYTES),
    )(dkv, memn, mem, g_kv, w_kv)


def _attn_bwd_q(q, kd, vd, dya, lse, sink_b, cos_t, sin_t, rides=()):
    t_len = q.shape[0]
    nb = t_len // BLK

    def body(q_ref, kp_ref, kc_ref, kn_ref, vp_ref, vc_ref, vn_ref, do_ref, lse_ref, sk_ref, c_ref, s_ref,
             dq_ref, dd_ref, dsk_ref):
        i = pl.program_id(0)
        kcat = jnp.concatenate([kp_ref[...], kc_ref[...], kn_ref[...]], axis=0)
        vcat = jnp.concatenate([vp_ref[...], vc_ref[...], vn_ref[...]], axis=0)
        valid = _band_mask_q(i, t_len)
        dqs, dsks = [], []
        for g in range(2):
            qs = _stack_heads(q_ref[:, 256 * g:256 * g + 256])
            dos = _stack_heads(do_ref[:, 256 * g:256 * g + 256])
            kk = kcat[:, 128 * g:128 * g + 128]
            s = jnp.where(valid, _dot_nt(qs, kk) * ATT_SCALE, NEG)
            lse_b = lse_ref[0, g]
            p = jnp.exp(s - _tile3(lse_b))
            dp = _dot_nt(dos, vcat[:, 128 * g:128 * g + 128])
            drow = jnp.sum(p * dp, axis=-1, keepdims=True)
            ds = _bf(p * (dp - drow) * ATT_SCALE)
            dqs.append(_unstack_heads(_dot(ds, kk)))
            d_b = jnp.broadcast_to(drow, (4 * BLK, 128))
            dd_ref[0, g] = d_b
            contrib = -(jnp.exp(_sink_col(sk_ref, g) - lse_b) * d_b)
            dsks.append(jnp.sum(contrib.reshape(4, BLK, 128), axis=1))
        dq = jnp.concatenate(dqs, axis=1)
        dq_ref[...] = _bf(_rope(dq, c_ref[...], s_ref[...], -1.0))
        dsk_ref[0] = jnp.concatenate(dsks, axis=0)

    stat = pl.BlockSpec((1, 2, 4 * BLK, 128), lambda i: (i, 0, 0, 0))
    return _call(
        body, rides=rides, name="attn_bwd_q", grid=(nb,),
        in_specs=[_rows(BLK, 512)] + _nbr_specs(256, nb) + _nbr_specs(256, nb) + [_rows(BLK, 512), stat, _res((8, 128)),
                                                                                 _rows(BLK, 128), _rows(BLK, 128)],
        out_specs=[_rows(BLK, 512), stat, pl.BlockSpec((1, 8, 128), lambda i: (i, 0, 0))],
        out_shape=(jax.ShapeDtypeStruct((t_len, 512), BF16), jax.ShapeDtypeStruct((nb, 2, 4 * BLK, 128), F32),
                   jax.ShapeDtypeStruct((nb, 8, 128), F32)),
        compiler_params=_cp(1),
    )(q, kd, kd, kd, vd, vd, vd, dya, lse, sink_b, cos_t, sin_t)


def _attn_bwd_kv(q, kd, vd, dya, lse, dd, cos_t, sin_t, rides=()):
    t_len = q.shape[0]
    nb = t_len // BLK

    def body(kc_ref, vc_ref, qp_ref, qc_ref, qn_ref, dop_ref, doc_ref, don_ref, lp_ref, lc_ref, ln_ref,
             dp_ref, dc_ref, dn_ref, c_ref, s_ref, dk_ref, dv_ref):
        j = pl.program_id(0)
        row = lax.broadcasted_iota(jnp.int32, (12 * BLK, BLK), 0)
        col = lax.broadcasted_iota(jnp.int32, (12 * BLK, BLK), 1)
        qblk = j - 1 + row // (4 * BLK)
        rel = col - (row & (BLK - 1)) + (j - qblk) * BLK
        valid = (jnp.abs(rel) <= BLK) & (qblk >= 0) & (qblk < nb)
        dks, dvs = [], []
        for g in range(2):
            cols = slice(256 * g, 256 * g + 256)
            qs = jnp.concatenate([_stack_heads(r[:, cols]) for r in (qp_ref, qc_ref, qn_ref)], axis=0)
            dos = jnp.concatenate([_stack_heads(r[:, cols]) for r in (dop_ref, doc_ref, don_ref)], axis=0)
            lse_b = jnp.concatenate([r[0, g] for r in (lp_ref, lc_ref, ln_ref)], axis=0)
            d_b = jnp.concatenate([r[0, g] for r in (dp_ref, dc_ref, dn_ref)], axis=0)
            kk = kc_ref[:, 128 * g:128 * g + 128]
            s = jnp.where(valid, _dot_nt(qs, kk) * ATT_SCALE, NEG)
            p = jnp.exp(s - lse_b)
            dp = _dot_nt(dos, vc_ref[:, 128 * g:128 * g + 128])
            ds = _bf(p * (dp - d_b) * ATT_SCALE)
            dvs.append(_dot_tn(_bf(p), dos))
            dks.append(_dot_tn(ds, qs))
        dk_ref[...] = _bf(_rope(_fold_heads(dks), c_ref[...], s_ref[...], -1.0))
        dv_ref[...] = _bf(_fold_heads(dvs))

    return _call(
        body, rides=rides, name="attn_bwd_kv", grid=(nb,),
        in_specs=[_rows(BLK, 256), _rows(BLK, 256)] + _nbr_specs(512, nb) + _nbr_specs(512, nb) + _nbr_specs4(nb) + _nbr_specs4(nb)
        + [_rows(BLK, 128), _rows(BLK, 128)],
        out_specs=[_rows(BLK, 128), _rows(BLK, 128)],
        out_shape=(jax.ShapeDtypeStruct((t_len, 128), BF16), jax.ShapeDtypeStruct((t_len, 128), BF16)),
        compiler_params=_cp(1),
    )(kd, vd, q, q, q, dya, dya, dya, lse, lse, lse, dd, dd, dd, cos_t, sin_t)


def _conv_norm_bwd(pre, dyc, g_ln, b_ln, tc):
    t_len = pre.shape[0]

    def body(pre_ref, dy_ref, g_ref, b_ref, dpre_ref, stats_ref):
        i = pl.program_id(0)
        pre_v = pre_ref[...]
        mu = jnp.mean(pre_v, axis=-1, keepdims=True)
        d = pre_v - mu
        rstd = lax.rsqrt(jnp.mean(d * d, axis=-1, keepdims=True) + EPS)
        xh = d * rstd
        g = g_ref[...]
        ln = xh * g + b_ref[...]
        sg = _sigmoid(ln)
        dln = dy_ref[...] * (sg * (1.0 + ln * (1.0 - sg)))
        dxh = dln * g
        dpre = rstd * (dxh - jnp.mean(dxh, axis=-1, keepdims=True) - xh * jnp.mean(dxh * xh, axis=-1, keepdims=True))
        dpre_ref[...] = dpre

        @pl.when(i == 0)
        def _():
            stats_ref[...] = jnp.zeros_like(stats_ref)

        stats_ref[0:1, :] += _colsum(dln * xh)
        stats_ref[1:2, :] += _colsum(dln)
        stats_ref[2:3, :] += _colsum(dpre)

    return _call(
        body, name="conv_norm_bwd", grid=(t_len // tc,),
        in_specs=[_rows(tc, CONV_CH), _rows(tc, CONV_CH), _res((1, CONV_CH)), _res((1, CONV_CH))],
        out_specs=[_rows(tc, CONV_CH), pl.BlockSpec((8, CONV_CH), lambda i: (0, 0))],
        out_shape=(jax.ShapeDtypeStruct((t_len, CONV_CH), F32), jax.ShapeDtypeStruct((8, CONV_CH), F32)),
        compiler_params=_cp(1),
    )(pre, dyc, g_ln, b_ln)


def _conv_bwd(dpre, ug, w_dw, tc, rides=()):
    t_len = ug.shape[0]
    n_tiles = t_len // tc

    def body(dp_ref, dm_ref, dn_ref, up_ref, um_ref, un_ref, w_ref, du_ref, dw_ref, dbuf, vbuf):
        i = pl.program_id(0)
        _fill_halo_buf(dbuf, dp_ref[...], dm_ref[...], dn_ref[...], i, n_tiles, tc)
        _fill_halo_buf(vbuf, _glu(up_ref[...]), _glu(um_ref[...]), _glu(un_ref[...]), i, n_tiles, tc)
        dmain = dm_ref[...]
        dv = jnp.zeros((tc, CONV_CH), F32)
        taps = []
        for k in range(CONV_W):
            dv = dv + w_ref[k:k + 1, :] * dbuf[31 - k:31 - k + tc, :]
            taps.append(_colsum(dmain * vbuf[k + 1:k + 1 + tc, :]))
        taps.append(jnp.zeros((1, CONV_CH), F32))
        um = um_ref[...]
        a, gt = um[:, 0:CONV_CH], um[:, CONV_CH:2 * CONV_CH]
        sg = _sigmoid(gt)
        du_ref[...] = _bf(jnp.concatenate([dv * sg, dv * a * (sg * (1.0 - sg))], axis=1))

        @pl.when(i == 0)
        def _():
            dw_ref[...] = jnp.zeros_like(dw_ref)

        dw_ref[...] += jnp.concatenate(taps, axis=0)

    return _call(
        body, rides=rides, name="conv_bwd", grid=(n_tiles,),
        in_specs=_halo_specs(tc, CONV_CH, t_len) + _halo_specs(tc, 1024, t_len) + [_res((32, CONV_CH))],
        out_specs=[_rows(tc, 1024), pl.BlockSpec((32, CONV_CH), lambda i: (0, 0))],
        out_shape=(jax.ShapeDtypeStruct((t_len, 1024), BF16), jax.ShapeDtypeStruct((32, CONV_CH), F32)),
        scratch_shapes=[pltpu.VMEM((tc + 32, CONV_CH), F32), pltpu.VMEM((tc + 32, CONV_CH), F32)],
        compiler_params=_cp(1),
    )(dpre, dpre, dpre, ug, ug, ug, w_dw)


def _in_proj_bwd(du_glu, dq, dk, dv, dx1, x, g_mix, w_t, tm):
    t_len = x.shape[0]
    n_ext = w_t.shape[0]

    def body(dg_ref, dq_ref, dk_ref, dv_ref, dx1_ref, x_ref, g_ref, w_ref, dx_ref, du_ref, db_ref, dgm_ref):
        i = pl.program_id(0)
        du = jnp.concatenate([dg_ref[...], dq_ref[...], dk_ref[...], dv_ref[...]], axis=1)
        du_ref[...] = du
        dh = _dot(du, w_ref[...])
        g = g_ref[...]
        _, xh, r = _rms_fwd(x_ref[...], g)
        dx_ref[...] = dx1_ref[...] + _rms_bwd(dh, xh, r, g)

        @pl.when(i == 0)
        def _():
            db_ref[...] = jnp.zeros_like(db_ref)
            dgm_ref[...] = jnp.zeros_like(dgm_ref)

        db_ref[...] += _colsum(du.astype(F32))
        dgm_ref[...] += _colsum(dh * xh)

    return _call(
        body, name="in_proj_bwd", grid=(t_len // tm,),
        in_specs=[_rows(tm, 1024), _rows(tm, 512), _rows(tm, 128), _rows(tm, 128), _rows(tm, D_MODEL), _rows(tm, D_MODEL),
                  _res((1, D_MODEL)), _res(w_t.shape)],
        out_specs=[_rows(tm, D_MODEL), _rows(tm, n_ext), pl.BlockSpec((1, n_ext), lambda i: (0, 0)),
                   pl.BlockSpec((1, D_MODEL), lambda i: (0, 0))],
        out_shape=(jax.ShapeDtypeStruct((t_len, D_MODEL), F32), jax.ShapeDtypeStruct((t_len, n_ext), BF16),
                   jax.ShapeDtypeStruct((1, n_ext), F32), jax.ShapeDtypeStruct((1, D_MODEL), F32)),
        compiler_params=_cp(1),
    )(du_glu, dq, dk, dv, dx1, x, g_mix, w_t)


def _weight_grad(a, d, name, tt):
    sa, t_len, k_dim = a.shape
    sd, _, n_dim = d.shape
    n_s = max(sa, sd)
    tn = n_dim if n_dim <= 1024 else 1024
    tt = min(tt, t_len)

    n_t = t_len // tt

    def body(a_ref, d_ref, o_ref, acc):
        t = pl.program_id(2)

        @pl.when(t == 0)
        def _():
            acc[...] = jnp.zeros_like(acc)

        acc[...] += _dot_tn(_bf(a_ref[0]), _bf(d_ref[0]))

        @pl.when(t == n_t - 1)
        def _():
            o_ref[0] = _bf(acc[...])

    return _call(
        body, name=name, grid=(n_s, n_dim // tn, n_t),
        in_specs=[pl.BlockSpec((1, tt, k_dim), (lambda s, n, t: (s, t, 0)) if sa > 1 else (lambda s, n, t: (0, t, 0))),
                  pl.BlockSpec((1, tt, tn), (lambda s, n, t: (s, t, n)) if sd > 1 else (lambda s, n, t: (0, t, n)))],
        out_specs=pl.BlockSpec((1, k_dim, tn), lambda s, n, t: (s, 0, n)),
        out_shape=jax.ShapeDtypeStruct((n_s, k_dim, n_dim), BF16),
        scratch_shapes=[pltpu.VMEM((k_dim, tn), F32)],
        compiler_params=_cp(3),
    )(a, d)


ANY = pl.BlockSpec(memory_space=pl.ANY)


def _place():
    x, y, c = lax.axis_index("x"), lax.axis_index("y"), lax.axis_index("c")
    chips = [(1 - x, y), (x, 1 - y), (1 - x, 1 - y)]
    return x, y, c, chips


def _remote(src, dst, send_sems, recv_sems, k, to):
    return pltpu.make_async_remote_copy(src_ref=src, dst_ref=dst, send_sem=send_sems.at[k], recv_sem=recv_sems.at[k],
                                        device_id=to, device_id_type=MESH)


def _cast_place(w, chip_idx, tr):
    rows, cols = w.shape
    h = rows // 2
    tr = _div_tile(h, tr)
    per = h // tr

    def body(s_ref, w_ref, o_ref):
        o_ref[0, 0] = _bf(w_ref[...])

    return _call(
        body, name="cast_place",
        grid_spec=pltpu.PrefetchScalarGridSpec(
            num_scalar_prefetch=1, grid=(2, per),
            in_specs=[pl.BlockSpec((tr, cols), lambda hh, r, s_ref: (hh * per + r, 0))],
            out_specs=pl.BlockSpec((1, 1, tr, cols), lambda hh, r, s_ref: (s_ref[0], hh, r, 0))),
        out_shape=jax.ShapeDtypeStruct((N_CHIPS, 2, h, cols), BF16),
        compiler_params=_cp(2),
    )(chip_idx, w)


def _same(arrays):
    return [jax.ShapeDtypeStruct(a.shape, a.dtype) for a in arrays]


def _gather_ride(bufs):
    n = len(bufs)

    def first_hop(outs, send, recv):
        x, y, c, chips = _place()
        mine = [outs[i].at[2 * x + y, c] for i in range(n)]
        return [_remote(mine[i], mine[i], send, recv, 3 * i + j, (cx, cy, c)) for i in range(n) for j, (cx, cy) in enumerate(chips)]

    def start(ins, outs, send, recv):
        for cp in first_hop(outs, send, recv):
            cp.start()

    def finish(ins, outs, send, recv):
        x, y, c, chips = _place()
        sib = (x, y, 1 - c)
        onward = []
        for i in range(n):
            for j, (cx, cy) in enumerate(chips):
                slab = outs[i].at[2 * cx + cy, c]
                _remote(slab, slab, send, recv, 3 * i + j, sib).wait_recv()
                onward.append(_remote(slab, slab, send, recv, 3 * n + 3 * i + j, sib))
                onward[-1].start()
        for i in range(n):
            for j, (cx, cy) in enumerate(chips):
                other = outs[i].at[2 * cx + cy, 1 - c]
                _remote(other, other, send, recv, 3 * n + 3 * i + j, sib).wait_recv()
        for cp in first_hop(outs, send, recv) + onward:
            cp.wait_send()

    return _Ride(bufs, _same(bufs), 6 * n, start, finish, aliases={i: i for i in range(n)})


def _spread_ride(buf):
    def sends(outs, send, recv):
        x, y, c, chips = _place()
        mine = outs[0].at[2 * x + y]
        return [_remote(mine, mine, send, recv, j, (cx, cy, c)) for j, (cx, cy) in enumerate(chips)]

    def start(ins, outs, send, recv):
        for cp in sends(outs, send, recv):
            cp.start()

    def finish(ins, outs, send, recv):
        _, _, c, chips = _place()
        for j, (cx, cy) in enumerate(chips):
            slab = outs[0].at[2 * cx + cy]
            _remote(slab, slab, send, recv, j, (cx, cy, c)).wait_recv()
        for cp in sends(outs, send, recv):
            cp.wait_send()

    return _Ride([buf], _same([buf]), 3, start, finish, aliases={0: 0})


def _pairwise_ride(arrays, out_shape, n_sem, copies):
    def start(ins, outs, send, recv):
        for cp in copies(ins, outs, send, recv):
            cp.start()

    def finish(ins, outs, send, recv):
        for cp in copies(ins, outs, send, recv):
            cp.wait()

    return _Ride(arrays, out_shape, n_sem, start, finish)


def _run_rides(name, rides):
    k_in = [len(r.operands) for r in rides]
    k_out = [len(r.out_shape) for r in rides]

    def body(*refs):
        pos, r_in, r_out = 0, [], []
        for k in k_in:
            r_in.append(refs[pos:pos + k])
            pos += k
        for k in k_out:
            r_out.append(refs[pos:pos + k])
            pos += k
        sems = refs[pos:]
        for j, r in enumerate(rides):
            r.start(r_in[j], r_out[j], sems[2 * j], sems[2 * j + 1])
        for j, r in enumerate(rides):
            r.finish(r_in[j], r_out[j], sems[2 * j], sems[2 * j + 1])

    aliases, off_in, off_out = {}, 0, 0
    for r, ki, ko in zip(rides, k_in, k_out):
        aliases.update({off_in + a: off_out + b for a, b in r.aliases.items()})
        off_in, off_out = off_in + ki, off_out + ko
    res = _call(
        body, name=name, in_specs=[ANY] * sum(k_in), out_specs=[ANY] * sum(k_out),
        out_shape=[s for r in rides for s in r.out_shape], input_output_aliases=aliases,
        scratch_shapes=[pltpu.SemaphoreType.DMA((r.n_sem,)) for r in rides for _ in range(2)],
    )(*[op for r in rides for op in r.operands])
    out, pos = [], 0
    for k in k_out:
        out.append(list(res[pos:pos + k]))
        pos += k
    return out


def _swap_ride(grads):
    def copies(ins, outs, send, recv):
        x, y, c, _ = _place()
        return [_remote(ins[i].at[:, 1 - c], outs[i], send, recv, i, (x, y, 1 - c)) for i in range(len(grads))]

    out_shape = [jax.ShapeDtypeStruct((g.shape[0],) + g.shape[2:], g.dtype) for g in grads]
    return _pairwise_ride(grads, out_shape, len(grads), copies)


def _pair_sum(grad, other, c_idx, tr):
    n_s, _, h, cols = grad.shape
    tr = _div_tile(h, tr)

    def body(c_ref, a_ref, b_ref, o_ref):
        o_ref[...] = _bf(a_ref[0].astype(F32) + b_ref[...].astype(F32))

    return _call(
        body, name="pair_sum",
        grid_spec=pltpu.PrefetchScalarGridSpec(
            num_scalar_prefetch=1, grid=(n_s, h // tr),
            in_specs=[pl.BlockSpec((1, 1, tr, cols), lambda s, r, c_ref: (s, c_ref[0], r, 0)),
                      pl.BlockSpec((1, tr, cols), lambda s, r, c_ref: (s, r, 0))],
            out_specs=pl.BlockSpec((1, tr, cols), lambda s, r, c_ref: (s, r, 0))),
        out_shape=jax.ShapeDtypeStruct((n_s, h, cols), BF16),
        compiler_params=_cp(2),
    )(c_idx, grad, other)


def _exchange_ride(sums):
    def copies(ins, outs, send, recv):
        _, _, c, chips = _place()
        return [_remote(ins[i].at[2 * cx + cy], outs[i].at[j], send, recv, 3 * i + j, (cx, cy, c))
                for i in range(len(sums)) for j, (cx, cy) in enumerate(chips)]

    out_shape = [jax.ShapeDtypeStruct((3,) + s.shape[1:], s.dtype) for s in sums]
    return _pairwise_ride(sums, out_shape, 3 * len(sums), copies)


def _chip_sum(own, others, chip_idx, tr):
    _, h, cols = own.shape
    tr = _div_tile(h, tr)

    def body(s_ref, a_ref, p_ref, o_ref):
        acc = a_ref[0].astype(F32)
        for j in range(N_CHIPS - 1):
            acc = acc + p_ref[j].astype(F32)
        o_ref[...] = acc

    return _call(
        body, name="chip_sum",
        grid_spec=pltpu.PrefetchScalarGridSpec(
            num_scalar_prefetch=1, grid=(h // tr,),
            in_specs=[pl.BlockSpec((1, tr, cols), lambda r, s_ref: (s_ref[0], r, 0)),
                      pl.BlockSpec((N_CHIPS - 1, tr, cols), lambda r, s_ref: (0, r, 0))],
            out_specs=pl.BlockSpec((tr, cols), lambda r, s_ref: (r, 0))),
        out_shape=jax.ShapeDtypeStruct((h, cols), F32),
        compiler_params=_cp(1),
    )(chip_idx, own, others)


def _share_ride(halves):
    def copies(ins, outs, send, recv):
        x, y, c, _ = _place()
        return [_remote(ins[i], outs[i], send, recv, i, (x, y, 1 - c)) for i in range(len(halves))]

    return _pairwise_ride(halves, _same(halves), len(halves), copies)


def _small_allreduce(pack):
    rows, cols = pack.shape

    def body(p_ref, o_ref, buf, send_sems, recv_sems):
        x, y, c, _ = _place()
        me = 4 * x + 2 * y + c
        buf[me] = p_ref[...]
        sent = []
        for k in range(1, N_DEV):
            fx, fy, fc = (k >> 2) & 1, (k >> 1) & 1, k & 1
            to = (x ^ fx, y ^ fy, c ^ fc)
            cp = _remote(p_ref, buf.at[me], send_sems, recv_sems, k - 1, to)
            cp.start()
            sent.append(cp)
        for k in range(1, N_DEV):
            fx, fy, fc = (k >> 2) & 1, (k >> 1) & 1, k & 1
            frm = 4 * (x ^ fx) + 2 * (y ^ fy) + (c ^ fc)
            _remote(p_ref, buf.at[frm], send_sems, recv_sems, k - 1, (x, y, c)).wait_recv()
        for cp in sent:
            cp.wait_send()
        acc = buf[0]
        for d in range(1, N_DEV):
            acc = acc + buf[d]
        o_ref[...] = acc

    return _call(
        body, name="small_allreduce",
        in_specs=[pl.BlockSpec(memory_space=pltpu.VMEM)], out_specs=pl.BlockSpec(memory_space=pltpu.VMEM),
        out_shape=jax.ShapeDtypeStruct(pack.shape, F32),
        scratch_shapes=[pltpu.VMEM((N_DEV, rows, cols), F32), pltpu.SemaphoreType.DMA((N_DEV - 1,)),
                        pltpu.SemaphoreType.DMA((N_DEV - 1,))],
    )(pack)


def _adamw_math(w, g, m, v):
    m_new = ADAM_B1 * m + (1.0 - ADAM_B1) * g
    v_new = ADAM_B2 * v + (1.0 - ADAM_B2) * (g * g)
    m_hat = m_new * (1.0 / (1.0 - ADAM_B1 ** ADAM_STEP))
    v_hat = v_new * (1.0 / (1.0 - ADAM_B2 ** ADAM_STEP))
    delta = -ADAM_LR * (m_hat / (jnp.sqrt(v_hat) + ADAM_EPS) + ADAM_WD * w)
    return delta, m_new, v_new


def _adamw(w, g_mine, g_other, m, v, core_idx, tr):
    rows, cols = w.shape
    h = rows // 2
    tr = _div_tile(h, tr)
    per = h // tr

    def body(c_ref, w_ref, ga_ref, gb_ref, m_ref, v_ref, g_ref, d_ref, mo_ref, vo_ref):
        g = jnp.where(pl.program_id(0) == c_ref[0], ga_ref[...], gb_ref[...])
        d, mn, vn = _adamw_math(w_ref[...], g, m_ref[...], v_ref[...])
        g_ref[...] = g
        d_ref[...] = d
        mo_ref[...] = mn
        vo_ref[...] = vn

    full = pl.BlockSpec((tr, cols), lambda hh, r, c_ref: (hh * per + r, 0))
    half = pl.BlockSpec((tr, cols), lambda hh, r, c_ref: (r, 0))
    shp = jax.ShapeDtypeStruct(w.shape, F32)
    return _call(
        body, name="adamw",
        grid_spec=pltpu.PrefetchScalarGridSpec(num_scalar_prefetch=1, grid=(2, per), in_specs=[full, half, half, full, full],
                                               out_specs=[full] * 4),
        out_shape=(shp, shp, shp, shp), compiler_params=_cp(2))(core_idx, w, g_mine, g_other, m, v)


def _adamw_small(ws, gs, ms, vs):
    n = len(ws)

    def body(*refs):
        w_r, g_r, m_r, v_r = refs[0:n], refs[n:2 * n], refs[2 * n:3 * n], refs[3 * n:4 * n]
        d_o, m_o, v_o = refs[4 * n:5 * n], refs[5 * n:6 * n], refs[6 * n:7 * n]
        for i in range(n):
            d, mn, vn = _adamw_math(w_r[i][...], g_r[i][...], m_r[i][...], v_r[i][...])
            d_o[i][...] = d
            m_o[i][...] = mn
            v_o[i][...] = vn

    shp = [jax.ShapeDtypeStruct(w.shape, F32) for w in ws]
    outs = _call(body, name="adamw_small", out_shape=shp * 3)(*ws, *gs, *ms, *vs)
    return outs[0:n], outs[n:2 * n], outs[2 * n:3 * n]


def _rope_tables(t_len):
    pos = jnp.arange(t_len, dtype=F32)
    inv_freq = ROPE_THETA ** (-jnp.arange(0, HEAD_DIM, 2, dtype=F32) / HEAD_DIM)
    ang = pos[:, None] * inv_freq[None, :]
    cos, sin = jnp.cos(ang), jnp.sin(ang)
    return jnp.tile(jnp.concatenate([cos, cos], axis=1), (1, 2)), jnp.tile(jnp.concatenate([-sin, sin], axis=1), (1, 2))


def _dup_heads(a):
    h0, h1 = a[..., 0:64], a[..., 64:128]
    return jnp.concatenate([h0, h0, h1, h1], axis=-1)


def _local_step(x, mem, target, small, wg, comm, tm_a=512, tm_b=256, tc=512, tt=1024):
    def run(stage, fn, n_own, *operands):
        rides = comm.rides(stage)
        res = list(fn(*operands, rides=rides))
        brought, pos = [], n_own
        for r in rides:
            brought.append(res[pos:pos + len(r.out_shape)])
            pos += len(r.out_shape)
        comm.landed(stage, brought, wg)
        return res[:n_own]

    t_len = x.shape[0]
    cos_t, sin_t = _rope_tables(t_len)
    b_in = small["b_in"]
    b_ext = jnp.concatenate([b_in[:, 0:1536], _dup_heads(b_in[:, 1536:1664]), _dup_heads(b_in[:, 1664:1792])], axis=1)
    w_dw = jnp.concatenate([wg["w_dw"], jnp.zeros((1, CONV_CH), F32)], axis=0)
    sink_b = jnp.broadcast_to(small["attn_sink"].reshape(8, 1), (8, 128))

    ug, q, kd, vd, h1 = run("in_proj_fwd", _in_proj_fwd, 5, x, small["g_mix"], wg["w_in"], b_ext, cos_t, sin_t, tm_a)
    memn, kv = _mem_kv_fwd(mem, small["g_mem_kv"], wg["w_mem_kv"])
    yc, pre = run("conv_fwd", _conv_fwd, 2, ug, w_dw, small["b_dw"], small["g_conv_ln"], small["b_conv_ln"], tc)
    ya, lse = run("attn_fwd", _attn_fwd, 2, q, kd, vd, sink_b)
    ymix, x1, hq, qm, om, x2 = run("mix_mem_fwd", _mix_mem_fwd, 6, x, yc, ya, wg["w_out"], small["b_out"], small["g_mem_q"],
                                   wg["w_mem_q"], kv, wg["w_mem_o"], tm_a)
    hf, gate, up, act, dx3, loss, d_g_final = _ffn_loss(x2, small["g_ffn"], wg["w_gate"], wg["w_up"], wg["w_down"],
                                                        small["g_final"], target, tm_b)

    dx2, dgate, dup, d_g_ffn = _ffn_bwd(dx3, x2, gate, up, small["g_ffn"], wg["w_gate"], wg["w_up"], wg["w_down"], tm_b)
    comm.grad("w_gate", _weight_grad(dgate, hf[None], "dw_gate", tt))
    comm.grad("w_up", _weight_grad(dup, hf[None], "dw_up", tt))
    comm.grad("w_down", _weight_grad(act, dx3[None], "dw_down", tt))
    dx1, dqm, dyc, dya, dkv, d_g_mem_q, d_b_out = run("mix_mem_bwd", _mix_mem_bwd, 7, dx2, x1, qm, kv, small["g_mem_q"],
                                                      wg["w_mem_q"], wg["w_mem_o"], wg["w_out"], tm_b)
    d_w_mem_kv, d_g_mem_kv = _mem_kv_bwd(dkv, memn, mem, small["g_mem_kv"], wg["w_mem_kv"])
    comm.grad("w_mem_kv", d_w_mem_kv)
    comm.grad("w_out", _weight_grad(ymix[None], dx1[None], "dw_out", tt)[0])
    comm.grad("w_mem_q", _weight_grad(hq[None], dqm[None], "dw_mem_q", tt)[0])
    comm.grad("w_mem_o", _weight_grad(om[None], dx2[None], "dw_mem_o", tt)[0])
    dq, dd, dsink = run("attn_bwd_q", _attn_bwd_q, 3, q, kd, vd, dya, lse, sink_b, cos_t, sin_t)
    dk, dv = run("attn_bwd_kv", _attn_bwd_kv, 2, q, kd, vd, dya, lse, dd, cos_t, sin_t)
    dpre, cstats = _conv_norm_bwd(pre, dyc, small["g_conv_ln"], small["b_conv_ln"], tc)
    du_glu, d_w_dw = run("conv_bwd", _conv_bwd, 2, dpre, ug, w_dw, tc)
    grad_x, du, d_b_in, d_g_mix = _in_proj_bwd(du_glu, dq, dk, dv, dx1, x, small["g_mix"], wg["w_in"], tm_a)
    comm.grad("w_in", _weight_grad(du[None], h1[None], "dw_in", tt)[0])

    grads = {
        "w_dw": d_w_dw[0:CONV_W],
        "g_mix": d_g_mix, "b_in": d_b_in, "b_dw": cstats[2:3], "g_conv_ln": cstats[0:1],
        "b_conv_ln": cstats[1:2], "attn_sink": jnp.sum(dsink[:, :, 0], axis=0)[None, :], "b_out": d_b_out,
        "g_mem_q": d_g_mem_q, "g_mem_kv": d_g_mem_kv, "g_ffn": d_g_ffn, "g_final": d_g_final,
    }
    return loss[0:1, 0:1], grad_x, grads


BIG = ["w_in", "w_out", "w_mem_q", "w_mem_kv", "w_mem_o", "w_gate", "w_up", "w_down"]
KEEP_SLABS = ("w_mem_kv", "w_gate", "w_up", "w_down")
TRANSPOSED = ("w_in", "w_gate", "w_up")
SMALL = ["g_mix", "b_in", "b_dw", "g_conv_ln", "b_conv_ln", "attn_sink", "b_out", "g_mem_q", "g_mem_kv", "g_ffn", "g_final"]
PACK_ROWS = 32

GATHER_ON = {"in_proj_fwd": ("w_mem_kv", "w_out"), "conv_fwd": ("w_mem_q", "w_mem_o", "w_gate"), "attn_fwd": ("w_up",),
             "mix_mem_fwd": ("w_down",)}
FFN_GROUP = ("w_gate", "w_up", "w_down")
MID_GROUP = ("w_mem_kv", "w_out", "w_mem_q", "w_mem_o")


def _as_weight(name, gathered):
    g = gathered.reshape(N_CHIPS, gathered.shape[2] * 2, gathered.shape[3])
    return g if name in KEEP_SLABS else g.reshape(-1, g.shape[2])


class _Overlap:
    def __init__(self, bufs, chip_idx, core_idx):
        self.bufs, self.chip_idx, self.core_idx = bufs, chip_idx, core_idx
        self.parts, self.sums, self.others = {}, {}, {}

    def rides(self, stage):
        if stage in GATHER_ON:
            return [_gather_ride([self.bufs[k] for k in GATHER_ON[stage]])]
        if stage == "mix_mem_bwd":
            return [_swap_ride([self.parts[k] for k in FFN_GROUP])]
        if stage == "attn_bwd_q":
            return [_exchange_ride([self.sums["w_gate"]]), _swap_ride([self.parts[k] for k in MID_GROUP])]
        if stage == "attn_bwd_kv":
            return [_exchange_ride([self.sums["w_up"], self.sums["w_down"]])]
        if stage == "conv_bwd":
            return [_exchange_ride([self.sums[k] for k in MID_GROUP])]
        return ()

    def landed(self, stage, brought, wg):
        if stage in GATHER_ON:
            for k, g in zip(GATHER_ON[stage], brought[0]):
                wg[k] = _as_weight(k, g)
        elif stage == "mix_mem_bwd":
            self._pair(FFN_GROUP, brought[0])
        elif stage == "attn_bwd_q":
            self.others["w_gate"] = brought[0][0]
            self._pair(MID_GROUP, brought[1])
        elif stage == "attn_bwd_kv":
            self.others["w_up"], self.others["w_down"] = brought[0]
        elif stage == "conv_bwd":
            self.others.update(zip(MID_GROUP, brought[0]))

    def grad(self, name, g):
        if g.ndim == 2:
            g = g.reshape(N_CHIPS, g.shape[0] // N_CHIPS, g.shape[1])
        self.parts[name] = g.reshape(N_CHIPS, 2, g.shape[1] // 2, g.shape[2])

    def _pair(self, names, from_sibling):
        for k, o in zip(names, from_sibling):
            self.sums[k] = _pair_sum(self.parts[k], o, self.core_idx, 256)

    def finish(self):
        (from_sibling,) = _run_rides("swap_last", [_swap_ride([self.parts["w_in"]])])
        self._pair(("w_in",), from_sibling)
        ((self.others["w_in"],),) = _run_rides("exchange_last", [_exchange_ride([self.sums["w_in"]])])
        mine = [_chip_sum(self.sums[k], self.others[k], self.chip_idx, 256) for k in BIG]
        (theirs,) = _run_rides("sibling_share", [_share_ride(mine)])
        return mine, theirs


def _pack_small(loss, grads):
    def row(a):
        a = a.reshape(1, -1)
        return jnp.pad(a, ((0, 0), (0, 1024 - a.shape[1])))

    rows = [row(grads[k]) for k in ("g_mix", "b_out", "g_mem_q", "g_mem_kv", "g_ffn", "g_final")]
    rows += [grads["b_in"][:, 0:1024], row(grads["b_in"][:, 1024:1792])]
    rows += [jnp.concatenate([grads["b_dw"], grads["g_conv_ln"]], axis=1), row(grads["b_conv_ln"]), row(grads["attn_sink"]),
             row(loss)]
    dw = jnp.pad(grads["w_dw"], ((0, 1), (0, 0))).reshape(16, 1024)
    pack = jnp.concatenate(rows + [dw], axis=0)
    return jnp.pad(pack, ((0, PACK_ROWS - pack.shape[0]), (0, 0)))


def _unpack_small(pack):
    out = {k: pack[i:i + 1] for i, k in enumerate(("g_mix", "b_out", "g_mem_q", "g_mem_kv", "g_ffn", "g_final"))}
    out["b_in"] = jnp.concatenate([pack[6:7], pack[7:8, 0:768]], axis=1)
    out["b_dw"], out["g_conv_ln"] = pack[8:9, 0:512], pack[8:9, 512:1024]
    out["b_conv_ln"] = pack[9:10, 0:512]
    out["attn_sink"] = pack[10:11, 0:8]
    loss = pack[11, 0]
    dw = pack[12:28].reshape(32, 512)[0:CONV_W]
    return loss, out, dw


def kernel(x, mem, g_mix, w_in, b_in, w_dw, b_dw, g_conv_ln, b_conv_ln, attn_sink, w_out, b_out, g_mem_q, g_mem_kv, w_mem_q, w_mem_kv, w_mem_o, g_ffn, w_gate, w_up, w_down, g_final, loss_target, m_g_mix, m_w_in, m_b_in, m_w_dw, m_b_dw, m_g_conv_ln, m_b_conv_ln, m_attn_sink, m_w_out, m_b_out, m_g_mem_q, m_g_mem_kv, m_w_mem_q, m_w_mem_kv, m_w_mem_o, m_g_ffn, m_w_gate, m_w_up, m_w_down, m_g_final, v_g_mix, v_w_in, v_b_in, v_w_dw, v_b_dw, v_g_conv_ln, v_b_conv_ln, v_attn_sink, v_w_out, v_b_out, v_g_mem_q, v_g_mem_kv, v_w_mem_q, v_w_mem_kv, v_w_mem_o, v_g_ffn, v_w_gate, v_w_up, v_w_down, v_g_final):
    args = dict(locals())
    weight_names = ["g_mix", "w_in", "b_in", "w_dw", "b_dw", "g_conv_ln", "b_conv_ln", "attn_sink", "w_out", "b_out", "g_mem_q",
                    "g_mem_kv", "w_mem_q", "w_mem_kv", "w_mem_o", "g_ffn", "w_gate", "w_up", "w_down", "g_final"]
    chip = 2 * lax.axis_index("x") + lax.axis_index("y")
    core = lax.axis_index("c")

    chip_idx = chip.astype(jnp.int32).reshape(1)
    core_idx = core.astype(jnp.int32).reshape(1)

    def block(name):
        a = args[name][0]
        weight = name[2:] if name[:2] in ("m_", "v_") else name
        return a.T if weight in TRANSPOSED else a

    comm = _Overlap({k: _cast_place(block(k), chip_idx, 256) for k in BIG}, chip_idx, core_idx)
    dw_buf = lax.dynamic_update_slice(jnp.zeros((N_CHIPS, CONV_W, 128), F32), w_dw, (chip, 0, 0))
    (first,), (dw_all,) = _run_rides("gather_first", [_gather_ride([comm.bufs["w_in"]]), _spread_ride(dw_buf)])
    wg = {"w_in": _as_weight("w_in", first), "w_dw": jnp.transpose(dw_all, (1, 0, 2)).reshape(CONV_W, CONV_CH)}
    small = {k: args[k].reshape(1, -1) for k in SMALL}

    loss, grad_x, grads = _local_step(x[0], mem[0], loss_target[0], small, wg, comm)

    halves, other_halves = comm.finish()

    loss_sum, small_grads, dw_full = _unpack_small(_small_allreduce(_pack_small(loss, grads)))
    dw_cols = jnp.transpose(dw_full.reshape(CONV_W, N_CHIPS, 128), (1, 0, 2))
    small_grads["w_dw"] = lax.dynamic_index_in_dim(dw_cols, chip, axis=0, keepdims=False)

    out_g, out_d, out_m, out_v = {}, {}, {}, {}
    for k, g_mine, g_other in zip(BIG, halves, other_halves):
        res = _adamw(block(k), g_mine, g_other, block("m_" + k), block("v_" + k), core_idx, 256)
        out_g[k], out_d[k], out_m[k], out_v[k] = [(r.T if k in TRANSPOSED else r)[None] for r in res]
    names = SMALL + ["w_dw"]

    def flat(a):
        return a[0] if a.ndim == 3 else a.reshape(1, -1)

    def pad_lanes(a):
        return jnp.pad(a, ((0, 0), (0, 128 - a.shape[1]))) if a.shape[1] < 128 else a

    ws = [flat(args[k]) for k in names]
    gs = [small_grads[k] for k in names]
    ms = [flat(args["m_" + k]) for k in names]
    vs = [flat(args["v_" + k]) for k in names]
    ds, mns, vns = _adamw_small([pad_lanes(a) for a in ws], [pad_lanes(a) for a in gs], [pad_lanes(a) for a in ms],
                                [pad_lanes(a) for a in vs])
    for i, k in enumerate(names):
        n_lanes = ws[i].shape[1]
        for out, val in ((out_g, gs[i]), (out_d, ds[i]), (out_m, mns[i]), (out_v, vns[i])):
            out[k] = val[:, 0:n_lanes].reshape(args[k].shape)

    return (loss_sum, grad_x[None], *[out_g[k] for k in weight_names], *[out_d[k] for k in weight_names],
            *[out_m[k] for k in weight_names], *[out_v[k] for k in weight_names])
```

```python
import jax
import jax.numpy as jnp
import numpy as np
from jax import lax
from jax.experimental import pallas as pl
from jax.experimental.pallas import tpu as pltpu

F32 = jnp.float32
BF16 = jnp.bfloat16
EPS = 1e-6
NEG = -1e30

D_MODEL = 1024
CONV_CH = 512
CONV_W = 31
HEAD_DIM = 64
BLK = 128
MEM_HEADS = 4
MEM_HD = 256
N_CHIPS = 4
N_DEV = 8
ATT_SCALE = HEAD_DIM ** -0.5
MEM_SCALE = MEM_HD ** -0.5
ROPE_THETA = 10000.0

ADAM_LR = 0.001
ADAM_B1 = 0.9
ADAM_B2 = 0.999
ADAM_EPS = 1e-08
ADAM_WD = 0.01
ADAM_STEP = 10

VMEM_LIMIT_BYTES = 56 * 1024 * 1024
MESH = pl.DeviceIdType.MESH


class _Ride:
    def __init__(self, operands, out_shape, n_sem, start, finish, aliases=None):
        self.operands, self.out_shape, self.n_sem = list(operands), list(out_shape), n_sem
        self.start, self.finish, self.aliases = start, finish, dict(aliases or {})


def _call(body, rides=(), **kw):
    if not rides:
        return pl.pallas_call(body, **kw)
    (n_steps,) = kw["grid"]
    n_in, n_out = len(kw["in_specs"]), len(kw["out_specs"])
    scratch = list(kw.get("scratch_shapes", ()))
    k_in = [len(r.operands) for r in rides]
    k_out = [len(r.out_shape) for r in rides]

    def carried(*refs):
        pos = n_in
        r_in, r_out = [], []
        for k in k_in:
            r_in.append(refs[pos:pos + k])
            pos += k
        own_out = refs[pos:pos + n_out]
        pos += n_out
        for k in k_out:
            r_out.append(refs[pos:pos + k])
            pos += k
        own_scratch = refs[pos:pos + len(scratch)]
        sems = refs[pos + len(scratch):]
        step = pl.program_id(0)

        @pl.when(step == 0)
        def _():
            for j, r in enumerate(rides):
                r.start(r_in[j], r_out[j], sems[2 * j], sems[2 * j + 1])

        body(*refs[:n_in], *own_out, *own_scratch)

        @pl.when(step == n_steps - 1)
        def _():
            for j, r in enumerate(rides):
                r.finish(r_in[j], r_out[j], sems[2 * j], sems[2 * j + 1])

    kw = dict(kw)
    kw["in_specs"] = list(kw["in_specs"]) + [ANY] * sum(k_in)
    kw["out_specs"] = list(kw["out_specs"]) + [ANY] * sum(k_out)
    kw["out_shape"] = list(kw["out_shape"]) + [s for r in rides for s in r.out_shape]
    kw["scratch_shapes"] = scratch + [pltpu.SemaphoreType.DMA((r.n_sem,)) for r in rides for _ in range(2)]
    aliases, off_in, off_out = {}, n_in, n_out
    for r, ki, ko in zip(rides, k_in, k_out):
        aliases.update({off_in + a: off_out + b for a, b in r.aliases.items()})
        off_in, off_out = off_in + ki, off_out + ko
    if aliases:
        kw["input_output_aliases"] = aliases
    call = pl.pallas_call(carried, **kw)
    return lambda *args: call(*args, *[op for r in rides for op in r.operands])


def _cp(n_grid):
    return pltpu.CompilerParams(dimension_semantics=("arbitrary",) * n_grid, vmem_limit_bytes=VMEM_LIMIT_BYTES)


def _res(shape):
    nd = len(shape)
    return pl.BlockSpec(shape, lambda *_: (0,) * nd, pipeline_mode=pl.Buffered(1))


def _rows(tm, n):
    return pl.BlockSpec((tm, n), lambda i: (i, 0))


def _div_tile(n, target):
    best = None
    for d in range(16, min(n, target) + 1, 16):
        if n % d == 0:
            best = d
    assert best is not None, (n, target)
    return best


def _dot(a, b):
    return jnp.dot(a, b, preferred_element_type=F32)


def _dot_nt(a, b):
    return lax.dot_general(a, b, (((1,), (1,)), ((), ())), preferred_element_type=F32)


def _dot_tn(a, b):
    return lax.dot_general(a, b, (((0,), (0,)), ((), ())), preferred_element_type=F32)


def _bf(x):
    return x.astype(BF16)


def _sigmoid(x):
    return 1.0 / (1.0 + jnp.exp(-x))


def _rms_fwd(x, g):
    r = lax.rsqrt(jnp.mean(x * x, axis=-1, keepdims=True) + EPS)
    xh = x * r
    return xh * g, xh, r


def _rms_bwd(dh, xh, r, g):
    dxh = dh * g
    return r * (dxh - xh * jnp.mean(dxh * xh, axis=-1, keepdims=True))


def _colsum(x):
    return jnp.sum(x, axis=0, keepdims=True)


def _rope(x, cos, sin, sign):
    n = x.shape[1] // 128
    c = jnp.tile(cos, (1, n)) if n > 1 else cos
    s = jnp.tile(sin, (1, n)) if n > 1 else sin
    lane = lax.broadcasted_iota(jnp.int32, x.shape, 1)
    first = (lane & 63) < 32
    partner = jnp.where(first, pltpu.roll(x, x.shape[1] - 32, 1), pltpu.roll(x, 32, 1))
    return x * c + sign * (partner * s)


def _lo_lanes(shape):
    return lax.broadcasted_iota(jnp.int32, shape, 1) < 64


def _stack_heads(t):
    t0, t1 = t[:, 0:128], t[:, 128:256]
    lo = _lo_lanes(t0.shape)
    z = jnp.zeros_like(t0)
    return jnp.concatenate([jnp.where(lo, t0, z), jnp.where(lo, z, t0), jnp.where(lo, t1, z), jnp.where(lo, z, t1)], axis=0)


def _unstack_heads(o):
    lo = _lo_lanes((BLK, 128))
    return jnp.concatenate([jnp.where(lo, o[0:128], o[128:256]), jnp.where(lo, o[256:384], o[384:512])], axis=1)


def _fold_heads(parts):
    a, b = (p + pltpu.roll(p, 64, 1) for p in parts)
    return jnp.where(_lo_lanes(a.shape), a, b)


def _sink_col(sk_ref, g):
    return jnp.concatenate([jnp.broadcast_to(sk_ref[4 * g + h:4 * g + h + 1, :], (BLK, 128)) for h in range(4)], axis=0)


def _tile3(x):
    return jnp.concatenate([x, x, x], axis=1)


def _mem_kv_fwd(mem, g_kv, w_kv):
    m_len = mem.shape[0]
    cols = w_kv.shape[2]

    def body(mem_ref, g_ref, w_ref, memn_ref, kv_ref):
        h, _, _ = _rms_fwd(mem_ref[...], g_ref[...])
        hb = _bf(h)
        memn_ref[...] = hb
        for s in range(N_CHIPS):
            kv_ref[s] = _bf(_dot(hb, w_ref[s]))

    return _call(
        body, name="mem_kv_fwd",
        out_shape=(jax.ShapeDtypeStruct((m_len, D_MODEL), BF16), jax.ShapeDtypeStruct((N_CHIPS, m_len, cols), BF16)),
        compiler_params=pltpu.CompilerParams(vmem_limit_bytes=VMEM_LIMIT_BYTES),
    )(mem, g_kv, w_kv)


def _dup_head_rows(w_ref, lo):
    h0, h1 = w_ref[lo:lo + 64, :], w_ref[lo + 64:lo + 128, :]
    return jnp.concatenate([h0, h0, h1, h1], axis=0)


def _in_proj_fwd(x, g_mix, w_t, b_ext, cos_t, sin_t, tm, rides=()):
    t_len = x.shape[0]

    def body(x_ref, g_ref, w_ref, b_ref, c_ref, s_ref, ug_ref, q_ref, k_ref, v_ref, h_ref):
        h, _, _ = _rms_fwd(x_ref[...], g_ref[...])
        hb = _bf(h)
        h_ref[...] = hb
        ug_ref[...] = _dot_nt(hb, w_ref[0:1024, :]) + b_ref[:, 0:1024]
        c, s = c_ref[...], s_ref[...]
        q_ref[...] = _bf(_rope(_dot_nt(hb, w_ref[1024:1536, :]) + b_ref[:, 1024:1536], c, s, 1.0))
        k_ref[...] = _bf(_rope(_dot_nt(hb, _dup_head_rows(w_ref, 1536)) + b_ref[:, 1536:1792], c, s, 1.0))
        v_ref[...] = _bf(_dot_nt(hb, _dup_head_rows(w_ref, 1664)) + b_ref[:, 1792:2048])

    return _call(
        body, rides=rides, name="in_proj_fwd", grid=(t_len // tm,),
        in_specs=[_rows(tm, D_MODEL), _res((1, D_MODEL)), _res(w_t.shape), _res(b_ext.shape), _rows(tm, 128), _rows(tm, 128)],
        out_specs=[_rows(tm, 1024), _rows(tm, 512), _rows(tm, 256), _rows(tm, 256), _rows(tm, D_MODEL)],
        out_shape=(jax.ShapeDtypeStruct((t_len, 1024), F32), jax.ShapeDtypeStruct((t_len, 512), BF16),
                   jax.ShapeDtypeStruct((t_len, 256), BF16), jax.ShapeDtypeStruct((t_len, 256), BF16),
                   jax.ShapeDtypeStruct((t_len, D_MODEL), BF16)),
        compiler_params=_cp(1),
    )(x, g_mix, w_t, b_ext, cos_t, sin_t)


def _halo_specs(tc, n, t_len):
    per = tc // 16
    last = t_len // 16 - 1
    return [pl.BlockSpec((16, n), lambda i: (jnp.maximum(i * per - 1, 0), 0)),
            pl.BlockSpec((tc, n), lambda i: (i, 0)),
            pl.BlockSpec((16, n), lambda i: (jnp.minimum((i + 1) * per, last), 0))]


def _glu(z):
    return z[:, 0:CONV_CH] * _sigmoid(z[:, CONV_CH:2 * CONV_CH])


def _fill_halo_buf(buf, prev, main, nxt, i, n_tiles, tc):
    buf[0:16, :] = jnp.where(i > 0, prev, jnp.zeros_like(prev))
    buf[16:16 + tc, :] = main
    buf[16 + tc:32 + tc, :] = jnp.where(i < n_tiles - 1, nxt, jnp.zeros_like(nxt))


CONV_ROWS = 64


def _shift_copies(buf, shifted, tc):
    for r in range(1, 8):
        shifted[r - 1, :, :] = buf[r:r + tc + 24, :]


def _shifted_rows(buf, shifted, offset, base):
    src = buf if offset % 8 == 0 else shifted.at[offset % 8 - 1]
    return src[pl.ds(pl.multiple_of(base + 8 * (offset // 8), 8), CONV_ROWS), :]


def _conv_fwd(ug, w_dw, b_dw, g_ln, b_ln, tc, rides=()):
    t_len = ug.shape[0]
    n_tiles = t_len // tc

    def body(up_ref, um_ref, un_ref, w_ref, bdw_ref, g_ref, b_ref, y_ref, pre_ref, buf, shifted):
        i = pl.program_id(0)
        _fill_halo_buf(buf, _glu(up_ref[...]), _glu(um_ref[...]), _glu(un_ref[...]), i, n_tiles, tc)
        _shift_copies(buf, shifted, tc)

        def chunk(c, carry):
            base = c * CONV_ROWS
            acc = jnp.zeros((CONV_ROWS, CONV_CH), F32)
            for k in range(CONV_W):
                acc = acc + w_ref[k:k + 1, :] * _shifted_rows(buf, shifted, k + 1, base)
            pre_ref[pl.ds(pl.multiple_of(base, CONV_ROWS), CONV_ROWS), :] = acc + bdw_ref[...]
            return carry

        lax.fori_loop(0, tc // CONV_ROWS, chunk, 0)
        pre = pre_ref[...]
        mu = jnp.mean(pre, axis=-1, keepdims=True)
        d = pre - mu
        rstd = lax.rsqrt(jnp.mean(d * d, axis=-1, keepdims=True) + EPS)
        ln = d * rstd * g_ref[...] + b_ref[...]
        y_ref[...] = _bf(ln * _sigmoid(ln))

    return _call(
        body, rides=rides, name="conv_fwd", grid=(n_tiles,),
        in_specs=_halo_specs(tc, 1024, t_len) + [_res((32, CONV_CH)), _res((1, CONV_CH)), _res((1, CONV_CH)), _res((1, CONV_CH))],
        out_specs=[_rows(tc, CONV_CH), _rows(tc, CONV_CH)],
        out_shape=(jax.ShapeDtypeStruct((t_len, CONV_CH), BF16), jax.ShapeDtypeStruct((t_len, CONV_CH), F32)),
        scratch_shapes=[pltpu.VMEM((tc + 32, CONV_CH), F32), pltpu.VMEM((7, tc + 24, CONV_CH), F32)],
        compiler_params=_cp(1),
    )(ug, ug, ug, w_dw, b_dw, g_ln, b_ln)


def _nbr_specs(n, nb):
    return [pl.BlockSpec((BLK, n), lambda i: (jnp.maximum(i - 1, 0), 0)),
            pl.BlockSpec((BLK, n), lambda i: (i, 0)),
            pl.BlockSpec((BLK, n), lambda i: (jnp.minimum(i + 1, nb - 1), 0))]


def _nbr_specs4(nb):
    return [pl.BlockSpec((1, 2, 4 * BLK, 128), lambda i: (jnp.maximum(i - 1, 0), 0, 0, 0)),
            pl.BlockSpec((1, 2, 4 * BLK, 128), lambda i: (i, 0, 0, 0)),
            pl.BlockSpec((1, 2, 4 * BLK, 128), lambda i: (jnp.minimum(i + 1, nb - 1), 0, 0, 0))]


def _band_bias():
    a = np.arange(4 * BLK)[:, None] % BLK
    c = np.arange(3 * BLK)[None, :]
    inside = np.abs(c - BLK - a) <= BLK
    q_side = np.stack([inside & (c >= BLK), inside, inside & (c < 2 * BLK)])
    blk = np.arange(12 * BLK)[:, None] // (4 * BLK)
    a = np.arange(12 * BLK)[:, None] % BLK
    c = np.arange(BLK)[None, :]
    inside = np.abs(c - a + (1 - blk) * BLK) <= BLK
    k_side = np.stack([inside & (blk >= 1), inside, inside & (blk <= 1)])
    return [jnp.asarray(np.where(m, 0.0, NEG).astype(np.float32)) for m in (q_side, k_side)]


def _edge_spec(shape, nb):
    return pl.BlockSpec((1,) + shape, lambda i: (jnp.where(i == 0, 0, jnp.where(i == nb - 1, 2, 1)),) + (0,) * len(shape))


def _attn_fwd(q, kd, vd, sink_b, bias, rides=()):
    t_len = q.shape[0]
    nb = t_len // BLK

    def body(q_ref, kp_ref, kc_ref, kn_ref, vp_ref, vc_ref, vn_ref, sk_ref, bias_ref, y_ref, lse_ref):
        kcat = jnp.concatenate([kp_ref[...], kc_ref[...], kn_ref[...]], axis=0)
        vcat = jnp.concatenate([vp_ref[...], vc_ref[...], vn_ref[...]], axis=0)
        ys = []
        for g in range(2):
            qs = _stack_heads(q_ref[:, 256 * g:256 * g + 256])
            s = _dot_nt(qs, kcat[:, 128 * g:128 * g + 128]) * ATT_SCALE + bias_ref[0]
            skc = _sink_col(sk_ref, g)
            m_b = jnp.maximum(jnp.max(s, axis=-1, keepdims=True), skc)
            p = jnp.exp(s - _tile3(m_b))
            den_b = jnp.sum(p, axis=-1, keepdims=True) + jnp.exp(skc - m_b)
            pn = p * _tile3(1.0 / den_b)
            o = _dot(_bf(pn), vcat[:, 128 * g:128 * g + 128])
            ys.append(_unstack_heads(o))
            lse_ref[0, g] = m_b + jnp.log(den_b)
        y_ref[...] = _bf(jnp.concatenate(ys, axis=1))

    return _call(
        body, rides=rides, name="attn_fwd", grid=(nb,),
        in_specs=[_rows(BLK, 512)] + _nbr_specs(256, nb) + _nbr_specs(256, nb) + [_res((8, 128)), _edge_spec((4 * BLK, 3 * BLK), nb)],
        out_specs=[_rows(BLK, 512), pl.BlockSpec((1, 2, 4 * BLK, 128), lambda i: (i, 0, 0, 0))],
        out_shape=(jax.ShapeDtypeStruct((t_len, 512), BF16), jax.ShapeDtypeStruct((nb, 2, 4 * BLK, 128), F32)),
        compiler_params=_cp(1),
    )(q, kd, kd, kd, vd, vd, vd, sink_b, bias)


def _mem_heads(kv_ref, h):
    lo = MEM_HD * (h % 2)
    return kv_ref[h // 2, :, lo:lo + MEM_HD], kv_ref[2 + h // 2, :, lo:lo + MEM_HD]


def _mix_mem_fwd(x, yc, ya, w_out, b_out, g_q, w_q, kv, w_o, tm, rides=()):
    t_len = x.shape[0]

    def body(x_ref, yc_ref, ya_ref, wout_ref, bout_ref, g_ref, wq_ref, kv_ref, wo_ref,
             ymix_ref, x1_ref, hq_ref, qm_ref, om_ref, x2_ref):
        ymix = jnp.concatenate([yc_ref[...], ya_ref[...]], axis=1)
        ymix_ref[...] = ymix
        x1 = x_ref[...] + _dot(ymix, wout_ref[...]) + bout_ref[...]
        x1_ref[...] = x1
        hq, _, _ = _rms_fwd(x1, g_ref[...])
        hqb = _bf(hq)
        hq_ref[...] = hqb
        qm = _bf(_dot(hqb, wq_ref[...]))
        qm_ref[...] = qm
        outs = []
        for h in range(MEM_HEADS):
            kh, vh = _mem_heads(kv_ref, h)
            s = _dot_nt(qm[:, MEM_HD * h:MEM_HD * (h + 1)], kh) * MEM_SCALE
            p = jnp.exp(s - jnp.max(s, axis=-1, keepdims=True))
            p = p * (1.0 / jnp.sum(p, axis=-1, keepdims=True))
            outs.append(_dot(_bf(p), vh))
        om = _bf(jnp.concatenate(outs, axis=1))
        om_ref[...] = om
        x2_ref[...] = x1 + _dot(om, wo_ref[...])

    act_b = jax.ShapeDtypeStruct((t_len, D_MODEL), BF16)
    act_f = jax.ShapeDtypeStruct((t_len, D_MODEL), F32)
    return _call(
        body, rides=rides, name="mix_mem_fwd", grid=(t_len // tm,),
        in_specs=[_rows(tm, D_MODEL), _rows(tm, 512), _rows(tm, 512), _res(w_out.shape), _res((1, D_MODEL)), _res((1, D_MODEL)),
                  _res(w_q.shape), _res(kv.shape), _res(w_o.shape)],
        out_specs=[_rows(tm, D_MODEL)] * 6,
        out_shape=(act_b, act_f, act_b, act_b, act_b, act_f),
        compiler_params=_cp(1),
    )(x, yc, ya, w_out, b_out, g_q, w_q, kv, w_o)


def _ffn_loss(x2, g_ffn, w_gate, w_up, w_down, g_final, target, tm):
    t_len = x2.shape[0]
    ff = w_gate.shape[1]
    n_tiles = t_len // tm

    def body(x2_ref, g_ref, wg_ref, wu_ref, wd_ref, gf_ref, tgt_ref,
             hf_ref, gate_ref, up_ref, act_ref, dx3_ref, loss_ref, dgf_ref):
        i = pl.program_id(0)
        x2v = x2_ref[...]
        hf, _, _ = _rms_fwd(x2v, g_ref[...])
        hfb = _bf(hf)
        hf_ref[...] = hfb
        acc = jnp.zeros((tm, D_MODEL), F32)
        for s in range(N_CHIPS):
            gate = _dot_nt(hfb, wg_ref[s])
            up = _dot_nt(hfb, wu_ref[s])
            act = _bf(gate * _sigmoid(gate) * up)
            gate_ref[s] = _bf(gate)
            up_ref[s] = _bf(up)
            act_ref[s] = act
            acc = acc + _dot(act, wd_ref[s])
        x3 = x2v + acc
        gf = gf_ref[...]
        y, xh, r = _rms_fwd(x3, gf)
        err = y - tgt_ref[...]
        part = 0.5 * jnp.sum(jnp.mean(err * err, axis=-1, keepdims=True), axis=0, keepdims=True)
        dy = err * (1.0 / D_MODEL)
        dx3_ref[...] = _rms_bwd(dy, xh, r, gf)

        @pl.when(i == 0)
        def _():
            loss_ref[...] = jnp.zeros_like(loss_ref)
            dgf_ref[...] = jnp.zeros_like(dgf_ref)

        loss_ref[...] += jnp.broadcast_to(part, loss_ref.shape)
        dgf_ref[...] += _colsum(dy * xh)

    hid = jax.ShapeDtypeStruct((N_CHIPS, t_len, ff), BF16)
    hid_spec = pl.BlockSpec((N_CHIPS, tm, ff), lambda i: (0, i, 0))
    return _call(
        body, name="ffn_loss", grid=(n_tiles,),
        in_specs=[_rows(tm, D_MODEL), _res((1, D_MODEL)), _res(w_gate.shape), _res(w_up.shape), _res(w_down.shape),
                  _res((1, D_MODEL)), _rows(tm, D_MODEL)],
        out_specs=[_rows(tm, D_MODEL), hid_spec, hid_spec, hid_spec, _rows(tm, D_MODEL),
                   pl.BlockSpec((1, D_MODEL), lambda i: (0, 0)), pl.BlockSpec((1, D_MODEL), lambda i: (0, 0))],
        out_shape=(jax.ShapeDtypeStruct((t_len, D_MODEL), BF16), hid, hid, hid, jax.ShapeDtypeStruct((t_len, D_MODEL), F32),
                   jax.ShapeDtypeStruct((1, D_MODEL), F32), jax.ShapeDtypeStruct((1, D_MODEL), F32)),
        compiler_params=_cp(1),
    )(x2, g_ffn, w_gate, w_up, w_down, g_final, target)


def _ffn_bwd(dx3, x2, gate, up, g_ffn, w_gate, w_up, w_down, tm):
    t_len = x2.shape[0]
    ff = w_gate.shape[1]

    def body(dx3_ref, x2_ref, gate_ref, up_ref, g_ref, wg_ref, wu_ref, wd_ref, dx2_ref, dgate_ref, dup_ref, dg_ref):
        i = pl.program_id(0)
        dx3 = dx3_ref[...]
        d3b = _bf(dx3)
        dh = jnp.zeros((tm, D_MODEL), F32)
        for s in range(N_CHIPS):
            dact = _dot_nt(d3b, wd_ref[s])
            gt = gate_ref[s].astype(F32)
            u = up_ref[s].astype(F32)
            sg = _sigmoid(gt)
            dup = _bf(dact * (gt * sg))
            dgate = _bf(dact * u * (sg * (1.0 + gt * (1.0 - sg))))
            dup_ref[s] = dup
            dgate_ref[s] = dgate
            dh = dh + _dot(dgate, wg_ref[s]) + _dot(dup, wu_ref[s])
        g = g_ref[...]
        _, xh, r = _rms_fwd(x2_ref[...], g)
        dx2_ref[...] = dx3 + _rms_bwd(dh, xh, r, g)

        @pl.when(i == 0)
        def _():
            dg_ref[...] = jnp.zeros_like(dg_ref)

        dg_ref[...] += _colsum(dh * xh)

    hid = jax.ShapeDtypeStruct((N_CHIPS, t_len, ff), BF16)
    hid_spec = pl.BlockSpec((N_CHIPS, tm, ff), lambda i: (0, i, 0))
    return _call(
        body, name="ffn_bwd", grid=(t_len // tm,),
        in_specs=[_rows(tm, D_MODEL), _rows(tm, D_MODEL), hid_spec, hid_spec, _res((1, D_MODEL)),
                  _res(w_gate.shape), _res(w_up.shape), _res(w_down.shape)],
        out_specs=[_rows(tm, D_MODEL), hid_spec, hid_spec, pl.BlockSpec((1, D_MODEL), lambda i: (0, 0))],
        out_shape=(jax.ShapeDtypeStruct((t_len, D_MODEL), F32), hid, hid, jax.ShapeDtypeStruct((1, D_MODEL), F32)),
        compiler_params=_cp(1),
    )(dx3, x2, gate, up, g_ffn, w_gate, w_up, w_down)


def _mix_mem_bwd(dx2, x1, qm, kv, g_q, w_q, w_o, w_out, tm, rides=()):
    t_len = x1.shape[0]
    m_len = kv.shape[1]

    def body(dx2_ref, x1_ref, qm_ref, kv_ref, g_ref, wq_ref, wo_ref, wout_ref,
             dx1_ref, dqm_ref, dyc_ref, dya_ref, dkv_ref, dgq_ref, dbout_ref):
        i = pl.program_id(0)

        @pl.when(i == 0)
        def _():
            dkv_ref[...] = jnp.zeros_like(dkv_ref)
            dgq_ref[...] = jnp.zeros_like(dgq_ref)
            dbout_ref[...] = jnp.zeros_like(dbout_ref)

        dx2 = dx2_ref[...]
        dom = _dot_nt(_bf(dx2), wo_ref[...])
        dqs = []
        for h in range(MEM_HEADS):
            kh, vh = _mem_heads(kv_ref, h)
            qh = qm_ref[:, MEM_HD * h:MEM_HD * (h + 1)]
            s = _dot_nt(qh, kh) * MEM_SCALE
            p = jnp.exp(s - jnp.max(s, axis=-1, keepdims=True))
            p = p * (1.0 / jnp.sum(p, axis=-1, keepdims=True))
            domh = _bf(dom[:, MEM_HD * h:MEM_HD * (h + 1)])
            dp = _dot_nt(domh, vh)
            ds = _bf(p * (dp - jnp.sum(p * dp, axis=-1, keepdims=True)) * MEM_SCALE)
            dqs.append(_dot(ds, kh))
            lo = MEM_HD * (h % 2)
            dkv_ref[h // 2, :, lo:lo + MEM_HD] += _dot_tn(ds, qh)
            dkv_ref[2 + h // 2, :, lo:lo + MEM_HD] += _dot_tn(_bf(p), domh)
        dqm = _bf(jnp.concatenate(dqs, axis=1))
        dqm_ref[...] = dqm
        dhq = _dot_nt(dqm, wq_ref[...])
        g = g_ref[...]
        _, xh, r = _rms_fwd(x1_ref[...], g)
        dx1 = dx2 + _rms_bwd(dhq, xh, r, g)
        dx1_ref[...] = dx1
        dgq_ref[...] += _colsum(dhq * xh)
        dbout_ref[...] += _colsum(dx1)
        dymix = _dot_nt(_bf(dx1), wout_ref[...])
        dyc_ref[...] = dymix[:, 0:CONV_CH]
        dya_ref[...] = _bf(dymix[:, CONV_CH:2 * CONV_CH])

    vec = pl.BlockSpec((1, D_MODEL), lambda i: (0, 0))
    return _call(
        body, rides=rides, name="mix_mem_bwd", grid=(t_len // tm,),
        in_specs=[_rows(tm, D_MODEL), _rows(tm, D_MODEL), _rows(tm, D_MODEL), _res(kv.shape), _res((1, D_MODEL)),
                  _res(w_q.shape), _res(w_o.shape), _res(w_out.shape)],
        out_specs=[_rows(tm, D_MODEL), _rows(tm, D_MODEL), _rows(tm, CONV_CH), _rows(tm, CONV_CH),
                   pl.BlockSpec(kv.shape, lambda i: (0, 0, 0)), vec, vec],
        out_shape=(jax.ShapeDtypeStruct((t_len, D_MODEL), F32), jax.ShapeDtypeStruct((t_len, D_MODEL), BF16),
                   jax.ShapeDtypeStruct((t_len, CONV_CH), F32), jax.ShapeDtypeStruct((t_len, CONV_CH), BF16),
                   jax.ShapeDtypeStruct((N_CHIPS, m_len, kv.shape[2]), F32),
                   jax.ShapeDtypeStruct((1, D_MODEL), F32), jax.ShapeDtypeStruct((1, D_MODEL), F32)),
        compiler_params=_cp(1),
    )(dx2, x1, qm, kv, g_q, w_q, w_o, w_out)


def _mem_kv_bwd(dkv, memn, mem, g_kv, w_kv):
    m_len = mem.shape[0]

    def body(dkv_ref, memn_ref, mem_ref, g_ref, w_ref, dw_ref, dg_ref):
        hb = memn_ref[...]
        dmn = jnp.zeros((m_len, D_MODEL), F32)
        for s in range(N_CHIPS):
            d = _bf(dkv_ref[s])
            dw_ref[s] = _bf(_dot_tn(hb, d))
            dmn = dmn + _dot_nt(d, w_ref[s])
        _, xh, _ = _rms_fwd(mem_ref[...], g_ref[...])
        dg_ref[...] = _colsum(dmn * xh)

    return _call(
        body, name="mem_kv_bwd",
        out_shape=(jax.ShapeDtypeStruct(w_kv.shape, BF16), jax.ShapeDtypeStruct((1, D_MODEL), F32)),
        compiler_params=pltpu.CompilerParams(vmem_limit_bytes=VMEM_LIMIT_BYTES),
    )(dkv, memn, mem, g_kv, w_kv)


def _attn_bwd_q(q, kd, vd, dya, lse, sink_b, bias, cos_t, sin_t, rides=()):
    t_len = q.shape[0]
    nb = t_len // BLK

    def body(q_ref, kp_ref, kc_ref, kn_ref, vp_ref, vc_ref, vn_ref, do_ref, lse_ref, sk_ref, bias_ref, c_ref, s_ref,
             dq_ref, dd_ref, dsk_ref):
        kcat = jnp.concatenate([kp_ref[...], kc_ref[...], kn_ref[...]], axis=0)
        vcat = jnp.concatenate([vp_ref[...], vc_ref[...], vn_ref[...]], axis=0)
        dqs, dsks = [], []
        for g in range(2):
            qs = _stack_heads(q_ref[:, 256 * g:256 * g + 256])
            dos = _stack_heads(do_ref[:, 256 * g:256 * g + 256])
            kk = kcat[:, 128 * g:128 * g + 128]
            s = _dot_nt(qs, kk) * ATT_SCALE + bias_ref[0]
            lse_b = lse_ref[0, g]
            p = jnp.exp(s - _tile3(lse_b))
            dp = _dot_nt(dos, vcat[:, 128 * g:128 * g + 128])
            drow = jnp.sum(p * dp, axis=-1, keepdims=True)
            ds = _bf(p * (dp - drow) * ATT_SCALE)
            dqs.append(_unstack_heads(_dot(ds, kk)))
            d_b = jnp.broadcast_to(drow, (4 * BLK, 128))
            dd_ref[0, g] = d_b
            contrib = -(jnp.exp(_sink_col(sk_ref, g) - lse_b) * d_b)
            dsks.append(jnp.sum(contrib.reshape(4, BLK, 128), axis=1))
        dq = jnp.concatenate(dqs, axis=1)
        dq_ref[...] = _bf(_rope(dq, c_ref[...], s_ref[...], -1.0))
        dsk_ref[0] = jnp.concatenate(dsks, axis=0)

    stat = pl.BlockSpec((1, 2, 4 * BLK, 128), lambda i: (i, 0, 0, 0))
    return _call(
        body, rides=rides, name="attn_bwd_q", grid=(nb,),
        in_specs=[_rows(BLK, 512)] + _nbr_specs(256, nb) + _nbr_specs(256, nb)
        + [_rows(BLK, 512), stat, _res((8, 128)), _edge_spec((4 * BLK, 3 * BLK), nb), _rows(BLK, 128), _rows(BLK, 128)],
        out_specs=[_rows(BLK, 512), stat, pl.BlockSpec((1, 8, 128), lambda i: (i, 0, 0))],
        out_shape=(jax.ShapeDtypeStruct((t_len, 512), BF16), jax.ShapeDtypeStruct((nb, 2, 4 * BLK, 128), F32),
                   jax.ShapeDtypeStruct((nb, 8, 128), F32)),
        compiler_params=_cp(1),
    )(q, kd, kd, kd, vd, vd, vd, dya, lse, sink_b, bias, cos_t, sin_t)


def _attn_bwd_kv(q, kd, vd, dya, lse, dd, bias, cos_t, sin_t, rides=()):
    t_len = q.shape[0]
    nb = t_len // BLK

    def body(kc_ref, vc_ref, qp_ref, qc_ref, qn_ref, dop_ref, doc_ref, don_ref, lp_ref, lc_ref, ln_ref,
             dp_ref, dc_ref, dn_ref, bias_ref, c_ref, s_ref, dk_ref, dv_ref):
        dks, dvs = [], []
        for g in range(2):
            cols = slice(256 * g, 256 * g + 256)
            qs = jnp.concatenate([_stack_heads(r[:, cols]) for r in (qp_ref, qc_ref, qn_ref)], axis=0)
            dos = jnp.concatenate([_stack_heads(r[:, cols]) for r in (dop_ref, doc_ref, don_ref)], axis=0)
            lse_b = jnp.concatenate([r[0, g] for r in (lp_ref, lc_ref, ln_ref)], axis=0)
            d_b = jnp.concatenate([r[0, g] for r in (dp_ref, dc_ref, dn_ref)], axis=0)
            kk = kc_ref[:, 128 * g:128 * g + 128]
            s = _dot_nt(qs, kk) * ATT_SCALE + bias_ref[0]
            p = jnp.exp(s - lse_b)
            dp = _dot_nt(dos, vc_ref[:, 128 * g:128 * g + 128])
            ds = _bf(p * (dp - d_b) * ATT_SCALE)
            dvs.append(_dot_tn(_bf(p), dos))
            dks.append(_dot_tn(ds, qs))
        dk_ref[...] = _bf(_rope(_fold_heads(dks), c_ref[...], s_ref[...], -1.0))
        dv_ref[...] = _bf(_fold_heads(dvs))

    return _call(
        body, rides=rides, name="attn_bwd_kv", grid=(nb,),
        in_specs=[_rows(BLK, 256), _rows(BLK, 256)] + _nbr_specs(512, nb) + _nbr_specs(512, nb) + _nbr_specs4(nb) + _nbr_specs4(nb)
        + [_edge_spec((12 * BLK, BLK), nb), _rows(BLK, 128), _rows(BLK, 128)],
        out_specs=[_rows(BLK, 128), _rows(BLK, 128)],
        out_shape=(jax.ShapeDtypeStruct((t_len, 128), BF16), jax.ShapeDtypeStruct((t_len, 128), BF16)),
        compiler_params=_cp(1),
    )(kd, vd, q, q, q, dya, dya, dya, lse, lse, lse, dd, dd, dd, bias, cos_t, sin_t)


def _conv_norm_bwd(pre, dyc, g_ln, b_ln, tc):
    t_len = pre.shape[0]

    def body(pre_ref, dy_ref, g_ref, b_ref, dpre_ref, stats_ref):
        i = pl.program_id(0)
        pre_v = pre_ref[...]
        mu = jnp.mean(pre_v, axis=-1, keepdims=True)
        d = pre_v - mu
        rstd = lax.rsqrt(jnp.mean(d * d, axis=-1, keepdims=True) + EPS)
        xh = d * rstd
        g = g_ref[...]
        ln = xh * g + b_ref[...]
        sg = _sigmoid(ln)
        dln = dy_ref[...] * (sg * (1.0 + ln * (1.0 - sg)))
        dxh = dln * g
        dpre = rstd * (dxh - jnp.mean(dxh, axis=-1, keepdims=True) - xh * jnp.mean(dxh * xh, axis=-1, keepdims=True))
        dpre_ref[...] = dpre

        @pl.when(i == 0)
        def _():
            stats_ref[...] = jnp.zeros_like(stats_ref)

        stats_ref[0:1, :] += _colsum(dln * xh)
        stats_ref[1:2, :] += _colsum(dln)
        stats_ref[2:3, :] += _colsum(dpre)

    return _call(
        body, name="conv_norm_bwd", grid=(t_len // tc,),
        in_specs=[_rows(tc, CONV_CH), _rows(tc, CONV_CH), _res((1, CONV_CH)), _res((1, CONV_CH))],
        out_specs=[_rows(tc, CONV_CH), pl.BlockSpec((8, CONV_CH), lambda i: (0, 0))],
        out_shape=(jax.ShapeDtypeStruct((t_len, CONV_CH), F32), jax.ShapeDtypeStruct((8, CONV_CH), F32)),
        compiler_params=_cp(1),
    )(pre, dyc, g_ln, b_ln)


def _conv_bwd(dpre, ug, w_dw, tc, rides=()):
    t_len = ug.shape[0]
    n_tiles = t_len // tc

    def body(dp_ref, dm_ref, dn_ref, up_ref, um_ref, un_ref, w_ref, du_ref, dw_ref, dbuf, vbuf, dshift, vshift):
        i = pl.program_id(0)
        _fill_halo_buf(dbuf, dp_ref[...], dm_ref[...], dn_ref[...], i, n_tiles, tc)
        _fill_halo_buf(vbuf, _glu(up_ref[...]), _glu(um_ref[...]), _glu(un_ref[...]), i, n_tiles, tc)
        _shift_copies(dbuf, dshift, tc)
        _shift_copies(vbuf, vshift, tc)

        @pl.when(i == 0)
        def _():
            dw_ref[...] = jnp.zeros_like(dw_ref)

        def chunk(c, carry):
            base = c * CONV_ROWS
            rows = pl.ds(pl.multiple_of(base, CONV_ROWS), CONV_ROWS)
            dmain = dm_ref[rows, :]
            dv = jnp.zeros((CONV_ROWS, CONV_CH), F32)
            for k in range(CONV_W):
                dv = dv + w_ref[k:k + 1, :] * _shifted_rows(dbuf, dshift, 31 - k, base)
                prod = dmain * _shifted_rows(vbuf, vshift, k + 1, base)
                dw_ref[8 * k:8 * k + 8, :] += jnp.sum(prod.reshape(CONV_ROWS // 8, 8, CONV_CH), axis=0)
            um = um_ref[rows, :]
            a, gt = um[:, 0:CONV_CH], um[:, CONV_CH:2 * CONV_CH]
            sg = _sigmoid(gt)
            du_ref[rows, :] = _bf(jnp.concatenate([dv * sg, dv * a * (sg * (1.0 - sg))], axis=1))
            return carry

        lax.fori_loop(0, tc // CONV_ROWS, chunk, 0)

    shifts = pltpu.VMEM((7, tc + 24, CONV_CH), F32)
    return _call(
        body, rides=rides, name="conv_bwd", grid=(n_tiles,),
        in_specs=_halo_specs(tc, CONV_CH, t_len) + _halo_specs(tc, 1024, t_len) + [_res((32, CONV_CH))],
        out_specs=[_rows(tc, 1024), pl.BlockSpec((8 * 32, CONV_CH), lambda i: (0, 0))],
        out_shape=(jax.ShapeDtypeStruct((t_len, 1024), BF16), jax.ShapeDtypeStruct((8 * 32, CONV_CH), F32)),
        scratch_shapes=[pltpu.VMEM((tc + 32, CONV_CH), F32), pltpu.VMEM((tc + 32, CONV_CH), F32), shifts, shifts],
        compiler_params=_cp(1),
    )(dpre, dpre, dpre, ug, ug, ug, w_dw)


def _in_proj_bwd(du_glu, dq, dk, dv, dx1, x, g_mix, w_t, tm):
    t_len = x.shape[0]
    n_ext = w_t.shape[0]

    def body(dg_ref, dq_ref, dk_ref, dv_ref, dx1_ref, x_ref, g_ref, w_ref, dx_ref, du_ref, db_ref, dgm_ref):
        i = pl.program_id(0)
        du = jnp.concatenate([dg_ref[...], dq_ref[...], dk_ref[...], dv_ref[...]], axis=1)
        du_ref[...] = du
        dh = _dot(du, w_ref[...])
        g = g_ref[...]
        _, xh, r = _rms_fwd(x_ref[...], g)
        dx_ref[...] = dx1_ref[...] + _rms_bwd(dh, xh, r, g)

        @pl.when(i == 0)
        def _():
            db_ref[...] = jnp.zeros_like(db_ref)
            dgm_ref[...] = jnp.zeros_like(dgm_ref)

        db_ref[...] += _colsum(du.astype(F32))
        dgm_ref[...] += _colsum(dh * xh)

    return _call(
        body, name="in_proj_bwd", grid=(t_len // tm,),
        in_specs=[_rows(tm, 1024), _rows(tm, 512), _rows(tm, 128), _rows(tm, 128), _rows(tm, D_MODEL), _rows(tm, D_MODEL),
                  _res((1, D_MODEL)), _res(w_t.shape)],
        out_specs=[_rows(tm, D_MODEL), _rows(tm, n_ext), pl.BlockSpec((1, n_ext), lambda i: (0, 0)),
                   pl.BlockSpec((1, D_MODEL), lambda i: (0, 0))],
        out_shape=(jax.ShapeDtypeStruct((t_len, D_MODEL), F32), jax.ShapeDtypeStruct((t_len, n_ext), BF16),
                   jax.ShapeDtypeStruct((1, n_ext), F32), jax.ShapeDtypeStruct((1, D_MODEL), F32)),
        compiler_params=_cp(1),
    )(du_glu, dq, dk, dv, dx1, x, g_mix, w_t)


def _weight_grad(a, d, name, tt):
    sa, t_len, k_dim = a.shape
    sd, _, n_dim = d.shape
    n_s = max(sa, sd)
    tn = n_dim if n_dim <= 1024 else 1024
    tt = min(tt, t_len)

    n_t = t_len // tt

    def body(a_ref, d_ref, o_ref, acc):
        t = pl.program_id(2)

        @pl.when(t == 0)
        def _():
            acc[...] = jnp.zeros_like(acc)

        acc[...] += _dot_tn(_bf(a_ref[0]), _bf(d_ref[0]))

        @pl.when(t == n_t - 1)
        def _():
            o_ref[0] = _bf(acc[...])

    return _call(
        body, name=name, grid=(n_s, n_dim // tn, n_t),
        in_specs=[pl.BlockSpec((1, tt, k_dim), (lambda s, n, t: (s, t, 0)) if sa > 1 else (lambda s, n, t: (0, t, 0))),
                  pl.BlockSpec((1, tt, tn), (lambda s, n, t: (s, t, n)) if sd > 1 else (lambda s, n, t: (0, t, n)))],
        out_specs=pl.BlockSpec((1, k_dim, tn), lambda s, n, t: (s, 0, n)),
        out_shape=jax.ShapeDtypeStruct((n_s, k_dim, n_dim), BF16),
        scratch_shapes=[pltpu.VMEM((k_dim, tn), F32)],
        compiler_params=_cp(3),
    )(a, d)


ANY = pl.BlockSpec(memory_space=pl.ANY)


def _place():
    x, y, c = lax.axis_index("x"), lax.axis_index("y"), lax.axis_index("c")
    chips = [(1 - x, y), (x, 1 - y), (1 - x, 1 - y)]
    return x, y, c, chips


def _remote(src, dst, send_sems, recv_sems, k, to):
    return pltpu.make_async_remote_copy(src_ref=src, dst_ref=dst, send_sem=send_sems.at[k], recv_sem=recv_sems.at[k],
                                        device_id=to, device_id_type=MESH)


def _cast_place(w, chip_idx, tr):
    rows, cols = w.shape
    h = rows // 2
    tr = _div_tile(h, tr)
    per = h // tr

    def body(s_ref, w_ref, o_ref):
        o_ref[0, 0] = _bf(w_ref[...])

    return _call(
        body, name="cast_place",
        grid_spec=pltpu.PrefetchScalarGridSpec(
            num_scalar_prefetch=1, grid=(2, per),
            in_specs=[pl.BlockSpec((tr, cols), lambda hh, r, s_ref: (hh * per + r, 0))],
            out_specs=pl.BlockSpec((1, 1, tr, cols), lambda hh, r, s_ref: (s_ref[0], hh, r, 0))),
        out_shape=jax.ShapeDtypeStruct((N_CHIPS, 2, h, cols), BF16),
        compiler_params=_cp(2),
    )(chip_idx, w)


def _same(arrays):
    return [jax.ShapeDtypeStruct(a.shape, a.dtype) for a in arrays]


def _gather_ride(bufs):
    n = len(bufs)

    def first_hop(outs, send, recv):
        x, y, c, chips = _place()
        mine = [outs[i].at[2 * x + y, c] for i in range(n)]
        return [_remote(mine[i], mine[i], send, recv, 3 * i + j, (cx, cy, c)) for i in range(n) for j, (cx, cy) in enumerate(chips)]

    def start(ins, outs, send, recv):
        for cp in first_hop(outs, send, recv):
            cp.start()

    def finish(ins, outs, send, recv):
        x, y, c, chips = _place()
        sib = (x, y, 1 - c)
        onward = []
        for i in range(n):
            for j, (cx, cy) in enumerate(chips):
                slab = outs[i].at[2 * cx + cy, c]
                _remote(slab, slab, send, recv, 3 * i + j, sib).wait_recv()
                onward.append(_remote(slab, slab, send, recv, 3 * n + 3 * i + j, sib))
                onward[-1].start()
        for i in range(n):
            for j, (cx, cy) in enumerate(chips):
                other = outs[i].at[2 * cx + cy, 1 - c]
                _remote(other, other, send, recv, 3 * n + 3 * i + j, sib).wait_recv()
        for cp in first_hop(outs, send, recv) + onward:
            cp.wait_send()

    return _Ride(bufs, _same(bufs), 6 * n, start, finish, aliases={i: i for i in range(n)})


def _spread_ride(buf):
    def sends(outs, send, recv):
        x, y, c, chips = _place()
        mine = outs[0].at[2 * x + y]
        return [_remote(mine, mine, send, recv, j, (cx, cy, c)) for j, (cx, cy) in enumerate(chips)]

    def start(ins, outs, send, recv):
        for cp in sends(outs, send, recv):
            cp.start()

    def finish(ins, outs, send, recv):
        _, _, c, chips = _place()
        for j, (cx, cy) in enumerate(chips):
            slab = outs[0].at[2 * cx + cy]
            _remote(slab, slab, send, recv, j, (cx, cy, c)).wait_recv()
        for cp in sends(outs, send, recv):
            cp.wait_send()

    return _Ride([buf], _same([buf]), 3, start, finish, aliases={0: 0})


def _pairwise_ride(arrays, out_shape, n_sem, copies):
    def start(ins, outs, send, recv):
        for cp in copies(ins, outs, send, recv):
            cp.start()

    def finish(ins, outs, send, recv):
        for cp in copies(ins, outs, send, recv):
            cp.wait()

    return _Ride(arrays, out_shape, n_sem, start, finish)


def _run_rides(name, rides):
    k_in = [len(r.operands) for r in rides]
    k_out = [len(r.out_shape) for r in rides]

    def body(*refs):
        pos, r_in, r_out = 0, [], []
        for k in k_in:
            r_in.append(refs[pos:pos + k])
            pos += k
        for k in k_out:
            r_out.append(refs[pos:pos + k])
            pos += k
        sems = refs[pos:]
        for j, r in enumerate(rides):
            r.start(r_in[j], r_out[j], sems[2 * j], sems[2 * j + 1])
        for j, r in enumerate(rides):
            r.finish(r_in[j], r_out[j], sems[2 * j], sems[2 * j + 1])

    aliases, off_in, off_out = {}, 0, 0
    for r, ki, ko in zip(rides, k_in, k_out):
        aliases.update({off_in + a: off_out + b for a, b in r.aliases.items()})
        off_in, off_out = off_in + ki, off_out + ko
    res = _call(
        body, name=name, in_specs=[ANY] * sum(k_in), out_specs=[ANY] * sum(k_out),
        out_shape=[s for r in rides for s in r.out_shape], input_output_aliases=aliases,
        scratch_shapes=[pltpu.SemaphoreType.DMA((r.n_sem,)) for r in rides for _ in range(2)],
    )(*[op for r in rides for op in r.operands])
    out, pos = [], 0
    for k in k_out:
        out.append(list(res[pos:pos + k]))
        pos += k
    return out


def _swap_ride(grads):
    def copies(ins, outs, send, recv):
        x, y, c, _ = _place()
        return [_remote(ins[i].at[:, 1 - c], outs[i], send, recv, i, (x, y, 1 - c)) for i in range(len(grads))]

    out_shape = [jax.ShapeDtypeStruct((g.shape[0],) + g.shape[2:], g.dtype) for g in grads]
    return _pairwise_ride(grads, out_shape, len(grads), copies)


def _pair_sum(grad, other, c_idx, tr):
    n_s, _, h, cols = grad.shape
    tr = _div_tile(h, tr)

    def body(c_ref, a_ref, b_ref, o_ref):
        o_ref[...] = _bf(a_ref[0].astype(F32) + b_ref[...].astype(F32))

    return _call(
        body, name="pair_sum",
        grid_spec=pltpu.PrefetchScalarGridSpec(
            num_scalar_prefetch=1, grid=(n_s, h // tr),
            in_specs=[pl.BlockSpec((1, 1, tr, cols), lambda s, r, c_ref: (s, c_ref[0], r, 0)),
                      pl.BlockSpec((1, tr, cols), lambda s, r, c_ref: (s, r, 0))],
            out_specs=pl.BlockSpec((1, tr, cols), lambda s, r, c_ref: (s, r, 0))),
        out_shape=jax.ShapeDtypeStruct((n_s, h, cols), BF16),
        compiler_params=_cp(2),
    )(c_idx, grad, other)


def _exchange_ride(sums):
    def copies(ins, outs, send, recv):
        _, _, c, chips = _place()
        return [_remote(ins[i].at[2 * cx + cy], outs[i].at[j], send, recv, 3 * i + j, (cx, cy, c))
                for i in range(len(sums)) for j, (cx, cy) in enumerate(chips)]

    out_shape = [jax.ShapeDtypeStruct((3,) + s.shape[1:], s.dtype) for s in sums]
    return _pairwise_ride(sums, out_shape, 3 * len(sums), copies)


def _chip_sum(own, others, chip_idx, tr):
    _, h, cols = own.shape
    tr = _div_tile(h, tr)

    def body(s_ref, a_ref, p_ref, o_ref):
        acc = a_ref[0].astype(F32)
        for j in range(N_CHIPS - 1):
            acc = acc + p_ref[j].astype(F32)
        o_ref[...] = acc

    return _call(
        body, name="chip_sum",
        grid_spec=pltpu.PrefetchScalarGridSpec(
            num_scalar_prefetch=1, grid=(h // tr,),
            in_specs=[pl.BlockSpec((1, tr, cols), lambda r, s_ref: (s_ref[0], r, 0)),
                      pl.BlockSpec((N_CHIPS - 1, tr, cols), lambda r, s_ref: (0, r, 0))],
            out_specs=pl.BlockSpec((tr, cols), lambda r, s_ref: (r, 0))),
        out_shape=jax.ShapeDtypeStruct((h, cols), F32),
        compiler_params=_cp(1),
    )(chip_idx, own, others)


def _share_ride(halves):
    def copies(ins, outs, send, recv):
        x, y, c, _ = _place()
        return [_remote(ins[i], outs[i], send, recv, i, (x, y, 1 - c)) for i in range(len(halves))]

    return _pairwise_ride(halves, _same(halves), len(halves), copies)


def _small_allreduce(pack):
    rows, cols = pack.shape

    def body(p_ref, o_ref, buf, send_sems, recv_sems):
        x, y, c, _ = _place()
        me = 4 * x + 2 * y + c
        buf[me] = p_ref[...]
        sent = []
        for k in range(1, N_DEV):
            fx, fy, fc = (k >> 2) & 1, (k >> 1) & 1, k & 1
            to = (x ^ fx, y ^ fy, c ^ fc)
            cp = _remote(p_ref, buf.at[me], send_sems, recv_sems, k - 1, to)
            cp.start()
            sent.append(cp)
        for k in range(1, N_DEV):
            fx, fy, fc = (k >> 2) & 1, (k >> 1) & 1, k & 1
            frm = 4 * (x ^ fx) + 2 * (y ^ fy) + (c ^ fc)
            _remote(p_ref, buf.at[frm], send_sems, recv_sems, k - 1, (x, y, c)).wait_recv()
        for cp in sent:
            cp.wait_send()
        acc = buf[0]
        for d in range(1, N_DEV):
            acc = acc + buf[d]
        o_ref[...] = acc

    return _call(
        body, name="small_allreduce",
        in_specs=[pl.BlockSpec(memory_space=pltpu.VMEM)], out_specs=pl.BlockSpec(memory_space=pltpu.VMEM),
        out_shape=jax.ShapeDtypeStruct(pack.shape, F32),
        scratch_shapes=[pltpu.VMEM((N_DEV, rows, cols), F32), pltpu.SemaphoreType.DMA((N_DEV - 1,)),
                        pltpu.SemaphoreType.DMA((N_DEV - 1,))],
    )(pack)


def _adamw_math(w, g, m, v):
    m_new = ADAM_B1 * m + (1.0 - ADAM_B1) * g
    v_new = ADAM_B2 * v + (1.0 - ADAM_B2) * (g * g)
    m_hat = m_new * (1.0 / (1.0 - ADAM_B1 ** ADAM_STEP))
    v_hat = v_new * (1.0 / (1.0 - ADAM_B2 ** ADAM_STEP))
    delta = -ADAM_LR * (m_hat / (jnp.sqrt(v_hat) + ADAM_EPS) + ADAM_WD * w)
    return delta, m_new, v_new


def _adamw(w, g_mine, g_other, m, v, core_idx, tr):
    rows, cols = w.shape
    h = rows // 2
    tr = _div_tile(h, tr)
    per = h // tr

    def body(c_ref, w_ref, ga_ref, gb_ref, m_ref, v_ref, g_ref, d_ref, mo_ref, vo_ref):
        g = jnp.where(pl.program_id(0) == c_ref[0], ga_ref[...], gb_ref[...])
        d, mn, vn = _adamw_math(w_ref[...], g, m_ref[...], v_ref[...])
        g_ref[...] = g
        d_ref[...] = d
        mo_ref[...] = mn
        vo_ref[...] = vn

    full = pl.BlockSpec((tr, cols), lambda hh, r, c_ref: (hh * per + r, 0))
    mine = pl.BlockSpec((tr, cols), lambda hh, r, c_ref: (jnp.where(hh == c_ref[0], r, 0), 0))
    other = pl.BlockSpec((tr, cols), lambda hh, r, c_ref: (jnp.where(hh == c_ref[0], 0, r), 0))
    shp = jax.ShapeDtypeStruct(w.shape, F32)
    return _call(
        body, name="adamw",
        grid_spec=pltpu.PrefetchScalarGridSpec(num_scalar_prefetch=1, grid=(2, per), in_specs=[full, mine, other, full, full],
                                               out_specs=[full] * 4),
        out_shape=(shp, shp, shp, shp), compiler_params=_cp(2))(core_idx, w, g_mine, g_other, m, v)


def _adamw_small(ws, gs, ms, vs):
    n = len(ws)

    def body(*refs):
        w_r, g_r, m_r, v_r = refs[0:n], refs[n:2 * n], refs[2 * n:3 * n], refs[3 * n:4 * n]
        d_o, m_o, v_o = refs[4 * n:5 * n], refs[5 * n:6 * n], refs[6 * n:7 * n]
        for i in range(n):
            d, mn, vn = _adamw_math(w_r[i][...], g_r[i][...], m_r[i][...], v_r[i][...])
            d_o[i][...] = d
            m_o[i][...] = mn
            v_o[i][...] = vn

    shp = [jax.ShapeDtypeStruct(w.shape, F32) for w in ws]
    outs = _call(body, name="adamw_small", out_shape=shp * 3)(*ws, *gs, *ms, *vs)
    return outs[0:n], outs[n:2 * n], outs[2 * n:3 * n]


def _rope_tables(t_len):
    pos = jnp.arange(t_len, dtype=F32)
    inv_freq = ROPE_THETA ** (-jnp.arange(0, HEAD_DIM, 2, dtype=F32) / HEAD_DIM)
    ang = pos[:, None] * inv_freq[None, :]
    cos, sin = jnp.cos(ang), jnp.sin(ang)
    return jnp.tile(jnp.concatenate([cos, cos], axis=1), (1, 2)), jnp.tile(jnp.concatenate([-sin, sin], axis=1), (1, 2))


def _dup_heads(a):
    h0, h1 = a[..., 0:64], a[..., 64:128]
    return jnp.concatenate([h0, h0, h1, h1], axis=-1)


def _local_step(x, mem, target, small, wg, comm, tm_a=512, tm_b=256, tc=512, tt=1024):
    def run(stage, fn, n_own, *operands):
        rides = comm.rides(stage)
        res = list(fn(*operands, rides=rides))
        brought, pos = [], n_own
        for r in rides:
            brought.append(res[pos:pos + len(r.out_shape)])
            pos += len(r.out_shape)
        comm.landed(stage, brought, wg)
        return res[:n_own]

    t_len = x.shape[0]
    cos_t, sin_t = _rope_tables(t_len)
    b_in = small["b_in"]
    b_ext = jnp.concatenate([b_in[:, 0:1536], _dup_heads(b_in[:, 1536:1664]), _dup_heads(b_in[:, 1664:1792])], axis=1)
    w_dw = jnp.concatenate([wg["w_dw"], jnp.zeros((1, CONV_CH), F32)], axis=0)
    sink_b = jnp.broadcast_to(small["attn_sink"].reshape(8, 1), (8, 128))
    bias_q, bias_k = _band_bias()

    ug, q, kd, vd, h1 = run("in_proj_fwd", _in_proj_fwd, 5, x, small["g_mix"], wg["w_in"], b_ext, cos_t, sin_t, tm_a)
    memn, kv = _mem_kv_fwd(mem, small["g_mem_kv"], wg["w_mem_kv"])
    yc, pre = run("conv_fwd", _conv_fwd, 2, ug, w_dw, small["b_dw"], small["g_conv_ln"], small["b_conv_ln"], tc)
    ya, lse = run("attn_fwd", _attn_fwd, 2, q, kd, vd, sink_b, bias_q)
    ymix, x1, hq, qm, om, x2 = run("mix_mem_fwd", _mix_mem_fwd, 6, x, yc, ya, wg["w_out"], small["b_out"], small["g_mem_q"],
                                   wg["w_mem_q"], kv, wg["w_mem_o"], tm_a)
    hf, gate, up, act, dx3, loss, d_g_final = _ffn_loss(x2, small["g_ffn"], wg["w_gate"], wg["w_up"], wg["w_down"],
                                                        small["g_final"], target, tm_b)

    dx2, dgate, dup, d_g_ffn = _ffn_bwd(dx3, x2, gate, up, small["g_ffn"], wg["w_gate"], wg["w_up"], wg["w_down"], tm_b)
    comm.grad("w_gate", _weight_grad(dgate, hf[None], "dw_gate", tt))
    comm.grad("w_up", _weight_grad(dup, hf[None], "dw_up", tt))
    comm.grad("w_down", _weight_grad(act, dx3[None], "dw_down", tt))
    dx1, dqm, dyc, dya, dkv, d_g_mem_q, d_b_out = run("mix_mem_bwd", _mix_mem_bwd, 7, dx2, x1, qm, kv, small["g_mem_q"],
                                                      wg["w_mem_q"], wg["w_mem_o"], wg["w_out"], tm_a)
    d_w_mem_kv, d_g_mem_kv = _mem_kv_bwd(dkv, memn, mem, small["g_mem_kv"], wg["w_mem_kv"])
    comm.grad("w_mem_kv", d_w_mem_kv)
    comm.grad("w_out", _weight_grad(ymix[None], dx1[None], "dw_out", tt)[0])
    comm.grad("w_mem_q", _weight_grad(hq[None], dqm[None], "dw_mem_q", tt)[0])
    comm.grad("w_mem_o", _weight_grad(om[None], dx2[None], "dw_mem_o", tt)[0])
    dq, dd, dsink = run("attn_bwd_q", _attn_bwd_q, 3, q, kd, vd, dya, lse, sink_b, bias_q, cos_t, sin_t)
    dk, dv = run("attn_bwd_kv", _attn_bwd_kv, 2, q, kd, vd, dya, lse, dd, bias_k, cos_t, sin_t)
    dpre, cstats = _conv_norm_bwd(pre, dyc, small["g_conv_ln"], small["b_conv_ln"], tc)
    du_glu, d_w_dw = run("conv_bwd", _conv_bwd, 2, dpre, ug, w_dw, tc)
    grad_x, du, d_b_in, d_g_mix = _in_proj_bwd(du_glu, dq, dk, dv, dx1, x, small["g_mix"], wg["w_in"], tm_a)
    comm.grad("w_in", _weight_grad(du[None], h1[None], "dw_in", tt)[0])

    grads = {
        "w_dw": jnp.sum(d_w_dw.reshape(32, 8, CONV_CH), axis=1)[0:CONV_W],
        "g_mix": d_g_mix, "b_in": d_b_in, "b_dw": cstats[2:3], "g_conv_ln": cstats[0:1],
        "b_conv_ln": cstats[1:2], "attn_sink": jnp.sum(dsink[:, :, 0], axis=0)[None, :], "b_out": d_b_out,
        "g_mem_q": d_g_mem_q, "g_mem_kv": d_g_mem_kv, "g_ffn": d_g_ffn, "g_final": d_g_final,
    }
    return loss[0:1, 0:1], grad_x, grads


BIG = ["w_in", "w_out", "w_mem_q", "w_mem_kv", "w_mem_o", "w_gate", "w_up", "w_down"]
KEEP_SLABS = ("w_mem_kv", "w_gate", "w_up", "w_down")
TRANSPOSED = ("w_in", "w_gate", "w_up")
SMALL = ["g_mix", "b_in", "b_dw", "g_conv_ln", "b_conv_ln", "attn_sink", "b_out", "g_mem_q", "g_mem_kv", "g_ffn", "g_final"]
PACK_ROWS = 32
ROWS_PER_STEP = 512

GATHER_ON = {"in_proj_fwd": ("w_mem_kv", "w_out"), "conv_fwd": ("w_mem_q", "w_mem_o", "w_gate"), "attn_fwd": ("w_up",),
             "mix_mem_fwd": ("w_down",)}
FFN_GROUP = ("w_gate", "w_up", "w_down")
MID_GROUP = ("w_mem_kv", "w_out", "w_mem_q", "w_mem_o")


def _as_weight(name, gathered):
    g = gathered.reshape(N_CHIPS, gathered.shape[2] * 2, gathered.shape[3])
    return g if name in KEEP_SLABS else g.reshape(-1, g.shape[2])


class _Overlap:
    def __init__(self, bufs, chip_idx, core_idx):
        self.bufs, self.chip_idx, self.core_idx = bufs, chip_idx, core_idx
        self.parts, self.sums, self.others = {}, {}, {}

    def rides(self, stage):
        if stage in GATHER_ON:
            return [_gather_ride([self.bufs[k] for k in GATHER_ON[stage]])]
        if stage == "mix_mem_bwd":
            return [_swap_ride([self.parts[k] for k in FFN_GROUP])]
        if stage == "attn_bwd_q":
            return [_exchange_ride([self.sums["w_gate"]]), _swap_ride([self.parts[k] for k in MID_GROUP])]
        if stage == "attn_bwd_kv":
            return [_exchange_ride([self.sums["w_up"], self.sums["w_down"]])]
        if stage == "conv_bwd":
            return [_exchange_ride([self.sums[k] for k in MID_GROUP])]
        return ()

    def landed(self, stage, brought, wg):
        if stage in GATHER_ON:
            for k, g in zip(GATHER_ON[stage], brought[0]):
                wg[k] = _as_weight(k, g)
        elif stage == "mix_mem_bwd":
            self._pair(FFN_GROUP, brought[0])
        elif stage == "attn_bwd_q":
            self.others["w_gate"] = brought[0][0]
            self._pair(MID_GROUP, brought[1])
        elif stage == "attn_bwd_kv":
            self.others["w_up"], self.others["w_down"] = brought[0]
        elif stage == "conv_bwd":
            self.others.update(zip(MID_GROUP, brought[0]))

    def grad(self, name, g):
        if g.ndim == 2:
            g = g.reshape(N_CHIPS, g.shape[0] // N_CHIPS, g.shape[1])
        self.parts[name] = g.reshape(N_CHIPS, 2, g.shape[1] // 2, g.shape[2])

    def _pair(self, names, from_sibling):
        for k, o in zip(names, from_sibling):
            self.sums[k] = _pair_sum(self.parts[k], o, self.core_idx, ROWS_PER_STEP)

    def finish(self):
        (from_sibling,) = _run_rides("swap_last", [_swap_ride([self.parts["w_in"]])])
        self._pair(("w_in",), from_sibling)
        ((self.others["w_in"],),) = _run_rides("exchange_last", [_exchange_ride([self.sums["w_in"]])])
        mine = [_chip_sum(self.sums[k], self.others[k], self.chip_idx, ROWS_PER_STEP) for k in BIG]
        (theirs,) = _run_rides("sibling_share", [_share_ride(mine)])
        return mine, theirs


def _pack_small(loss, grads):
    def row(a):
        a = a.reshape(1, -1)
        return jnp.pad(a, ((0, 0), (0, 1024 - a.shape[1])))

    rows = [row(grads[k]) for k in ("g_mix", "b_out", "g_mem_q", "g_mem_kv", "g_ffn", "g_final")]
    rows += [grads["b_in"][:, 0:1024], row(grads["b_in"][:, 1024:1792])]
    rows += [jnp.concatenate([grads["b_dw"], grads["g_conv_ln"]], axis=1), row(grads["b_conv_ln"]), row(grads["attn_sink"]),
             row(loss)]
    dw = jnp.pad(grads["w_dw"], ((0, 1), (0, 0))).reshape(16, 1024)
    pack = jnp.concatenate(rows + [dw], axis=0)
    return jnp.pad(pack, ((0, PACK_ROWS - pack.shape[0]), (0, 0)))


def _unpack_small(pack):
    out = {k: pack[i:i + 1] for i, k in enumerate(("g_mix", "b_out", "g_mem_q", "g_mem_kv", "g_ffn", "g_final"))}
    out["b_in"] = jnp.concatenate([pack[6:7], pack[7:8, 0:768]], axis=1)
    out["b_dw"], out["g_conv_ln"] = pack[8:9, 0:512], pack[8:9, 512:1024]
    out["b_conv_ln"] = pack[9:10, 0:512]
    out["attn_sink"] = pack[10:11, 0:8]
    loss = pack[11, 0]
    dw = pack[12:28].reshape(32, 512)[0:CONV_W]
    return loss, out, dw


def kernel(x, mem, g_mix, w_in, b_in, w_dw, b_dw, g_conv_ln, b_conv_ln, attn_sink, w_out, b_out, g_mem_q, g_mem_kv, w_mem_q, w_mem_kv, w_mem_o, g_ffn, w_gate, w_up, w_down, g_final, loss_target, m_g_mix, m_w_in, m_b_in, m_w_dw, m_b_dw, m_g_conv_ln, m_b_conv_ln, m_attn_sink, m_w_out, m_b_out, m_g_mem_q, m_g_mem_kv, m_w_mem_q, m_w_mem_kv, m_w_mem_o, m_g_ffn, m_w_gate, m_w_up, m_w_down, m_g_final, v_g_mix, v_w_in, v_b_in, v_w_dw, v_b_dw, v_g_conv_ln, v_b_conv_ln, v_attn_sink, v_w_out, v_b_out, v_g_mem_q, v_g_mem_kv, v_w_mem_q, v_w_mem_kv, v_w_mem_o, v_g_ffn, v_w_gate, v_w_up, v_w_down, v_g_final):
    args = dict(locals())
    weight_names = ["g_mix", "w_in", "b_in", "w_dw", "b_dw", "g_conv_ln", "b_conv_ln", "attn_sink", "w_out", "b_out", "g_mem_q",
                    "g_mem_kv", "w_mem_q", "w_mem_kv", "w_mem_o", "g_ffn", "w_gate", "w_up", "w_down", "g_final"]
    chip = 2 * lax.axis_index("x") + lax.axis_index("y")
    core = lax.axis_index("c")

    chip_idx = chip.astype(jnp.int32).reshape(1)
    core_idx = core.astype(jnp.int32).reshape(1)

    def block(name):
        a = args[name][0]
        weight = name[2:] if name[:2] in ("m_", "v_") else name
        return a.T if weight in TRANSPOSED else a

    comm = _Overlap({k: _cast_place(block(k), chip_idx, ROWS_PER_STEP) for k in BIG}, chip_idx, core_idx)
    dw_buf = lax.dynamic_update_slice(jnp.zeros((N_CHIPS, CONV_W, 128), F32), w_dw, (chip, 0, 0))
    (first,), (dw_all,) = _run_rides("gather_first", [_gather_ride([comm.bufs["w_in"]]), _spread_ride(dw_buf)])
    wg = {"w_in": _as_weight("w_in", first), "w_dw": jnp.transpose(dw_all, (1, 0, 2)).reshape(CONV_W, CONV_CH)}
    small = {k: args[k].reshape(1, -1) for k in SMALL}

    loss, grad_x, grads = _local_step(x[0], mem[0], loss_target[0], small, wg, comm)

    halves, other_halves = comm.finish()

    loss_sum, small_grads, dw_full = _unpack_small(_small_allreduce(_pack_small(loss, grads)))
    dw_cols = jnp.transpose(dw_full.reshape(CONV_W, N_CHIPS, 128), (1, 0, 2))
    small_grads["w_dw"] = lax.dynamic_index_in_dim(dw_cols, chip, axis=0, keepdims=False)

    out_g, out_d, out_m, out_v = {}, {}, {}, {}
    for k, g_mine, g_other in zip(BIG, halves, other_halves):
        res = _adamw(block(k), g_mine, g_other, block("m_" + k), block("v_" + k), core_idx, ROWS_PER_STEP)
        out_g[k], out_d[k], out_m[k], out_v[k] = [(r.T if k in TRANSPOSED else r)[None] for r in res]
    names = SMALL + ["w_dw"]

    def flat(a):
        return a[0] if a.ndim == 3 else a.reshape(1, -1)

    def pad_lanes(a):
        return jnp.pad(a, ((0, 0), (0, 128 - a.shape[1]))) if a.shape[1] < 128 else a

    ws = [flat(args[k]) for k in names]
    gs = [small_grads[k] for k in names]
    ms = [flat(args["m_" + k]) for k in names]
    vs = [flat(args["v_" + k]) for k in names]
    ds, mns, vns = _adamw_small([pad_lanes(a) for a in ws], [pad_lanes(a) for a in gs], [pad_lanes(a) for a in ms],
                                [pad_lanes(a) for a in vs])
    for i, k in enumerate(names):
        n_lanes = ws[i].shape[1]
        for out, val in ((out_g, gs[i]), (out_d, ds[i]), (out_m, mns[i]), (out_v, vns[i])):
            out[k] = val[:, 0:n_lanes].reshape(args[k].shape)

    return (loss_sum, grad_x[None], *[out_g[k] for k in weight_names], *[out_d[k] for k in weight_names],
            *[out_m[k] for k in weight_names], *[out_v[k] for k in weight_names])
```

```python
import jax
import jax.numpy as jnp
import numpy as np
from jax import lax
from jax.experimental import pallas as pl
from jax.experimental.pallas import tpu as pltpu

F32 = jnp.float32
BF16 = jnp.bfloat16
EPS = 1e-6
NEG = -1e30

D_MODEL = 1024
CONV_CH = 512
CONV_W = 31
HEAD_DIM = 64
BLK = 128
MEM_HEADS = 4
MEM_HD = 256
N_CHIPS = 4
N_DEV = 8
ATT_SCALE = HEAD_DIM ** -0.5
MEM_SCALE = MEM_HD ** -0.5
ROPE_THETA = 10000.0

ADAM_LR = 0.001
ADAM_B1 = 0.9
ADAM_B2 = 0.999
ADAM_EPS = 1e-08
ADAM_WD = 0.01
ADAM_STEP = 10

VMEM_LIMIT_BYTES = 56 * 1024 * 1024
MESH = pl.DeviceIdType.MESH


class _Ride:
    def __init__(self, operands, out_shape, n_sem, start, finish, aliases=None):
        self.operands, self.out_shape, self.n_sem = list(operands), list(out_shape), n_sem
        self.start, self.finish, self.aliases = start, finish, dict(aliases or {})


def _call(body, rides=(), **kw):
    if not rides:
        return pl.pallas_call(body, **kw)
    (n_steps,) = kw["grid"]
    n_in, n_out = len(kw["in_specs"]), len(kw["out_specs"])
    scratch = list(kw.get("scratch_shapes", ()))
    k_in = [len(r.operands) for r in rides]
    k_out = [len(r.out_shape) for r in rides]

    def carried(*refs):
        pos = n_in
        r_in, r_out = [], []
        for k in k_in:
            r_in.append(refs[pos:pos + k])
            pos += k
        own_out = refs[pos:pos + n_out]
        pos += n_out
        for k in k_out:
            r_out.append(refs[pos:pos + k])
            pos += k
        own_scratch = refs[pos:pos + len(scratch)]
        sems = refs[pos + len(scratch):]
        step = pl.program_id(0)

        @pl.when(step == 0)
        def _():
            for j, r in enumerate(rides):
                r.start(r_in[j], r_out[j], sems[2 * j], sems[2 * j + 1])

        body(*refs[:n_in], *own_out, *own_scratch)

        @pl.when(step == n_steps - 1)
        def _():
            for j, r in enumerate(rides):
                r.finish(r_in[j], r_out[j], sems[2 * j], sems[2 * j + 1])

    kw = dict(kw)
    kw["in_specs"] = list(kw["in_specs"]) + [ANY] * sum(k_in)
    kw["out_specs"] = list(kw["out_specs"]) + [ANY] * sum(k_out)
    kw["out_shape"] = list(kw["out_shape"]) + [s for r in rides for s in r.out_shape]
    kw["scratch_shapes"] = scratch + [pltpu.SemaphoreType.DMA((r.n_sem,)) for r in rides for _ in range(2)]
    aliases, off_in, off_out = {}, n_in, n_out
    for r, ki, ko in zip(rides, k_in, k_out):
        aliases.update({off_in + a: off_out + b for a, b in r.aliases.items()})
        off_in, off_out = off_in + ki, off_out + ko
    if aliases:
        kw["input_output_aliases"] = aliases
    call = pl.pallas_call(carried, **kw)
    return lambda *args: call(*args, *[op for r in rides for op in r.operands])


def _cp(n_grid):
    return pltpu.CompilerParams(dimension_semantics=("arbitrary",) * n_grid, vmem_limit_bytes=VMEM_LIMIT_BYTES)


def _res(shape):
    nd = len(shape)
    return pl.BlockSpec(shape, lambda *_: (0,) * nd, pipeline_mode=pl.Buffered(1))


def _rows(tm, n):
    return pl.BlockSpec((tm, n), lambda i: (i, 0))


def _div_tile(n, target):
    best = None
    for d in range(16, min(n, target) + 1, 16):
        if n % d == 0:
            best = d
    assert best is not None, (n, target)
    return best


def _dot(a, b):
    return jnp.dot(a, b, preferred_element_type=F32)


def _dot_nt(a, b):
    return lax.dot_general(a, b, (((1,), (1,)), ((), ())), preferred_element_type=F32)


def _dot_tn(a, b):
    return lax.dot_general(a, b, (((0,), (0,)), ((), ())), preferred_element_type=F32)


def _bf(x):
    return x.astype(BF16)


def _sigmoid(x):
    return 1.0 / (1.0 + jnp.exp(-x))


def _rms_fwd(x, g):
    r = lax.rsqrt(jnp.mean(x * x, axis=-1, keepdims=True) + EPS)
    xh = x * r
    return xh * g, xh, r


def _rms_bwd(dh, xh, r, g):
    dxh = dh * g
    return r * (dxh - xh * jnp.mean(dxh * xh, axis=-1, keepdims=True))


def _colsum(x):
    return jnp.sum(x, axis=0, keepdims=True)


def _rope(x, cos, sin, sign):
    n = x.shape[1] // 128
    c = jnp.tile(cos, (1, n)) if n > 1 else cos
    s = jnp.tile(sin, (1, n)) if n > 1 else sin
    lane = lax.broadcasted_iota(jnp.int32, x.shape, 1)
    first = (lane & 63) < 32
    partner = jnp.where(first, pltpu.roll(x, x.shape[1] - 32, 1), pltpu.roll(x, 32, 1))
    return x * c + sign * (partner * s)


def _lo_lanes(shape):
    return lax.broadcasted_iota(jnp.int32, shape, 1) < 64


def _stack_heads(t):
    t0, t1 = t[:, 0:128], t[:, 128:256]
    lo = _lo_lanes(t0.shape)
    z = jnp.zeros_like(t0)
    return jnp.concatenate([jnp.where(lo, t0, z), jnp.where(lo, z, t0), jnp.where(lo, t1, z), jnp.where(lo, z, t1)], axis=0)


def _unstack_heads(o):
    lo = _lo_lanes((BLK, 128))
    return jnp.concatenate([jnp.where(lo, o[0:128], o[128:256]), jnp.where(lo, o[256:384], o[384:512])], axis=1)


def _fold_heads(parts):
    a, b = (p + pltpu.roll(p, 64, 1) for p in parts)
    return jnp.where(_lo_lanes(a.shape), a, b)


def _sink_col(sk_ref, g):
    return jnp.concatenate([jnp.broadcast_to(sk_ref[4 * g + h:4 * g + h + 1, :], (BLK, 128)) for h in range(4)], axis=0)


def _tile3(x):
    return jnp.concatenate([x, x, x], axis=1)


def _mem_kv_fwd(mem, g_kv, w_kv):
    m_len = mem.shape[0]
    cols = w_kv.shape[2]

    def body(mem_ref, g_ref, w_ref, memn_ref, kv_ref):
        h, _, _ = _rms_fwd(mem_ref[...], g_ref[...])
        hb = _bf(h)
        memn_ref[...] = hb
        for s in range(N_CHIPS):
            kv_ref[s] = _bf(_dot(hb, w_ref[s]))

    return _call(
        body, name="mem_kv_fwd",
        out_shape=(jax.ShapeDtypeStruct((m_len, D_MODEL), BF16), jax.ShapeDtypeStruct((N_CHIPS, m_len, cols), BF16)),
        compiler_params=pltpu.CompilerParams(vmem_limit_bytes=VMEM_LIMIT_BYTES),
    )(mem, g_kv, w_kv)


def _dup_head_rows(w_ref, lo):
    h0, h1 = w_ref[lo:lo + 64, :], w_ref[lo + 64:lo + 128, :]
    return jnp.concatenate([h0, h0, h1, h1], axis=0)


def _in_proj_fwd(x, g_mix, w_t, b_ext, cos_t, sin_t, tm, rides=()):
    t_len = x.shape[0]

    def body(x_ref, g_ref, w_ref, b_ref, c_ref, s_ref, ug_ref, q_ref, k_ref, v_ref, h_ref):
        h, _, _ = _rms_fwd(x_ref[...], g_ref[...])
        hb = _bf(h)
        h_ref[...] = hb
        ug_ref[...] = _dot_nt(hb, w_ref[0:1024, :]) + b_ref[:, 0:1024]
        c, s = c_ref[...], s_ref[...]
        q_ref[...] = _bf(_rope(_dot_nt(hb, w_ref[1024:1536, :]) + b_ref[:, 1024:1536], c, s, 1.0))
        k_ref[...] = _bf(_rope(_dot_nt(hb, _dup_head_rows(w_ref, 1536)) + b_ref[:, 1536:1792], c, s, 1.0))
        v_ref[...] = _bf(_dot_nt(hb, _dup_head_rows(w_ref, 1664)) + b_ref[:, 1792:2048])

    return _call(
        body, rides=rides, name="in_proj_fwd", grid=(t_len // tm,),
        in_specs=[_rows(tm, D_MODEL), _res((1, D_MODEL)), _res(w_t.shape), _res(b_ext.shape), _rows(tm, 128), _rows(tm, 128)],
        out_specs=[_rows(tm, 1024), _rows(tm, 512), _rows(tm, 256), _rows(tm, 256), _rows(tm, D_MODEL)],
        out_shape=(jax.ShapeDtypeStruct((t_len, 1024), F32), jax.ShapeDtypeStruct((t_len, 512), BF16),
                   jax.ShapeDtypeStruct((t_len, 256), BF16), jax.ShapeDtypeStruct((t_len, 256), BF16),
                   jax.ShapeDtypeStruct((t_len, D_MODEL), BF16)),
        compiler_params=_cp(1),
    )(x, g_mix, w_t, b_ext, cos_t, sin_t)


def _halo_specs(tc, n, t_len):
    per = tc // 16
    last = t_len // 16 - 1
    return [pl.BlockSpec((16, n), lambda i: (jnp.maximum(i * per - 1, 0), 0)),
            pl.BlockSpec((tc, n), lambda i: (i, 0)),
            pl.BlockSpec((16, n), lambda i: (jnp.minimum((i + 1) * per, last), 0))]


def _glu(z):
    return z[:, 0:CONV_CH] * _sigmoid(z[:, CONV_CH:2 * CONV_CH])


def _fill_halo_buf(buf, prev, main, nxt, i, n_tiles, tc):
    buf[0:16, :] = jnp.where(i > 0, prev, jnp.zeros_like(prev))
    buf[16:16 + tc, :] = main
    buf[16 + tc:32 + tc, :] = jnp.where(i < n_tiles - 1, nxt, jnp.zeros_like(nxt))


CONV_ROWS = 64


def _shift_copies(buf, shifted, tc):
    for r in range(1, 8):
        shifted[r - 1, :, :] = buf[r:r + tc + 24, :]


def _shifted_rows(buf, shifted, offset, base):
    src = buf if offset % 8 == 0 else shifted.at[offset % 8 - 1]
    return src[pl.ds(pl.multiple_of(base + 8 * (offset // 8), 8), CONV_ROWS), :]


def _conv_fwd(ug, w_dw, b_dw, g_ln, b_ln, tc, rides=()):
    t_len = ug.shape[0]
    n_tiles = t_len // tc

    def body(up_ref, um_ref, un_ref, w_ref, bdw_ref, g_ref, b_ref, y_ref, pre_ref, buf, shifted):
        i = pl.program_id(0)
        _fill_halo_buf(buf, _glu(up_ref[...]), _glu(um_ref[...]), _glu(un_ref[...]), i, n_tiles, tc)
        _shift_copies(buf, shifted, tc)

        def chunk(c, carry):
            base = c * CONV_ROWS
            acc = jnp.zeros((CONV_ROWS, CONV_CH), F32)
            for k in range(CONV_W):
                acc = acc + w_ref[k:k + 1, :] * _shifted_rows(buf, shifted, k + 1, base)
            pre_ref[pl.ds(pl.multiple_of(base, CONV_ROWS), CONV_ROWS), :] = acc + bdw_ref[...]
            return carry

        lax.fori_loop(0, tc // CONV_ROWS, chunk, 0)
        pre = pre_ref[...]
        mu = jnp.mean(pre, axis=-1, keepdims=True)
        d = pre - mu
        rstd = lax.rsqrt(jnp.mean(d * d, axis=-1, keepdims=True) + EPS)
        ln = d * rstd * g_ref[...] + b_ref[...]
        y_ref[...] = _bf(ln * _sigmoid(ln))

    return _call(
        body, rides=rides, name="conv_fwd", grid=(n_tiles,),
        in_specs=_halo_specs(tc, 1024, t_len) + [_res((32, CONV_CH)), _res((1, CONV_CH)), _res((1, CONV_CH)), _res((1, CONV_CH))],
        out_specs=[_rows(tc, CONV_CH), _rows(tc, CONV_CH)],
        out_shape=(jax.ShapeDtypeStruct((t_len, CONV_CH), BF16), jax.ShapeDtypeStruct((t_len, CONV_CH), F32)),
        scratch_shapes=[pltpu.VMEM((tc + 32, CONV_CH), F32), pltpu.VMEM((7, tc + 24, CONV_CH), F32)],
        compiler_params=_cp(1),
    )(ug, ug, ug, w_dw, b_dw, g_ln, b_ln)


def _nbr_specs(n, nb):
    return [pl.BlockSpec((BLK, n), lambda i: (jnp.maximum(i - 1, 0), 0)),
            pl.BlockSpec((BLK, n), lambda i: (i, 0)),
            pl.BlockSpec((BLK, n), lambda i: (jnp.minimum(i + 1, nb - 1), 0))]


def _nbr_specs4(nb):
    return [pl.BlockSpec((1, 2, 4 * BLK, 128), lambda i: (jnp.maximum(i - 1, 0), 0, 0, 0)),
            pl.BlockSpec((1, 2, 4 * BLK, 128), lambda i: (i, 0, 0, 0)),
            pl.BlockSpec((1, 2, 4 * BLK, 128), lambda i: (jnp.minimum(i + 1, nb - 1), 0, 0, 0))]


def _band_bias():
    a = np.arange(4 * BLK)[:, None] % BLK
    c = np.arange(3 * BLK)[None, :]
    inside = np.abs(c - BLK - a) <= BLK
    q_side = np.stack([inside & (c >= BLK), inside, inside & (c < 2 * BLK)])
    blk = np.arange(12 * BLK)[:, None] // (4 * BLK)
    a = np.arange(12 * BLK)[:, None] % BLK
    c = np.arange(BLK)[None, :]
    inside = np.abs(c - a + (1 - blk) * BLK) <= BLK
    k_side = np.stack([inside & (blk >= 1), inside, inside & (blk <= 1)])
    return [jnp.asarray(np.where(m, 0.0, NEG).astype(np.float32)) for m in (q_side, k_side)]


def _edge_spec(shape, nb):
    return pl.BlockSpec((1,) + shape, lambda i: (jnp.where(i == 0, 0, jnp.where(i == nb - 1, 2, 1)),) + (0,) * len(shape))


def _attn_fwd(q, kd, vd, sink_b, bias, rides=()):
    t_len = q.shape[0]
    nb = t_len // BLK

    def body(q_ref, kp_ref, kc_ref, kn_ref, vp_ref, vc_ref, vn_ref, sk_ref, bias_ref, y_ref, lse_ref):
        kcat = jnp.concatenate([kp_ref[...], kc_ref[...], kn_ref[...]], axis=0)
        vcat = jnp.concatenate([vp_ref[...], vc_ref[...], vn_ref[...]], axis=0)
        ys = []
        for g in range(2):
            qs = _stack_heads(q_ref[:, 256 * g:256 * g + 256])
            s = _dot_nt(qs, kcat[:, 128 * g:128 * g + 128]) * ATT_SCALE + bias_ref[0]
            skc = _sink_col(sk_ref, g)
            m_b = jnp.maximum(jnp.max(s, axis=-1, keepdims=True), skc)
            p = jnp.exp(s - _tile3(m_b))
            den_b = jnp.sum(p, axis=-1, keepdims=True) + jnp.exp(skc - m_b)
            pn = p * _tile3(1.0 / den_b)
            o = _dot(_bf(pn), vcat[:, 128 * g:128 * g + 128])
            ys.append(_unstack_heads(o))
            lse_ref[0, g] = m_b + jnp.log(den_b)
        y_ref[...] = _bf(jnp.concatenate(ys, axis=1))

    return _call(
        body, rides=rides, name="attn_fwd", grid=(nb,),
        in_specs=[_rows(BLK, 512)] + _nbr_specs(256, nb) + _nbr_specs(256, nb) + [_res((8, 128)), _edge_spec((4 * BLK, 3 * BLK), nb)],
        out_specs=[_rows(BLK, 512), pl.BlockSpec((1, 2, 4 * BLK, 128), lambda i: (i, 0, 0, 0))],
        out_shape=(jax.ShapeDtypeStruct((t_len, 512), BF16), jax.ShapeDtypeStruct((nb, 2, 4 * BLK, 128), F32)),
        compiler_params=_cp(1),
    )(q, kd, kd, kd, vd, vd, vd, sink_b, bias)


def _mem_heads(kv_ref, h):
    lo = MEM_HD * (h % 2)
    return kv_ref[h // 2, :, lo:lo + MEM_HD], kv_ref[2 + h // 2, :, lo:lo + MEM_HD]


def _mix_mem_fwd(x, yc, ya, w_out, b_out, g_q, w_q, kv, w_o, tm, rides=()):
    t_len = x.shape[0]

    def body(x_ref, yc_ref, ya_ref, wout_ref, bout_ref, g_ref, wq_ref, kv_ref, wo_ref,
             ymix_ref, x1_ref, hq_ref, qm_ref, om_ref, x2_ref):
        ymix = jnp.concatenate([yc_ref[...], ya_ref[...]], axis=1)
        ymix_ref[...] = ymix
        x1 = x_ref[...] + _dot(ymix, wout_ref[...]) + bout_ref[...]
        x1_ref[...] = x1
        hq, _, _ = _rms_fwd(x1, g_ref[...])
        hqb = _bf(hq)
        hq_ref[...] = hqb
        qm = _bf(_dot(hqb, wq_ref[...]))
        qm_ref[...] = qm
        outs = []
        for h in range(MEM_HEADS):
            kh, vh = _mem_heads(kv_ref, h)
            s = _dot_nt(qm[:, MEM_HD * h:MEM_HD * (h + 1)], kh) * MEM_SCALE
            p = jnp.exp(s - jnp.max(s, axis=-1, keepdims=True))
            p = p * (1.0 / jnp.sum(p, axis=-1, keepdims=True))
            outs.append(_dot(_bf(p), vh))
        om = _bf(jnp.concatenate(outs, axis=1))
        om_ref[...] = om
        x2_ref[...] = x1 + _dot(om, wo_ref[...])

    act_b = jax.ShapeDtypeStruct((t_len, D_MODEL), BF16)
    act_f = jax.ShapeDtypeStruct((t_len, D_MODEL), F32)
    return _call(
        body, rides=rides, name="mix_mem_fwd", grid=(t_len // tm,),
        in_specs=[_rows(tm, D_MODEL), _rows(tm, 512), _rows(tm, 512), _res(w_out.shape), _res((1, D_MODEL)), _res((1, D_MODEL)),
                  _res(w_q.shape), _res(kv.shape), _res(w_o.shape)],
        out_specs=[_rows(tm, D_MODEL)] * 6,
        out_shape=(act_b, act_f, act_b, act_b, act_b, act_f),
        compiler_params=_cp(1),
    )(x, yc, ya, w_out, b_out, g_q, w_q, kv, w_o)


def _hidden_chunks(ff, width=1024):
    return [(lo, min(lo + width, ff)) for lo in range(0, ff, width)]


def _ffn_loss(x2, g_ffn, w_gate, w_up, w_down, g_final, target, tm):
    t_len = x2.shape[0]
    ff = w_gate.shape[0]
    n_tiles = t_len // tm

    def body(x2_ref, g_ref, wg_ref, wu_ref, wd_ref, gf_ref, tgt_ref,
             hf_ref, gate_ref, up_ref, act_ref, dx3_ref, loss_ref, dgf_ref):
        i = pl.program_id(0)
        x2v = x2_ref[...]
        hf, _, _ = _rms_fwd(x2v, g_ref[...])
        hfb = _bf(hf)
        hf_ref[...] = hfb
        acc = jnp.zeros((tm, D_MODEL), F32)
        for lo, hi in _hidden_chunks(ff):
            gate = _dot_nt(hfb, wg_ref[lo:hi, :])
            up = _dot_nt(hfb, wu_ref[lo:hi, :])
            act = _bf(gate * _sigmoid(gate) * up)
            gate_ref[:, lo:hi] = _bf(gate)
            up_ref[:, lo:hi] = _bf(up)
            act_ref[:, lo:hi] = act
            acc = acc + _dot(act, wd_ref[lo:hi, :])
        x3 = x2v + acc
        gf = gf_ref[...]
        y, xh, r = _rms_fwd(x3, gf)
        err = y - tgt_ref[...]
        part = 0.5 * jnp.sum(jnp.mean(err * err, axis=-1, keepdims=True), axis=0, keepdims=True)
        dy = err * (1.0 / D_MODEL)
        dx3_ref[...] = _rms_bwd(dy, xh, r, gf)

        @pl.when(i == 0)
        def _():
            loss_ref[...] = jnp.zeros_like(loss_ref)
            dgf_ref[...] = jnp.zeros_like(dgf_ref)

        loss_ref[...] += jnp.broadcast_to(part, loss_ref.shape)
        dgf_ref[...] += _colsum(dy * xh)

    hid = jax.ShapeDtypeStruct((t_len, ff), BF16)
    hid_spec = _rows(tm, ff)
    return _call(
        body, name="ffn_loss", grid=(n_tiles,),
        in_specs=[_rows(tm, D_MODEL), _res((1, D_MODEL)), _res(w_gate.shape), _res(w_up.shape), _res(w_down.shape),
                  _res((1, D_MODEL)), _rows(tm, D_MODEL)],
        out_specs=[_rows(tm, D_MODEL), hid_spec, hid_spec, hid_spec, _rows(tm, D_MODEL),
                   pl.BlockSpec((1, D_MODEL), lambda i: (0, 0)), pl.BlockSpec((1, D_MODEL), lambda i: (0, 0))],
        out_shape=(jax.ShapeDtypeStruct((t_len, D_MODEL), BF16), hid, hid, hid, jax.ShapeDtypeStruct((t_len, D_MODEL), F32),
                   jax.ShapeDtypeStruct((1, D_MODEL), F32), jax.ShapeDtypeStruct((1, D_MODEL), F32)),
        compiler_params=_cp(1),
    )(x2, g_ffn, w_gate, w_up, w_down, g_final, target)


def _ffn_bwd(dx3, x2, gate, up, g_ffn, w_gate, w_up, w_down, tm):
    t_len = x2.shape[0]
    ff = w_gate.shape[0]

    def body(dx3_ref, x2_ref, gate_ref, up_ref, g_ref, wg_ref, wu_ref, wd_ref, dx2_ref, dgate_ref, dup_ref, dg_ref):
        i = pl.program_id(0)
        dx3 = dx3_ref[...]
        d3b = _bf(dx3)
        dh = jnp.zeros((tm, D_MODEL), F32)
        for lo, hi in _hidden_chunks(ff):
            dact = _dot_nt(d3b, wd_ref[lo:hi, :])
            gt = gate_ref[:, lo:hi].astype(F32)
            u = up_ref[:, lo:hi].astype(F32)
            sg = _sigmoid(gt)
            dup = _bf(dact * (gt * sg))
            dgate = _bf(dact * u * (sg * (1.0 + gt * (1.0 - sg))))
            dup_ref[:, lo:hi] = dup
            dgate_ref[:, lo:hi] = dgate
            dh = dh + _dot(dgate, wg_ref[lo:hi, :]) + _dot(dup, wu_ref[lo:hi, :])
        g = g_ref[...]
        _, xh, r = _rms_fwd(x2_ref[...], g)
        dx2_ref[...] = dx3 + _rms_bwd(dh, xh, r, g)

        @pl.when(i == 0)
        def _():
            dg_ref[...] = jnp.zeros_like(dg_ref)

        dg_ref[...] += _colsum(dh * xh)

    hid = jax.ShapeDtypeStruct((t_len, ff), BF16)
    hid_spec = _rows(tm, ff)
    return _call(
        body, name="ffn_bwd", grid=(t_len // tm,),
        in_specs=[_rows(tm, D_MODEL), _rows(tm, D_MODEL), hid_spec, hid_spec, _res((1, D_MODEL)),
                  _res(w_gate.shape), _res(w_up.shape), _res(w_down.shape)],
        out_specs=[_rows(tm, D_MODEL), hid_spec, hid_spec, pl.BlockSpec((1, D_MODEL), lambda i: (0, 0))],
        out_shape=(jax.ShapeDtypeStruct((t_len, D_MODEL), F32), hid, hid, jax.ShapeDtypeStruct((1, D_MODEL), F32)),
        compiler_params=_cp(1),
    )(dx3, x2, gate, up, g_ffn, w_gate, w_up, w_down)


def _mix_mem_bwd(dx2, x1, qm, kv, g_q, w_q, w_o, w_out, tm, rides=()):
    t_len = x1.shape[0]
    m_len = kv.shape[1]

    def body(dx2_ref, x1_ref, qm_ref, kv_ref, g_ref, wq_ref, wo_ref, wout_ref,
             dx1_ref, dqm_ref, dyc_ref, dya_ref, dkv_ref, dgq_ref, dbout_ref):
        i = pl.program_id(0)

        @pl.when(i == 0)
        def _():
            dkv_ref[...] = jnp.zeros_like(dkv_ref)
            dgq_ref[...] = jnp.zeros_like(dgq_ref)
            dbout_ref[...] = jnp.zeros_like(dbout_ref)

        dx2 = dx2_ref[...]
        dom = _dot_nt(_bf(dx2), wo_ref[...])
        dqs = []
        for h in range(MEM_HEADS):
            kh, vh = _mem_heads(kv_ref, h)
            qh = qm_ref[:, MEM_HD * h:MEM_HD * (h + 1)]
            s = _dot_nt(qh, kh) * MEM_SCALE
            p = jnp.exp(s - jnp.max(s, axis=-1, keepdims=True))
            p = p * (1.0 / jnp.sum(p, axis=-1, keepdims=True))
            domh = _bf(dom[:, MEM_HD * h:MEM_HD * (h + 1)])
            dp = _dot_nt(domh, vh)
            ds = _bf(p * (dp - jnp.sum(p * dp, axis=-1, keepdims=True)) * MEM_SCALE)
            dqs.append(_dot(ds, kh))
            lo = MEM_HD * (h % 2)
            dkv_ref[h // 2, :, lo:lo + MEM_HD] += _dot_tn(ds, qh)
            dkv_ref[2 + h // 2, :, lo:lo + MEM_HD] += _dot_tn(_bf(p), domh)
        dqm = _bf(jnp.concatenate(dqs, axis=1))
        dqm_ref[...] = dqm
        dhq = _dot_nt(dqm, wq_ref[...])
        g = g_ref[...]
        _, xh, r = _rms_fwd(x1_ref[...], g)
        dx1 = dx2 + _rms_bwd(dhq, xh, r, g)
        dx1_ref[...] = dx1
        dgq_ref[...] += _colsum(dhq * xh)
        dbout_ref[...] += _colsum(dx1)
        dymix = _dot_nt(_bf(dx1), wout_ref[...])
        dyc_ref[...] = dymix[:, 0:CONV_CH]
        dya_ref[...] = _bf(dymix[:, CONV_CH:2 * CONV_CH])

    vec = pl.BlockSpec((1, D_MODEL), lambda i: (0, 0))
    return _call(
        body, rides=rides, name="mix_mem_bwd", grid=(t_len // tm,),
        in_specs=[_rows(tm, D_MODEL), _rows(tm, D_MODEL), _rows(tm, D_MODEL), _res(kv.shape), _res((1, D_MODEL)),
                  _res(w_q.shape), _res(w_o.shape), _res(w_out.shape)],
        out_specs=[_rows(tm, D_MODEL), _rows(tm, D_MODEL), _rows(tm, CONV_CH), _rows(tm, CONV_CH),
                   pl.BlockSpec(kv.shape, lambda i: (0, 0, 0)), vec, vec],
        out_shape=(jax.ShapeDtypeStruct((t_len, D_MODEL), F32), jax.ShapeDtypeStruct((t_len, D_MODEL), BF16),
                   jax.ShapeDtypeStruct((t_len, CONV_CH), F32), jax.ShapeDtypeStruct((t_len, CONV_CH), BF16),
                   jax.ShapeDtypeStruct((N_CHIPS, m_len, kv.shape[2]), F32),
                   jax.ShapeDtypeStruct((1, D_MODEL), F32), jax.ShapeDtypeStruct((1, D_MODEL), F32)),
        compiler_params=_cp(1),
    )(dx2, x1, qm, kv, g_q, w_q, w_o, w_out)


def _mem_kv_bwd(dkv, memn, mem, g_kv, w_kv):
    m_len = mem.shape[0]

    def body(dkv_ref, memn_ref, mem_ref, g_ref, w_ref, dw_ref, dg_ref):
        hb = memn_ref[...]
        dmn = jnp.zeros((m_len, D_MODEL), F32)
        for s in range(N_CHIPS):
            d = _bf(dkv_ref[s])
            dw_ref[s] = _bf(_dot_tn(hb, d))
            dmn = dmn + _dot_nt(d, w_ref[s])
        _, xh, _ = _rms_fwd(mem_ref[...], g_ref[...])
        dg_ref[...] = _colsum(dmn * xh)

    return _call(
        body, name="mem_kv_bwd",
        out_shape=(jax.ShapeDtypeStruct(w_kv.shape, BF16), jax.ShapeDtypeStruct((1, D_MODEL), F32)),
        compiler_params=pltpu.CompilerParams(vmem_limit_bytes=VMEM_LIMIT_BYTES),
    )(dkv, memn, mem, g_kv, w_kv)


def _attn_bwd_q(q, kd, vd, dya, lse, sink_b, bias, cos_t, sin_t, rides=()):
    t_len = q.shape[0]
    nb = t_len // BLK

    def body(q_ref, kp_ref, kc_ref, kn_ref, vp_ref, vc_ref, vn_ref, do_ref, lse_ref, sk_ref, bias_ref, c_ref, s_ref,
             dq_ref, dd_ref, dsk_ref):
        kcat = jnp.concatenate([kp_ref[...], kc_ref[...], kn_ref[...]], axis=0)
        vcat = jnp.concatenate([vp_ref[...], vc_ref[...], vn_ref[...]], axis=0)
        dqs, dsks = [], []
        for g in range(2):
            qs = _stack_heads(q_ref[:, 256 * g:256 * g + 256])
            dos = _stack_heads(do_ref[:, 256 * g:256 * g + 256])
            kk = kcat[:, 128 * g:128 * g + 128]
            s = _dot_nt(qs, kk) * ATT_SCALE + bias_ref[0]
            lse_b = lse_ref[0, g]
            p = jnp.exp(s - _tile3(lse_b))
            dp = _dot_nt(dos, vcat[:, 128 * g:128 * g + 128])
            drow = jnp.sum(p * dp, axis=-1, keepdims=True)
            ds = _bf(p * (dp - drow) * ATT_SCALE)
            dqs.append(_unstack_heads(_dot(ds, kk)))
            d_b = jnp.broadcast_to(drow, (4 * BLK, 128))
            dd_ref[0, g] = d_b
            contrib = -(jnp.exp(_sink_col(sk_ref, g) - lse_b) * d_b)
            dsks.append(jnp.sum(contrib.reshape(4, BLK, 128), axis=1))
        dq = jnp.concatenate(dqs, axis=1)
        dq_ref[...] = _bf(_rope(dq, c_ref[...], s_ref[...], -1.0))
        dsk_ref[0] = jnp.concatenate(dsks, axis=0)

    stat = pl.BlockSpec((1, 2, 4 * BLK, 128), lambda i: (i, 0, 0, 0))
    return _call(
        body, rides=rides, name="attn_bwd_q", grid=(nb,),
        in_specs=[_rows(BLK, 512)] + _nbr_specs(256, nb) + _nbr_specs(256, nb)
        + [_rows(BLK, 512), stat, _res((8, 128)), _edge_spec((4 * BLK, 3 * BLK), nb), _rows(BLK, 128), _rows(BLK, 128)],
        out_specs=[_rows(BLK, 512), stat, pl.BlockSpec((1, 8, 128), lambda i: (i, 0, 0))],
        out_shape=(jax.ShapeDtypeStruct((t_len, 512), BF16), jax.ShapeDtypeStruct((nb, 2, 4 * BLK, 128), F32),
                   jax.ShapeDtypeStruct((nb, 8, 128), F32)),
        compiler_params=_cp(1),
    )(q, kd, kd, kd, vd, vd, vd, dya, lse, sink_b, bias, cos_t, sin_t)


def _attn_bwd_kv(q, kd, vd, dya, lse, dd, bias, cos_t, sin_t, rides=()):
    t_len = q.shape[0]
    nb = t_len // BLK

    def body(kc_ref, vc_ref, qp_ref, qc_ref, qn_ref, dop_ref, doc_ref, don_ref, lp_ref, lc_ref, ln_ref,
             dp_ref, dc_ref, dn_ref, bias_ref, c_ref, s_ref, dk_ref, dv_ref):
        dks, dvs = [], []
        for g in range(2):
            cols = slice(256 * g, 256 * g + 256)
            qs = jnp.concatenate([_stack_heads(r[:, cols]) for r in (qp_ref, qc_ref, qn_ref)], axis=0)
            dos = jnp.concatenate([_stack_heads(r[:, cols]) for r in (dop_ref, doc_ref, don_ref)], axis=0)
            lse_b = jnp.concatenate([r[0, g] for r in (lp_ref, lc_ref, ln_ref)], axis=0)
            d_b = jnp.concatenate([r[0, g] for r in (dp_ref, dc_ref, dn_ref)], axis=0)
            kk = kc_ref[:, 128 * g:128 * g + 128]
            s = _dot_nt(qs, kk) * ATT_SCALE + bias_ref[0]
            p = jnp.exp(s - lse_b)
            dp = _dot_nt(dos, vc_ref[:, 128 * g:128 * g + 128])
            ds = _bf(p * (dp - d_b) * ATT_SCALE)
            dvs.append(_dot_tn(_bf(p), dos))
            dks.append(_dot_tn(ds, qs))
        dk_ref[...] = _bf(_rope(_fold_heads(dks), c_ref[...], s_ref[...], -1.0))
        dv_ref[...] = _bf(_fold_heads(dvs))

    return _call(
        body, rides=rides, name="attn_bwd_kv", grid=(nb,),
        in_specs=[_rows(BLK, 256), _rows(BLK, 256)] + _nbr_specs(512, nb) + _nbr_specs(512, nb) + _nbr_specs4(nb) + _nbr_specs4(nb)
        + [_edge_spec((12 * BLK, BLK), nb), _rows(BLK, 128), _rows(BLK, 128)],
        out_specs=[_rows(BLK, 128), _rows(BLK, 128)],
        out_shape=(jax.ShapeDtypeStruct((t_len, 128), BF16), jax.ShapeDtypeStruct((t_len, 128), BF16)),
        compiler_params=_cp(1),
    )(kd, vd, q, q, q, dya, dya, dya, lse, lse, lse, dd, dd, dd, bias, cos_t, sin_t)


def _conv_norm_bwd(pre, dyc, g_ln, b_ln, tc):
    t_len = pre.shape[0]

    def body(pre_ref, dy_ref, g_ref, b_ref, dpre_ref, stats_ref):
        i = pl.program_id(0)
        pre_v = pre_ref[...]
        mu = jnp.mean(pre_v, axis=-1, keepdims=True)
        d = pre_v - mu
        rstd = lax.rsqrt(jnp.mean(d * d, axis=-1, keepdims=True) + EPS)
        xh = d * rstd
        g = g_ref[...]
        ln = xh * g + b_ref[...]
        sg = _sigmoid(ln)
        dln = dy_ref[...] * (sg * (1.0 + ln * (1.0 - sg)))
        dxh = dln * g
        dpre = rstd * (dxh - jnp.mean(dxh, axis=-1, keepdims=True) - xh * jnp.mean(dxh * xh, axis=-1, keepdims=True))
        dpre_ref[...] = dpre

        @pl.when(i == 0)
        def _():
            stats_ref[...] = jnp.zeros_like(stats_ref)

        stats_ref[0:1, :] += _colsum(dln * xh)
        stats_ref[1:2, :] += _colsum(dln)
        stats_ref[2:3, :] += _colsum(dpre)

    return _call(
        body, name="conv_norm_bwd", grid=(t_len // tc,),
        in_specs=[_rows(tc, CONV_CH), _rows(tc, CONV_CH), _res((1, CONV_CH)), _res((1, CONV_CH))],
        out_specs=[_rows(tc, CONV_CH), pl.BlockSpec((8, CONV_CH), lambda i: (0, 0))],
        out_shape=(jax.ShapeDtypeStruct((t_len, CONV_CH), F32), jax.ShapeDtypeStruct((8, CONV_CH), F32)),
        compiler_params=_cp(1),
    )(pre, dyc, g_ln, b_ln)


def _conv_bwd(dpre, ug, w_dw, tc, rides=()):
    t_len = ug.shape[0]
    n_tiles = t_len // tc

    def body(dp_ref, dm_ref, dn_ref, up_ref, um_ref, un_ref, w_ref, du_ref, dw_ref, dbuf, vbuf, dshift, vshift):
        i = pl.program_id(0)
        _fill_halo_buf(dbuf, dp_ref[...], dm_ref[...], dn_ref[...], i, n_tiles, tc)
        _fill_halo_buf(vbuf, _glu(up_ref[...]), _glu(um_ref[...]), _glu(un_ref[...]), i, n_tiles, tc)
        _shift_copies(dbuf, dshift, tc)
        _shift_copies(vbuf, vshift, tc)

        @pl.when(i == 0)
        def _():
            dw_ref[...] = jnp.zeros_like(dw_ref)

        def chunk(c, carry):
            base = c * CONV_ROWS
            rows = pl.ds(pl.multiple_of(base, CONV_ROWS), CONV_ROWS)
            dmain = dm_ref[rows, :]
            dv = jnp.zeros((CONV_ROWS, CONV_CH), F32)
            for k in range(CONV_W):
                dv = dv + w_ref[k:k + 1, :] * _shifted_rows(dbuf, dshift, 31 - k, base)
                prod = dmain * _shifted_rows(vbuf, vshift, k + 1, base)
                dw_ref[8 * k:8 * k + 8, :] += jnp.sum(prod.reshape(CONV_ROWS // 8, 8, CONV_CH), axis=0)
            um = um_ref[rows, :]
            a, gt = um[:, 0:CONV_CH], um[:, CONV_CH:2 * CONV_CH]
            sg = _sigmoid(gt)
            du_ref[rows, :] = _bf(jnp.concatenate([dv * sg, dv * a * (sg * (1.0 - sg))], axis=1))
            return carry

        lax.fori_loop(0, tc // CONV_ROWS, chunk, 0)

    shifts = pltpu.VMEM((7, tc + 24, CONV_CH), F32)
    return _call(
        body, rides=rides, name="conv_bwd", grid=(n_tiles,),
        in_specs=_halo_specs(tc, CONV_CH, t_len) + _halo_specs(tc, 1024, t_len) + [_res((32, CONV_CH))],
        out_specs=[_rows(tc, 1024), pl.BlockSpec((8 * 32, CONV_CH), lambda i: (0, 0))],
        out_shape=(jax.ShapeDtypeStruct((t_len, 1024), BF16), jax.ShapeDtypeStruct((8 * 32, CONV_CH), F32)),
        scratch_shapes=[pltpu.VMEM((tc + 32, CONV_CH), F32), pltpu.VMEM((tc + 32, CONV_CH), F32), shifts, shifts],
        compiler_params=_cp(1),
    )(dpre, dpre, dpre, ug, ug, ug, w_dw)


def _in_proj_bwd(du_glu, dq, dk, dv, dx1, x, g_mix, w_t, tm, rides=()):
    t_len = x.shape[0]
    n_ext = w_t.shape[0]

    def body(dg_ref, dq_ref, dk_ref, dv_ref, dx1_ref, x_ref, g_ref, w_ref, dx_ref, du_ref, db_ref, dgm_ref):
        i = pl.program_id(0)
        du = jnp.concatenate([dg_ref[...], dq_ref[...], dk_ref[...], dv_ref[...]], axis=1)
        du_ref[...] = du
        dh = _dot(du, w_ref[...])
        g = g_ref[...]
        _, xh, r = _rms_fwd(x_ref[...], g)
        dx_ref[...] = dx1_ref[...] + _rms_bwd(dh, xh, r, g)

        @pl.when(i == 0)
        def _():
            db_ref[...] = jnp.zeros_like(db_ref)
            dgm_ref[...] = jnp.zeros_like(dgm_ref)

        db_ref[...] += _colsum(du.astype(F32))
        dgm_ref[...] += _colsum(dh * xh)

    return _call(
        body, rides=rides, name="in_proj_bwd", grid=(t_len // tm,),
        in_specs=[_rows(tm, 1024), _rows(tm, 512), _rows(tm, 128), _rows(tm, 128), _rows(tm, D_MODEL), _rows(tm, D_MODEL),
                  _res((1, D_MODEL)), _res(w_t.shape)],
        out_specs=[_rows(tm, D_MODEL), _rows(tm, n_ext), pl.BlockSpec((1, n_ext), lambda i: (0, 0)),
                   pl.BlockSpec((1, D_MODEL), lambda i: (0, 0))],
        out_shape=(jax.ShapeDtypeStruct((t_len, D_MODEL), F32), jax.ShapeDtypeStruct((t_len, n_ext), BF16),
                   jax.ShapeDtypeStruct((1, n_ext), F32), jax.ShapeDtypeStruct((1, D_MODEL), F32)),
        compiler_params=_cp(1),
    )(du_glu, dq, dk, dv, dx1, x, g_mix, w_t)


def _weight_grad(a, d, name, tt):
    t_len, k_dim = a.shape
    n_dim = d.shape[1]
    tk = k_dim if k_dim <= 1792 else k_dim // 2
    assert k_dim % tk == 0 and tk % 128 == 0 and n_dim % 128 == 0
    tt = min(tt, t_len)
    n_t = t_len // tt

    def body(a_ref, d_ref, o_ref, acc):
        t = pl.program_id(1)

        @pl.when(t == 0)
        def _():
            acc[...] = jnp.zeros_like(acc)

        acc[...] += _dot_tn(_bf(a_ref[...]), _bf(d_ref[...]))

        @pl.when(t == n_t - 1)
        def _():
            o_ref[...] = _bf(acc[...])

    return _call(
        body, name=name, grid=(k_dim // tk, n_t),
        in_specs=[pl.BlockSpec((tt, tk), lambda k, t: (t, k)), pl.BlockSpec((tt, n_dim), lambda k, t: (t, 0))],
        out_specs=pl.BlockSpec((tk, n_dim), lambda k, t: (k, 0)),
        out_shape=jax.ShapeDtypeStruct((k_dim, n_dim), BF16),
        scratch_shapes=[pltpu.VMEM((tk, n_dim), F32)],
        compiler_params=_cp(2),
    )(a, d)


ANY = pl.BlockSpec(memory_space=pl.ANY)


def _place():
    x, y, c = lax.axis_index("x"), lax.axis_index("y"), lax.axis_index("c")
    chips = [(1 - x, y), (x, 1 - y), (1 - x, 1 - y)]
    return x, y, c, chips


def _remote(src, dst, send_sems, recv_sems, k, to):
    return pltpu.make_async_remote_copy(src_ref=src, dst_ref=dst, send_sem=send_sems.at[k], recv_sem=recv_sems.at[k],
                                        device_id=to, device_id_type=MESH)


def _cast_place(w, chip_idx, tr):
    rows, cols = w.shape
    h = rows // 2
    tr = _div_tile(h, tr)
    per = h // tr

    def body(s_ref, w_ref, o_ref):
        o_ref[0, 0] = _bf(w_ref[...])

    return _call(
        body, name="cast_place",
        grid_spec=pltpu.PrefetchScalarGridSpec(
            num_scalar_prefetch=1, grid=(2, per),
            in_specs=[pl.BlockSpec((tr, cols), lambda hh, r, s_ref: (hh * per + r, 0))],
            out_specs=pl.BlockSpec((1, 1, tr, cols), lambda hh, r, s_ref: (s_ref[0], hh, r, 0))),
        out_shape=jax.ShapeDtypeStruct((N_CHIPS, 2, h, cols), BF16),
        compiler_params=_cp(2),
    )(chip_idx, w)


def _same(arrays):
    return [jax.ShapeDtypeStruct(a.shape, a.dtype) for a in arrays]


def _gather_ride(bufs):
    n = len(bufs)

    def first_hop(outs, send, recv):
        x, y, c, chips = _place()
        mine = [outs[i].at[2 * x + y, c] for i in range(n)]
        return [_remote(mine[i], mine[i], send, recv, 3 * i + j, (cx, cy, c)) for i in range(n) for j, (cx, cy) in enumerate(chips)]

    def start(ins, outs, send, recv):
        for cp in first_hop(outs, send, recv):
            cp.start()

    def finish(ins, outs, send, recv):
        x, y, c, chips = _place()
        sib = (x, y, 1 - c)
        onward = []
        for i in range(n):
            for j, (cx, cy) in enumerate(chips):
                slab = outs[i].at[2 * cx + cy, c]
                _remote(slab, slab, send, recv, 3 * i + j, sib).wait_recv()
                onward.append(_remote(slab, slab, send, recv, 3 * n + 3 * i + j, sib))
                onward[-1].start()
        for i in range(n):
            for j, (cx, cy) in enumerate(chips):
                other = outs[i].at[2 * cx + cy, 1 - c]
                _remote(other, other, send, recv, 3 * n + 3 * i + j, sib).wait_recv()
        for cp in first_hop(outs, send, recv) + onward:
            cp.wait_send()

    return _Ride(bufs, _same(bufs), 6 * n, start, finish, aliases={i: i for i in range(n)})


def _spread_ride(buf):
    def sends(outs, send, recv):
        x, y, c, chips = _place()
        mine = outs[0].at[2 * x + y]
        return [_remote(mine, mine, send, recv, j, (cx, cy, c)) for j, (cx, cy) in enumerate(chips)]

    def start(ins, outs, send, recv):
        for cp in sends(outs, send, recv):
            cp.start()

    def finish(ins, outs, send, recv):
        _, _, c, chips = _place()
        for j, (cx, cy) in enumerate(chips):
            slab = outs[0].at[2 * cx + cy]
            _remote(slab, slab, send, recv, j, (cx, cy, c)).wait_recv()
        for cp in sends(outs, send, recv):
            cp.wait_send()

    return _Ride([buf], _same([buf]), 3, start, finish, aliases={0: 0})


def _pairwise_ride(arrays, out_shape, n_sem, copies):
    def start(ins, outs, send, recv):
        for cp in copies(ins, outs, send, recv):
            cp.start()

    def finish(ins, outs, send, recv):
        for cp in copies(ins, outs, send, recv):
            cp.wait()

    return _Ride(arrays, out_shape, n_sem, start, finish)


def _run_rides(name, rides):
    k_in = [len(r.operands) for r in rides]
    k_out = [len(r.out_shape) for r in rides]

    def body(*refs):
        pos, r_in, r_out = 0, [], []
        for k in k_in:
            r_in.append(refs[pos:pos + k])
            pos += k
        for k in k_out:
            r_out.append(refs[pos:pos + k])
            pos += k
        sems = refs[pos:]
        for j, r in enumerate(rides):
            r.start(r_in[j], r_out[j], sems[2 * j], sems[2 * j + 1])
        for j, r in enumerate(rides):
            r.finish(r_in[j], r_out[j], sems[2 * j], sems[2 * j + 1])

    aliases, off_in, off_out = {}, 0, 0
    for r, ki, ko in zip(rides, k_in, k_out):
        aliases.update({off_in + a: off_out + b for a, b in r.aliases.items()})
        off_in, off_out = off_in + ki, off_out + ko
    res = _call(
        body, name=name, in_specs=[ANY] * sum(k_in), out_specs=[ANY] * sum(k_out),
        out_shape=[s for r in rides for s in r.out_shape], input_output_aliases=aliases,
        scratch_shapes=[pltpu.SemaphoreType.DMA((r.n_sem,)) for r in rides for _ in range(2)],
    )(*[op for r in rides for op in r.operands])
    out, pos = [], 0
    for k in k_out:
        out.append(list(res[pos:pos + k]))
        pos += k
    return out


def _swap_ride(grads):
    def copies(ins, outs, send, recv):
        x, y, c, _ = _place()
        return [_remote(ins[i].at[:, 1 - c], outs[i], send, recv, i, (x, y, 1 - c)) for i in range(len(grads))]

    out_shape = [jax.ShapeDtypeStruct((g.shape[0],) + g.shape[2:], g.dtype) for g in grads]
    return _pairwise_ride(grads, out_shape, len(grads), copies)


def _pair_sum(grad, other, c_idx, tr):
    n_s, _, h, cols = grad.shape
    tr = _div_tile(h, tr)

    def body(c_ref, a_ref, b_ref, o_ref):
        o_ref[...] = _bf(a_ref[0].astype(F32) + b_ref[...].astype(F32))

    return _call(
        body, name="pair_sum",
        grid_spec=pltpu.PrefetchScalarGridSpec(
            num_scalar_prefetch=1, grid=(n_s, h // tr),
            in_specs=[pl.BlockSpec((1, 1, tr, cols), lambda s, r, c_ref: (s, c_ref[0], r, 0)),
                      pl.BlockSpec((1, tr, cols), lambda s, r, c_ref: (s, r, 0))],
            out_specs=pl.BlockSpec((1, tr, cols), lambda s, r, c_ref: (s, r, 0))),
        out_shape=jax.ShapeDtypeStruct((n_s, h, cols), BF16),
        compiler_params=_cp(2),
    )(c_idx, grad, other)


def _exchange_ride(sums):
    def copies(ins, outs, send, recv):
        _, _, c, chips = _place()
        return [_remote(ins[i].at[2 * cx + cy], outs[i].at[j], send, recv, 3 * i + j, (cx, cy, c))
                for i in range(len(sums)) for j, (cx, cy) in enumerate(chips)]

    out_shape = [jax.ShapeDtypeStruct((3,) + s.shape[1:], s.dtype) for s in sums]
    return _pairwise_ride(sums, out_shape, 3 * len(sums), copies)


def _chip_sum(own, others, chip_idx, tr):
    _, h, cols = own.shape
    tr = _div_tile(h, tr)

    def body(s_ref, a_ref, p_ref, o_ref):
        acc = a_ref[0].astype(F32)
        for j in range(N_CHIPS - 1):
            acc = acc + p_ref[j].astype(F32)
        o_ref[...] = acc

    return _call(
        body, name="chip_sum",
        grid_spec=pltpu.PrefetchScalarGridSpec(
            num_scalar_prefetch=1, grid=(h // tr,),
            in_specs=[pl.BlockSpec((1, tr, cols), lambda r, s_ref: (s_ref[0], r, 0)),
                      pl.BlockSpec((N_CHIPS - 1, tr, cols), lambda r, s_ref: (0, r, 0))],
            out_specs=pl.BlockSpec((tr, cols), lambda r, s_ref: (r, 0))),
        out_shape=jax.ShapeDtypeStruct((h, cols), F32),
        compiler_params=_cp(1),
    )(chip_idx, own, others)


def _share_ride(halves):
    def copies(ins, outs, send, recv):
        x, y, c, _ = _place()
        return [_remote(ins[i], outs[i], send, recv, i, (x, y, 1 - c)) for i in range(len(halves))]

    return _pairwise_ride(halves, _same(halves), len(halves), copies)


def _small_allreduce(pack):
    rows, cols = pack.shape

    def body(p_ref, o_ref, buf, send_sems, recv_sems):
        x, y, c, _ = _place()
        me = 4 * x + 2 * y + c
        buf[me] = p_ref[...]
        sent = []
        for k in range(1, N_DEV):
            fx, fy, fc = (k >> 2) & 1, (k >> 1) & 1, k & 1
            to = (x ^ fx, y ^ fy, c ^ fc)
            cp = _remote(p_ref, buf.at[me], send_sems, recv_sems, k - 1, to)
            cp.start()
            sent.append(cp)
        for k in range(1, N_DEV):
            fx, fy, fc = (k >> 2) & 1, (k >> 1) & 1, k & 1
            frm = 4 * (x ^ fx) + 2 * (y ^ fy) + (c ^ fc)
            _remote(p_ref, buf.at[frm], send_sems, recv_sems, k - 1, (x, y, c)).wait_recv()
        for cp in sent:
            cp.wait_send()
        acc = buf[0]
        for d in range(1, N_DEV):
            acc = acc + buf[d]
        o_ref[...] = acc

    return _call(
        body, name="small_allreduce",
        in_specs=[pl.BlockSpec(memory_space=pltpu.VMEM)], out_specs=pl.BlockSpec(memory_space=pltpu.VMEM),
        out_shape=jax.ShapeDtypeStruct(pack.shape, F32),
        scratch_shapes=[pltpu.VMEM((N_DEV, rows, cols), F32), pltpu.SemaphoreType.DMA((N_DEV - 1,)),
                        pltpu.SemaphoreType.DMA((N_DEV - 1,))],
    )(pack)


def _adamw_math(w, g, m, v):
    m_new = ADAM_B1 * m + (1.0 - ADAM_B1) * g
    v_new = ADAM_B2 * v + (1.0 - ADAM_B2) * (g * g)
    m_hat = m_new * (1.0 / (1.0 - ADAM_B1 ** ADAM_STEP))
    v_hat = v_new * (1.0 / (1.0 - ADAM_B2 ** ADAM_STEP))
    delta = -ADAM_LR * (m_hat / (jnp.sqrt(v_hat) + ADAM_EPS) + ADAM_WD * w)
    return delta, m_new, v_new


def _adamw(w, g_mine, g_other, m, v, core_idx, tr):
    rows, cols = w.shape
    h = rows // 2
    tr = _div_tile(h, tr)
    per = h // tr

    def body(c_ref, w_ref, ga_ref, gb_ref, m_ref, v_ref, g_ref, d_ref, mo_ref, vo_ref):
        g = jnp.where(pl.program_id(0) == c_ref[0], ga_ref[...], gb_ref[...])
        d, mn, vn = _adamw_math(w_ref[...], g, m_ref[...], v_ref[...])
        g_ref[...] = g
        d_ref[...] = d
        mo_ref[...] = mn
        vo_ref[...] = vn

    full = pl.BlockSpec((tr, cols), lambda hh, r, c_ref: (hh * per + r, 0))
    mine = pl.BlockSpec((tr, cols), lambda hh, r, c_ref: (jnp.where(hh == c_ref[0], r, 0), 0))
    other = pl.BlockSpec((tr, cols), lambda hh, r, c_ref: (jnp.where(hh == c_ref[0], 0, r), 0))
    shp = jax.ShapeDtypeStruct(w.shape, F32)
    return _call(
        body, name="adamw",
        grid_spec=pltpu.PrefetchScalarGridSpec(num_scalar_prefetch=1, grid=(2, per), in_specs=[full, mine, other, full, full],
                                               out_specs=[full] * 4),
        out_shape=(shp, shp, shp, shp), compiler_params=_cp(2))(core_idx, w, g_mine, g_other, m, v)


def _adamw_small(ws, gs, ms, vs):
    n = len(ws)

    def body(*refs):
        w_r, g_r, m_r, v_r = refs[0:n], refs[n:2 * n], refs[2 * n:3 * n], refs[3 * n:4 * n]
        d_o, m_o, v_o = refs[4 * n:5 * n], refs[5 * n:6 * n], refs[6 * n:7 * n]
        for i in range(n):
            d, mn, vn = _adamw_math(w_r[i][...], g_r[i][...], m_r[i][...], v_r[i][...])
            d_o[i][...] = d
            m_o[i][...] = mn
            v_o[i][...] = vn

    shp = [jax.ShapeDtypeStruct(w.shape, F32) for w in ws]
    outs = _call(body, name="adamw_small", out_shape=shp * 3)(*ws, *gs, *ms, *vs)
    return outs[0:n], outs[n:2 * n], outs[2 * n:3 * n]


def _rope_tables(t_len):
    pos = jnp.arange(t_len, dtype=F32)
    inv_freq = ROPE_THETA ** (-jnp.arange(0, HEAD_DIM, 2, dtype=F32) / HEAD_DIM)
    ang = pos[:, None] * inv_freq[None, :]
    cos, sin = jnp.cos(ang), jnp.sin(ang)
    return jnp.tile(jnp.concatenate([cos, cos], axis=1), (1, 2)), jnp.tile(jnp.concatenate([-sin, sin], axis=1), (1, 2))


def _dup_heads(a):
    h0, h1 = a[..., 0:64], a[..., 64:128]
    return jnp.concatenate([h0, h0, h1, h1], axis=-1)


def _local_step(x, mem, target, small, wg, comm, tm_a=512, tm_b=256, tc=512, tt=1024):
    def run(stage, fn, n_own, *operands):
        rides = comm.rides(stage)
        res = list(fn(*operands, rides=rides))
        brought, pos = [], n_own
        for r in rides:
            brought.append(res[pos:pos + len(r.out_shape)])
            pos += len(r.out_shape)
        comm.landed(stage, brought, wg)
        return res[:n_own]

    t_len = x.shape[0]
    cos_t, sin_t = _rope_tables(t_len)
    b_in = small["b_in"]
    b_ext = jnp.concatenate([b_in[:, 0:1536], _dup_heads(b_in[:, 1536:1664]), _dup_heads(b_in[:, 1664:1792])], axis=1)
    w_dw = jnp.concatenate([wg["w_dw"], jnp.zeros((1, CONV_CH), F32)], axis=0)
    sink_b = jnp.broadcast_to(small["attn_sink"].reshape(8, 1), (8, 128))
    bias_q, bias_k = _band_bias()

    ug, q, kd, vd, h1 = run("in_proj_fwd", _in_proj_fwd, 5, x, small["g_mix"], wg["w_in"], b_ext, cos_t, sin_t, tm_a)
    yc, pre = run("conv_fwd", _conv_fwd, 2, ug, w_dw, small["b_dw"], small["g_conv_ln"], small["b_conv_ln"], tc)
    memn, kv = _mem_kv_fwd(mem, small["g_mem_kv"], wg["w_mem_kv"])
    ya, lse = run("attn_fwd", _attn_fwd, 2, q, kd, vd, sink_b, bias_q)
    ymix, x1, hq, qm, om, x2 = run("mix_mem_fwd", _mix_mem_fwd, 6, x, yc, ya, wg["w_out"], small["b_out"], small["g_mem_q"],
                                   wg["w_mem_q"], kv, wg["w_mem_o"], tm_a)
    hf, gate, up, act, dx3, loss, d_g_final = _ffn_loss(x2, small["g_ffn"], wg["w_gate"], wg["w_up"], wg["w_down"],
                                                        small["g_final"], target, tm_b)

    dx2, dgate, dup, d_g_ffn = _ffn_bwd(dx3, x2, gate, up, small["g_ffn"], wg["w_gate"], wg["w_up"], wg["w_down"], tm_b)
    comm.grad("w_gate", _weight_grad(dgate, hf, "dw_gate", tt))
    comm.grad("w_up", _weight_grad(dup, hf, "dw_up", tt))
    comm.grad("w_down", _weight_grad(act, dx3, "dw_down", tt))
    dx1, dqm, dyc, dya, dkv, d_g_mem_q, d_b_out = run("mix_mem_bwd", _mix_mem_bwd, 7, dx2, x1, qm, kv, small["g_mem_q"],
                                                      wg["w_mem_q"], wg["w_mem_o"], wg["w_out"], tm_a)
    d_w_mem_kv, d_g_mem_kv = _mem_kv_bwd(dkv, memn, mem, small["g_mem_kv"], wg["w_mem_kv"])
    comm.grad("w_mem_kv", d_w_mem_kv)
    comm.grad("w_out", _weight_grad(ymix, dx1, "dw_out", tt))
    comm.grad("w_mem_q", _weight_grad(hq, dqm, "dw_mem_q", tt))
    comm.grad("w_mem_o", _weight_grad(om, dx2, "dw_mem_o", tt))
    dq, dd, dsink = run("attn_bwd_q", _attn_bwd_q, 3, q, kd, vd, dya, lse, sink_b, bias_q, cos_t, sin_t)
    dk, dv = run("attn_bwd_kv", _attn_bwd_kv, 2, q, kd, vd, dya, lse, dd, bias_k, cos_t, sin_t)
    dpre, cstats = _conv_norm_bwd(pre, dyc, small["g_conv_ln"], small["b_conv_ln"], tc)
    du_glu, d_w_dw = run("conv_bwd", _conv_bwd, 2, dpre, ug, w_dw, tc)
    grad_x, du, d_b_in, d_g_mix = run("in_proj_bwd", _in_proj_bwd, 4, du_glu, dq, dk, dv, dx1, x, small["g_mix"], wg["w_in"],
                                      tm_a)
    comm.grad("w_in", _weight_grad(du, h1, "dw_in", tt))

    grads = {
        "w_dw": jnp.sum(d_w_dw.reshape(32, 8, CONV_CH), axis=1)[0:CONV_W],
        "g_mix": d_g_mix, "b_in": d_b_in, "b_dw": cstats[2:3], "g_conv_ln": cstats[0:1],
        "b_conv_ln": cstats[1:2], "attn_sink": jnp.sum(dsink[:, :, 0], axis=0)[None, :], "b_out": d_b_out,
        "g_mem_q": d_g_mem_q, "g_mem_kv": d_g_mem_kv, "g_ffn": d_g_ffn, "g_final": d_g_final,
    }
    return loss[0:1, 0:1], grad_x, grads


BIG = ["w_in", "w_out", "w_mem_q", "w_mem_kv", "w_mem_o", "w_gate", "w_up", "w_down"]
KEEP_SLABS = ("w_mem_kv",)
TRANSPOSED = ("w_in", "w_gate", "w_up")
SMALL = ["g_mix", "b_in", "b_dw", "g_conv_ln", "b_conv_ln", "attn_sink", "b_out", "g_mem_q", "g_mem_kv", "g_ffn", "g_final"]
PACK_ROWS = 32
ROWS_PER_STEP = 512

GATHER_ON = {"in_proj_fwd": ("w_out", "w_mem_q"), "conv_fwd": ("w_mem_kv", "w_mem_o"), "attn_fwd": ("w_gate",),
             "mix_mem_fwd": ("w_up", "w_down")}
FFN_GROUP = ("w_gate", "w_up", "w_down")
MID_GROUP = ("w_mem_kv", "w_out", "w_mem_q", "w_mem_o")
SWAP_ON = {"mix_mem_bwd": FFN_GROUP, "attn_bwd_q": MID_GROUP}
EXCHANGE_ON = {"attn_bwd_q": ("w_gate",), "attn_bwd_kv": ("w_up", "w_mem_kv"), "conv_bwd": ("w_down", "w_out", "w_mem_q"),
               "in_proj_bwd": ("w_mem_o",)}


def _as_weight(name, gathered):
    g = gathered.reshape(N_CHIPS, gathered.shape[2] * 2, gathered.shape[3])
    return g if name in KEEP_SLABS else g.reshape(-1, g.shape[2])


class _Overlap:
    def __init__(self, bufs, chip_idx, core_idx):
        self.bufs, self.chip_idx, self.core_idx = bufs, chip_idx, core_idx
        self.parts, self.sums, self.others = {}, {}, {}

    def rides(self, stage):
        rides = []
        if stage in GATHER_ON:
            rides.append(_gather_ride([self.bufs[k] for k in GATHER_ON[stage]]))
        if stage in EXCHANGE_ON:
            rides.append(_exchange_ride([self.sums[k] for k in EXCHANGE_ON[stage]]))
        if stage in SWAP_ON:
            rides.append(_swap_ride([self.parts[k] for k in SWAP_ON[stage]]))
        return rides

    def landed(self, stage, brought, wg):
        brought = list(brought)
        if stage in GATHER_ON:
            for k, g in zip(GATHER_ON[stage], brought.pop(0)):
                wg[k] = _as_weight(k, g)
        if stage in EXCHANGE_ON:
            self.others.update(zip(EXCHANGE_ON[stage], brought.pop(0)))
        if stage in SWAP_ON:
            self._pair(SWAP_ON[stage], brought.pop(0))

    def grad(self, name, g):
        if g.ndim == 2:
            g = g.reshape(N_CHIPS, g.shape[0] // N_CHIPS, g.shape[1])
        self.parts[name] = g.reshape(N_CHIPS, 2, g.shape[1] // 2, g.shape[2])

    def _pair(self, names, from_sibling):
        for k, o in zip(names, from_sibling):
            self.sums[k] = _pair_sum(self.parts[k], o, self.core_idx, ROWS_PER_STEP)

    def finish(self):
        (from_sibling,) = _run_rides("swap_last", [_swap_ride([self.parts["w_in"]])])
        self._pair(("w_in",), from_sibling)
        ((self.others["w_in"],),) = _run_rides("exchange_last", [_exchange_ride([self.sums["w_in"]])])
        mine = [_chip_sum(self.sums[k], self.others[k], self.chip_idx, ROWS_PER_STEP) for k in BIG]
        (theirs,) = _run_rides("sibling_share", [_share_ride(mine)])
        return mine, theirs


def _pack_small(loss, grads):
    def row(a):
        a = a.reshape(1, -1)
        return jnp.pad(a, ((0, 0), (0, 1024 - a.shape[1])))

    rows = [row(grads[k]) for k in ("g_mix", "b_out", "g_mem_q", "g_mem_kv", "g_ffn", "g_final")]
    rows += [grads["b_in"][:, 0:1024], row(grads["b_in"][:, 1024:1792])]
    rows += [jnp.concatenate([grads["b_dw"], grads["g_conv_ln"]], axis=1), row(grads["b_conv_ln"]), row(grads["attn_sink"]),
             row(loss)]
    dw = jnp.pad(grads["w_dw"], ((0, 1), (0, 0))).reshape(16, 1024)
    pack = jnp.concatenate(rows + [dw], axis=0)
    return jnp.pad(pack, ((0, PACK_ROWS - pack.shape[0]), (0, 0)))


def _unpack_small(pack):
    out = {k: pack[i:i + 1] for i, k in enumerate(("g_mix", "b_out", "g_mem_q", "g_mem_kv", "g_ffn", "g_final"))}
    out["b_in"] = jnp.concatenate([pack[6:7], pack[7:8, 0:768]], axis=1)
    out["b_dw"], out["g_conv_ln"] = pack[8:9, 0:512], pack[8:9, 512:1024]
    out["b_conv_ln"] = pack[9:10, 0:512]
    out["attn_sink"] = pack[10:11, 0:8]
    loss = pack[11, 0]
    dw = pack[12:28].reshape(32, 512)[0:CONV_W]
    return loss, out, dw


def kernel(x, mem, g_mix, w_in, b_in, w_dw, b_dw, g_conv_ln, b_conv_ln, attn_sink, w_out, b_out, g_mem_q, g_mem_kv, w_mem_q, w_mem_kv, w_mem_o, g_ffn, w_gate, w_up, w_down, g_final, loss_target, m_g_mix, m_w_in, m_b_in, m_w_dw, m_b_dw, m_g_conv_ln, m_b_conv_ln, m_attn_sink, m_w_out, m_b_out, m_g_mem_q, m_g_mem_kv, m_w_mem_q, m_w_mem_kv, m_w_mem_o, m_g_ffn, m_w_gate, m_w_up, m_w_down, m_g_final, v_g_mix, v_w_in, v_b_in, v_w_dw, v_b_dw, v_g_conv_ln, v_b_conv_ln, v_attn_sink, v_w_out, v_b_out, v_g_mem_q, v_g_mem_kv, v_w_mem_q, v_w_mem_kv, v_w_mem_o, v_g_ffn, v_w_gate, v_w_up, v_w_down, v_g_final):
    args = dict(locals())
    weight_names = ["g_mix", "w_in", "b_in", "w_dw", "b_dw", "g_conv_ln", "b_conv_ln", "attn_sink", "w_out", "b_out", "g_mem_q",
                    "g_mem_kv", "w_mem_q", "w_mem_kv", "w_mem_o", "g_ffn", "w_gate", "w_up", "w_down", "g_final"]
    chip = 2 * lax.axis_index("x") + lax.axis_index("y")
    core = lax.axis_index("c")

    chip_idx = chip.astype(jnp.int32).reshape(1)
    core_idx = core.astype(jnp.int32).reshape(1)

    def block(name):
        a = args[name][0]
        weight = name[2:] if name[:2] in ("m_", "v_") else name
        return a.T if weight in TRANSPOSED else a

    comm = _Overlap({k: _cast_place(block(k), chip_idx, ROWS_PER_STEP) for k in BIG}, chip_idx, core_idx)
    dw_buf = lax.dynamic_update_slice(jnp.zeros((N_CHIPS, CONV_W, 128), F32), w_dw, (chip, 0, 0))
    (first,), (dw_all,) = _run_rides("gather_first", [_gather_ride([comm.bufs["w_in"]]), _spread_ride(dw_buf)])
    wg = {"w_in": _as_weight("w_in", first), "w_dw": jnp.transpose(dw_all, (1, 0, 2)).reshape(CONV_W, CONV_CH)}
    small = {k: args[k].reshape(1, -1) for k in SMALL}

    loss, grad_x, grads = _local_step(x[0], mem[0], loss_target[0], small, wg, comm)

    halves, other_halves = comm.finish()

    loss_sum, small_grads, dw_full = _unpack_small(_small_allreduce(_pack_small(loss, grads)))
    dw_cols = jnp.transpose(dw_full.reshape(CONV_W, N_CHIPS, 128), (1, 0, 2))
    small_grads["w_dw"] = lax.dynamic_index_in_dim(dw_cols, chip, axis=0, keepdims=False)

    out_g, out_d, out_m, out_v = {}, {}, {}, {}
    for k, g_mine, g_other in zip(BIG, halves, other_halves):
        res = _adamw(block(k), g_mine, g_other, block("m_" + k), block("v_" + k), core_idx, ROWS_PER_STEP)
        out_g[k], out_d[k], out_m[k], out_v[k] = [(r.T if k in TRANSPOSED else r)[None] for r in res]
    names = SMALL + ["w_dw"]

    def flat(a):
        return a[0] if a.ndim == 3 else a.reshape(1, -1)

    def pad_lanes(a):
        return jnp.pad(a, ((0, 0), (0, 128 - a.shape[1]))) if a.shape[1] < 128 else a

    ws = [flat(args[k]) for k in names]
    gs = [small_grads[k] for k in names]
    ms = [flat(args["m_" + k]) for k in names]
    vs = [flat(args["v_" + k]) for k in names]
    ds, mns, vns = _adamw_small([pad_lanes(a) for a in ws], [pad_lanes(a) for a in gs], [pad_lanes(a) for a in ms],
                                [pad_lanes(a) for a in vs])
    for i, k in enumerate(names):
        n_lanes = ws[i].shape[1]
        for out, val in ((out_g, gs[i]), (out_d, ds[i]), (out_m, mns[i]), (out_v, vns[i])):
            out[k] = val[:, 0:n_lanes].reshape(args[k].shape)

    return (loss_sum, grad_x[None], *[out_g[k] for k in weight_names], *[out_d[k] for k in weight_names],
            *[out_m[k] for k in weight_names], *[out_v[k] for k in weight_names])
```

```python
import jax
import jax.numpy as jnp
import numpy as np
from jax import lax
from jax.experimental import pallas as pl
from jax.experimental.pallas import tpu as pltpu

F32 = jnp.float32
BF16 = jnp.bfloat16
EPS = 1e-6
NEG = -1e30

D_MODEL = 1024
CONV_CH = 512
CONV_W = 31
HEAD_DIM = 64
BLK = 128
MEM_HEADS = 4
MEM_HD = 256
N_CHIPS = 4
N_DEV = 8
ATT_SCALE = HEAD_DIM ** -0.5
MEM_SCALE = MEM_HD ** -0.5
ROPE_THETA = 10000.0

ADAM_LR = 0.001
ADAM_B1 = 0.9
ADAM_B2 = 0.999
ADAM_EPS = 1e-08
ADAM_WD = 0.01
ADAM_STEP = 10

VMEM_LIMIT_BYTES = 56 * 1024 * 1024
MESH = pl.DeviceIdType.MESH


class _Ride:
    def __init__(self, operands, out_shape, n_sem, start, finish, aliases=None):
        self.operands, self.out_shape, self.n_sem = list(operands), list(out_shape), n_sem
        self.start, self.finish, self.aliases = start, finish, dict(aliases or {})


def _call(body, rides=(), **kw):
    if not rides:
        return pl.pallas_call(body, **kw)
    grid = kw["grid"]
    n_in, n_out = len(kw["in_specs"]), len(kw["out_specs"])
    scratch = list(kw.get("scratch_shapes", ()))
    k_in = [len(r.operands) for r in rides]
    k_out = [len(r.out_shape) for r in rides]

    def carried(*refs):
        pos = n_in
        r_in, r_out = [], []
        for k in k_in:
            r_in.append(refs[pos:pos + k])
            pos += k
        own_out = refs[pos:pos + n_out]
        pos += n_out
        for k in k_out:
            r_out.append(refs[pos:pos + k])
            pos += k
        own_scratch = refs[pos:pos + len(scratch)]
        sems = refs[pos + len(scratch):]
        first = last = None
        for axis, n_steps in enumerate(grid):
            step = pl.program_id(axis)
            first = (step == 0) if first is None else first & (step == 0)
            last = (step == n_steps - 1) if last is None else last & (step == n_steps - 1)

        @pl.when(first)
        def _():
            for j, r in enumerate(rides):
                r.start(r_in[j], r_out[j], sems[2 * j], sems[2 * j + 1])

        body(*refs[:n_in], *own_out, *own_scratch)

        @pl.when(last)
        def _():
            for j, r in enumerate(rides):
                r.finish(r_in[j], r_out[j], sems[2 * j], sems[2 * j + 1])

    kw = dict(kw)
    kw["in_specs"] = list(kw["in_specs"]) + [ANY] * sum(k_in)
    kw["out_specs"] = list(kw["out_specs"]) + [ANY] * sum(k_out)
    kw["out_shape"] = list(kw["out_shape"]) + [s for r in rides for s in r.out_shape]
    kw["scratch_shapes"] = scratch + [pltpu.SemaphoreType.DMA((r.n_sem,)) for r in rides for _ in range(2)]
    aliases, off_in, off_out = {}, n_in, n_out
    for r, ki, ko in zip(rides, k_in, k_out):
        aliases.update({off_in + a: off_out + b for a, b in r.aliases.items()})
        off_in, off_out = off_in + ki, off_out + ko
    if aliases:
        kw["input_output_aliases"] = aliases
    call = pl.pallas_call(carried, **kw)
    return lambda *args: call(*args, *[op for r in rides for op in r.operands])


def _cp(n_grid):
    return pltpu.CompilerParams(dimension_semantics=("arbitrary",) * n_grid, vmem_limit_bytes=VMEM_LIMIT_BYTES)


def _res(shape):
    nd = len(shape)
    return pl.BlockSpec(shape, lambda *_: (0,) * nd, pipeline_mode=pl.Buffered(1))


def _rows(tm, n):
    return pl.BlockSpec((tm, n), lambda i: (i, 0))


def _div_tile(n, target):
    best = None
    for d in range(16, min(n, target) + 1, 16):
        if n % d == 0:
            best = d
    assert best is not None, (n, target)
    return best


def _dot(a, b):
    return jnp.dot(a, b, preferred_element_type=F32)


def _dot_nt(a, b):
    return lax.dot_general(a, b, (((1,), (1,)), ((), ())), preferred_element_type=F32)


def _dot_tn(a, b):
    return lax.dot_general(a, b, (((0,), (0,)), ((), ())), preferred_element_type=F32)


def _bf(x):
    return x.astype(BF16)


def _sigmoid(x):
    return 1.0 / (1.0 + jnp.exp(-x))


def _rms_fwd(x, g):
    r = lax.rsqrt(jnp.mean(x * x, axis=-1, keepdims=True) + EPS)
    xh = x * r
    return xh * g, xh, r


def _rms_bwd(dh, xh, r, g):
    dxh = dh * g
    return r * (dxh - xh * jnp.mean(dxh * xh, axis=-1, keepdims=True))


def _colsum(x):
    return jnp.sum(x, axis=0, keepdims=True)


def _rope(x, cos, sin, sign):
    n = x.shape[1] // 128
    c = jnp.tile(cos, (1, n)) if n > 1 else cos
    s = jnp.tile(sin, (1, n)) if n > 1 else sin
    lane = lax.broadcasted_iota(jnp.int32, x.shape, 1)
    first = (lane & 63) < 32
    partner = jnp.where(first, pltpu.roll(x, x.shape[1] - 32, 1), pltpu.roll(x, 32, 1))
    return x * c + sign * (partner * s)


def _lo_lanes(shape):
    return lax.broadcasted_iota(jnp.int32, shape, 1) < 64


def _stack_heads(t):
    t0, t1 = t[:, 0:128], t[:, 128:256]
    lo = _lo_lanes(t0.shape)
    z = jnp.zeros_like(t0)
    return jnp.concatenate([jnp.where(lo, t0, z), jnp.where(lo, z, t0), jnp.where(lo, t1, z), jnp.where(lo, z, t1)], axis=0)


def _unstack_heads(o):
    lo = _lo_lanes((BLK, 128))
    return jnp.concatenate([jnp.where(lo, o[0:128], o[128:256]), jnp.where(lo, o[256:384], o[384:512])], axis=1)


def _fold_heads(parts):
    a, b = (p + pltpu.roll(p, 64, 1) for p in parts)
    return jnp.where(_lo_lanes(a.shape), a, b)


def _sink_col(sk_ref, g):
    return jnp.concatenate([jnp.broadcast_to(sk_ref[4 * g + h:4 * g + h + 1, :], (BLK, 128)) for h in range(4)], axis=0)


def _tile3(x):
    return jnp.concatenate([x, x, x], axis=1)


def _mem_kv_fwd(mem, g_kv, w_kv):
    m_len = mem.shape[0]
    cols = w_kv.shape[2]

    def body(mem_ref, g_ref, w_ref, memn_ref, kv_ref):
        h, _, _ = _rms_fwd(mem_ref[...], g_ref[...])
        hb = _bf(h)
        memn_ref[...] = hb
        for s in range(N_CHIPS):
            kv_ref[s] = _bf(_dot(hb, w_ref[s]))

    return _call(
        body, name="mem_kv_fwd",
        out_shape=(jax.ShapeDtypeStruct((m_len, D_MODEL), BF16), jax.ShapeDtypeStruct((N_CHIPS, m_len, cols), BF16)),
        compiler_params=pltpu.CompilerParams(vmem_limit_bytes=VMEM_LIMIT_BYTES),
    )(mem, g_kv, w_kv)


def _dup_head_rows(w_ref, lo):
    h0, h1 = w_ref[lo:lo + 64, :], w_ref[lo + 64:lo + 128, :]
    return jnp.concatenate([h0, h0, h1, h1], axis=0)


def _in_proj_fwd(x, g_mix, w_t, b_ext, cos_t, sin_t, tm, rides=()):
    t_len = x.shape[0]

    def body(x_ref, g_ref, w_ref, b_ref, c_ref, s_ref, ug_ref, q_ref, k_ref, v_ref, h_ref):
        h, _, _ = _rms_fwd(x_ref[...], g_ref[...])
        hb = _bf(h)
        h_ref[...] = hb
        ug_ref[...] = _dot_nt(hb, w_ref[0:1024, :]) + b_ref[:, 0:1024]
        c, s = c_ref[...], s_ref[...]
        q_ref[...] = _bf(_rope(_dot_nt(hb, w_ref[1024:1536, :]) + b_ref[:, 1024:1536], c, s, 1.0))
        k_ref[...] = _bf(_rope(_dot_nt(hb, _dup_head_rows(w_ref, 1536)) + b_ref[:, 1536:1792], c, s, 1.0))
        v_ref[...] = _bf(_dot_nt(hb, _dup_head_rows(w_ref, 1664)) + b_ref[:, 1792:2048])

    return _call(
        body, rides=rides, name="in_proj_fwd", grid=(t_len // tm,),
        in_specs=[_rows(tm, D_MODEL), _res((1, D_MODEL)), _res(w_t.shape), _res(b_ext.shape), _rows(tm, 128), _rows(tm, 128)],
        out_specs=[_rows(tm, 1024), _rows(tm, 512), _rows(tm, 256), _rows(tm, 256), _rows(tm, D_MODEL)],
        out_shape=(jax.ShapeDtypeStruct((t_len, 1024), F32), jax.ShapeDtypeStruct((t_len, 512), BF16),
                   jax.ShapeDtypeStruct((t_len, 256), BF16), jax.ShapeDtypeStruct((t_len, 256), BF16),
                   jax.ShapeDtypeStruct((t_len, D_MODEL), BF16)),
        compiler_params=_cp(1),
    )(x, g_mix, w_t, b_ext, cos_t, sin_t)


def _halo_specs(tc, n, t_len):
    per = tc // 16
    last = t_len // 16 - 1
    return [pl.BlockSpec((16, n), lambda i: (jnp.maximum(i * per - 1, 0), 0)),
            pl.BlockSpec((tc, n), lambda i: (i, 0)),
            pl.BlockSpec((16, n), lambda i: (jnp.minimum((i + 1) * per, last), 0))]


def _glu(z):
    return z[:, 0:CONV_CH] * _sigmoid(z[:, CONV_CH:2 * CONV_CH])


def _fill_halo_buf(buf, prev, main, nxt, i, n_tiles, tc):
    buf[0:16, :] = jnp.where(i > 0, prev, jnp.zeros_like(prev))
    buf[16:16 + tc, :] = main
    buf[16 + tc:32 + tc, :] = jnp.where(i < n_tiles - 1, nxt, jnp.zeros_like(nxt))


CONV_ROWS = 64


def _shift_copies(buf, shifted, tc):
    for r in range(1, 8):
        shifted[r - 1, :, :] = buf[r:r + tc + 24, :]


def _shifted_rows(buf, shifted, offset, base):
    src = buf if offset % 8 == 0 else shifted.at[offset % 8 - 1]
    return src[pl.ds(pl.multiple_of(base + 8 * (offset // 8), 8), CONV_ROWS), :]


def _conv_fwd(ug, w_dw, b_dw, g_ln, b_ln, tc, rides=()):
    t_len = ug.shape[0]
    n_tiles = t_len // tc

    def body(up_ref, um_ref, un_ref, w_ref, bdw_ref, g_ref, b_ref, y_ref, pre_ref, buf, shifted):
        i = pl.program_id(0)
        _fill_halo_buf(buf, _glu(up_ref[...]), _glu(um_ref[...]), _glu(un_ref[...]), i, n_tiles, tc)
        _shift_copies(buf, shifted, tc)

        def chunk(c, carry):
            base = c * CONV_ROWS
            acc = jnp.zeros((CONV_ROWS, CONV_CH), F32)
            for k in range(CONV_W):
                acc = acc + w_ref[k:k + 1, :] * _shifted_rows(buf, shifted, k + 1, base)
            pre_ref[pl.ds(pl.multiple_of(base, CONV_ROWS), CONV_ROWS), :] = acc + bdw_ref[...]
            return carry

        lax.fori_loop(0, tc // CONV_ROWS, chunk, 0)
        pre = pre_ref[...]
        mu = jnp.mean(pre, axis=-1, keepdims=True)
        d = pre - mu
        rstd = lax.rsqrt(jnp.mean(d * d, axis=-1, keepdims=True) + EPS)
        ln = d * rstd * g_ref[...] + b_ref[...]
        y_ref[...] = _bf(ln * _sigmoid(ln))

    return _call(
        body, rides=rides, name="conv_fwd", grid=(n_tiles,),
        in_specs=_halo_specs(tc, 1024, t_len) + [_res((32, CONV_CH)), _res((1, CONV_CH)), _res((1, CONV_CH)), _res((1, CONV_CH))],
        out_specs=[_rows(tc, CONV_CH), _rows(tc, CONV_CH)],
        out_shape=(jax.ShapeDtypeStruct((t_len, CONV_CH), BF16), jax.ShapeDtypeStruct((t_len, CONV_CH), F32)),
        scratch_shapes=[pltpu.VMEM((tc + 32, CONV_CH), F32), pltpu.VMEM((7, tc + 24, CONV_CH), F32)],
        compiler_params=_cp(1),
    )(ug, ug, ug, w_dw, b_dw, g_ln, b_ln)


def _nbr_specs(n, nb):
    return [pl.BlockSpec((BLK, n), lambda i: (jnp.maximum(i - 1, 0), 0)),
            pl.BlockSpec((BLK, n), lambda i: (i, 0)),
            pl.BlockSpec((BLK, n), lambda i: (jnp.minimum(i + 1, nb - 1), 0))]


def _nbr_specs4(nb):
    return [pl.BlockSpec((1, 2, 4 * BLK, 128), lambda i: (jnp.maximum(i - 1, 0), 0, 0, 0)),
            pl.BlockSpec((1, 2, 4 * BLK, 128), lambda i: (i, 0, 0, 0)),
            pl.BlockSpec((1, 2, 4 * BLK, 128), lambda i: (jnp.minimum(i + 1, nb - 1), 0, 0, 0))]


def _band_bias():
    a = np.arange(4 * BLK)[:, None] % BLK
    c = np.arange(3 * BLK)[None, :]
    inside = np.abs(c - BLK - a) <= BLK
    q_side = np.stack([inside & (c >= BLK), inside, inside & (c < 2 * BLK)])
    blk = np.arange(12 * BLK)[:, None] // (4 * BLK)
    a = np.arange(12 * BLK)[:, None] % BLK
    c = np.arange(BLK)[None, :]
    inside = np.abs(c - a + (1 - blk) * BLK) <= BLK
    k_side = np.stack([inside & (blk >= 1), inside, inside & (blk <= 1)])
    return [jnp.asarray(np.where(m, 0.0, NEG).astype(np.float32)) for m in (q_side, k_side)]


def _edge_spec(shape, nb):
    return pl.BlockSpec((1,) + shape, lambda i: (jnp.where(i == 0, 0, jnp.where(i == nb - 1, 2, 1)),) + (0,) * len(shape))


def _attn_fwd(q, kd, vd, sink_b, bias, rides=()):
    t_len = q.shape[0]
    nb = t_len // BLK

    def body(q_ref, kp_ref, kc_ref, kn_ref, vp_ref, vc_ref, vn_ref, sk_ref, bias_ref, y_ref, lse_ref):
        kcat = jnp.concatenate([kp_ref[...], kc_ref[...], kn_ref[...]], axis=0)
        vcat = jnp.concatenate([vp_ref[...], vc_ref[...], vn_ref[...]], axis=0)
        ys = []
        for g in range(2):
            qs = _stack_heads(q_ref[:, 256 * g:256 * g + 256])
            s = _dot_nt(qs, kcat[:, 128 * g:128 * g + 128]) * ATT_SCALE + bias_ref[0]
            skc = _sink_col(sk_ref, g)
            m_b = jnp.maximum(jnp.max(s, axis=-1, keepdims=True), skc)
            p = jnp.exp(s - _tile3(m_b))
            den_b = jnp.sum(p, axis=-1, keepdims=True) + jnp.exp(skc - m_b)
            pn = p * _tile3(1.0 / den_b)
            o = _dot(_bf(pn), vcat[:, 128 * g:128 * g + 128])
            ys.append(_unstack_heads(o))
            lse_ref[0, g] = m_b + jnp.log(den_b)
        y_ref[...] = _bf(jnp.concatenate(ys, axis=1))

    return _call(
        body, rides=rides, name="attn_fwd", grid=(nb,),
        in_specs=[_rows(BLK, 512)] + _nbr_specs(256, nb) + _nbr_specs(256, nb) + [_res((8, 128)), _edge_spec((4 * BLK, 3 * BLK), nb)],
        out_specs=[_rows(BLK, 512), pl.BlockSpec((1, 2, 4 * BLK, 128), lambda i: (i, 0, 0, 0))],
        out_shape=(jax.ShapeDtypeStruct((t_len, 512), BF16), jax.ShapeDtypeStruct((nb, 2, 4 * BLK, 128), F32)),
        compiler_params=_cp(1),
    )(q, kd, kd, kd, vd, vd, vd, sink_b, bias)


def _mem_heads(kv_ref, h):
    lo = MEM_HD * (h % 2)
    return kv_ref[h // 2, :, lo:lo + MEM_HD], kv_ref[2 + h // 2, :, lo:lo + MEM_HD]


def _mix_mem_fwd(x, yc, ya, w_out, b_out, g_q, w_q, kv, w_o, tm, rides=()):
    t_len = x.shape[0]

    def body(x_ref, yc_ref, ya_ref, wout_ref, bout_ref, g_ref, wq_ref, kv_ref, wo_ref,
             ymix_ref, x1_ref, hq_ref, qm_ref, om_ref, x2_ref):
        ymix = jnp.concatenate([yc_ref[...], ya_ref[...]], axis=1)
        ymix_ref[...] = ymix
        x1 = x_ref[...] + _dot(ymix, wout_ref[...]) + bout_ref[...]
        x1_ref[...] = x1
        hq, _, _ = _rms_fwd(x1, g_ref[...])
        hqb = _bf(hq)
        hq_ref[...] = hqb
        qm = _bf(_dot(hqb, wq_ref[...]))
        qm_ref[...] = qm
        outs = []
        for h in range(MEM_HEADS):
            kh, vh = _mem_heads(kv_ref, h)
            s = _dot_nt(qm[:, MEM_HD * h:MEM_HD * (h + 1)], kh) * MEM_SCALE
            p = jnp.exp(s - jnp.max(s, axis=-1, keepdims=True))
            p = p * (1.0 / jnp.sum(p, axis=-1, keepdims=True))
            outs.append(_dot(_bf(p), vh))
        om = _bf(jnp.concatenate(outs, axis=1))
        om_ref[...] = om
        x2_ref[...] = x1 + _dot(om, wo_ref[...])

    act_b = jax.ShapeDtypeStruct((t_len, D_MODEL), BF16)
    act_f = jax.ShapeDtypeStruct((t_len, D_MODEL), F32)
    return _call(
        body, rides=rides, name="mix_mem_fwd", grid=(t_len // tm,),
        in_specs=[_rows(tm, D_MODEL), _rows(tm, 512), _rows(tm, 512), _res(w_out.shape), _res((1, D_MODEL)), _res((1, D_MODEL)),
                  _res(w_q.shape), _res(kv.shape), _res(w_o.shape)],
        out_specs=[_rows(tm, D_MODEL)] * 6,
        out_shape=(act_b, act_f, act_b, act_b, act_b, act_f),
        compiler_params=_cp(1),
    )(x, yc, ya, w_out, b_out, g_q, w_q, kv, w_o)


def _hidden_chunks(ff, width=1024):
    return [(lo, min(lo + width, ff)) for lo in range(0, ff, width)]


def _ffn_up(x2, g_ffn, w_gate, w_up, tm, rides=()):
    t_len = x2.shape[0]
    ff = w_gate.shape[0]

    def body(x2_ref, g_ref, wg_ref, wu_ref, hf_ref, gate_ref, up_ref, act_ref):
        hf, _, _ = _rms_fwd(x2_ref[...], g_ref[...])
        hfb = _bf(hf)
        hf_ref[...] = hfb
        for lo, hi in _hidden_chunks(ff):
            gate = _dot_nt(hfb, wg_ref[lo:hi, :])
            up = _dot_nt(hfb, wu_ref[lo:hi, :])
            gate_ref[:, lo:hi] = _bf(gate)
            up_ref[:, lo:hi] = _bf(up)
            act_ref[:, lo:hi] = _bf(gate * _sigmoid(gate) * up)

    hid = jax.ShapeDtypeStruct((t_len, ff), BF16)
    return _call(
        body, rides=rides, name="ffn_up", grid=(t_len // tm,),
        in_specs=[_rows(tm, D_MODEL), _res((1, D_MODEL)), _res(w_gate.shape), _res(w_up.shape)],
        out_specs=[_rows(tm, D_MODEL), _rows(tm, ff), _rows(tm, ff), _rows(tm, ff)],
        out_shape=[jax.ShapeDtypeStruct((t_len, D_MODEL), BF16), hid, hid, hid],
        compiler_params=_cp(1),
    )(x2, g_ffn, w_gate, w_up)


def _ffn_down_loss(x2, act, w_down, g_final, target, tm):
    t_len = x2.shape[0]
    ff = w_down.shape[0]

    def body(x2_ref, act_ref, wd_ref, gf_ref, tgt_ref, dx3_ref, loss_ref, dgf_ref):
        i = pl.program_id(0)
        x3 = x2_ref[...]
        for lo, hi in _hidden_chunks(ff):
            x3 = x3 + _dot(act_ref[:, lo:hi], wd_ref[lo:hi, :])
        gf = gf_ref[...]
        y, xh, r = _rms_fwd(x3, gf)
        err = y - tgt_ref[...]
        part = 0.5 * jnp.sum(jnp.mean(err * err, axis=-1, keepdims=True), axis=0, keepdims=True)
        dy = err * (1.0 / D_MODEL)
        dx3_ref[...] = _rms_bwd(dy, xh, r, gf)

        @pl.when(i == 0)
        def _():
            loss_ref[...] = jnp.zeros_like(loss_ref)
            dgf_ref[...] = jnp.zeros_like(dgf_ref)

        loss_ref[...] += jnp.broadcast_to(part, loss_ref.shape)
        dgf_ref[...] += _colsum(dy * xh)

    vec = pl.BlockSpec((1, D_MODEL), lambda i: (0, 0))
    return _call(
        body, name="ffn_down_loss", grid=(t_len // tm,),
        in_specs=[_rows(tm, D_MODEL), _rows(tm, ff), _res(w_down.shape), _res((1, D_MODEL)), _rows(tm, D_MODEL)],
        out_specs=[_rows(tm, D_MODEL), vec, vec],
        out_shape=(jax.ShapeDtypeStruct((t_len, D_MODEL), F32), jax.ShapeDtypeStruct((1, D_MODEL), F32),
                   jax.ShapeDtypeStruct((1, D_MODEL), F32)),
        compiler_params=_cp(1),
    )(x2, act, w_down, g_final, target)


def _ffn_bwd(dx3, x2, gate, up, g_ffn, w_gate, w_up, w_down, tm):
    t_len = x2.shape[0]
    ff = w_gate.shape[0]

    def body(dx3_ref, x2_ref, gate_ref, up_ref, g_ref, wg_ref, wu_ref, wd_ref, dx2_ref, dgate_ref, dup_ref, dg_ref):
        i = pl.program_id(0)
        dx3 = dx3_ref[...]
        d3b = _bf(dx3)
        dh = jnp.zeros((tm, D_MODEL), F32)
        for lo, hi in _hidden_chunks(ff):
            dact = _dot_nt(d3b, wd_ref[lo:hi, :])
            gt = gate_ref[:, lo:hi].astype(F32)
            u = up_ref[:, lo:hi].astype(F32)
            sg = _sigmoid(gt)
            dup = _bf(dact * (gt * sg))
            dgate = _bf(dact * u * (sg * (1.0 + gt * (1.0 - sg))))
            dup_ref[:, lo:hi] = dup
            dgate_ref[:, lo:hi] = dgate
            dh = dh + _dot(dgate, wg_ref[lo:hi, :]) + _dot(dup, wu_ref[lo:hi, :])
        g = g_ref[...]
        _, xh, r = _rms_fwd(x2_ref[...], g)
        dx2_ref[...] = dx3 + _rms_bwd(dh, xh, r, g)

        @pl.when(i == 0)
        def _():
            dg_ref[...] = jnp.zeros_like(dg_ref)

        dg_ref[...] += _colsum(dh * xh)

    hid = jax.ShapeDtypeStruct((t_len, ff), BF16)
    hid_spec = _rows(tm, ff)
    return _call(
        body, name="ffn_bwd", grid=(t_len // tm,),
        in_specs=[_rows(tm, D_MODEL), _rows(tm, D_MODEL), hid_spec, hid_spec, _res((1, D_MODEL)),
                  _res(w_gate.shape), _res(w_up.shape), _res(w_down.shape)],
        out_specs=[_rows(tm, D_MODEL), hid_spec, hid_spec, pl.BlockSpec((1, D_MODEL), lambda i: (0, 0))],
        out_shape=(jax.ShapeDtypeStruct((t_len, D_MODEL), F32), hid, hid, jax.ShapeDtypeStruct((1, D_MODEL), F32)),
        compiler_params=_cp(1),
    )(dx3, x2, gate, up, g_ffn, w_gate, w_up, w_down)


def _mix_mem_bwd(dx2, x1, qm, kv, g_q, w_q, w_o, w_out, tm, rides=()):
    t_len = x1.shape[0]
    m_len = kv.shape[1]

    def body(dx2_ref, x1_ref, qm_ref, kv_ref, g_ref, wq_ref, wo_ref, wout_ref,
             dx1_ref, dqm_ref, dyc_ref, dya_ref, dkv_ref, dgq_ref, dbout_ref):
        i = pl.program_id(0)

        @pl.when(i == 0)
        def _():
            dkv_ref[...] = jnp.zeros_like(dkv_ref)
            dgq_ref[...] = jnp.zeros_like(dgq_ref)
            dbout_ref[...] = jnp.zeros_like(dbout_ref)

        dx2 = dx2_ref[...]
        dom = _dot_nt(_bf(dx2), wo_ref[...])
        dqs = []
        for h in range(MEM_HEADS):
            kh, vh = _mem_heads(kv_ref, h)
            qh = qm_ref[:, MEM_HD * h:MEM_HD * (h + 1)]
            s = _dot_nt(qh, kh) * MEM_SCALE
            p = jnp.exp(s - jnp.max(s, axis=-1, keepdims=True))
            p = p * (1.0 / jnp.sum(p, axis=-1, keepdims=True))
            domh = _bf(dom[:, MEM_HD * h:MEM_HD * (h + 1)])
            dp = _dot_nt(domh, vh)
            ds = _bf(p * (dp - jnp.sum(p * dp, axis=-1, keepdims=True)) * MEM_SCALE)
            dqs.append(_dot(ds, kh))
            lo = MEM_HD * (h % 2)
            dkv_ref[h // 2, :, lo:lo + MEM_HD] += _dot_tn(ds, qh)
            dkv_ref[2 + h // 2, :, lo:lo + MEM_HD] += _dot_tn(_bf(p), domh)
        dqm = _bf(jnp.concatenate(dqs, axis=1))
        dqm_ref[...] = dqm
        dhq = _dot_nt(dqm, wq_ref[...])
        g = g_ref[...]
        _, xh, r = _rms_fwd(x1_ref[...], g)
        dx1 = dx2 + _rms_bwd(dhq, xh, r, g)
        dx1_ref[...] = dx1
        dgq_ref[...] += _colsum(dhq * xh)
        dbout_ref[...] += _colsum(dx1)
        dymix = _dot_nt(_bf(dx1), wout_ref[...])
        dyc_ref[...] = dymix[:, 0:CONV_CH]
        dya_ref[...] = _bf(dymix[:, CONV_CH:2 * CONV_CH])

    vec = pl.BlockSpec((1, D_MODEL), lambda i: (0, 0))
    return _call(
        body, rides=rides, name="mix_mem_bwd", grid=(t_len // tm,),
        in_specs=[_rows(tm, D_MODEL), _rows(tm, D_MODEL), _rows(tm, D_MODEL), _res(kv.shape), _res((1, D_MODEL)),
                  _res(w_q.shape), _res(w_o.shape), _res(w_out.shape)],
        out_specs=[_rows(tm, D_MODEL), _rows(tm, D_MODEL), _rows(tm, CONV_CH), _rows(tm, CONV_CH),
                   pl.BlockSpec(kv.shape, lambda i: (0, 0, 0)), vec, vec],
        out_shape=(jax.ShapeDtypeStruct((t_len, D_MODEL), F32), jax.ShapeDtypeStruct((t_len, D_MODEL), BF16),
                   jax.ShapeDtypeStruct((t_len, CONV_CH), F32), jax.ShapeDtypeStruct((t_len, CONV_CH), BF16),
                   jax.ShapeDtypeStruct((N_CHIPS, m_len, kv.shape[2]), F32),
                   jax.ShapeDtypeStruct((1, D_MODEL), F32), jax.ShapeDtypeStruct((1, D_MODEL), F32)),
        compiler_params=_cp(1),
    )(dx2, x1, qm, kv, g_q, w_q, w_o, w_out)


def _mem_kv_bwd(dkv, memn, mem, g_kv, w_kv):
    m_len = mem.shape[0]

    def body(dkv_ref, memn_ref, mem_ref, g_ref, w_ref, dw_ref, dg_ref):
        hb = memn_ref[...]
        dmn = jnp.zeros((m_len, D_MODEL), F32)
        for s in range(N_CHIPS):
            d = _bf(dkv_ref[s])
            dw_ref[s] = _bf(_dot_tn(hb, d))
            dmn = dmn + _dot_nt(d, w_ref[s])
        _, xh, _ = _rms_fwd(mem_ref[...], g_ref[...])
        dg_ref[...] = _colsum(dmn * xh)

    return _call(
        body, name="mem_kv_bwd",
        out_shape=(jax.ShapeDtypeStruct(w_kv.shape, BF16), jax.ShapeDtypeStruct((1, D_MODEL), F32)),
        compiler_params=pltpu.CompilerParams(vmem_limit_bytes=VMEM_LIMIT_BYTES),
    )(dkv, memn, mem, g_kv, w_kv)


def _attn_bwd_q(q, kd, vd, dya, lse, sink_b, bias, cos_t, sin_t, rides=()):
    t_len = q.shape[0]
    nb = t_len // BLK

    def body(q_ref, kp_ref, kc_ref, kn_ref, vp_ref, vc_ref, vn_ref, do_ref, lse_ref, sk_ref, bias_ref, c_ref, s_ref,
             dq_ref, dd_ref, dsk_ref):
        kcat = jnp.concatenate([kp_ref[...], kc_ref[...], kn_ref[...]], axis=0)
        vcat = jnp.concatenate([vp_ref[...], vc_ref[...], vn_ref[...]], axis=0)
        dqs, dsks = [], []
        for g in range(2):
            qs = _stack_heads(q_ref[:, 256 * g:256 * g + 256])
            dos = _stack_heads(do_ref[:, 256 * g:256 * g + 256])
            kk = kcat[:, 128 * g:128 * g + 128]
            s = _dot_nt(qs, kk) * ATT_SCALE + bias_ref[0]
            lse_b = lse_ref[0, g]
            p = jnp.exp(s - _tile3(lse_b))
            dp = _dot_nt(dos, vcat[:, 128 * g:128 * g + 128])
            drow = jnp.sum(p * dp, axis=-1, keepdims=True)
            ds = _bf(p * (dp - drow) * ATT_SCALE)
            dqs.append(_unstack_heads(_dot(ds, kk)))
            d_b = jnp.broadcast_to(drow, (4 * BLK, 128))
            dd_ref[0, g] = d_b
            contrib = -(jnp.exp(_sink_col(sk_ref, g) - lse_b) * d_b)
            dsks.append(jnp.sum(contrib.reshape(4, BLK, 128), axis=1))
        dq = jnp.concatenate(dqs, axis=1)
        dq_ref[...] = _bf(_rope(dq, c_ref[...], s_ref[...], -1.0))
        dsk_ref[0] = jnp.concatenate(dsks, axis=0)

    stat = pl.BlockSpec((1, 2, 4 * BLK, 128), lambda i: (i, 0, 0, 0))
    return _call(
        body, rides=rides, name="attn_bwd_q", grid=(nb,),
        in_specs=[_rows(BLK, 512)] + _nbr_specs(256, nb) + _nbr_specs(256, nb)
        + [_rows(BLK, 512), stat, _res((8, 128)), _edge_spec((4 * BLK, 3 * BLK), nb), _rows(BLK, 128), _rows(BLK, 128)],
        out_specs=[_rows(BLK, 512), stat, pl.BlockSpec((1, 8, 128), lambda i: (i, 0, 0))],
        out_shape=(jax.ShapeDtypeStruct((t_len, 512), BF16), jax.ShapeDtypeStruct((nb, 2, 4 * BLK, 128), F32),
                   jax.ShapeDtypeStruct((nb, 8, 128), F32)),
        compiler_params=_cp(1),
    )(q, kd, kd, kd, vd, vd, vd, dya, lse, sink_b, bias, cos_t, sin_t)


def _attn_bwd_kv(q, kd, vd, dya, lse, dd, bias, cos_t, sin_t, rides=()):
    t_len = q.shape[0]
    nb = t_len // BLK

    def body(kc_ref, vc_ref, qp_ref, qc_ref, qn_ref, dop_ref, doc_ref, don_ref, lp_ref, lc_ref, ln_ref,
             dp_ref, dc_ref, dn_ref, bias_ref, c_ref, s_ref, dk_ref, dv_ref):
        dks, dvs = [], []
        for g in range(2):
            cols = slice(256 * g, 256 * g + 256)
            qs = jnp.concatenate([_stack_heads(r[:, cols]) for r in (qp_ref, qc_ref, qn_ref)], axis=0)
            dos = jnp.concatenate([_stack_heads(r[:, cols]) for r in (dop_ref, doc_ref, don_ref)], axis=0)
            lse_b = jnp.concatenate([r[0, g] for r in (lp_ref, lc_ref, ln_ref)], axis=0)
            d_b = jnp.concatenate([r[0, g] for r in (dp_ref, dc_ref, dn_ref)], axis=0)
            kk = kc_ref[:, 128 * g:128 * g + 128]
            s = _dot_nt(qs, kk) * ATT_SCALE + bias_ref[0]
            p = jnp.exp(s - lse_b)
            dp = _dot_nt(dos, vc_ref[:, 128 * g:128 * g + 128])
            ds = _bf(p * (dp - d_b) * ATT_SCALE)
            dvs.append(_dot_tn(_bf(p), dos))
            dks.append(_dot_tn(ds, qs))
        dk_ref[...] = _bf(_rope(_fold_heads(dks), c_ref[...], s_ref[...], -1.0))
        dv_ref[...] = _bf(_fold_heads(dvs))

    return _call(
        body, rides=rides, name="attn_bwd_kv", grid=(nb,),
        in_specs=[_rows(BLK, 256), _rows(BLK, 256)] + _nbr_specs(512, nb) + _nbr_specs(512, nb) + _nbr_specs4(nb) + _nbr_specs4(nb)
        + [_edge_spec((12 * BLK, BLK), nb), _rows(BLK, 128), _rows(BLK, 128)],
        out_specs=[_rows(BLK, 128), _rows(BLK, 128)],
        out_shape=(jax.ShapeDtypeStruct((t_len, 128), BF16), jax.ShapeDtypeStruct((t_len, 128), BF16)),
        compiler_params=_cp(1),
    )(kd, vd, q, q, q, dya, dya, dya, lse, lse, lse, dd, dd, dd, bias, cos_t, sin_t)


def _conv_norm_bwd(pre, dyc, g_ln, b_ln, tc):
    t_len = pre.shape[0]

    def body(pre_ref, dy_ref, g_ref, b_ref, dpre_ref, stats_ref):
        i = pl.program_id(0)
        pre_v = pre_ref[...]
        mu = jnp.mean(pre_v, axis=-1, keepdims=True)
        d = pre_v - mu
        rstd = lax.rsqrt(jnp.mean(d * d, axis=-1, keepdims=True) + EPS)
        xh = d * rstd
        g = g_ref[...]
        ln = xh * g + b_ref[...]
        sg = _sigmoid(ln)
        dln = dy_ref[...] * (sg * (1.0 + ln * (1.0 - sg)))
        dxh = dln * g
        dpre = rstd * (dxh - jnp.mean(dxh, axis=-1, keepdims=True) - xh * jnp.mean(dxh * xh, axis=-1, keepdims=True))
        dpre_ref[...] = dpre

        @pl.when(i == 0)
        def _():
            stats_ref[...] = jnp.zeros_like(stats_ref)

        stats_ref[0:1, :] += _colsum(dln * xh)
        stats_ref[1:2, :] += _colsum(dln)
        stats_ref[2:3, :] += _colsum(dpre)

    return _call(
        body, name="conv_norm_bwd", grid=(t_len // tc,),
        in_specs=[_rows(tc, CONV_CH), _rows(tc, CONV_CH), _res((1, CONV_CH)), _res((1, CONV_CH))],
        out_specs=[_rows(tc, CONV_CH), pl.BlockSpec((8, CONV_CH), lambda i: (0, 0))],
        out_shape=(jax.ShapeDtypeStruct((t_len, CONV_CH), F32), jax.ShapeDtypeStruct((8, CONV_CH), F32)),
        compiler_params=_cp(1),
    )(pre, dyc, g_ln, b_ln)


def _conv_bwd(dpre, ug, w_dw, tc, rides=()):
    t_len = ug.shape[0]
    n_tiles = t_len // tc

    def body(dp_ref, dm_ref, dn_ref, up_ref, um_ref, un_ref, w_ref, du_ref, dw_ref, dbuf, vbuf, dshift, vshift):
        i = pl.program_id(0)
        _fill_halo_buf(dbuf, dp_ref[...], dm_ref[...], dn_ref[...], i, n_tiles, tc)
        _fill_halo_buf(vbuf, _glu(up_ref[...]), _glu(um_ref[...]), _glu(un_ref[...]), i, n_tiles, tc)
        _shift_copies(dbuf, dshift, tc)
        _shift_copies(vbuf, vshift, tc)

        @pl.when(i == 0)
        def _():
            dw_ref[...] = jnp.zeros_like(dw_ref)

        def chunk(c, carry):
            base = c * CONV_ROWS
            rows = pl.ds(pl.multiple_of(base, CONV_ROWS), CONV_ROWS)
            dmain = dm_ref[rows, :]
            dv = jnp.zeros((CONV_ROWS, CONV_CH), F32)
            for k in range(CONV_W):
                dv = dv + w_ref[k:k + 1, :] * _shifted_rows(dbuf, dshift, 31 - k, base)
                prod = dmain * _shifted_rows(vbuf, vshift, k + 1, base)
                dw_ref[8 * k:8 * k + 8, :] += jnp.sum(prod.reshape(CONV_ROWS // 8, 8, CONV_CH), axis=0)
            um = um_ref[rows, :]
            a, gt = um[:, 0:CONV_CH], um[:, CONV_CH:2 * CONV_CH]
            sg = _sigmoid(gt)
            du_ref[rows, :] = _bf(jnp.concatenate([dv * sg, dv * a * (sg * (1.0 - sg))], axis=1))
            return carry

        lax.fori_loop(0, tc // CONV_ROWS, chunk, 0)

    shifts = pltpu.VMEM((7, tc + 24, CONV_CH), F32)
    return _call(
        body, rides=rides, name="conv_bwd", grid=(n_tiles,),
        in_specs=_halo_specs(tc, CONV_CH, t_len) + _halo_specs(tc, 1024, t_len) + [_res((32, CONV_CH))],
        out_specs=[_rows(tc, 1024), pl.BlockSpec((8 * 32, CONV_CH), lambda i: (0, 0))],
        out_shape=(jax.ShapeDtypeStruct((t_len, 1024), BF16), jax.ShapeDtypeStruct((8 * 32, CONV_CH), F32)),
        scratch_shapes=[pltpu.VMEM((tc + 32, CONV_CH), F32), pltpu.VMEM((tc + 32, CONV_CH), F32), shifts, shifts],
        compiler_params=_cp(1),
    )(dpre, dpre, dpre, ug, ug, ug, w_dw)


def _in_proj_bwd(du_glu, dq, dk, dv, dx1, x, g_mix, w_t, tm, rides=()):
    t_len = x.shape[0]
    n_ext = w_t.shape[0]

    def body(dg_ref, dq_ref, dk_ref, dv_ref, dx1_ref, x_ref, g_ref, w_ref, dx_ref, du_ref, db_ref, dgm_ref):
        i = pl.program_id(0)
        du = jnp.concatenate([dg_ref[...], dq_ref[...], dk_ref[...], dv_ref[...]], axis=1)
        du_ref[...] = du
        dh = _dot(du, w_ref[...])
        g = g_ref[...]
        _, xh, r = _rms_fwd(x_ref[...], g)
        dx_ref[...] = dx1_ref[...] + _rms_bwd(dh, xh, r, g)

        @pl.when(i == 0)
        def _():
            db_ref[...] = jnp.zeros_like(db_ref)
            dgm_ref[...] = jnp.zeros_like(dgm_ref)

        db_ref[...] += _colsum(du.astype(F32))
        dgm_ref[...] += _colsum(dh * xh)

    return _call(
        body, rides=rides, name="in_proj_bwd", grid=(t_len // tm,),
        in_specs=[_rows(tm, 1024), _rows(tm, 512), _rows(tm, 128), _rows(tm, 128), _rows(tm, D_MODEL), _rows(tm, D_MODEL),
                  _res((1, D_MODEL)), _res(w_t.shape)],
        out_specs=[_rows(tm, D_MODEL), _rows(tm, n_ext), pl.BlockSpec((1, n_ext), lambda i: (0, 0)),
                   pl.BlockSpec((1, D_MODEL), lambda i: (0, 0))],
        out_shape=(jax.ShapeDtypeStruct((t_len, D_MODEL), F32), jax.ShapeDtypeStruct((t_len, n_ext), BF16),
                   jax.ShapeDtypeStruct((1, n_ext), F32), jax.ShapeDtypeStruct((1, D_MODEL), F32)),
        compiler_params=_cp(1),
    )(du_glu, dq, dk, dv, dx1, x, g_mix, w_t)


def _weight_grad(a, d, name, tt, rides=()):
    t_len, k_dim = a.shape
    n_dim = d.shape[1]
    tk = k_dim if k_dim <= 1792 else k_dim // 2
    assert k_dim % tk == 0 and tk % 128 == 0 and n_dim % 128 == 0
    tt = min(tt, t_len)
    n_t = t_len // tt

    def body(a_ref, d_ref, o_ref, acc):
        t = pl.program_id(1)

        @pl.when(t == 0)
        def _():
            acc[...] = jnp.zeros_like(acc)

        acc[...] += _dot_tn(_bf(a_ref[...]), _bf(d_ref[...]))

        @pl.when(t == n_t - 1)
        def _():
            o_ref[...] = _bf(acc[...])

    res = _call(
        body, rides=rides, name=name, grid=(k_dim // tk, n_t),
        in_specs=[pl.BlockSpec((tt, tk), lambda k, t: (t, k)), pl.BlockSpec((tt, n_dim), lambda k, t: (t, 0))],
        out_specs=[pl.BlockSpec((tk, n_dim), lambda k, t: (k, 0))],
        out_shape=[jax.ShapeDtypeStruct((k_dim, n_dim), BF16)],
        scratch_shapes=[pltpu.VMEM((tk, n_dim), F32)],
        compiler_params=_cp(2),
    )(a, d)
    return res if rides else res[0]


ANY = pl.BlockSpec(memory_space=pl.ANY)


def _place():
    x, y, c = lax.axis_index("x"), lax.axis_index("y"), lax.axis_index("c")
    chips = [(1 - x, y), (x, 1 - y), (1 - x, 1 - y)]
    return x, y, c, chips


def _remote(src, dst, send_sems, recv_sems, k, to):
    return pltpu.make_async_remote_copy(src_ref=src, dst_ref=dst, send_sem=send_sems.at[k], recv_sem=recv_sems.at[k],
                                        device_id=to, device_id_type=MESH)


def _cast_place(w, chip_idx, tr):
    rows, cols = w.shape
    h = rows // 2
    tr = _div_tile(h, tr)
    per = h // tr

    def body(s_ref, w_ref, o_ref):
        o_ref[0, 0] = _bf(w_ref[...])

    return _call(
        body, name="cast_place",
        grid_spec=pltpu.PrefetchScalarGridSpec(
            num_scalar_prefetch=1, grid=(2, per),
            in_specs=[pl.BlockSpec((tr, cols), lambda hh, r, s_ref: (hh * per + r, 0))],
            out_specs=pl.BlockSpec((1, 1, tr, cols), lambda hh, r, s_ref: (s_ref[0], hh, r, 0))),
        out_shape=jax.ShapeDtypeStruct((N_CHIPS, 2, h, cols), BF16),
        compiler_params=_cp(2),
    )(chip_idx, w)


def _same(arrays):
    return [jax.ShapeDtypeStruct(a.shape, a.dtype) for a in arrays]


def _gather_ride(bufs):
    n = len(bufs)

    def first_hop(outs, send, recv):
        x, y, c, chips = _place()
        mine = [outs[i].at[2 * x + y, c] for i in range(n)]
        return [_remote(mine[i], mine[i], send, recv, 3 * i + j, (cx, cy, c)) for i in range(n) for j, (cx, cy) in enumerate(chips)]

    def start(ins, outs, send, recv):
        for cp in first_hop(outs, send, recv):
            cp.start()

    def finish(ins, outs, send, recv):
        x, y, c, chips = _place()
        sib = (x, y, 1 - c)
        onward = []
        for i in range(n):
            for j, (cx, cy) in enumerate(chips):
                slab = outs[i].at[2 * cx + cy, c]
                _remote(slab, slab, send, recv, 3 * i + j, sib).wait_recv()
                onward.append(_remote(slab, slab, send, recv, 3 * n + 3 * i + j, sib))
                onward[-1].start()
        for i in range(n):
            for j, (cx, cy) in enumerate(chips):
                other = outs[i].at[2 * cx + cy, 1 - c]
                _remote(other, other, send, recv, 3 * n + 3 * i + j, sib).wait_recv()
        for cp in first_hop(outs, send, recv) + onward:
            cp.wait_send()

    return _Ride(bufs, _same(bufs), 6 * n, start, finish, aliases={i: i for i in range(n)})


def _spread_ride(buf):
    def sends(outs, send, recv):
        x, y, c, chips = _place()
        mine = outs[0].at[2 * x + y]
        return [_remote(mine, mine, send, recv, j, (cx, cy, c)) for j, (cx, cy) in enumerate(chips)]

    def start(ins, outs, send, recv):
        for cp in sends(outs, send, recv):
            cp.start()

    def finish(ins, outs, send, recv):
        _, _, c, chips = _place()
        for j, (cx, cy) in enumerate(chips):
            slab = outs[0].at[2 * cx + cy]
            _remote(slab, slab, send, recv, j, (cx, cy, c)).wait_recv()
        for cp in sends(outs, send, recv):
            cp.wait_send()

    return _Ride([buf], _same([buf]), 3, start, finish, aliases={0: 0})


def _allgather_ride(buf):
    def peers():
        x, y, c, _ = _place()
        return [(x ^ ((k >> 2) & 1), y ^ ((k >> 1) & 1), c ^ (k & 1)) for k in range(1, N_DEV)], 4 * x + 2 * y + c

    def sends(outs, send, recv):
        to, me = peers()
        mine = outs[0].at[me]
        return [_remote(mine, mine, send, recv, k, p) for k, p in enumerate(to)]

    def start(ins, outs, send, recv):
        for cp in sends(outs, send, recv):
            cp.start()

    def finish(ins, outs, send, recv):
        for k, (px, py, pc) in enumerate(peers()[0]):
            slab = outs[0].at[4 * px + 2 * py + pc]
            _remote(slab, slab, send, recv, k, (px, py, pc)).wait_recv()
        for cp in sends(outs, send, recv):
            cp.wait_send()

    return _Ride([buf], _same([buf]), N_DEV - 1, start, finish, aliases={0: 0})


def _sum_slabs(buf):
    def body(b_ref, o_ref):
        acc = b_ref[0]
        for d in range(1, buf.shape[0]):
            acc = acc + b_ref[d]
        o_ref[...] = acc

    return _call(body, name="sum_slabs", out_shape=jax.ShapeDtypeStruct(buf.shape[1:], buf.dtype))(buf)


def _pairwise_ride(arrays, out_shape, n_sem, copies):
    def start(ins, outs, send, recv):
        for cp in copies(ins, outs, send, recv):
            cp.start()

    def finish(ins, outs, send, recv):
        for cp in copies(ins, outs, send, recv):
            cp.wait()

    return _Ride(arrays, out_shape, n_sem, start, finish)


def _run_rides(name, rides):
    k_in = [len(r.operands) for r in rides]
    k_out = [len(r.out_shape) for r in rides]

    def body(*refs):
        pos, r_in, r_out = 0, [], []
        for k in k_in:
            r_in.append(refs[pos:pos + k])
            pos += k
        for k in k_out:
            r_out.append(refs[pos:pos + k])
            pos += k
        sems = refs[pos:]
        for j, r in enumerate(rides):
            r.start(r_in[j], r_out[j], sems[2 * j], sems[2 * j + 1])
        for j, r in enumerate(rides):
            r.finish(r_in[j], r_out[j], sems[2 * j], sems[2 * j + 1])

    aliases, off_in, off_out = {}, 0, 0
    for r, ki, ko in zip(rides, k_in, k_out):
        aliases.update({off_in + a: off_out + b for a, b in r.aliases.items()})
        off_in, off_out = off_in + ki, off_out + ko
    res = _call(
        body, name=name, in_specs=[ANY] * sum(k_in), out_specs=[ANY] * sum(k_out),
        out_shape=[s for r in rides for s in r.out_shape], input_output_aliases=aliases,
        scratch_shapes=[pltpu.SemaphoreType.DMA((r.n_sem,)) for r in rides for _ in range(2)],
    )(*[op for r in rides for op in r.operands])
    out, pos = [], 0
    for k in k_out:
        out.append(list(res[pos:pos + k]))
        pos += k
    return out


def _swap_ride(grads):
    def copies(ins, outs, send, recv):
        x, y, c, _ = _place()
        return [_remote(ins[i].at[:, 1 - c], outs[i], send, recv, i, (x, y, 1 - c)) for i in range(len(grads))]

    out_shape = [jax.ShapeDtypeStruct((g.shape[0],) + g.shape[2:], g.dtype) for g in grads]
    return _pairwise_ride(grads, out_shape, len(grads), copies)


def _pair_sum(grad, other, c_idx, tr):
    n_s, _, h, cols = grad.shape
    tr = _div_tile(h, tr)

    def body(c_ref, a_ref, b_ref, o_ref):
        o_ref[...] = _bf(a_ref[0].astype(F32) + b_ref[...].astype(F32))

    return _call(
        body, name="pair_sum",
        grid_spec=pltpu.PrefetchScalarGridSpec(
            num_scalar_prefetch=1, grid=(n_s, h // tr),
            in_specs=[pl.BlockSpec((1, 1, tr, cols), lambda s, r, c_ref: (s, c_ref[0], r, 0)),
                      pl.BlockSpec((1, tr, cols), lambda s, r, c_ref: (s, r, 0))],
            out_specs=pl.BlockSpec((1, tr, cols), lambda s, r, c_ref: (s, r, 0))),
        out_shape=jax.ShapeDtypeStruct((n_s, h, cols), BF16),
        compiler_params=_cp(2),
    )(c_idx, grad, other)


def _exchange_ride(sums):
    def copies(ins, outs, send, recv):
        _, _, c, chips = _place()
        return [_remote(ins[i].at[2 * cx + cy], outs[i].at[j], send, recv, 3 * i + j, (cx, cy, c))
                for i in range(len(sums)) for j, (cx, cy) in enumerate(chips)]

    out_shape = [jax.ShapeDtypeStruct((3,) + s.shape[1:], s.dtype) for s in sums]
    return _pairwise_ride(sums, out_shape, 3 * len(sums), copies)


def _chip_sum(own, others, chip_idx, tr):
    _, h, cols = own.shape
    tr = _div_tile(h, tr)

    def body(s_ref, a_ref, p_ref, o_ref):
        acc = a_ref[0].astype(F32)
        for j in range(N_CHIPS - 1):
            acc = acc + p_ref[j].astype(F32)
        o_ref[...] = acc

    return _call(
        body, name="chip_sum",
        grid_spec=pltpu.PrefetchScalarGridSpec(
            num_scalar_prefetch=1, grid=(h // tr,),
            in_specs=[pl.BlockSpec((1, tr, cols), lambda r, s_ref: (s_ref[0], r, 0)),
                      pl.BlockSpec((N_CHIPS - 1, tr, cols), lambda r, s_ref: (0, r, 0))],
            out_specs=pl.BlockSpec((tr, cols), lambda r, s_ref: (r, 0))),
        out_shape=jax.ShapeDtypeStruct((h, cols), F32),
        compiler_params=_cp(1),
    )(chip_idx, own, others)


def _share_ride(halves):
    def copies(ins, outs, send, recv):
        x, y, c, _ = _place()
        return [_remote(ins[i], outs[i], send, recv, i, (x, y, 1 - c)) for i in range(len(halves))]

    return _pairwise_ride(halves, _same(halves), len(halves), copies)


def _adamw_math(w, g, m, v):
    m_new = ADAM_B1 * m + (1.0 - ADAM_B1) * g
    v_new = ADAM_B2 * v + (1.0 - ADAM_B2) * (g * g)
    m_hat = m_new * (1.0 / (1.0 - ADAM_B1 ** ADAM_STEP))
    v_hat = v_new * (1.0 / (1.0 - ADAM_B2 ** ADAM_STEP))
    delta = -ADAM_LR * (m_hat / (jnp.sqrt(v_hat) + ADAM_EPS) + ADAM_WD * w)
    return delta, m_new, v_new


def _adamw(w, g_mine, g_other, m, v, core_idx, tr):
    rows, cols = w.shape
    h = rows // 2
    tr = _div_tile(h, tr)
    per = h // tr

    def body(c_ref, w_ref, ga_ref, gb_ref, m_ref, v_ref, g_ref, d_ref, mo_ref, vo_ref):
        g = jnp.where(pl.program_id(0) == c_ref[0], ga_ref[...], gb_ref[...])
        d, mn, vn = _adamw_math(w_ref[...], g, m_ref[...], v_ref[...])
        g_ref[...] = g
        d_ref[...] = d
        mo_ref[...] = mn
        vo_ref[...] = vn

    full = pl.BlockSpec((tr, cols), lambda hh, r, c_ref: (hh * per + r, 0))
    mine = pl.BlockSpec((tr, cols), lambda hh, r, c_ref: (jnp.where(hh == c_ref[0], r, 0), 0))
    other = pl.BlockSpec((tr, cols), lambda hh, r, c_ref: (jnp.where(hh == c_ref[0], 0, r), 0))
    shp = jax.ShapeDtypeStruct(w.shape, F32)
    return _call(
        body, name="adamw",
        grid_spec=pltpu.PrefetchScalarGridSpec(num_scalar_prefetch=1, grid=(2, per), in_specs=[full, mine, other, full, full],
                                               out_specs=[full] * 4),
        out_shape=(shp, shp, shp, shp), compiler_params=_cp(2))(core_idx, w, g_mine, g_other, m, v)


def _adamw_small(ws, gs, ms, vs):
    n = len(ws)

    def body(*refs):
        w_r, g_r, m_r, v_r = refs[0:n], refs[n:2 * n], refs[2 * n:3 * n], refs[3 * n:4 * n]
        d_o, m_o, v_o = refs[4 * n:5 * n], refs[5 * n:6 * n], refs[6 * n:7 * n]
        for i in range(n):
            d, mn, vn = _adamw_math(w_r[i][...], g_r[i][...], m_r[i][...], v_r[i][...])
            d_o[i][...] = d
            m_o[i][...] = mn
            v_o[i][...] = vn

    shp = [jax.ShapeDtypeStruct(w.shape, F32) for w in ws]
    outs = _call(body, name="adamw_small", out_shape=shp * 3)(*ws, *gs, *ms, *vs)
    return outs[0:n], outs[n:2 * n], outs[2 * n:3 * n]


def _rope_tables(t_len):
    pos = jnp.arange(t_len, dtype=F32)
    inv_freq = ROPE_THETA ** (-jnp.arange(0, HEAD_DIM, 2, dtype=F32) / HEAD_DIM)
    ang = pos[:, None] * inv_freq[None, :]
    cos, sin = jnp.cos(ang), jnp.sin(ang)
    return jnp.tile(jnp.concatenate([cos, cos], axis=1), (1, 2)), jnp.tile(jnp.concatenate([-sin, sin], axis=1), (1, 2))


def _dup_heads(a):
    h0, h1 = a[..., 0:64], a[..., 64:128]
    return jnp.concatenate([h0, h0, h1, h1], axis=-1)


def _local_step(x, mem, target, small, wg, comm, tm_a=512, tm_b=256, tc=512, tt=1024):
    def run(stage, fn, n_own, *operands):
        rides = comm.rides(stage)
        res = fn(*operands, rides=rides)
        res = list(res) if isinstance(res, (list, tuple)) else [res]
        brought, pos = [], n_own
        for r in rides:
            brought.append(res[pos:pos + len(r.out_shape)])
            pos += len(r.out_shape)
        comm.landed(stage, brought, wg)
        return res[:n_own]

    t_len = x.shape[0]
    cos_t, sin_t = _rope_tables(t_len)
    b_in = small["b_in"]
    b_ext = jnp.concatenate([b_in[:, 0:1536], _dup_heads(b_in[:, 1536:1664]), _dup_heads(b_in[:, 1664:1792])], axis=1)
    w_dw = jnp.concatenate([wg["w_dw"], jnp.zeros((1, CONV_CH), F32)], axis=0)
    sink_b = jnp.broadcast_to(small["attn_sink"].reshape(8, 1), (8, 128))
    bias_q, bias_k = _band_bias()

    ug, q, kd, vd, h1 = run("in_proj_fwd", _in_proj_fwd, 5, x, small["g_mix"], wg["w_in"], b_ext, cos_t, sin_t, tm_a)
    yc, pre = run("conv_fwd", _conv_fwd, 2, ug, w_dw, small["b_dw"], small["g_conv_ln"], small["b_conv_ln"], tc)
    memn, kv = _mem_kv_fwd(mem, small["g_mem_kv"], wg["w_mem_kv"])
    ya, lse = run("attn_fwd", _attn_fwd, 2, q, kd, vd, sink_b, bias_q)
    ymix, x1, hq, qm, om, x2 = run("mix_mem_fwd", _mix_mem_fwd, 6, x, yc, ya, wg["w_out"], small["b_out"], small["g_mem_q"],
                                   wg["w_mem_q"], kv, wg["w_mem_o"], tm_a)
    hf, gate, up, act = run("ffn_up", _ffn_up, 4, x2, small["g_ffn"], wg["w_gate"], wg["w_up"], tm_a)
    dx3, loss, d_g_final = _ffn_down_loss(x2, act, wg["w_down"], small["g_final"], target, tm_a)

    dx2, dgate, dup, d_g_ffn = _ffn_bwd(dx3, x2, gate, up, small["g_ffn"], wg["w_gate"], wg["w_up"], wg["w_down"], tm_b)
    comm.grad("w_gate", _weight_grad(dgate, hf, "dw_gate", tt))
    comm.grad("w_up", _weight_grad(dup, hf, "dw_up", tt))
    comm.grad("w_down", _weight_grad(act, dx3, "dw_down", tt))
    dx1, dqm, dyc, dya, dkv, d_g_mem_q, d_b_out = run("mix_mem_bwd", _mix_mem_bwd, 7, dx2, x1, qm, kv, small["g_mem_q"],
                                                      wg["w_mem_q"], wg["w_mem_o"], wg["w_out"], tm_a)
    d_w_mem_kv, d_g_mem_kv = _mem_kv_bwd(dkv, memn, mem, small["g_mem_kv"], wg["w_mem_kv"])
    comm.grad("w_mem_kv", d_w_mem_kv)
    comm.grad("w_out", _weight_grad(ymix, dx1, "dw_out", tt))
    comm.grad("w_mem_q", _weight_grad(hq, dqm, "dw_mem_q", tt))
    comm.grad("w_mem_o", _weight_grad(om, dx2, "dw_mem_o", tt))
    dq, dd, dsink = run("attn_bwd_q", _attn_bwd_q, 3, q, kd, vd, dya, lse, sink_b, bias_q, cos_t, sin_t)
    dk, dv = run("attn_bwd_kv", _attn_bwd_kv, 2, q, kd, vd, dya, lse, dd, bias_k, cos_t, sin_t)
    dpre, cstats = _conv_norm_bwd(pre, dyc, small["g_conv_ln"], small["b_conv_ln"], tc)
    du_glu, d_w_dw = run("conv_bwd", _conv_bwd, 2, dpre, ug, w_dw, tc)
    grad_x, du, d_b_in, d_g_mix = run("in_proj_bwd", _in_proj_bwd, 4, du_glu, dq, dk, dv, dx1, x, small["g_mix"], wg["w_in"],
                                      tm_a)
    grads = {
        "w_dw": jnp.sum(d_w_dw.reshape(32, 8, CONV_CH), axis=1)[0:CONV_W],
        "g_mix": d_g_mix, "b_in": d_b_in, "b_dw": cstats[2:3], "g_conv_ln": cstats[0:1],
        "b_conv_ln": cstats[1:2], "attn_sink": jnp.sum(dsink[:, :, 0], axis=0)[None, :], "b_out": d_b_out,
        "g_mem_q": d_g_mem_q, "g_mem_kv": d_g_mem_kv, "g_ffn": d_g_ffn, "g_final": d_g_final,
    }
    comm.small(loss[0:1, 0:1], grads)
    (d_w_in,) = run("dw_in", _weight_grad, 1, du, h1, "dw_in", tt)
    comm.grad("w_in", d_w_in)
    return loss[0:1, 0:1], grad_x, grads


BIG = ["w_in", "w_out", "w_mem_q", "w_mem_kv", "w_mem_o", "w_gate", "w_up", "w_down"]
KEEP_SLABS = ("w_mem_kv",)
TRANSPOSED = ("w_in", "w_gate", "w_up")
SMALL = ["g_mix", "b_in", "b_dw", "g_conv_ln", "b_conv_ln", "attn_sink", "b_out", "g_mem_q", "g_mem_kv", "g_ffn", "g_final"]
PACK_ROWS = 32
ROWS_PER_STEP = 512

GATHER_ON = {"in_proj_fwd": ("w_out", "w_mem_q"), "conv_fwd": ("w_mem_kv", "w_mem_o"), "attn_fwd": ("w_gate",),
             "mix_mem_fwd": ("w_up",), "ffn_up": ("w_down",)}
FFN_GROUP = ("w_gate", "w_up", "w_down")
MID_GROUP = ("w_mem_kv", "w_out", "w_mem_q", "w_mem_o")
SWAP_ON = {"mix_mem_bwd": FFN_GROUP, "attn_bwd_q": MID_GROUP}
EXCHANGE_ON = {"attn_bwd_q": ("w_gate",), "attn_bwd_kv": ("w_up", "w_mem_kv"), "conv_bwd": ("w_down", "w_out"),
               "dw_in": ("w_mem_q", "w_mem_o")}
SMALL_ON = "dw_in"


def _as_weight(name, gathered):
    g = gathered.reshape(N_CHIPS, gathered.shape[2] * 2, gathered.shape[3])
    return g if name in KEEP_SLABS else g.reshape(-1, g.shape[2])


class _Overlap:
    def __init__(self, bufs, chip_idx, core_idx):
        self.bufs, self.chip_idx, self.core_idx = bufs, chip_idx, core_idx
        self.parts, self.sums, self.others = {}, {}, {}

    def rides(self, stage):
        rides = []
        if stage in GATHER_ON:
            rides.append(_gather_ride([self.bufs[k] for k in GATHER_ON[stage]]))
        if stage in EXCHANGE_ON:
            rides.append(_exchange_ride([self.sums[k] for k in EXCHANGE_ON[stage]]))
        if stage in SWAP_ON:
            rides.append(_swap_ride([self.parts[k] for k in SWAP_ON[stage]]))
        if stage == SMALL_ON:
            rides.append(_allgather_ride(self.packs))
        return rides

    def landed(self, stage, brought, wg):
        brought = list(brought)
        if stage in GATHER_ON:
            for k, g in zip(GATHER_ON[stage], brought.pop(0)):
                wg[k] = _as_weight(k, g)
        if stage in EXCHANGE_ON:
            self.others.update(zip(EXCHANGE_ON[stage], brought.pop(0)))
        if stage in SWAP_ON:
            self._pair(SWAP_ON[stage], brought.pop(0))
        if stage == SMALL_ON:
            (self.packs,) = brought.pop(0)

    def small(self, loss, grads):
        x, y, c = lax.axis_index("x"), lax.axis_index("y"), lax.axis_index("c")
        pack = _pack_small(loss, grads)
        self.packs = lax.dynamic_update_slice(jnp.zeros((N_DEV,) + pack.shape, F32), pack[None], (4 * x + 2 * y + c, 0, 0))

    def grad(self, name, g):
        if g.ndim == 2:
            g = g.reshape(N_CHIPS, g.shape[0] // N_CHIPS, g.shape[1])
        self.parts[name] = g.reshape(N_CHIPS, 2, g.shape[1] // 2, g.shape[2])

    def _pair(self, names, from_sibling):
        for k, o in zip(names, from_sibling):
            self.sums[k] = _pair_sum(self.parts[k], o, self.core_idx, ROWS_PER_STEP)

    def finish(self):
        (from_sibling,) = _run_rides("swap_last", [_swap_ride([self.parts["w_in"]])])
        self._pair(("w_in",), from_sibling)
        ((self.others["w_in"],),) = _run_rides("exchange_last", [_exchange_ride([self.sums["w_in"]])])
        mine = [_chip_sum(self.sums[k], self.others[k], self.chip_idx, ROWS_PER_STEP) for k in BIG]
        (theirs,) = _run_rides("sibling_share", [_share_ride(mine)])
        return mine, theirs, _sum_slabs(self.packs)


def _pack_small(loss, grads):
    def row(a):
        a = a.reshape(1, -1)
        return jnp.pad(a, ((0, 0), (0, 1024 - a.shape[1])))

    rows = [row(grads[k]) for k in ("g_mix", "b_out", "g_mem_q", "g_mem_kv", "g_ffn", "g_final")]
    rows += [grads["b_in"][:, 0:1024], row(grads["b_in"][:, 1024:1792])]
    rows += [jnp.concatenate([grads["b_dw"], grads["g_conv_ln"]], axis=1), row(grads["b_conv_ln"]), row(grads["attn_sink"]),
             row(loss)]
    dw = jnp.pad(grads["w_dw"], ((0, 1), (0, 0))).reshape(16, 1024)
    pack = jnp.concatenate(rows + [dw], axis=0)
    return jnp.pad(pack, ((0, PACK_ROWS - pack.shape[0]), (0, 0)))


def _unpack_small(pack):
    out = {k: pack[i:i + 1] for i, k in enumerate(("g_mix", "b_out", "g_mem_q", "g_mem_kv", "g_ffn", "g_final"))}
    out["b_in"] = jnp.concatenate([pack[6:7], pack[7:8, 0:768]], axis=1)
    out["b_dw"], out["g_conv_ln"] = pack[8:9, 0:512], pack[8:9, 512:1024]
    out["b_conv_ln"] = pack[9:10, 0:512]
    out["attn_sink"] = pack[10:11, 0:8]
    loss = pack[11, 0]
    dw = pack[12:28].reshape(32, 512)[0:CONV_W]
    return loss, out, dw


def kernel(x, mem, g_mix, w_in, b_in, w_dw, b_dw, g_conv_ln, b_conv_ln, attn_sink, w_out, b_out, g_mem_q, g_mem_kv, w_mem_q, w_mem_kv, w_mem_o, g_ffn, w_gate, w_up, w_down, g_final, loss_target, m_g_mix, m_w_in, m_b_in, m_w_dw, m_b_dw, m_g_conv_ln, m_b_conv_ln, m_attn_sink, m_w_out, m_b_out, m_g_mem_q, m_g_mem_kv, m_w_mem_q, m_w_mem_kv, m_w_mem_o, m_g_ffn, m_w_gate, m_w_up, m_w_down, m_g_final, v_g_mix, v_w_in, v_b_in, v_w_dw, v_b_dw, v_g_conv_ln, v_b_conv_ln, v_attn_sink, v_w_out, v_b_out, v_g_mem_q, v_g_mem_kv, v_w_mem_q, v_w_mem_kv, v_w_mem_o, v_g_ffn, v_w_gate, v_w_up, v_w_down, v_g_final):
    args = dict(locals())
    weight_names = ["g_mix", "w_in", "b_in", "w_dw", "b_dw", "g_conv_ln", "b_conv_ln", "attn_sink", "w_out", "b_out", "g_mem_q",
                    "g_mem_kv", "w_mem_q", "w_mem_kv", "w_mem_o", "g_ffn", "w_gate", "w_up", "w_down", "g_final"]
    chip = 2 * lax.axis_index("x") + lax.axis_index("y")
    core = lax.axis_index("c")

    chip_idx = chip.astype(jnp.int32).reshape(1)
    core_idx = core.astype(jnp.int32).reshape(1)

    def block(name):
        a = args[name][0]
        weight = name[2:] if name[:2] in ("m_", "v_") else name
        return a.T if weight in TRANSPOSED else a

    comm = _Overlap({k: _cast_place(block(k), chip_idx, ROWS_PER_STEP) for k in BIG}, chip_idx, core_idx)
    dw_buf = lax.dynamic_update_slice(jnp.zeros((N_CHIPS, CONV_W, 128), F32), w_dw, (chip, 0, 0))
    (first,), (dw_all,) = _run_rides("gather_first", [_gather_ride([comm.bufs["w_in"]]), _spread_ride(dw_buf)])
    wg = {"w_in": _as_weight("w_in", first), "w_dw": jnp.transpose(dw_all, (1, 0, 2)).reshape(CONV_W, CONV_CH)}
    small = {k: args[k].reshape(1, -1) for k in SMALL}

    loss, grad_x, grads = _local_step(x[0], mem[0], loss_target[0], small, wg, comm)

    halves, other_halves, pack_sum = comm.finish()

    loss_sum, small_grads, dw_full = _unpack_small(pack_sum)
    dw_cols = jnp.transpose(dw_full.reshape(CONV_W, N_CHIPS, 128), (1, 0, 2))
    small_grads["w_dw"] = lax.dynamic_index_in_dim(dw_cols, chip, axis=0, keepdims=False)

    out_g, out_d, out_m, out_v = {}, {}, {}, {}
    for k, g_mine, g_other in zip(BIG, halves, other_halves):
        res = _adamw(block(k), g_mine, g_other, block("m_" + k), block("v_" + k), core_idx, ROWS_PER_STEP)
        out_g[k], out_d[k], out_m[k], out_v[k] = [(r.T if k in TRANSPOSED else r)[None] for r in res]
    names = SMALL + ["w_dw"]

    def flat(a):
        return a[0] if a.ndim == 3 else a.reshape(1, -1)

    def pad_lanes(a):
        return jnp.pad(a, ((0, 0), (0, 128 - a.shape[1]))) if a.shape[1] < 128 else a

    ws = [flat(args[k]) for k in names]
    gs = [small_grads[k] for k in names]
    ms = [flat(args["m_" + k]) for k in names]
    vs = [flat(args["v_" + k]) for k in names]
    ds, mns, vns = _adamw_small([pad_lanes(a) for a in ws], [pad_lanes(a) for a in gs], [pad_lanes(a) for a in ms],
                                [pad_lanes(a) for a in vs])
    for i, k in enumerate(names):
        n_lanes = ws[i].shape[1]
        for out, val in ((out_g, gs[i]), (out_d, ds[i]), (out_m, mns[i]), (out_v, vns[i])):
            out[k] = val[:, 0:n_lanes].reshape(args[k].shape)

    return (loss_sum, grad_x[None], *[out_g[k] for k in weight_names], *[out_d[k] for k in weight_names],
            *[out_m[k] for k in weight_names], *[out_v[k] for k in weight_names])
```

```python
import jax
import jax.numpy as jnp
import numpy as np
from jax import lax
from jax.experimental import pallas as pl
from jax.experimental.pallas import tpu as pltpu

F32 = jnp.float32
BF16 = jnp.bfloat16
EPS = 1e-6
NEG = -1e30

D_MODEL = 1024
CONV_CH = 512
CONV_W = 31
HEAD_DIM = 64
BLK = 128
MEM_HEADS = 4
MEM_HD = 256
N_CHIPS = 4
N_DEV = 8
ATT_SCALE = HEAD_DIM ** -0.5
MEM_SCALE = MEM_HD ** -0.5
ROPE_THETA = 10000.0

ADAM_LR = 0.001
ADAM_B1 = 0.9
ADAM_B2 = 0.999
ADAM_EPS = 1e-08
ADAM_WD = 0.01
ADAM_STEP = 10

VMEM_LIMIT_BYTES = 56 * 1024 * 1024
MESH = pl.DeviceIdType.MESH


class _Ride:
    def __init__(self, operands, out_shape, n_sem, start, finish, aliases=None):
        self.operands, self.out_shape, self.n_sem = list(operands), list(out_shape), n_sem
        self.start, self.finish, self.aliases = start, finish, dict(aliases or {})


def _call(body, rides=(), **kw):
    if not rides:
        return pl.pallas_call(body, **kw)
    grid = kw["grid"]
    n_in, n_out = len(kw["in_specs"]), len(kw["out_specs"])
    scratch = list(kw.get("scratch_shapes", ()))
    k_in = [len(r.operands) for r in rides]
    k_out = [len(r.out_shape) for r in rides]

    def carried(*refs):
        pos = n_in
        r_in, r_out = [], []
        for k in k_in:
            r_in.append(refs[pos:pos + k])
            pos += k
        own_out = refs[pos:pos + n_out]
        pos += n_out
        for k in k_out:
            r_out.append(refs[pos:pos + k])
            pos += k
        own_scratch = refs[pos:pos + len(scratch)]
        sems = refs[pos + len(scratch):]
        first = last = None
        for axis, n_steps in enumerate(grid):
            step = pl.program_id(axis)
            first = (step == 0) if first is None else first & (step == 0)
            last = (step == n_steps - 1) if last is None else last & (step == n_steps - 1)

        @pl.when(first)
        def _():
            for j, r in enumerate(rides):
                r.start(r_in[j], r_out[j], sems[2 * j], sems[2 * j + 1])

        body(*refs[:n_in], *own_out, *own_scratch)

        @pl.when(last)
        def _():
            for j, r in enumerate(rides):
                r.finish(r_in[j], r_out[j], sems[2 * j], sems[2 * j + 1])

    kw = dict(kw)
    kw["in_specs"] = list(kw["in_specs"]) + [ANY] * sum(k_in)
    kw["out_specs"] = list(kw["out_specs"]) + [ANY] * sum(k_out)
    kw["out_shape"] = list(kw["out_shape"]) + [s for r in rides for s in r.out_shape]
    kw["scratch_shapes"] = scratch + [pltpu.SemaphoreType.DMA((r.n_sem,)) for r in rides for _ in range(2)]
    aliases, off_in, off_out = {}, n_in, n_out
    for r, ki, ko in zip(rides, k_in, k_out):
        aliases.update({off_in + a: off_out + b for a, b in r.aliases.items()})
        off_in, off_out = off_in + ki, off_out + ko
    if aliases:
        kw["input_output_aliases"] = aliases
    call = pl.pallas_call(carried, **kw)
    return lambda *args: call(*args, *[op for r in rides for op in r.operands])


def _cp(n_grid):
    return pltpu.CompilerParams(dimension_semantics=("arbitrary",) * n_grid, vmem_limit_bytes=VMEM_LIMIT_BYTES)


def _res(shape):
    nd = len(shape)
    return pl.BlockSpec(shape, lambda *_: (0,) * nd, pipeline_mode=pl.Buffered(1))


def _rows(tm, n):
    return pl.BlockSpec((tm, n), lambda i: (i, 0))


def _div_tile(n, target):
    best = None
    for d in range(16, min(n, target) + 1, 16):
        if n % d == 0:
            best = d
    assert best is not None, (n, target)
    return best


def _dot(a, b):
    return jnp.dot(a, b, preferred_element_type=F32)


def _dot_nt(a, b):
    return lax.dot_general(a, b, (((1,), (1,)), ((), ())), preferred_element_type=F32)


def _dot_tn(a, b):
    return lax.dot_general(a, b, (((0,), (0,)), ((), ())), preferred_element_type=F32)


def _bf(x):
    return x.astype(BF16)


def _sigmoid(x):
    return 1.0 / (1.0 + jnp.exp(-x))


def _rms_fwd(x, g):
    r = lax.rsqrt(jnp.mean(x * x, axis=-1, keepdims=True) + EPS)
    xh = x * r
    return xh * g, xh, r


def _rms_bwd(dh, xh, r, g):
    dxh = dh * g
    return r * (dxh - xh * jnp.mean(dxh * xh, axis=-1, keepdims=True))


def _colsum(x):
    return jnp.sum(x, axis=0, keepdims=True)


def _rope(x, cos, sin, sign):
    n = x.shape[1] // 128
    c = jnp.tile(cos, (1, n)) if n > 1 else cos
    s = jnp.tile(sin, (1, n)) if n > 1 else sin
    lane = lax.broadcasted_iota(jnp.int32, x.shape, 1)
    first = (lane & 63) < 32
    partner = jnp.where(first, pltpu.roll(x, x.shape[1] - 32, 1), pltpu.roll(x, 32, 1))
    return x * c + sign * (partner * s)


def _lo_lanes(shape):
    return lax.broadcasted_iota(jnp.int32, shape, 1) < 64


def _stack_heads(t):
    t0, t1 = t[:, 0:128], t[:, 128:256]
    lo = _lo_lanes(t0.shape)
    z = jnp.zeros_like(t0)
    return jnp.concatenate([jnp.where(lo, t0, z), jnp.where(lo, z, t0), jnp.where(lo, t1, z), jnp.where(lo, z, t1)], axis=0)


def _unstack_heads(o):
    lo = _lo_lanes((BLK, 128))
    return jnp.concatenate([jnp.where(lo, o[0:128], o[128:256]), jnp.where(lo, o[256:384], o[384:512])], axis=1)


def _fold_heads(parts):
    a, b = (p + pltpu.roll(p, 64, 1) for p in parts)
    return jnp.where(_lo_lanes(a.shape), a, b)


def _sink_col(sk_ref, g):
    return jnp.concatenate([jnp.broadcast_to(sk_ref[4 * g + h:4 * g + h + 1, :], (BLK, 128)) for h in range(4)], axis=0)


def _tile3(x):
    return jnp.concatenate([x, x, x], axis=1)


def _mem_kv_fwd(mem, g_kv, w_kv):
    m_len = mem.shape[0]
    cols = w_kv.shape[2]

    def body(mem_ref, g_ref, w_ref, memn_ref, kv_ref):
        h, _, _ = _rms_fwd(mem_ref[...], g_ref[...])
        hb = _bf(h)
        memn_ref[...] = hb
        for s in range(N_CHIPS):
            kv_ref[s] = _bf(_dot(hb, w_ref[s]))

    return _call(
        body, name="mem_kv_fwd",
        out_shape=(jax.ShapeDtypeStruct((m_len, D_MODEL), BF16), jax.ShapeDtypeStruct((N_CHIPS, m_len, cols), BF16)),
        compiler_params=pltpu.CompilerParams(vmem_limit_bytes=VMEM_LIMIT_BYTES),
    )(mem, g_kv, w_kv)


def _dup_head_rows(w_ref, lo):
    h0, h1 = w_ref[lo:lo + 64, :], w_ref[lo + 64:lo + 128, :]
    return jnp.concatenate([h0, h0, h1, h1], axis=0)


def _in_proj_fwd(x, g_mix, w_t, b_ext, cos_t, sin_t, tm, rides=()):
    t_len = x.shape[0]

    def body(x_ref, g_ref, w_ref, b_ref, c_ref, s_ref, ug_ref, q_ref, k_ref, v_ref, h_ref):
        h, _, _ = _rms_fwd(x_ref[...], g_ref[...])
        hb = _bf(h)
        h_ref[...] = hb
        ug_ref[...] = _dot_nt(hb, w_ref[0:1024, :]) + b_ref[:, 0:1024]
        c, s = c_ref[...], s_ref[...]
        q_ref[...] = _bf(_rope(_dot_nt(hb, w_ref[1024:1536, :]) + b_ref[:, 1024:1536], c, s, 1.0))
        k_ref[...] = _bf(_rope(_dot_nt(hb, _dup_head_rows(w_ref, 1536)) + b_ref[:, 1536:1792], c, s, 1.0))
        v_ref[...] = _bf(_dot_nt(hb, _dup_head_rows(w_ref, 1664)) + b_ref[:, 1792:2048])

    return _call(
        body, rides=rides, name="in_proj_fwd", grid=(t_len // tm,),
        in_specs=[_rows(tm, D_MODEL), _res((1, D_MODEL)), _res(w_t.shape), _res(b_ext.shape), _rows(tm, 128), _rows(tm, 128)],
        out_specs=[_rows(tm, 1024), _rows(tm, 512), _rows(tm, 256), _rows(tm, 256), _rows(tm, D_MODEL)],
        out_shape=(jax.ShapeDtypeStruct((t_len, 1024), F32), jax.ShapeDtypeStruct((t_len, 512), BF16),
                   jax.ShapeDtypeStruct((t_len, 256), BF16), jax.ShapeDtypeStruct((t_len, 256), BF16),
                   jax.ShapeDtypeStruct((t_len, D_MODEL), BF16)),
        compiler_params=_cp(1),
    )(x, g_mix, w_t, b_ext, cos_t, sin_t)


def _halo_specs(tc, n, t_len):
    per = tc // 16
    last = t_len // 16 - 1
    return [pl.BlockSpec((16, n), lambda i: (jnp.maximum(i * per - 1, 0), 0)),
            pl.BlockSpec((tc, n), lambda i: (i, 0)),
            pl.BlockSpec((16, n), lambda i: (jnp.minimum((i + 1) * per, last), 0))]


def _glu(z):
    return z[:, 0:CONV_CH] * _sigmoid(z[:, CONV_CH:2 * CONV_CH])


def _fill_halo_buf(buf, prev, main, nxt, i, n_tiles, tc):
    buf[0:16, :] = jnp.where(i > 0, prev, jnp.zeros_like(prev))
    buf[16:16 + tc, :] = main
    buf[16 + tc:32 + tc, :] = jnp.where(i < n_tiles - 1, nxt, jnp.zeros_like(nxt))


CONV_ROWS = 64


def _shift_copies(buf, shifted, tc):
    for r in range(1, 8):
        shifted[r - 1, :, :] = buf[r:r + tc + 24, :]


def _shifted_rows(buf, shifted, offset, base):
    src = buf if offset % 8 == 0 else shifted.at[offset % 8 - 1]
    return src[pl.ds(pl.multiple_of(base + 8 * (offset // 8), 8), CONV_ROWS), :]


def _conv_fwd(ug, w_dw, b_dw, g_ln, b_ln, tc, rides=()):
    t_len = ug.shape[0]
    n_tiles = t_len // tc

    def body(up_ref, um_ref, un_ref, w_ref, bdw_ref, g_ref, b_ref, y_ref, pre_ref, buf, shifted):
        i = pl.program_id(0)
        _fill_halo_buf(buf, _glu(up_ref[...]), _glu(um_ref[...]), _glu(un_ref[...]), i, n_tiles, tc)
        _shift_copies(buf, shifted, tc)

        def chunk(c, carry):
            base = c * CONV_ROWS
            acc = jnp.zeros((CONV_ROWS, CONV_CH), F32)
            for k in range(CONV_W):
                acc = acc + w_ref[k:k + 1, :] * _shifted_rows(buf, shifted, k + 1, base)
            pre_ref[pl.ds(pl.multiple_of(base, CONV_ROWS), CONV_ROWS), :] = acc + bdw_ref[...]
            return carry

        lax.fori_loop(0, tc // CONV_ROWS, chunk, 0)
        pre = pre_ref[...]
        mu = jnp.mean(pre, axis=-1, keepdims=True)
        d = pre - mu
        rstd = lax.rsqrt(jnp.mean(d * d, axis=-1, keepdims=True) + EPS)
        ln = d * rstd * g_ref[...] + b_ref[...]
        y_ref[...] = _bf(ln * _sigmoid(ln))

    return _call(
        body, rides=rides, name="conv_fwd", grid=(n_tiles,),
        in_specs=_halo_specs(tc, 1024, t_len) + [_res((32, CONV_CH)), _res((1, CONV_CH)), _res((1, CONV_CH)), _res((1, CONV_CH))],
        out_specs=[_rows(tc, CONV_CH), _rows(tc, CONV_CH)],
        out_shape=(jax.ShapeDtypeStruct((t_len, CONV_CH), BF16), jax.ShapeDtypeStruct((t_len, CONV_CH), F32)),
        scratch_shapes=[pltpu.VMEM((tc + 32, CONV_CH), F32), pltpu.VMEM((7, tc + 24, CONV_CH), F32)],
        compiler_params=_cp(1),
    )(ug, ug, ug, w_dw, b_dw, g_ln, b_ln)


def _nbr_specs(n, nb):
    return [pl.BlockSpec((BLK, n), lambda i: (jnp.maximum(i - 1, 0), 0)),
            pl.BlockSpec((BLK, n), lambda i: (i, 0)),
            pl.BlockSpec((BLK, n), lambda i: (jnp.minimum(i + 1, nb - 1), 0))]


def _nbr_specs4(nb):
    return [pl.BlockSpec((1, 2, 4 * BLK, 128), lambda i: (jnp.maximum(i - 1, 0), 0, 0, 0)),
            pl.BlockSpec((1, 2, 4 * BLK, 128), lambda i: (i, 0, 0, 0)),
            pl.BlockSpec((1, 2, 4 * BLK, 128), lambda i: (jnp.minimum(i + 1, nb - 1), 0, 0, 0))]


def _band_bias():
    a = np.arange(4 * BLK)[:, None] % BLK
    c = np.arange(3 * BLK)[None, :]
    inside = np.abs(c - BLK - a) <= BLK
    q_side = np.stack([inside & (c >= BLK), inside, inside & (c < 2 * BLK)])
    blk = np.arange(12 * BLK)[:, None] // (4 * BLK)
    a = np.arange(12 * BLK)[:, None] % BLK
    c = np.arange(BLK)[None, :]
    inside = np.abs(c - a + (1 - blk) * BLK) <= BLK
    k_side = np.stack([inside & (blk >= 1), inside, inside & (blk <= 1)])
    return [jnp.asarray(np.where(m, 0.0, NEG).astype(np.float32)) for m in (q_side, k_side)]


def _edge_spec(shape, nb):
    return pl.BlockSpec((1,) + shape, lambda i: (jnp.where(i == 0, 0, jnp.where(i == nb - 1, 2, 1)),) + (0,) * len(shape))


def _attn_fwd(q, kd, vd, sink_b, bias, rides=()):
    t_len = q.shape[0]
    nb = t_len // BLK

    def body(q_ref, kp_ref, kc_ref, kn_ref, vp_ref, vc_ref, vn_ref, sk_ref, bias_ref, y_ref, lse_ref):
        kcat = jnp.concatenate([kp_ref[...], kc_ref[...], kn_ref[...]], axis=0)
        vcat = jnp.concatenate([vp_ref[...], vc_ref[...], vn_ref[...]], axis=0)
        ys = []
        for g in range(2):
            qs = _stack_heads(q_ref[:, 256 * g:256 * g + 256])
            s = _dot_nt(qs, kcat[:, 128 * g:128 * g + 128]) * ATT_SCALE + bias_ref[0]
            skc = _sink_col(sk_ref, g)
            m_b = jnp.maximum(jnp.max(s, axis=-1, keepdims=True), skc)
            p = jnp.exp(s - _tile3(m_b))
            den_b = jnp.sum(p, axis=-1, keepdims=True) + jnp.exp(skc - m_b)
            pn = p * _tile3(1.0 / den_b)
            o = _dot(_bf(pn), vcat[:, 128 * g:128 * g + 128])
            ys.append(_unstack_heads(o))
            lse_ref[0, g] = m_b + jnp.log(den_b)
        y_ref[...] = _bf(jnp.concatenate(ys, axis=1))

    return _call(
        body, rides=rides, name="attn_fwd", grid=(nb,),
        in_specs=[_rows(BLK, 512)] + _nbr_specs(256, nb) + _nbr_specs(256, nb) + [_res((8, 128)), _edge_spec((4 * BLK, 3 * BLK), nb)],
        out_specs=[_rows(BLK, 512), pl.BlockSpec((1, 2, 4 * BLK, 128), lambda i: (i, 0, 0, 0))],
        out_shape=(jax.ShapeDtypeStruct((t_len, 512), BF16), jax.ShapeDtypeStruct((nb, 2, 4 * BLK, 128), F32)),
        compiler_params=_cp(1),
    )(q, kd, kd, kd, vd, vd, vd, sink_b, bias)


def _mem_heads(kv_ref, h):
    lo = MEM_HD * (h % 2)
    return kv_ref[h // 2, :, lo:lo + MEM_HD], kv_ref[2 + h // 2, :, lo:lo + MEM_HD]


def _mix_mem_fwd(x, yc, ya, w_out, b_out, g_q, w_q, kv, w_o, tm, rides=()):
    t_len = x.shape[0]

    def body(x_ref, yc_ref, ya_ref, wout_ref, bout_ref, g_ref, wq_ref, kv_ref, wo_ref,
             ymix_ref, x1_ref, hq_ref, qm_ref, om_ref, x2_ref):
        ymix = jnp.concatenate([yc_ref[...], ya_ref[...]], axis=1)
        ymix_ref[...] = ymix
        x1 = x_ref[...] + _dot(ymix, wout_ref[...]) + bout_ref[...]
        x1_ref[...] = x1
        hq, _, _ = _rms_fwd(x1, g_ref[...])
        hqb = _bf(hq)
        hq_ref[...] = hqb
        qm = _bf(_dot(hqb, wq_ref[...]))
        qm_ref[...] = qm
        outs = []
        for h in range(MEM_HEADS):
            kh, vh = _mem_heads(kv_ref, h)
            s = _dot_nt(qm[:, MEM_HD * h:MEM_HD * (h + 1)], kh) * MEM_SCALE
            p = jnp.exp(s - jnp.max(s, axis=-1, keepdims=True))
            p = p * (1.0 / jnp.sum(p, axis=-1, keepdims=True))
            outs.append(_dot(_bf(p), vh))
        om = _bf(jnp.concatenate(outs, axis=1))
        om_ref[...] = om
        x2_ref[...] = x1 + _dot(om, wo_ref[...])

    act_b = jax.ShapeDtypeStruct((t_len, D_MODEL), BF16)
    act_f = jax.ShapeDtypeStruct((t_len, D_MODEL), F32)
    return _call(
        body, rides=rides, name="mix_mem_fwd", grid=(t_len // tm,),
        in_specs=[_rows(tm, D_MODEL), _rows(tm, 512), _rows(tm, 512), _res(w_out.shape), _res((1, D_MODEL)), _res((1, D_MODEL)),
                  _res(w_q.shape), _res(kv.shape), _res(w_o.shape)],
        out_specs=[_rows(tm, D_MODEL)] * 6,
        out_shape=(act_b, act_f, act_b, act_b, act_b, act_f),
        compiler_params=_cp(1),
    )(x, yc, ya, w_out, b_out, g_q, w_q, kv, w_o)


def _hidden_chunks(ff, width=1024):
    return [(lo, min(lo + width, ff)) for lo in range(0, ff, width)]


def _ffn_up(x2, g_ffn, w_gate, w_up, tm, rides=()):
    t_len = x2.shape[0]
    ff = w_gate.shape[0]

    def body(x2_ref, g_ref, wg_ref, wu_ref, hf_ref, gate_ref, up_ref, act_ref):
        hf, _, _ = _rms_fwd(x2_ref[...], g_ref[...])
        hfb = _bf(hf)
        hf_ref[...] = hfb
        for lo, hi in _hidden_chunks(ff):
            gate = _dot_nt(hfb, wg_ref[lo:hi, :])
            up = _dot_nt(hfb, wu_ref[lo:hi, :])
            gate_ref[:, lo:hi] = _bf(gate)
            up_ref[:, lo:hi] = _bf(up)
            act_ref[:, lo:hi] = _bf(gate * _sigmoid(gate) * up)

    hid = jax.ShapeDtypeStruct((t_len, ff), BF16)
    return _call(
        body, rides=rides, name="ffn_up", grid=(t_len // tm,),
        in_specs=[_rows(tm, D_MODEL), _res((1, D_MODEL)), _res(w_gate.shape), _res(w_up.shape)],
        out_specs=[_rows(tm, D_MODEL), _rows(tm, ff), _rows(tm, ff), _rows(tm, ff)],
        out_shape=[jax.ShapeDtypeStruct((t_len, D_MODEL), BF16), hid, hid, hid],
        compiler_params=_cp(1),
    )(x2, g_ffn, w_gate, w_up)


def _ffn_down_loss(x2, act, w_down, g_final, target, tm):
    t_len = x2.shape[0]
    ff = w_down.shape[0]

    def body(x2_ref, act_ref, wd_ref, gf_ref, tgt_ref, dx3_ref, loss_ref, dgf_ref):
        i = pl.program_id(0)
        x3 = x2_ref[...]
        for lo, hi in _hidden_chunks(ff):
            x3 = x3 + _dot(act_ref[:, lo:hi], wd_ref[lo:hi, :])
        gf = gf_ref[...]
        y, xh, r = _rms_fwd(x3, gf)
        err = y - tgt_ref[...]
        part = 0.5 * jnp.sum(jnp.mean(err * err, axis=-1, keepdims=True), axis=0, keepdims=True)
        dy = err * (1.0 / D_MODEL)
        dx3_ref[...] = _rms_bwd(dy, xh, r, gf)

        @pl.when(i == 0)
        def _():
            loss_ref[...] = jnp.zeros_like(loss_ref)
            dgf_ref[...] = jnp.zeros_like(dgf_ref)

        loss_ref[...] += jnp.broadcast_to(part, loss_ref.shape)
        dgf_ref[...] += _colsum(dy * xh)

    vec = pl.BlockSpec((1, D_MODEL), lambda i: (0, 0))
    return _call(
        body, name="ffn_down_loss", grid=(t_len // tm,),
        in_specs=[_rows(tm, D_MODEL), _rows(tm, ff), _res(w_down.shape), _res((1, D_MODEL)), _rows(tm, D_MODEL)],
        out_specs=[_rows(tm, D_MODEL), vec, vec],
        out_shape=(jax.ShapeDtypeStruct((t_len, D_MODEL), F32), jax.ShapeDtypeStruct((1, D_MODEL), F32),
                   jax.ShapeDtypeStruct((1, D_MODEL), F32)),
        compiler_params=_cp(1),
    )(x2, act, w_down, g_final, target)


def _ffn_bwd(dx3, x2, gate, up, g_ffn, w_gate, w_up, w_down, tm):
    t_len = x2.shape[0]
    ff = w_gate.shape[0]

    def body(dx3_ref, x2_ref, gate_ref, up_ref, g_ref, wg_ref, wu_ref, wd_ref, dx2_ref, dgate_ref, dup_ref, dg_ref):
        i = pl.program_id(0)
        dx3 = dx3_ref[...]
        d3b = _bf(dx3)
        dh = jnp.zeros((tm, D_MODEL), F32)
        for lo, hi in _hidden_chunks(ff):
            dact = _dot_nt(d3b, wd_ref[lo:hi, :])
            gt = gate_ref[:, lo:hi].astype(F32)
            u = up_ref[:, lo:hi].astype(F32)
            sg = _sigmoid(gt)
            dup = _bf(dact * (gt * sg))
            dgate = _bf(dact * u * (sg * (1.0 + gt * (1.0 - sg))))
            dup_ref[:, lo:hi] = dup
            dgate_ref[:, lo:hi] = dgate
            dh = dh + _dot(dgate, wg_ref[lo:hi, :]) + _dot(dup, wu_ref[lo:hi, :])
        g = g_ref[...]
        _, xh, r = _rms_fwd(x2_ref[...], g)
        dx2_ref[...] = dx3 + _rms_bwd(dh, xh, r, g)

        @pl.when(i == 0)
        def _():
            dg_ref[...] = jnp.zeros_like(dg_ref)

        dg_ref[...] += _colsum(dh * xh)

    hid = jax.ShapeDtypeStruct((t_len, ff), BF16)
    hid_spec = _rows(tm, ff)
    return _call(
        body, name="ffn_bwd", grid=(t_len // tm,),
        in_specs=[_rows(tm, D_MODEL), _rows(tm, D_MODEL), hid_spec, hid_spec, _res((1, D_MODEL)),
                  _res(w_gate.shape), _res(w_up.shape), _res(w_down.shape)],
        out_specs=[_rows(tm, D_MODEL), hid_spec, hid_spec, pl.BlockSpec((1, D_MODEL), lambda i: (0, 0))],
        out_shape=(jax.ShapeDtypeStruct((t_len, D_MODEL), F32), hid, hid, jax.ShapeDtypeStruct((1, D_MODEL), F32)),
        compiler_params=_cp(1),
    )(dx3, x2, gate, up, g_ffn, w_gate, w_up, w_down)


def _mix_mem_bwd(dx2, x1, qm, kv, g_q, w_q, w_o, w_out, tm, rides=()):
    t_len = x1.shape[0]
    m_len = kv.shape[1]

    def body(dx2_ref, x1_ref, qm_ref, kv_ref, g_ref, wq_ref, wo_ref, wout_ref,
             dx1_ref, dqm_ref, dyc_ref, dya_ref, dkv_ref, dgq_ref, dbout_ref):
        i = pl.program_id(0)

        @pl.when(i == 0)
        def _():
            dkv_ref[...] = jnp.zeros_like(dkv_ref)
            dgq_ref[...] = jnp.zeros_like(dgq_ref)
            dbout_ref[...] = jnp.zeros_like(dbout_ref)

        dx2 = dx2_ref[...]
        dom = _dot_nt(_bf(dx2), wo_ref[...])
        dqs = []
        for h in range(MEM_HEADS):
            kh, vh = _mem_heads(kv_ref, h)
            qh = qm_ref[:, MEM_HD * h:MEM_HD * (h + 1)]
            s = _dot_nt(qh, kh) * MEM_SCALE
            p = jnp.exp(s - jnp.max(s, axis=-1, keepdims=True))
            p = p * (1.0 / jnp.sum(p, axis=-1, keepdims=True))
            domh = _bf(dom[:, MEM_HD * h:MEM_HD * (h + 1)])
            dp = _dot_nt(domh, vh)
            ds = _bf(p * (dp - jnp.sum(p * dp, axis=-1, keepdims=True)) * MEM_SCALE)
            dqs.append(_dot(ds, kh))
            lo = MEM_HD * (h % 2)
            dkv_ref[h // 2, :, lo:lo + MEM_HD] += _dot_tn(ds, qh)
            dkv_ref[2 + h // 2, :, lo:lo + MEM_HD] += _dot_tn(_bf(p), domh)
        dqm = _bf(jnp.concatenate(dqs, axis=1))
        dqm_ref[...] = dqm
        dhq = _dot_nt(dqm, wq_ref[...])
        g = g_ref[...]
        _, xh, r = _rms_fwd(x1_ref[...], g)
        dx1 = dx2 + _rms_bwd(dhq, xh, r, g)
        dx1_ref[...] = dx1
        dgq_ref[...] += _colsum(dhq * xh)
        dbout_ref[...] += _colsum(dx1)
        dymix = _dot_nt(_bf(dx1), wout_ref[...])
        dyc_ref[...] = dymix[:, 0:CONV_CH]
        dya_ref[...] = _bf(dymix[:, CONV_CH:2 * CONV_CH])

    vec = pl.BlockSpec((1, D_MODEL), lambda i: (0, 0))
    return _call(
        body, rides=rides, name="mix_mem_bwd", grid=(t_len // tm,),
        in_specs=[_rows(tm, D_MODEL), _rows(tm, D_MODEL), _rows(tm, D_MODEL), _res(kv.shape), _res((1, D_MODEL)),
                  _res(w_q.shape), _res(w_o.shape), _res(w_out.shape)],
        out_specs=[_rows(tm, D_MODEL), _rows(tm, D_MODEL), _rows(tm, CONV_CH), _rows(tm, CONV_CH),
                   pl.BlockSpec(kv.shape, lambda i: (0, 0, 0)), vec, vec],
        out_shape=(jax.ShapeDtypeStruct((t_len, D_MODEL), F32), jax.ShapeDtypeStruct((t_len, D_MODEL), BF16),
                   jax.ShapeDtypeStruct((t_len, CONV_CH), F32), jax.ShapeDtypeStruct((t_len, CONV_CH), BF16),
                   jax.ShapeDtypeStruct((N_CHIPS, m_len, kv.shape[2]), F32),
                   jax.ShapeDtypeStruct((1, D_MODEL), F32), jax.ShapeDtypeStruct((1, D_MODEL), F32)),
        compiler_params=_cp(1),
    )(dx2, x1, qm, kv, g_q, w_q, w_o, w_out)


def _mem_kv_bwd(dkv, memn, mem, g_kv, w_kv):
    m_len = mem.shape[0]

    def body(dkv_ref, memn_ref, mem_ref, g_ref, w_ref, dw_ref, dg_ref):
        hb = memn_ref[...]
        dmn = jnp.zeros((m_len, D_MODEL), F32)
        for s in range(N_CHIPS):
            d = _bf(dkv_ref[s])
            dw_ref[s] = _bf(_dot_tn(hb, d))
            dmn = dmn + _dot_nt(d, w_ref[s])
        _, xh, _ = _rms_fwd(mem_ref[...], g_ref[...])
        dg_ref[...] = _colsum(dmn * xh)

    return _call(
        body, name="mem_kv_bwd",
        out_shape=(jax.ShapeDtypeStruct(w_kv.shape, BF16), jax.ShapeDtypeStruct((1, D_MODEL), F32)),
        compiler_params=pltpu.CompilerParams(vmem_limit_bytes=VMEM_LIMIT_BYTES),
    )(dkv, memn, mem, g_kv, w_kv)


def _attn_bwd_q(q, kd, vd, dya, lse, sink_b, bias, cos_t, sin_t, rides=()):
    t_len = q.shape[0]
    nb = t_len // BLK

    def body(q_ref, kp_ref, kc_ref, kn_ref, vp_ref, vc_ref, vn_ref, do_ref, lse_ref, sk_ref, bias_ref, c_ref, s_ref,
             dq_ref, dd_ref, dsk_ref):
        kcat = jnp.concatenate([kp_ref[...], kc_ref[...], kn_ref[...]], axis=0)
        vcat = jnp.concatenate([vp_ref[...], vc_ref[...], vn_ref[...]], axis=0)
        dqs, dsks = [], []
        for g in range(2):
            qs = _stack_heads(q_ref[:, 256 * g:256 * g + 256])
            dos = _stack_heads(do_ref[:, 256 * g:256 * g + 256])
            kk = kcat[:, 128 * g:128 * g + 128]
            s = _dot_nt(qs, kk) * ATT_SCALE + bias_ref[0]
            lse_b = lse_ref[0, g]
            p = jnp.exp(s - _tile3(lse_b))
            dp = _dot_nt(dos, vcat[:, 128 * g:128 * g + 128])
            drow = jnp.sum(p * dp, axis=-1, keepdims=True)
            ds = _bf(p * (dp - drow) * ATT_SCALE)
            dqs.append(_unstack_heads(_dot(ds, kk)))
            d_b = jnp.broadcast_to(drow, (4 * BLK, 128))
            dd_ref[0, g] = d_b
            contrib = -(jnp.exp(_sink_col(sk_ref, g) - lse_b) * d_b)
            dsks.append(jnp.sum(contrib.reshape(4, BLK, 128), axis=1))
        dq = jnp.concatenate(dqs, axis=1)
        dq_ref[...] = _bf(_rope(dq, c_ref[...], s_ref[...], -1.0))
        dsk_ref[0] = jnp.concatenate(dsks, axis=0)

    stat = pl.BlockSpec((1, 2, 4 * BLK, 128), lambda i: (i, 0, 0, 0))
    return _call(
        body, rides=rides, name="attn_bwd_q", grid=(nb,),
        in_specs=[_rows(BLK, 512)] + _nbr_specs(256, nb) + _nbr_specs(256, nb)
        + [_rows(BLK, 512), stat, _res((8, 128)), _edge_spec((4 * BLK, 3 * BLK), nb), _rows(BLK, 128), _rows(BLK, 128)],
        out_specs=[_rows(BLK, 512), stat, pl.BlockSpec((1, 8, 128), lambda i: (i, 0, 0))],
        out_shape=(jax.ShapeDtypeStruct((t_len, 512), BF16), jax.ShapeDtypeStruct((nb, 2, 4 * BLK, 128), F32),
                   jax.ShapeDtypeStruct((nb, 8, 128), F32)),
        compiler_params=_cp(1),
    )(q, kd, kd, kd, vd, vd, vd, dya, lse, sink_b, bias, cos_t, sin_t)


def _attn_bwd_kv(q, kd, vd, dya, lse, dd, bias, cos_t, sin_t, rides=()):
    t_len = q.shape[0]
    nb = t_len // BLK

    def body(kc_ref, vc_ref, qp_ref, qc_ref, qn_ref, dop_ref, doc_ref, don_ref, lp_ref, lc_ref, ln_ref,
             dp_ref, dc_ref, dn_ref, bias_ref, c_ref, s_ref, dk_ref, dv_ref):
        dks, dvs = [], []
        for g in range(2):
            cols = slice(256 * g, 256 * g + 256)
            qs = jnp.concatenate([_stack_heads(r[:, cols]) for r in (qp_ref, qc_ref, qn_ref)], axis=0)
            dos = jnp.concatenate([_stack_heads(r[:, cols]) for r in (dop_ref, doc_ref, don_ref)], axis=0)
            lse_b = jnp.concatenate([r[0, g] for r in (lp_ref, lc_ref, ln_ref)], axis=0)
            d_b = jnp.concatenate([r[0, g] for r in (dp_ref, dc_ref, dn_ref)], axis=0)
            kk = kc_ref[:, 128 * g:128 * g + 128]
            s = _dot_nt(qs, kk) * ATT_SCALE + bias_ref[0]
            p = jnp.exp(s - lse_b)
            dp = _dot_nt(dos, vc_ref[:, 128 * g:128 * g + 128])
            ds = _bf(p * (dp - d_b) * ATT_SCALE)
            dvs.append(_dot_tn(_bf(p), dos))
            dks.append(_dot_tn(ds, qs))
        dk_ref[...] = _bf(_rope(_fold_heads(dks), c_ref[...], s_ref[...], -1.0))
        dv_ref[...] = _bf(_fold_heads(dvs))

    return _call(
        body, rides=rides, name="attn_bwd_kv", grid=(nb,),
        in_specs=[_rows(BLK, 256), _rows(BLK, 256)] + _nbr_specs(512, nb) + _nbr_specs(512, nb) + _nbr_specs4(nb) + _nbr_specs4(nb)
        + [_edge_spec((12 * BLK, BLK), nb), _rows(BLK, 128), _rows(BLK, 128)],
        out_specs=[_rows(BLK, 128), _rows(BLK, 128)],
        out_shape=(jax.ShapeDtypeStruct((t_len, 128), BF16), jax.ShapeDtypeStruct((t_len, 128), BF16)),
        compiler_params=_cp(1),
    )(kd, vd, q, q, q, dya, dya, dya, lse, lse, lse, dd, dd, dd, bias, cos_t, sin_t)


def _conv_norm_bwd(pre, dyc, g_ln, b_ln, tc):
    t_len = pre.shape[0]

    def body(pre_ref, dy_ref, g_ref, b_ref, dpre_ref, stats_ref):
        i = pl.program_id(0)
        pre_v = pre_ref[...]
        mu = jnp.mean(pre_v, axis=-1, keepdims=True)
        d = pre_v - mu
        rstd = lax.rsqrt(jnp.mean(d * d, axis=-1, keepdims=True) + EPS)
        xh = d * rstd
        g = g_ref[...]
        ln = xh * g + b_ref[...]
        sg = _sigmoid(ln)
        dln = dy_ref[...] * (sg * (1.0 + ln * (1.0 - sg)))
        dxh = dln * g
        dpre = rstd * (dxh - jnp.mean(dxh, axis=-1, keepdims=True) - xh * jnp.mean(dxh * xh, axis=-1, keepdims=True))
        dpre_ref[...] = dpre

        @pl.when(i == 0)
        def _():
            stats_ref[...] = jnp.zeros_like(stats_ref)

        stats_ref[0:1, :] += _colsum(dln * xh)
        stats_ref[1:2, :] += _colsum(dln)
        stats_ref[2:3, :] += _colsum(dpre)

    return _call(
        body, name="conv_norm_bwd", grid=(t_len // tc,),
        in_specs=[_rows(tc, CONV_CH), _rows(tc, CONV_CH), _res((1, CONV_CH)), _res((1, CONV_CH))],
        out_specs=[_rows(tc, CONV_CH), pl.BlockSpec((8, CONV_CH), lambda i: (0, 0))],
        out_shape=(jax.ShapeDtypeStruct((t_len, CONV_CH), F32), jax.ShapeDtypeStruct((8, CONV_CH), F32)),
        compiler_params=_cp(1),
    )(pre, dyc, g_ln, b_ln)


def _conv_bwd(dpre, ug, w_dw, tc, rides=()):
    t_len = ug.shape[0]
    n_tiles = t_len // tc

    def body(dp_ref, dm_ref, dn_ref, up_ref, um_ref, un_ref, w_ref, du_ref, dw_ref, dbuf, vbuf, dshift, vshift):
        i = pl.program_id(0)
        _fill_halo_buf(dbuf, dp_ref[...], dm_ref[...], dn_ref[...], i, n_tiles, tc)
        _fill_halo_buf(vbuf, _glu(up_ref[...]), _glu(um_ref[...]), _glu(un_ref[...]), i, n_tiles, tc)
        _shift_copies(dbuf, dshift, tc)
        _shift_copies(vbuf, vshift, tc)

        @pl.when(i == 0)
        def _():
            dw_ref[...] = jnp.zeros_like(dw_ref)

        def chunk(c, carry):
            base = c * CONV_ROWS
            rows = pl.ds(pl.multiple_of(base, CONV_ROWS), CONV_ROWS)
            dmain = dm_ref[rows, :]
            dv = jnp.zeros((CONV_ROWS, CONV_CH), F32)
            for k in range(CONV_W):
                dv = dv + w_ref[k:k + 1, :] * _shifted_rows(dbuf, dshift, 31 - k, base)
                prod = dmain * _shifted_rows(vbuf, vshift, k + 1, base)
                dw_ref[8 * k:8 * k + 8, :] += jnp.sum(prod.reshape(CONV_ROWS // 8, 8, CONV_CH), axis=0)
            um = um_ref[rows, :]
            a, gt = um[:, 0:CONV_CH], um[:, CONV_CH:2 * CONV_CH]
            sg = _sigmoid(gt)
            du_ref[rows, :] = _bf(jnp.concatenate([dv * sg, dv * a * (sg * (1.0 - sg))], axis=1))
            return carry

        lax.fori_loop(0, tc // CONV_ROWS, chunk, 0)

    shifts = pltpu.VMEM((7, tc + 24, CONV_CH), F32)
    return _call(
        body, rides=rides, name="conv_bwd", grid=(n_tiles,),
        in_specs=_halo_specs(tc, CONV_CH, t_len) + _halo_specs(tc, 1024, t_len) + [_res((32, CONV_CH))],
        out_specs=[_rows(tc, 1024), pl.BlockSpec((8 * 32, CONV_CH), lambda i: (0, 0))],
        out_shape=(jax.ShapeDtypeStruct((t_len, 1024), BF16), jax.ShapeDtypeStruct((8 * 32, CONV_CH), F32)),
        scratch_shapes=[pltpu.VMEM((tc + 32, CONV_CH), F32), pltpu.VMEM((tc + 32, CONV_CH), F32), shifts, shifts],
        compiler_params=_cp(1),
    )(dpre, dpre, dpre, ug, ug, ug, w_dw)


def _in_proj_bwd(du_glu, dq, dk, dv, dx1, x, g_mix, w_t, tm, rides=()):
    t_len = x.shape[0]
    n_ext = w_t.shape[0]

    def body(dg_ref, dq_ref, dk_ref, dv_ref, dx1_ref, x_ref, g_ref, w_ref, dx_ref, du_ref, db_ref, dgm_ref):
        i = pl.program_id(0)
        du = jnp.concatenate([dg_ref[...], dq_ref[...], dk_ref[...], dv_ref[...]], axis=1)
        du_ref[...] = du
        dh = _dot(du, w_ref[...])
        g = g_ref[...]
        _, xh, r = _rms_fwd(x_ref[...], g)
        dx_ref[...] = dx1_ref[...] + _rms_bwd(dh, xh, r, g)

        @pl.when(i == 0)
        def _():
            db_ref[...] = jnp.zeros_like(db_ref)
            dgm_ref[...] = jnp.zeros_like(dgm_ref)

        db_ref[...] += _colsum(du.astype(F32))
        dgm_ref[...] += _colsum(dh * xh)

    return _call(
        body, rides=rides, name="in_proj_bwd", grid=(t_len // tm,),
        in_specs=[_rows(tm, 1024), _rows(tm, 512), _rows(tm, 128), _rows(tm, 128), _rows(tm, D_MODEL), _rows(tm, D_MODEL),
                  _res((1, D_MODEL)), _res(w_t.shape)],
        out_specs=[_rows(tm, D_MODEL), _rows(tm, n_ext), pl.BlockSpec((1, n_ext), lambda i: (0, 0)),
                   pl.BlockSpec((1, D_MODEL), lambda i: (0, 0))],
        out_shape=(jax.ShapeDtypeStruct((t_len, D_MODEL), F32), jax.ShapeDtypeStruct((t_len, n_ext), BF16),
                   jax.ShapeDtypeStruct((1, n_ext), F32), jax.ShapeDtypeStruct((1, D_MODEL), F32)),
        compiler_params=_cp(1),
    )(du_glu, dq, dk, dv, dx1, x, g_mix, w_t)


def _weight_grad(a, d, name, tt, rides=()):
    t_len, k_dim = a.shape
    n_dim = d.shape[1]
    tk = k_dim if k_dim <= 1792 else k_dim // 2
    assert k_dim % tk == 0 and tk % 128 == 0 and n_dim % 128 == 0
    tt = min(tt, t_len)
    n_t = t_len // tt

    def body(a_ref, d_ref, o_ref, acc):
        t = pl.program_id(1)

        @pl.when(t == 0)
        def _():
            acc[...] = jnp.zeros_like(acc)

        acc[...] += _dot_tn(_bf(a_ref[...]), _bf(d_ref[...]))

        @pl.when(t == n_t - 1)
        def _():
            o_ref[...] = _bf(acc[...])

    res = _call(
        body, rides=rides, name=name, grid=(k_dim // tk, n_t),
        in_specs=[pl.BlockSpec((tt, tk), lambda k, t: (t, k)), pl.BlockSpec((tt, n_dim), lambda k, t: (t, 0))],
        out_specs=[pl.BlockSpec((tk, n_dim), lambda k, t: (k, 0))],
        out_shape=[jax.ShapeDtypeStruct((k_dim, n_dim), BF16)],
        scratch_shapes=[pltpu.VMEM((tk, n_dim), F32)],
        compiler_params=_cp(2),
    )(a, d)
    return res if rides else res[0]


ANY = pl.BlockSpec(memory_space=pl.ANY)


def _place():
    x, y, c = lax.axis_index("x"), lax.axis_index("y"), lax.axis_index("c")
    chips = [(1 - x, y), (x, 1 - y), (1 - x, 1 - y)]
    return x, y, c, chips


def _remote(src, dst, send_sems, recv_sems, k, to):
    return pltpu.make_async_remote_copy(src_ref=src, dst_ref=dst, send_sem=send_sems.at[k], recv_sem=recv_sems.at[k],
                                        device_id=to, device_id_type=MESH)


def _by_shape(names, arrays):
    groups = {}
    for k in names:
        groups.setdefault(arrays[k].shape, []).append(k)
    return list(groups.values())


def _cast_place(ws, chip_idx, tr):
    n = len(ws)
    rows, cols = ws[0].shape
    h = rows // 2
    tr = _div_tile(h, tr)
    per = h // tr

    def body(s_ref, *refs):
        for w_ref, o_ref in zip(refs[:n], refs[n:]):
            o_ref[0, 0] = _bf(w_ref[...])

    return _call(
        body, name="cast_place",
        grid_spec=pltpu.PrefetchScalarGridSpec(
            num_scalar_prefetch=1, grid=(2, per),
            in_specs=[pl.BlockSpec((tr, cols), lambda hh, r, s_ref: (hh * per + r, 0))] * n,
            out_specs=[pl.BlockSpec((1, 1, tr, cols), lambda hh, r, s_ref: (s_ref[0], hh, r, 0))] * n),
        out_shape=[jax.ShapeDtypeStruct((N_CHIPS, 2, h, cols), BF16)] * n,
        compiler_params=_cp(2),
    )(chip_idx, *ws)


def _same(arrays):
    return [jax.ShapeDtypeStruct(a.shape, a.dtype) for a in arrays]


def _gather_ride(bufs):
    n = len(bufs)

    def first_hop(outs, send, recv):
        x, y, c, chips = _place()
        mine = [outs[i].at[2 * x + y, c] for i in range(n)]
        return [_remote(mine[i], mine[i], send, recv, 3 * i + j, (cx, cy, c)) for i in range(n) for j, (cx, cy) in enumerate(chips)]

    def start(ins, outs, send, recv):
        for cp in first_hop(outs, send, recv):
            cp.start()

    def finish(ins, outs, send, recv):
        x, y, c, chips = _place()
        sib = (x, y, 1 - c)
        onward = []
        for i in range(n):
            for j, (cx, cy) in enumerate(chips):
                slab = outs[i].at[2 * cx + cy, c]
                _remote(slab, slab, send, recv, 3 * i + j, sib).wait_recv()
                onward.append(_remote(slab, slab, send, recv, 3 * n + 3 * i + j, sib))
                onward[-1].start()
        for i in range(n):
            for j, (cx, cy) in enumerate(chips):
                other = outs[i].at[2 * cx + cy, 1 - c]
                _remote(other, other, send, recv, 3 * n + 3 * i + j, sib).wait_recv()
        for cp in first_hop(outs, send, recv) + onward:
            cp.wait_send()

    return _Ride(bufs, _same(bufs), 6 * n, start, finish, aliases={i: i for i in range(n)})


def _spread_ride(buf):
    def sends(outs, send, recv):
        x, y, c, chips = _place()
        mine = outs[0].at[2 * x + y]
        return [_remote(mine, mine, send, recv, j, (cx, cy, c)) for j, (cx, cy) in enumerate(chips)]

    def start(ins, outs, send, recv):
        for cp in sends(outs, send, recv):
            cp.start()

    def finish(ins, outs, send, recv):
        _, _, c, chips = _place()
        for j, (cx, cy) in enumerate(chips):
            slab = outs[0].at[2 * cx + cy]
            _remote(slab, slab, send, recv, j, (cx, cy, c)).wait_recv()
        for cp in sends(outs, send, recv):
            cp.wait_send()

    return _Ride([buf], _same([buf]), 3, start, finish, aliases={0: 0})


def _allgather_ride(buf):
    def peers():
        x, y, c, _ = _place()
        return [(x ^ ((k >> 2) & 1), y ^ ((k >> 1) & 1), c ^ (k & 1)) for k in range(1, N_DEV)], 4 * x + 2 * y + c

    def sends(outs, send, recv):
        to, me = peers()
        mine = outs[0].at[me]
        return [_remote(mine, mine, send, recv, k, p) for k, p in enumerate(to)]

    def start(ins, outs, send, recv):
        for cp in sends(outs, send, recv):
            cp.start()

    def finish(ins, outs, send, recv):
        for k, (px, py, pc) in enumerate(peers()[0]):
            slab = outs[0].at[4 * px + 2 * py + pc]
            _remote(slab, slab, send, recv, k, (px, py, pc)).wait_recv()
        for cp in sends(outs, send, recv):
            cp.wait_send()

    return _Ride([buf], _same([buf]), N_DEV - 1, start, finish, aliases={0: 0})


def _sum_slabs(buf):
    def body(b_ref, o_ref):
        acc = b_ref[0]
        for d in range(1, buf.shape[0]):
            acc = acc + b_ref[d]
        o_ref[...] = acc

    return _call(body, name="sum_slabs", out_shape=jax.ShapeDtypeStruct(buf.shape[1:], buf.dtype))(buf)


def _pairwise_ride(arrays, out_shape, n_sem, copies):
    def start(ins, outs, send, recv):
        for cp in copies(ins, outs, send, recv):
            cp.start()

    def finish(ins, outs, send, recv):
        for cp in copies(ins, outs, send, recv):
            cp.wait()

    return _Ride(arrays, out_shape, n_sem, start, finish)


def _run_rides(name, rides):
    k_in = [len(r.operands) for r in rides]
    k_out = [len(r.out_shape) for r in rides]

    def body(*refs):
        pos, r_in, r_out = 0, [], []
        for k in k_in:
            r_in.append(refs[pos:pos + k])
            pos += k
        for k in k_out:
            r_out.append(refs[pos:pos + k])
            pos += k
        sems = refs[pos:]
        for j, r in enumerate(rides):
            r.start(r_in[j], r_out[j], sems[2 * j], sems[2 * j + 1])
        for j, r in enumerate(rides):
            r.finish(r_in[j], r_out[j], sems[2 * j], sems[2 * j + 1])

    aliases, off_in, off_out = {}, 0, 0
    for r, ki, ko in zip(rides, k_in, k_out):
        aliases.update({off_in + a: off_out + b for a, b in r.aliases.items()})
        off_in, off_out = off_in + ki, off_out + ko
    res = _call(
        body, name=name, in_specs=[ANY] * sum(k_in), out_specs=[ANY] * sum(k_out),
        out_shape=[s for r in rides for s in r.out_shape], input_output_aliases=aliases,
        scratch_shapes=[pltpu.SemaphoreType.DMA((r.n_sem,)) for r in rides for _ in range(2)],
    )(*[op for r in rides for op in r.operands])
    out, pos = [], 0
    for k in k_out:
        out.append(list(res[pos:pos + k]))
        pos += k
    return out


def _swap_ride(grads):
    def copies(ins, outs, send, recv):
        x, y, c, _ = _place()
        return [_remote(ins[i].at[:, 1 - c], outs[i], send, recv, i, (x, y, 1 - c)) for i in range(len(grads))]

    out_shape = [jax.ShapeDtypeStruct((g.shape[0],) + g.shape[2:], g.dtype) for g in grads]
    return _pairwise_ride(grads, out_shape, len(grads), copies)


def _pair_sum(grads, others, c_idx, tr):
    n = len(grads)
    n_s, _, h, cols = grads[0].shape
    tr = _div_tile(h, tr)

    def body(c_ref, *refs):
        for a_ref, b_ref, o_ref in zip(refs[:n], refs[n:2 * n], refs[2 * n:]):
            o_ref[...] = _bf(a_ref[0].astype(F32) + b_ref[...].astype(F32))

    return _call(
        body, name="pair_sum",
        grid_spec=pltpu.PrefetchScalarGridSpec(
            num_scalar_prefetch=1, grid=(n_s, h // tr),
            in_specs=[pl.BlockSpec((1, 1, tr, cols), lambda s, r, c_ref: (s, c_ref[0], r, 0))] * n
            + [pl.BlockSpec((1, tr, cols), lambda s, r, c_ref: (s, r, 0))] * n,
            out_specs=[pl.BlockSpec((1, tr, cols), lambda s, r, c_ref: (s, r, 0))] * n),
        out_shape=[jax.ShapeDtypeStruct((n_s, h, cols), BF16)] * n,
        compiler_params=_cp(2),
    )(c_idx, *grads, *others)


def _exchange_ride(sums):
    def copies(ins, outs, send, recv):
        _, _, c, chips = _place()
        return [_remote(ins[i].at[2 * cx + cy], outs[i].at[j], send, recv, 3 * i + j, (cx, cy, c))
                for i in range(len(sums)) for j, (cx, cy) in enumerate(chips)]

    out_shape = [jax.ShapeDtypeStruct((3,) + s.shape[1:], s.dtype) for s in sums]
    return _pairwise_ride(sums, out_shape, 3 * len(sums), copies)


def _chip_sum(owns, others, chip_idx, tr):
    n = len(owns)
    _, h, cols = owns[0].shape
    tr = _div_tile(h, tr)

    def body(s_ref, *refs):
        for a_ref, p_ref, o_ref in zip(refs[:n], refs[n:2 * n], refs[2 * n:]):
            acc = a_ref[0].astype(F32)
            for j in range(N_CHIPS - 1):
                acc = acc + p_ref[j].astype(F32)
            o_ref[...] = acc

    return _call(
        body, name="chip_sum",
        grid_spec=pltpu.PrefetchScalarGridSpec(
            num_scalar_prefetch=1, grid=(h // tr,),
            in_specs=[pl.BlockSpec((1, tr, cols), lambda r, s_ref: (s_ref[0], r, 0))] * n
            + [pl.BlockSpec((N_CHIPS - 1, tr, cols), lambda r, s_ref: (0, r, 0))] * n,
            out_specs=[pl.BlockSpec((tr, cols), lambda r, s_ref: (r, 0))] * n),
        out_shape=[jax.ShapeDtypeStruct((h, cols), F32)] * n,
        compiler_params=_cp(1),
    )(chip_idx, *owns, *others)


def _share_ride(halves):
    def copies(ins, outs, send, recv):
        x, y, c, _ = _place()
        return [_remote(ins[i], outs[i], send, recv, i, (x, y, 1 - c)) for i in range(len(halves))]

    return _pairwise_ride(halves, _same(halves), len(halves), copies)


def _adamw_math(w, g, m, v):
    m_new = ADAM_B1 * m + (1.0 - ADAM_B1) * g
    v_new = ADAM_B2 * v + (1.0 - ADAM_B2) * (g * g)
    m_hat = m_new * (1.0 / (1.0 - ADAM_B1 ** ADAM_STEP))
    v_hat = v_new * (1.0 / (1.0 - ADAM_B2 ** ADAM_STEP))
    delta = -ADAM_LR * (m_hat / (jnp.sqrt(v_hat) + ADAM_EPS) + ADAM_WD * w)
    return delta, m_new, v_new


def _adamw(w, g_mine, g_other, m, v, core_idx, tr):
    rows, cols = w.shape
    h = rows // 2
    tr = _div_tile(h, tr)
    per = h // tr

    def body(c_ref, w_ref, ga_ref, gb_ref, m_ref, v_ref, g_ref, d_ref, mo_ref, vo_ref):
        g = jnp.where(pl.program_id(0) == c_ref[0], ga_ref[...], gb_ref[...])
        d, mn, vn = _adamw_math(w_ref[...], g, m_ref[...], v_ref[...])
        g_ref[...] = g
        d_ref[...] = d
        mo_ref[...] = mn
        vo_ref[...] = vn

    full = pl.BlockSpec((tr, cols), lambda hh, r, c_ref: (hh * per + r, 0))
    mine = pl.BlockSpec((tr, cols), lambda hh, r, c_ref: (jnp.where(hh == c_ref[0], r, 0), 0))
    other = pl.BlockSpec((tr, cols), lambda hh, r, c_ref: (jnp.where(hh == c_ref[0], 0, r), 0))
    shp = jax.ShapeDtypeStruct(w.shape, F32)
    return _call(
        body, name="adamw",
        grid_spec=pltpu.PrefetchScalarGridSpec(num_scalar_prefetch=1, grid=(2, per), in_specs=[full, mine, other, full, full],
                                               out_specs=[full] * 4),
        out_shape=(shp, shp, shp, shp), compiler_params=_cp(2))(core_idx, w, g_mine, g_other, m, v)


def _adamw_small(ws, gs, ms, vs):
    n = len(ws)

    def body(*refs):
        w_r, g_r, m_r, v_r = refs[0:n], refs[n:2 * n], refs[2 * n:3 * n], refs[3 * n:4 * n]
        d_o, m_o, v_o = refs[4 * n:5 * n], refs[5 * n:6 * n], refs[6 * n:7 * n]
        for i in range(n):
            d, mn, vn = _adamw_math(w_r[i][...], g_r[i][...], m_r[i][...], v_r[i][...])
            d_o[i][...] = d
            m_o[i][...] = mn
            v_o[i][...] = vn

    shp = [jax.ShapeDtypeStruct(w.shape, F32) for w in ws]
    outs = _call(body, name="adamw_small", out_shape=shp * 3)(*ws, *gs, *ms, *vs)
    return outs[0:n], outs[n:2 * n], outs[2 * n:3 * n]


def _rope_tables(t_len):
    pos = jnp.arange(t_len, dtype=F32)
    inv_freq = ROPE_THETA ** (-jnp.arange(0, HEAD_DIM, 2, dtype=F32) / HEAD_DIM)
    ang = pos[:, None] * inv_freq[None, :]
    cos, sin = jnp.cos(ang), jnp.sin(ang)
    return jnp.tile(jnp.concatenate([cos, cos], axis=1), (1, 2)), jnp.tile(jnp.concatenate([-sin, sin], axis=1), (1, 2))


def _dup_heads(a):
    h0, h1 = a[..., 0:64], a[..., 64:128]
    return jnp.concatenate([h0, h0, h1, h1], axis=-1)


def _local_step(x, mem, target, small, wg, comm, tm_a=512, tm_b=256, tc=512, tt=1024):
    def run(stage, fn, n_own, *operands):
        rides = comm.rides(stage)
        res = fn(*operands, rides=rides)
        res = list(res) if isinstance(res, (list, tuple)) else [res]
        brought, pos = [], n_own
        for r in rides:
            brought.append(res[pos:pos + len(r.out_shape)])
            pos += len(r.out_shape)
        comm.landed(stage, brought, wg)
        return res[:n_own]

    t_len = x.shape[0]
    cos_t, sin_t = _rope_tables(t_len)
    b_in = small["b_in"]
    b_ext = jnp.concatenate([b_in[:, 0:1536], _dup_heads(b_in[:, 1536:1664]), _dup_heads(b_in[:, 1664:1792])], axis=1)
    w_dw = jnp.concatenate([wg["w_dw"], jnp.zeros((1, CONV_CH), F32)], axis=0)
    sink_b = jnp.broadcast_to(small["attn_sink"].reshape(8, 1), (8, 128))
    bias_q, bias_k = _band_bias()

    ug, q, kd, vd, h1 = run("in_proj_fwd", _in_proj_fwd, 5, x, small["g_mix"], wg["w_in"], b_ext, cos_t, sin_t, tm_a)
    yc, pre = run("conv_fwd", _conv_fwd, 2, ug, w_dw, small["b_dw"], small["g_conv_ln"], small["b_conv_ln"], tc)
    memn, kv = _mem_kv_fwd(mem, small["g_mem_kv"], wg["w_mem_kv"])
    ya, lse = run("attn_fwd", _attn_fwd, 2, q, kd, vd, sink_b, bias_q)
    ymix, x1, hq, qm, om, x2 = run("mix_mem_fwd", _mix_mem_fwd, 6, x, yc, ya, wg["w_out"], small["b_out"], small["g_mem_q"],
                                   wg["w_mem_q"], kv, wg["w_mem_o"], tm_a)
    hf, gate, up, act = run("ffn_up", _ffn_up, 4, x2, small["g_ffn"], wg["w_gate"], wg["w_up"], tm_a)
    dx3, loss, d_g_final = _ffn_down_loss(x2, act, wg["w_down"], small["g_final"], target, tm_a)

    dx2, dgate, dup, d_g_ffn = _ffn_bwd(dx3, x2, gate, up, small["g_ffn"], wg["w_gate"], wg["w_up"], wg["w_down"], tm_b)
    comm.grad("w_gate", _weight_grad(dgate, hf, "dw_gate", tt))
    comm.grad("w_up", _weight_grad(dup, hf, "dw_up", tt))
    comm.grad("w_down", _weight_grad(act, dx3, "dw_down", tt))
    dx1, dqm, dyc, dya, dkv, d_g_mem_q, d_b_out = run("mix_mem_bwd", _mix_mem_bwd, 7, dx2, x1, qm, kv, small["g_mem_q"],
                                                      wg["w_mem_q"], wg["w_mem_o"], wg["w_out"], tm_a)
    d_w_mem_kv, d_g_mem_kv = _mem_kv_bwd(dkv, memn, mem, small["g_mem_kv"], wg["w_mem_kv"])
    comm.grad("w_mem_kv", d_w_mem_kv)
    comm.grad("w_out", _weight_grad(ymix, dx1, "dw_out", tt))
    comm.grad("w_mem_q", _weight_grad(hq, dqm, "dw_mem_q", tt))
    comm.grad("w_mem_o", _weight_grad(om, dx2, "dw_mem_o", tt))
    dq, dd, dsink = run("attn_bwd_q", _attn_bwd_q, 3, q, kd, vd, dya, lse, sink_b, bias_q, cos_t, sin_t)
    dk, dv = run("attn_bwd_kv", _attn_bwd_kv, 2, q, kd, vd, dya, lse, dd, bias_k, cos_t, sin_t)
    dpre, cstats = _conv_norm_bwd(pre, dyc, small["g_conv_ln"], small["b_conv_ln"], tc)
    du_glu, d_w_dw = run("conv_bwd", _conv_bwd, 2, dpre, ug, w_dw, tc)
    grad_x, du, d_b_in, d_g_mix = run("in_proj_bwd", _in_proj_bwd, 4, du_glu, dq, dk, dv, dx1, x, small["g_mix"], wg["w_in"],
                                      tm_a)
    grads = {
        "w_dw": jnp.sum(d_w_dw.reshape(32, 8, CONV_CH), axis=1)[0:CONV_W],
        "g_mix": d_g_mix, "b_in": d_b_in, "b_dw": cstats[2:3], "g_conv_ln": cstats[0:1],
        "b_conv_ln": cstats[1:2], "attn_sink": jnp.sum(dsink[:, :, 0], axis=0)[None, :], "b_out": d_b_out,
        "g_mem_q": d_g_mem_q, "g_mem_kv": d_g_mem_kv, "g_ffn": d_g_ffn, "g_final": d_g_final,
    }
    comm.small(loss[0:1, 0:1], grads)
    (d_w_in,) = run("dw_in", _weight_grad, 1, du, h1, "dw_in", tt)
    comm.grad("w_in", d_w_in)
    return loss[0:1, 0:1], grad_x, grads


BIG = ["w_in", "w_out", "w_mem_q", "w_mem_kv", "w_mem_o", "w_gate", "w_up", "w_down"]
KEEP_SLABS = ("w_mem_kv",)
TRANSPOSED = ("w_in", "w_gate", "w_up")
SMALL = ["g_mix", "b_in", "b_dw", "g_conv_ln", "b_conv_ln", "attn_sink", "b_out", "g_mem_q", "g_mem_kv", "g_ffn", "g_final"]
PACK_ROWS = 32
ROWS_PER_STEP = 512

GATHER_ON = {"in_proj_fwd": ("w_out", "w_mem_q"), "conv_fwd": ("w_mem_kv", "w_mem_o"), "attn_fwd": ("w_gate",),
             "mix_mem_fwd": ("w_up",), "ffn_up": ("w_down",)}
FFN_GROUP = ("w_gate", "w_up", "w_down")
MID_GROUP = ("w_mem_kv", "w_out", "w_mem_q", "w_mem_o")
SWAP_ON = {"mix_mem_bwd": FFN_GROUP, "attn_bwd_q": MID_GROUP}
EXCHANGE_ON = {"attn_bwd_q": ("w_gate",), "attn_bwd_kv": ("w_up", "w_mem_kv"), "conv_bwd": ("w_down", "w_out"),
               "dw_in": ("w_mem_q", "w_mem_o")}
SMALL_ON = "dw_in"


def _as_weight(name, gathered):
    g = gathered.reshape(N_CHIPS, gathered.shape[2] * 2, gathered.shape[3])
    return g if name in KEEP_SLABS else g.reshape(-1, g.shape[2])


class _Overlap:
    def __init__(self, bufs, chip_idx, core_idx):
        self.bufs, self.chip_idx, self.core_idx = bufs, chip_idx, core_idx
        self.parts, self.sums, self.others = {}, {}, {}

    def rides(self, stage):
        rides = []
        if stage in GATHER_ON:
            rides.append(_gather_ride([self.bufs[k] for k in GATHER_ON[stage]]))
        if stage in EXCHANGE_ON:
            rides.append(_exchange_ride([self.sums[k] for k in EXCHANGE_ON[stage]]))
        if stage in SWAP_ON:
            rides.append(_swap_ride([self.parts[k] for k in SWAP_ON[stage]]))
        if stage == SMALL_ON:
            rides.append(_allgather_ride(self.packs))
        return rides

    def landed(self, stage, brought, wg):
        brought = list(brought)
        if stage in GATHER_ON:
            for k, g in zip(GATHER_ON[stage], brought.pop(0)):
                wg[k] = _as_weight(k, g)
        if stage in EXCHANGE_ON:
            self.others.update(zip(EXCHANGE_ON[stage], brought.pop(0)))
        if stage in SWAP_ON:
            self._pair(SWAP_ON[stage], brought.pop(0))
        if stage == SMALL_ON:
            (self.packs,) = brought.pop(0)

    def small(self, loss, grads):
        x, y, c = lax.axis_index("x"), lax.axis_index("y"), lax.axis_index("c")
        pack = _pack_small(loss, grads)
        self.packs = lax.dynamic_update_slice(jnp.zeros((N_DEV,) + pack.shape, F32), pack[None], (4 * x + 2 * y + c, 0, 0))

    def grad(self, name, g):
        if g.ndim == 2:
            g = g.reshape(N_CHIPS, g.shape[0] // N_CHIPS, g.shape[1])
        self.parts[name] = g.reshape(N_CHIPS, 2, g.shape[1] // 2, g.shape[2])

    def _pair(self, names, from_sibling):
        came = dict(zip(names, from_sibling))
        for group in _by_shape(names, self.parts):
            sums = _pair_sum([self.parts[k] for k in group], [came[k] for k in group], self.core_idx, ROWS_PER_STEP)
            self.sums.update(zip(group, sums))

    def finish(self):
        (from_sibling,) = _run_rides("swap_last", [_swap_ride([self.parts["w_in"]])])
        self._pair(("w_in",), from_sibling)
        ((self.others["w_in"],),) = _run_rides("exchange_last", [_exchange_ride([self.sums["w_in"]])])
        halves = {}
        for group in _by_shape(BIG, self.sums):
            res = _chip_sum([self.sums[k] for k in group], [self.others[k] for k in group], self.chip_idx, ROWS_PER_STEP)
            halves.update(zip(group, res))
        mine = [halves[k] for k in BIG]
        (theirs,) = _run_rides("sibling_share", [_share_ride(mine)])
        return mine, theirs, _sum_slabs(self.packs)


def _pack_small(loss, grads):
    def row(a):
        a = a.reshape(1, -1)
        return jnp.pad(a, ((0, 0), (0, 1024 - a.shape[1])))

    rows = [row(grads[k]) for k in ("g_mix", "b_out", "g_mem_q", "g_mem_kv", "g_ffn", "g_final")]
    rows += [grads["b_in"][:, 0:1024], row(grads["b_in"][:, 1024:1792])]
    rows += [jnp.concatenate([grads["b_dw"], grads["g_conv_ln"]], axis=1), row(grads["b_conv_ln"]), row(grads["attn_sink"]),
             row(loss)]
    dw = jnp.pad(grads["w_dw"], ((0, 1), (0, 0))).reshape(16, 1024)
    pack = jnp.concatenate(rows + [dw], axis=0)
    return jnp.pad(pack, ((0, PACK_ROWS - pack.shape[0]), (0, 0)))


def _unpack_small(pack):
    out = {k: pack[i:i + 1] for i, k in enumerate(("g_mix", "b_out", "g_mem_q", "g_mem_kv", "g_ffn", "g_final"))}
    out["b_in"] = jnp.concatenate([pack[6:7], pack[7:8, 0:768]], axis=1)
    out["b_dw"], out["g_conv_ln"] = pack[8:9, 0:512], pack[8:9, 512:1024]
    out["b_conv_ln"] = pack[9:10, 0:512]
    out["attn_sink"] = pack[10:11, 0:8]
    loss = pack[11, 0]
    dw = pack[12:28].reshape(32, 512)[0:CONV_W]
    return loss, out, dw


def kernel(x, mem, g_mix, w_in, b_in, w_dw, b_dw, g_conv_ln, b_conv_ln, attn_sink, w_out, b_out, g_mem_q, g_mem_kv, w_mem_q, w_mem_kv, w_mem_o, g_ffn, w_gate, w_up, w_down, g_final, loss_target, m_g_mix, m_w_in, m_b_in, m_w_dw, m_b_dw, m_g_conv_ln, m_b_conv_ln, m_attn_sink, m_w_out, m_b_out, m_g_mem_q, m_g_mem_kv, m_w_mem_q, m_w_mem_kv, m_w_mem_o, m_g_ffn, m_w_gate, m_w_up, m_w_down, m_g_final, v_g_mix, v_w_in, v_b_in, v_w_dw, v_b_dw, v_g_conv_ln, v_b_conv_ln, v_attn_sink, v_w_out, v_b_out, v_g_mem_q, v_g_mem_kv, v_w_mem_q, v_w_mem_kv, v_w_mem_o, v_g_ffn, v_w_gate, v_w_up, v_w_down, v_g_final):
    args = dict(locals())
    weight_names = ["g_mix", "w_in", "b_in", "w_dw", "b_dw", "g_conv_ln", "b_conv_ln", "attn_sink", "w_out", "b_out", "g_mem_q",
                    "g_mem_kv", "w_mem_q", "w_mem_kv", "w_mem_o", "g_ffn", "w_gate", "w_up", "w_down", "g_final"]
    chip = 2 * lax.axis_index("x") + lax.axis_index("y")
    core = lax.axis_index("c")

    chip_idx = chip.astype(jnp.int32).reshape(1)
    core_idx = core.astype(jnp.int32).reshape(1)

    def block(name):
        a = args[name][0]
        weight = name[2:] if name[:2] in ("m_", "v_") else name
        return a.T if weight in TRANSPOSED else a

    blocks = {k: block(k) for k in BIG}
    bufs = {}
    for group in _by_shape(BIG, blocks):
        bufs.update(zip(group, _cast_place([blocks[k] for k in group], chip_idx, ROWS_PER_STEP)))
    comm = _Overlap(bufs, chip_idx, core_idx)
    dw_buf = lax.dynamic_update_slice(jnp.zeros((N_CHIPS, CONV_W, 128), F32), w_dw, (chip, 0, 0))
    (first,), (dw_all,) = _run_rides("gather_first", [_gather_ride([comm.bufs["w_in"]]), _spread_ride(dw_buf)])
    wg = {"w_in": _as_weight("w_in", first), "w_dw": jnp.transpose(dw_all, (1, 0, 2)).reshape(CONV_W, CONV_CH)}
    small = {k: args[k].reshape(1, -1) for k in SMALL}

    loss, grad_x, grads = _local_step(x[0], mem[0], loss_target[0], small, wg, comm)

    halves, other_halves, pack_sum = comm.finish()

    loss_sum, small_grads, dw_full = _unpack_small(pack_sum)
    dw_cols = jnp.transpose(dw_full.reshape(CONV_W, N_CHIPS, 128), (1, 0, 2))
    small_grads["w_dw"] = lax.dynamic_index_in_dim(dw_cols, chip, axis=0, keepdims=False)

    out_g, out_d, out_m, out_v = {}, {}, {}, {}
    for k, g_mine, g_other in zip(BIG, halves, other_halves):
        res = _adamw(block(k), g_mine, g_other, block("m_" + k), block("v_" + k), core_idx, ROWS_PER_STEP)
        out_g[k], out_d[k], out_m[k], out_v[k] = [(r.T if k in TRANSPOSED else r)[None] for r in res]
    names = SMALL + ["w_dw"]

    def flat(a):
        return a[0] if a.ndim == 3 else a.reshape(1, -1)

    def pad_lanes(a):
        return jnp.pad(a, ((0, 0), (0, 128 - a.shape[1]))) if a.shape[1] < 128 else a

    ws = [flat(args[k]) for k in names]
    gs = [small_grads[k] for k in names]
    ms = [flat(args["m_" + k]) for k in names]
    vs = [flat(args["v_" + k]) for k in names]
    ds, mns, vns = _adamw_small([pad_lanes(a) for a in ws], [pad_lanes(a) for a in gs], [pad_lanes(a) for a in ms],
                                [pad_lanes(a) for a in vs])
    for i, k in enumerate(names):
        n_lanes = ws[i].shape[1]
        for out, val in ((out_g, gs[i]), (out_d, ds[i]), (out_m, mns[i]), (out_v, vns[i])):
            out[k] = val[:, 0:n_lanes].reshape(args[k].shape)

    return (loss_sum, grad_x[None], *[out_g[k] for k in weight_names], *[out_d[k] for k in weight_names],
            *[out_m[k] for k in weight_names], *[out_v[k] for k in weight_names])
```

```python
import jax
import jax.numpy as jnp
import numpy as np
from jax import lax
from jax.experimental import pallas as pl
from jax.experimental.pallas import tpu as pltpu

F32 = jnp.float32
BF16 = jnp.bfloat16
EPS = 1e-6
NEG = -1e30

D_MODEL = 1024
CONV_CH = 512
CONV_W = 31
HEAD_DIM = 64
BLK = 128
MEM_HEADS = 4
MEM_HD = 256
N_CHIPS = 4
N_DEV = 8
ATT_SCALE = HEAD_DIM ** -0.5
MEM_SCALE = MEM_HD ** -0.5
ROPE_THETA = 10000.0

ADAM_LR = 0.001
ADAM_B1 = 0.9
ADAM_B2 = 0.999
ADAM_EPS = 1e-08
ADAM_WD = 0.01
ADAM_STEP = 10

VMEM_LIMIT_BYTES = 56 * 1024 * 1024
MESH = pl.DeviceIdType.MESH


class _Ride:
    def __init__(self, operands, out_shape, n_sem, start, finish, aliases=None):
        self.operands, self.out_shape, self.n_sem = list(operands), list(out_shape), n_sem
        self.start, self.finish, self.aliases = start, finish, dict(aliases or {})


def _call(body, rides=(), **kw):
    if not rides:
        return pl.pallas_call(body, **kw)
    grid = kw["grid"]
    n_in, n_out = len(kw["in_specs"]), len(kw["out_specs"])
    scratch = list(kw.get("scratch_shapes", ()))
    k_in = [len(r.operands) for r in rides]
    k_out = [len(r.out_shape) for r in rides]

    def carried(*refs):
        pos = n_in
        r_in, r_out = [], []
        for k in k_in:
            r_in.append(refs[pos:pos + k])
            pos += k
        own_out = refs[pos:pos + n_out]
        pos += n_out
        for k in k_out:
            r_out.append(refs[pos:pos + k])
            pos += k
        own_scratch = refs[pos:pos + len(scratch)]
        sems = refs[pos + len(scratch):]
        first = last = None
        for axis, n_steps in enumerate(grid):
            step = pl.program_id(axis)
            first = (step == 0) if first is None else first & (step == 0)
            last = (step == n_steps - 1) if last is None else last & (step == n_steps - 1)

        @pl.when(first)
        def _():
            for j, r in enumerate(rides):
                r.start(r_in[j], r_out[j], sems[2 * j], sems[2 * j + 1])

        body(*refs[:n_in], *own_out, *own_scratch)

        @pl.when(last)
        def _():
            for j, r in enumerate(rides):
                r.finish(r_in[j], r_out[j], sems[2 * j], sems[2 * j + 1])

    kw = dict(kw)
    kw["in_specs"] = list(kw["in_specs"]) + [ANY] * sum(k_in)
    kw["out_specs"] = list(kw["out_specs"]) + [ANY] * sum(k_out)
    kw["out_shape"] = list(kw["out_shape"]) + [s for r in rides for s in r.out_shape]
    kw["scratch_shapes"] = scratch + [pltpu.SemaphoreType.DMA((r.n_sem,)) for r in rides for _ in range(2)]
    aliases, off_in, off_out = {}, n_in, n_out
    for r, ki, ko in zip(rides, k_in, k_out):
        aliases.update({off_in + a: off_out + b for a, b in r.aliases.items()})
        off_in, off_out = off_in + ki, off_out + ko
    if aliases:
        kw["input_output_aliases"] = aliases
    call = pl.pallas_call(carried, **kw)
    return lambda *args: call(*args, *[op for r in rides for op in r.operands])


def _cp(n_grid):
    return pltpu.CompilerParams(dimension_semantics=("arbitrary",) * n_grid, vmem_limit_bytes=VMEM_LIMIT_BYTES)


def _res(shape):
    nd = len(shape)
    return pl.BlockSpec(shape, lambda *_: (0,) * nd, pipeline_mode=pl.Buffered(1))


def _rows(tm, n):
    return pl.BlockSpec((tm, n), lambda i: (i, 0))


def _div_tile(n, target):
    best = None
    for d in range(16, min(n, target) + 1, 16):
        if n % d == 0:
            best = d
    assert best is not None, (n, target)
    return best


def _dot(a, b):
    return jnp.dot(a, b, preferred_element_type=F32)


def _dot_nt(a, b):
    return lax.dot_general(a, b, (((1,), (1,)), ((), ())), preferred_element_type=F32)


def _dot_tn(a, b):
    return lax.dot_general(a, b, (((0,), (0,)), ((), ())), preferred_element_type=F32)


def _bf(x):
    return x.astype(BF16)


def _sigmoid(x):
    return 1.0 / (1.0 + jnp.exp(-x))


def _rms_fwd(x, g):
    r = lax.rsqrt(jnp.mean(x * x, axis=-1, keepdims=True) + EPS)
    xh = x * r
    return xh * g, xh, r


def _rms_bwd(dh, xh, r, g):
    dxh = dh * g
    return r * (dxh - xh * jnp.mean(dxh * xh, axis=-1, keepdims=True))


def _colsum(x):
    return jnp.sum(x, axis=0, keepdims=True)


def _rope(x, cos, sin, sign):
    n = x.shape[1] // 128
    c = jnp.tile(cos, (1, n)) if n > 1 else cos
    s = jnp.tile(sin, (1, n)) if n > 1 else sin
    lane = lax.broadcasted_iota(jnp.int32, x.shape, 1)
    first = (lane & 63) < 32
    partner = jnp.where(first, pltpu.roll(x, x.shape[1] - 32, 1), pltpu.roll(x, 32, 1))
    return x * c + sign * (partner * s)


def _lo_lanes(shape):
    return lax.broadcasted_iota(jnp.int32, shape, 1) < 64


def _stack_heads(t):
    t0, t1 = t[:, 0:128], t[:, 128:256]
    lo = _lo_lanes(t0.shape)
    z = jnp.zeros_like(t0)
    return jnp.concatenate([jnp.where(lo, t0, z), jnp.where(lo, z, t0), jnp.where(lo, t1, z), jnp.where(lo, z, t1)], axis=0)


def _unstack_heads(o):
    lo = _lo_lanes((BLK, 128))
    return jnp.concatenate([jnp.where(lo, o[0:128], o[128:256]), jnp.where(lo, o[256:384], o[384:512])], axis=1)


def _fold_heads(parts):
    a, b = (p + pltpu.roll(p, 64, 1) for p in parts)
    return jnp.where(_lo_lanes(a.shape), a, b)


def _sink_col(sk_ref, g):
    return jnp.concatenate([jnp.broadcast_to(sk_ref[4 * g + h:4 * g + h + 1, :], (BLK, 128)) for h in range(4)], axis=0)


def _tile3(x):
    return jnp.concatenate([x, x, x], axis=1)


def _mem_kv_fwd(mem, g_kv, w_kv):
    m_len = mem.shape[0]
    cols = w_kv.shape[2]

    def body(mem_ref, g_ref, w_ref, memn_ref, kv_ref):
        h, _, _ = _rms_fwd(mem_ref[...], g_ref[...])
        hb = _bf(h)
        memn_ref[...] = hb
        for s in range(N_CHIPS):
            kv_ref[s] = _bf(_dot(hb, w_ref[s]))

    return _call(
        body, name="mem_kv_fwd",
        out_shape=(jax.ShapeDtypeStruct((m_len, D_MODEL), BF16), jax.ShapeDtypeStruct((N_CHIPS, m_len, cols), BF16)),
        compiler_params=pltpu.CompilerParams(vmem_limit_bytes=VMEM_LIMIT_BYTES),
    )(mem, g_kv, w_kv)


def _dup_head_rows(w_ref, lo):
    h0, h1 = w_ref[lo:lo + 64, :], w_ref[lo + 64:lo + 128, :]
    return jnp.concatenate([h0, h0, h1, h1], axis=0)


def _in_proj_fwd(x, g_mix, w_t, b_ext, cos_t, sin_t, tm, rides=()):
    t_len = x.shape[0]

    def body(x_ref, g_ref, w_ref, b_ref, c_ref, s_ref, ug_ref, q_ref, k_ref, v_ref, h_ref):
        h, _, _ = _rms_fwd(x_ref[...], g_ref[...])
        hb = _bf(h)
        h_ref[...] = hb
        ug_ref[...] = _dot_nt(hb, w_ref[0:1024, :]) + b_ref[:, 0:1024]
        c, s = c_ref[...], s_ref[...]
        q_ref[...] = _bf(_rope(_dot_nt(hb, w_ref[1024:1536, :]) + b_ref[:, 1024:1536], c, s, 1.0))
        k_ref[...] = _bf(_rope(_dot_nt(hb, _dup_head_rows(w_ref, 1536)) + b_ref[:, 1536:1792], c, s, 1.0))
        v_ref[...] = _bf(_dot_nt(hb, _dup_head_rows(w_ref, 1664)) + b_ref[:, 1792:2048])

    return _call(
        body, rides=rides, name="in_proj_fwd", grid=(t_len // tm,),
        in_specs=[_rows(tm, D_MODEL), _res((1, D_MODEL)), _res(w_t.shape), _res(b_ext.shape), _rows(tm, 128), _rows(tm, 128)],
        out_specs=[_rows(tm, 1024), _rows(tm, 512), _rows(tm, 256), _rows(tm, 256), _rows(tm, D_MODEL)],
        out_shape=(jax.ShapeDtypeStruct((t_len, 1024), F32), jax.ShapeDtypeStruct((t_len, 512), BF16),
                   jax.ShapeDtypeStruct((t_len, 256), BF16), jax.ShapeDtypeStruct((t_len, 256), BF16),
                   jax.ShapeDtypeStruct((t_len, D_MODEL), BF16)),
        compiler_params=_cp(1),
    )(x, g_mix, w_t, b_ext, cos_t, sin_t)


def _halo_specs(tc, n, t_len):
    per = tc // 16
    last = t_len // 16 - 1
    return [pl.BlockSpec((16, n), lambda i: (jnp.maximum(i * per - 1, 0), 0)),
            pl.BlockSpec((tc, n), lambda i: (i, 0)),
            pl.BlockSpec((16, n), lambda i: (jnp.minimum((i + 1) * per, last), 0))]


def _glu(z):
    return z[:, 0:CONV_CH] * _sigmoid(z[:, CONV_CH:2 * CONV_CH])


def _fill_halo_buf(buf, prev, main, nxt, i, n_tiles, tc):
    buf[0:16, :] = jnp.where(i > 0, prev, jnp.zeros_like(prev))
    buf[16:16 + tc, :] = main
    buf[16 + tc:32 + tc, :] = jnp.where(i < n_tiles - 1, nxt, jnp.zeros_like(nxt))


CONV_ROWS = 64


def _shift_copies(buf, shifted, tc):
    for r in range(1, 8):
        shifted[r - 1, :, :] = buf[r:r + tc + 24, :]


def _shifted_rows(buf, shifted, offset, base):
    src = buf if offset % 8 == 0 else shifted.at[offset % 8 - 1]
    return src[pl.ds(pl.multiple_of(base + 8 * (offset // 8), 8), CONV_ROWS), :]


def _conv_fwd(ug, w_dw, b_dw, g_ln, b_ln, tc, rides=()):
    t_len = ug.shape[0]
    n_tiles = t_len // tc

    def body(up_ref, um_ref, un_ref, w_ref, bdw_ref, g_ref, b_ref, y_ref, pre_ref, buf, shifted):
        i = pl.program_id(0)
        _fill_halo_buf(buf, _glu(up_ref[...]), _glu(um_ref[...]), _glu(un_ref[...]), i, n_tiles, tc)
        _shift_copies(buf, shifted, tc)

        def chunk(c, carry):
            base = c * CONV_ROWS
            acc = jnp.zeros((CONV_ROWS, CONV_CH), F32)
            for k in range(CONV_W):
                acc = acc + w_ref[k:k + 1, :] * _shifted_rows(buf, shifted, k + 1, base)
            pre_ref[pl.ds(pl.multiple_of(base, CONV_ROWS), CONV_ROWS), :] = acc + bdw_ref[...]
            return carry

        lax.fori_loop(0, tc // CONV_ROWS, chunk, 0)
        pre = pre_ref[...]
        mu = jnp.mean(pre, axis=-1, keepdims=True)
        d = pre - mu
        rstd = lax.rsqrt(jnp.mean(d * d, axis=-1, keepdims=True) + EPS)
        ln = d * rstd * g_ref[...] + b_ref[...]
        y_ref[...] = _bf(ln * _sigmoid(ln))

    return _call(
        body, rides=rides, name="conv_fwd", grid=(n_tiles,),
        in_specs=_halo_specs(tc, 1024, t_len) + [_res((32, CONV_CH)), _res((1, CONV_CH)), _res((1, CONV_CH)), _res((1, CONV_CH))],
        out_specs=[_rows(tc, CONV_CH), _rows(tc, CONV_CH)],
        out_shape=(jax.ShapeDtypeStruct((t_len, CONV_CH), BF16), jax.ShapeDtypeStruct((t_len, CONV_CH), F32)),
        scratch_shapes=[pltpu.VMEM((tc + 32, CONV_CH), F32), pltpu.VMEM((7, tc + 24, CONV_CH), F32)],
        compiler_params=_cp(1),
    )(ug, ug, ug, w_dw, b_dw, g_ln, b_ln)


def _nbr_specs(n, nb):
    return [pl.BlockSpec((BLK, n), lambda i: (jnp.maximum(i - 1, 0), 0)),
            pl.BlockSpec((BLK, n), lambda i: (i, 0)),
            pl.BlockSpec((BLK, n), lambda i: (jnp.minimum(i + 1, nb - 1), 0))]


def _nbr_specs4(nb):
    return [pl.BlockSpec((1, 2, 4 * BLK, 128), lambda i: (jnp.maximum(i - 1, 0), 0, 0, 0)),
            pl.BlockSpec((1, 2, 4 * BLK, 128), lambda i: (i, 0, 0, 0)),
            pl.BlockSpec((1, 2, 4 * BLK, 128), lambda i: (jnp.minimum(i + 1, nb - 1), 0, 0, 0))]


def _band_bias():
    a = np.arange(4 * BLK)[:, None] % BLK
    c = np.arange(3 * BLK)[None, :]
    inside = np.abs(c - BLK - a) <= BLK
    q_side = np.stack([inside & (c >= BLK), inside, inside & (c < 2 * BLK)])
    blk = np.arange(12 * BLK)[:, None] // (4 * BLK)
    a = np.arange(12 * BLK)[:, None] % BLK
    c = np.arange(BLK)[None, :]
    inside = np.abs(c - a + (1 - blk) * BLK) <= BLK
    k_side = np.stack([inside & (blk >= 1), inside, inside & (blk <= 1)])
    return [jnp.asarray(np.where(m, 0.0, NEG).astype(np.float32)) for m in (q_side, k_side)]


def _edge_spec(shape, nb):
    return pl.BlockSpec((1,) + shape, lambda i: (jnp.where(i == 0, 0, jnp.where(i == nb - 1, 2, 1)),) + (0,) * len(shape))


def _attn_fwd(q, kd, vd, sink_b, bias, rides=()):
    t_len = q.shape[0]
    nb = t_len // BLK

    def body(q_ref, kp_ref, kc_ref, kn_ref, vp_ref, vc_ref, vn_ref, sk_ref, bias_ref, y_ref, lse_ref):
        kcat = jnp.concatenate([kp_ref[...], kc_ref[...], kn_ref[...]], axis=0)
        vcat = jnp.concatenate([vp_ref[...], vc_ref[...], vn_ref[...]], axis=0)
        ys = []
        for g in range(2):
            qs = _stack_heads(q_ref[:, 256 * g:256 * g + 256])
            s = _dot_nt(qs, kcat[:, 128 * g:128 * g + 128]) * ATT_SCALE + bias_ref[0]
            skc = _sink_col(sk_ref, g)
            m_b = jnp.maximum(jnp.max(s, axis=-1, keepdims=True), skc)
            p = jnp.exp(s - _tile3(m_b))
            den_b = jnp.sum(p, axis=-1, keepdims=True) + jnp.exp(skc - m_b)
            pn = p * _tile3(1.0 / den_b)
            o = _dot(_bf(pn), vcat[:, 128 * g:128 * g + 128])
            ys.append(_unstack_heads(o))
            lse_ref[0, g] = m_b + jnp.log(den_b)
        y_ref[...] = _bf(jnp.concatenate(ys, axis=1))

    return _call(
        body, rides=rides, name="attn_fwd", grid=(nb,),
        in_specs=[_rows(BLK, 512)] + _nbr_specs(256, nb) + _nbr_specs(256, nb) + [_res((8, 128)), _edge_spec((4 * BLK, 3 * BLK), nb)],
        out_specs=[_rows(BLK, 512), pl.BlockSpec((1, 2, 4 * BLK, 128), lambda i: (i, 0, 0, 0))],
        out_shape=(jax.ShapeDtypeStruct((t_len, 512), BF16), jax.ShapeDtypeStruct((nb, 2, 4 * BLK, 128), F32)),
        compiler_params=_cp(1),
    )(q, kd, kd, kd, vd, vd, vd, sink_b, bias)


def _mem_heads(kv_ref, h):
    lo = MEM_HD * (h % 2)
    return kv_ref[h // 2, :, lo:lo + MEM_HD], kv_ref[2 + h // 2, :, lo:lo + MEM_HD]


def _mix_mem_fwd(x, yc, ya, w_out, b_out, g_q, w_q, kv, w_o, tm, rides=()):
    t_len = x.shape[0]

    def body(x_ref, yc_ref, ya_ref, wout_ref, bout_ref, g_ref, wq_ref, kv_ref, wo_ref,
             ymix_ref, x1_ref, hq_ref, qm_ref, om_ref, x2_ref):
        ymix = jnp.concatenate([yc_ref[...], ya_ref[...]], axis=1)
        ymix_ref[...] = ymix
        x1 = x_ref[...] + _dot(ymix, wout_ref[...]) + bout_ref[...]
        x1_ref[...] = x1
        hq, _, _ = _rms_fwd(x1, g_ref[...])
        hqb = _bf(hq)
        hq_ref[...] = hqb
        qm = _bf(_dot(hqb, wq_ref[...]))
        qm_ref[...] = qm
        outs = []
        for h in range(MEM_HEADS):
            kh, vh = _mem_heads(kv_ref, h)
            s = _dot_nt(qm[:, MEM_HD * h:MEM_HD * (h + 1)], kh) * MEM_SCALE
            p = jnp.exp(s - jnp.max(s, axis=-1, keepdims=True))
            p = p * (1.0 / jnp.sum(p, axis=-1, keepdims=True))
            outs.append(_dot(_bf(p), vh))
        om = _bf(jnp.concatenate(outs, axis=1))
        om_ref[...] = om
        x2_ref[...] = x1 + _dot(om, wo_ref[...])

    act_b = jax.ShapeDtypeStruct((t_len, D_MODEL), BF16)
    act_f = jax.ShapeDtypeStruct((t_len, D_MODEL), F32)
    return _call(
        body, rides=rides, name="mix_mem_fwd", grid=(t_len // tm,),
        in_specs=[_rows(tm, D_MODEL), _rows(tm, 512), _rows(tm, 512), _res(w_out.shape), _res((1, D_MODEL)), _res((1, D_MODEL)),
                  _res(w_q.shape), _res(kv.shape), _res(w_o.shape)],
        out_specs=[_rows(tm, D_MODEL)] * 6,
        out_shape=(act_b, act_f, act_b, act_b, act_b, act_f),
        compiler_params=_cp(1),
    )(x, yc, ya, w_out, b_out, g_q, w_q, kv, w_o)


def _hidden_chunks(ff, width=1024):
    return [(lo, min(lo + width, ff)) for lo in range(0, ff, width)]


def _ffn_up(x2, g_ffn, w_gate, w_up, tm, rides=()):
    t_len = x2.shape[0]
    ff = w_gate.shape[0]

    def body(x2_ref, g_ref, wg_ref, wu_ref, hf_ref, gate_ref, up_ref, act_ref):
        hf, _, _ = _rms_fwd(x2_ref[...], g_ref[...])
        hfb = _bf(hf)
        hf_ref[...] = hfb
        for lo, hi in _hidden_chunks(ff):
            gate = _dot_nt(hfb, wg_ref[lo:hi, :])
            up = _dot_nt(hfb, wu_ref[lo:hi, :])
            gate_ref[:, lo:hi] = _bf(gate)
            up_ref[:, lo:hi] = _bf(up)
            act_ref[:, lo:hi] = _bf(gate * _sigmoid(gate) * up)

    hid = jax.ShapeDtypeStruct((t_len, ff), BF16)
    return _call(
        body, rides=rides, name="ffn_up", grid=(t_len // tm,),
        in_specs=[_rows(tm, D_MODEL), _res((1, D_MODEL)), _res(w_gate.shape), _res(w_up.shape)],
        out_specs=[_rows(tm, D_MODEL), _rows(tm, ff), _rows(tm, ff), _rows(tm, ff)],
        out_shape=[jax.ShapeDtypeStruct((t_len, D_MODEL), BF16), hid, hid, hid],
        compiler_params=_cp(1),
    )(x2, g_ffn, w_gate, w_up)


def _ffn_down_loss(x2, act, w_down, g_final, target, tm):
    t_len = x2.shape[0]
    ff = w_down.shape[0]

    def body(x2_ref, act_ref, wd_ref, gf_ref, tgt_ref, dx3_ref, loss_ref, dgf_ref):
        i = pl.program_id(0)
        x3 = x2_ref[...]
        for lo, hi in _hidden_chunks(ff):
            x3 = x3 + _dot(act_ref[:, lo:hi], wd_ref[lo:hi, :])
        gf = gf_ref[...]
        y, xh, r = _rms_fwd(x3, gf)
        err = y - tgt_ref[...]
        part = 0.5 * jnp.sum(jnp.mean(err * err, axis=-1, keepdims=True), axis=0, keepdims=True)
        dy = err * (1.0 / D_MODEL)
        dx3_ref[...] = _rms_bwd(dy, xh, r, gf)

        @pl.when(i == 0)
        def _():
            loss_ref[...] = jnp.zeros_like(loss_ref)
            dgf_ref[...] = jnp.zeros_like(dgf_ref)

        loss_ref[...] += jnp.broadcast_to(part, loss_ref.shape)
        dgf_ref[...] += _colsum(dy * xh)

    vec = pl.BlockSpec((1, D_MODEL), lambda i: (0, 0))
    return _call(
        body, name="ffn_down_loss", grid=(t_len // tm,),
        in_specs=[_rows(tm, D_MODEL), _rows(tm, ff), _res(w_down.shape), _res((1, D_MODEL)), _rows(tm, D_MODEL)],
        out_specs=[_rows(tm, D_MODEL), vec, vec],
        out_shape=(jax.ShapeDtypeStruct((t_len, D_MODEL), F32), jax.ShapeDtypeStruct((1, D_MODEL), F32),
                   jax.ShapeDtypeStruct((1, D_MODEL), F32)),
        compiler_params=_cp(1),
    )(x2, act, w_down, g_final, target)


def _ffn_bwd(dx3, x2, gate, up, g_ffn, w_gate, w_up, w_down, tm):
    t_len = x2.shape[0]
    ff = w_gate.shape[0]

    def body(dx3_ref, x2_ref, gate_ref, up_ref, g_ref, wg_ref, wu_ref, wd_ref, dx2_ref, dgate_ref, dup_ref, dg_ref):
        i = pl.program_id(0)
        dx3 = dx3_ref[...]
        d3b = _bf(dx3)
        dh = jnp.zeros((tm, D_MODEL), F32)
        for lo, hi in _hidden_chunks(ff):
            dact = _dot_nt(d3b, wd_ref[lo:hi, :])
            gt = gate_ref[:, lo:hi].astype(F32)
            u = up_ref[:, lo:hi].astype(F32)
            sg = _sigmoid(gt)
            dup = _bf(dact * (gt * sg))
            dgate = _bf(dact * u * (sg * (1.0 + gt * (1.0 - sg))))
            dup_ref[:, lo:hi] = dup
            dgate_ref[:, lo:hi] = dgate
            dh = dh + _dot(dgate, wg_ref[lo:hi, :]) + _dot(dup, wu_ref[lo:hi, :])
        g = g_ref[...]
        _, xh, r = _rms_fwd(x2_ref[...], g)
        dx2_ref[...] = dx3 + _rms_bwd(dh, xh, r, g)

        @pl.when(i == 0)
        def _():
            dg_ref[...] = jnp.zeros_like(dg_ref)

        dg_ref[...] += _colsum(dh * xh)

    hid = jax.ShapeDtypeStruct((t_len, ff), BF16)
    hid_spec = _rows(tm, ff)
    return _call(
        body, name="ffn_bwd", grid=(t_len // tm,),
        in_specs=[_rows(tm, D_MODEL), _rows(tm, D_MODEL), hid_spec, hid_spec, _res((1, D_MODEL)),
                  _res(w_gate.shape), _res(w_up.shape), _res(w_down.shape)],
        out_specs=[_rows(tm, D_MODEL), hid_spec, hid_spec, pl.BlockSpec((1, D_MODEL), lambda i: (0, 0))],
        out_shape=(jax.ShapeDtypeStruct((t_len, D_MODEL), F32), hid, hid, jax.ShapeDtypeStruct((1, D_MODEL), F32)),
        compiler_params=_cp(1),
    )(dx3, x2, gate, up, g_ffn, w_gate, w_up, w_down)


def _mix_mem_bwd(dx2, x1, qm, kv, g_q, w_q, w_o, w_out, tm, rides=()):
    t_len = x1.shape[0]
    m_len = kv.shape[1]

    def body(dx2_ref, x1_ref, qm_ref, kv_ref, g_ref, wq_ref, wo_ref, wout_ref,
             dx1_ref, dqm_ref, dyc_ref, dya_ref, dkv_ref, dgq_ref, dbout_ref):
        i = pl.program_id(0)

        @pl.when(i == 0)
        def _():
            dkv_ref[...] = jnp.zeros_like(dkv_ref)
            dgq_ref[...] = jnp.zeros_like(dgq_ref)
            dbout_ref[...] = jnp.zeros_like(dbout_ref)

        dx2 = dx2_ref[...]
        dom = _dot_nt(_bf(dx2), wo_ref[...])
        dqs = []
        for h in range(MEM_HEADS):
            kh, vh = _mem_heads(kv_ref, h)
            qh = qm_ref[:, MEM_HD * h:MEM_HD * (h + 1)]
            s = _dot_nt(qh, kh) * MEM_SCALE
            p = jnp.exp(s - jnp.max(s, axis=-1, keepdims=True))
            p = p * (1.0 / jnp.sum(p, axis=-1, keepdims=True))
            domh = _bf(dom[:, MEM_HD * h:MEM_HD * (h + 1)])
            dp = _dot_nt(domh, vh)
            ds = _bf(p * (dp - jnp.sum(p * dp, axis=-1, keepdims=True)) * MEM_SCALE)
            dqs.append(_dot(ds, kh))
            lo = MEM_HD * (h % 2)
            dkv_ref[h // 2, :, lo:lo + MEM_HD] += _dot_tn(ds, qh)
            dkv_ref[2 + h // 2, :, lo:lo + MEM_HD] += _dot_tn(_bf(p), domh)
        dqm = _bf(jnp.concatenate(dqs, axis=1))
        dqm_ref[...] = dqm
        dhq = _dot_nt(dqm, wq_ref[...])
        g = g_ref[...]
        _, xh, r = _rms_fwd(x1_ref[...], g)
        dx1 = dx2 + _rms_bwd(dhq, xh, r, g)
        dx1_ref[...] = dx1
        dgq_ref[...] += _colsum(dhq * xh)
        dbout_ref[...] += _colsum(dx1)
        dymix = _dot_nt(_bf(dx1), wout_ref[...])
        dyc_ref[...] = dymix[:, 0:CONV_CH]
        dya_ref[...] = _bf(dymix[:, CONV_CH:2 * CONV_CH])

    vec = pl.BlockSpec((1, D_MODEL), lambda i: (0, 0))
    return _call(
        body, rides=rides, name="mix_mem_bwd", grid=(t_len // tm,),
        in_specs=[_rows(tm, D_MODEL), _rows(tm, D_MODEL), _rows(tm, D_MODEL), _res(kv.shape), _res((1, D_MODEL)),
                  _res(w_q.shape), _res(w_o.shape), _res(w_out.shape)],
        out_specs=[_rows(tm, D_MODEL), _rows(tm, D_MODEL), _rows(tm, CONV_CH), _rows(tm, CONV_CH),
                   pl.BlockSpec(kv.shape, lambda i: (0, 0, 0)), vec, vec],
        out_shape=(jax.ShapeDtypeStruct((t_len, D_MODEL), F32), jax.ShapeDtypeStruct((t_len, D_MODEL), BF16),
                   jax.ShapeDtypeStruct((t_len, CONV_CH), F32), jax.ShapeDtypeStruct((t_len, CONV_CH), BF16),
                   jax.ShapeDtypeStruct((N_CHIPS, m_len, kv.shape[2]), F32),
                   jax.ShapeDtypeStruct((1, D_MODEL), F32), jax.ShapeDtypeStruct((1, D_MODEL), F32)),
        compiler_params=_cp(1),
    )(dx2, x1, qm, kv, g_q, w_q, w_o, w_out)


def _mem_kv_bwd(dkv, memn, mem, g_kv, w_kv):
    m_len = mem.shape[0]

    def body(dkv_ref, memn_ref, mem_ref, g_ref, w_ref, dw_ref, dg_ref):
        hb = memn_ref[...]
        dmn = jnp.zeros((m_len, D_MODEL), F32)
        for s in range(N_CHIPS):
            d = _bf(dkv_ref[s])
            dw_ref[s] = _bf(_dot_tn(hb, d))
            dmn = dmn + _dot_nt(d, w_ref[s])
        _, xh, _ = _rms_fwd(mem_ref[...], g_ref[...])
        dg_ref[...] = _colsum(dmn * xh)

    return _call(
        body, name="mem_kv_bwd",
        out_shape=(jax.ShapeDtypeStruct(w_kv.shape, BF16), jax.ShapeDtypeStruct((1, D_MODEL), F32)),
        compiler_params=pltpu.CompilerParams(vmem_limit_bytes=VMEM_LIMIT_BYTES),
    )(dkv, memn, mem, g_kv, w_kv)


def _attn_bwd_q(q, kd, vd, dya, lse, sink_b, bias, cos_t, sin_t, rides=()):
    t_len = q.shape[0]
    nb = t_len // BLK

    def body(q_ref, kp_ref, kc_ref, kn_ref, vp_ref, vc_ref, vn_ref, do_ref, lse_ref, sk_ref, bias_ref, c_ref, s_ref,
             dq_ref, dd_ref, dsk_ref):
        kcat = jnp.concatenate([kp_ref[...], kc_ref[...], kn_ref[...]], axis=0)
        vcat = jnp.concatenate([vp_ref[...], vc_ref[...], vn_ref[...]], axis=0)
        dqs, dsks = [], []
        for g in range(2):
            qs = _stack_heads(q_ref[:, 256 * g:256 * g + 256])
            dos = _stack_heads(do_ref[:, 256 * g:256 * g + 256])
            kk = kcat[:, 128 * g:128 * g + 128]
            s = _dot_nt(qs, kk) * ATT_SCALE + bias_ref[0]
            lse_b = lse_ref[0, g]
            p = jnp.exp(s - _tile3(lse_b))
            dp = _dot_nt(dos, vcat[:, 128 * g:128 * g + 128])
            drow = jnp.sum(p * dp, axis=-1, keepdims=True)
            ds = _bf(p * (dp - drow) * ATT_SCALE)
            dqs.append(_unstack_heads(_dot(ds, kk)))
            d_b = jnp.broadcast_to(drow, (4 * BLK, 128))
            dd_ref[0, g] = d_b
            contrib = -(jnp.exp(_sink_col(sk_ref, g) - lse_b) * d_b)
            dsks.append(jnp.sum(contrib.reshape(4, BLK, 128), axis=1))
        dq = jnp.concatenate(dqs, axis=1)
        dq_ref[...] = _bf(_rope(dq, c_ref[...], s_ref[...], -1.0))
        dsk_ref[0] = jnp.concatenate(dsks, axis=0)

    stat = pl.BlockSpec((1, 2, 4 * BLK, 128), lambda i: (i, 0, 0, 0))
    return _call(
        body, rides=rides, name="attn_bwd_q", grid=(nb,),
        in_specs=[_rows(BLK, 512)] + _nbr_specs(256, nb) + _nbr_specs(256, nb)
        + [_rows(BLK, 512), stat, _res((8, 128)), _edge_spec((4 * BLK, 3 * BLK), nb), _rows(BLK, 128), _rows(BLK, 128)],
        out_specs=[_rows(BLK, 512), stat, pl.BlockSpec((1, 8, 128), lambda i: (i, 0, 0))],
        out_shape=(jax.ShapeDtypeStruct((t_len, 512), BF16), jax.ShapeDtypeStruct((nb, 2, 4 * BLK, 128), F32),
                   jax.ShapeDtypeStruct((nb, 8, 128), F32)),
        compiler_params=_cp(1),
    )(q, kd, kd, kd, vd, vd, vd, dya, lse, sink_b, bias, cos_t, sin_t)


def _attn_bwd_kv(q, kd, vd, dya, lse, dd, bias, cos_t, sin_t, rides=()):
    t_len = q.shape[0]
    nb = t_len // BLK

    def body(kc_ref, vc_ref, qp_ref, qc_ref, qn_ref, dop_ref, doc_ref, don_ref, lp_ref, lc_ref, ln_ref,
             dp_ref, dc_ref, dn_ref, bias_ref, c_ref, s_ref, dk_ref, dv_ref):
        dks, dvs = [], []
        for g in range(2):
            cols = slice(256 * g, 256 * g + 256)
            qs = jnp.concatenate([_stack_heads(r[:, cols]) for r in (qp_ref, qc_ref, qn_ref)], axis=0)
            dos = jnp.concatenate([_stack_heads(r[:, cols]) for r in (dop_ref, doc_ref, don_ref)], axis=0)
            lse_b = jnp.concatenate([r[0, g] for r in (lp_ref, lc_ref, ln_ref)], axis=0)
            d_b = jnp.concatenate([r[0, g] for r in (dp_ref, dc_ref, dn_ref)], axis=0)
            kk = kc_ref[:, 128 * g:128 * g + 128]
            s = _dot_nt(qs, kk) * ATT_SCALE + bias_ref[0]
            p = jnp.exp(s - lse_b)
            dp = _dot_nt(dos, vc_ref[:, 128 * g:128 * g + 128])
            ds = _bf(p * (dp - d_b) * ATT_SCALE)
            dvs.append(_dot_tn(_bf(p), dos))
            dks.append(_dot_tn(ds, qs))
        dk_ref[...] = _bf(_rope(_fold_heads(dks), c_ref[...], s_ref[...], -1.0))
        dv_ref[...] = _bf(_fold_heads(dvs))

    return _call(
        body, rides=rides, name="attn_bwd_kv", grid=(nb,),
        in_specs=[_rows(BLK, 256), _rows(BLK, 256)] + _nbr_specs(512, nb) + _nbr_specs(512, nb) + _nbr_specs4(nb) + _nbr_specs4(nb)
        + [_edge_spec((12 * BLK, BLK), nb), _rows(BLK, 128), _rows(BLK, 128)],
        out_specs=[_rows(BLK, 128), _rows(BLK, 128)],
        out_shape=(jax.ShapeDtypeStruct((t_len, 128), BF16), jax.ShapeDtypeStruct((t_len, 128), BF16)),
        compiler_params=_cp(1),
    )(kd, vd, q, q, q, dya, dya, dya, lse, lse, lse, dd, dd, dd, bias, cos_t, sin_t)


def _conv_norm_bwd(pre, dyc, g_ln, b_ln, tc, rides=()):
    t_len = pre.shape[0]

    def body(pre_ref, dy_ref, g_ref, b_ref, dpre_ref, stats_ref):
        i = pl.program_id(0)
        pre_v = pre_ref[...]
        mu = jnp.mean(pre_v, axis=-1, keepdims=True)
        d = pre_v - mu
        rstd = lax.rsqrt(jnp.mean(d * d, axis=-1, keepdims=True) + EPS)
        xh = d * rstd
        g = g_ref[...]
        ln = xh * g + b_ref[...]
        sg = _sigmoid(ln)
        dln = dy_ref[...] * (sg * (1.0 + ln * (1.0 - sg)))
        dxh = dln * g
        dpre = rstd * (dxh - jnp.mean(dxh, axis=-1, keepdims=True) - xh * jnp.mean(dxh * xh, axis=-1, keepdims=True))
        dpre_ref[...] = dpre

        @pl.when(i == 0)
        def _():
            stats_ref[...] = jnp.zeros_like(stats_ref)

        stats_ref[0:1, :] += _colsum(dln * xh)
        stats_ref[1:2, :] += _colsum(dln)
        stats_ref[2:3, :] += _colsum(dpre)

    return _call(
        body, rides=rides, name="conv_norm_bwd", grid=(t_len // tc,),
        in_specs=[_rows(tc, CONV_CH), _rows(tc, CONV_CH), _res((1, CONV_CH)), _res((1, CONV_CH))],
        out_specs=[_rows(tc, CONV_CH), pl.BlockSpec((8, CONV_CH), lambda i: (0, 0))],
        out_shape=(jax.ShapeDtypeStruct((t_len, CONV_CH), F32), jax.ShapeDtypeStruct((8, CONV_CH), F32)),
        compiler_params=_cp(1),
    )(pre, dyc, g_ln, b_ln)


def _conv_bwd(dpre, ug, w_dw, tc, rides=()):
    t_len = ug.shape[0]
    n_tiles = t_len // tc

    def body(dp_ref, dm_ref, dn_ref, up_ref, um_ref, un_ref, w_ref, du_ref, dw_ref, dbuf, vbuf, dshift, vshift):
        i = pl.program_id(0)
        _fill_halo_buf(dbuf, dp_ref[...], dm_ref[...], dn_ref[...], i, n_tiles, tc)
        _fill_halo_buf(vbuf, _glu(up_ref[...]), _glu(um_ref[...]), _glu(un_ref[...]), i, n_tiles, tc)
        _shift_copies(dbuf, dshift, tc)
        _shift_copies(vbuf, vshift, tc)

        @pl.when(i == 0)
        def _():
            dw_ref[...] = jnp.zeros_like(dw_ref)

        def chunk(c, carry):
            base = c * CONV_ROWS
            rows = pl.ds(pl.multiple_of(base, CONV_ROWS), CONV_ROWS)
            dmain = dm_ref[rows, :]
            dv = jnp.zeros((CONV_ROWS, CONV_CH), F32)
            for k in range(CONV_W):
                dv = dv + w_ref[k:k + 1, :] * _shifted_rows(dbuf, dshift, 31 - k, base)
                prod = dmain * _shifted_rows(vbuf, vshift, k + 1, base)
                dw_ref[8 * k:8 * k + 8, :] += jnp.sum(prod.reshape(CONV_ROWS // 8, 8, CONV_CH), axis=0)
            um = um_ref[rows, :]
            a, gt = um[:, 0:CONV_CH], um[:, CONV_CH:2 * CONV_CH]
            sg = _sigmoid(gt)
            du_ref[rows, :] = _bf(jnp.concatenate([dv * sg, dv * a * (sg * (1.0 - sg))], axis=1))
            return carry

        lax.fori_loop(0, tc // CONV_ROWS, chunk, 0)

    shifts = pltpu.VMEM((7, tc + 24, CONV_CH), F32)
    return _call(
        body, rides=rides, name="conv_bwd", grid=(n_tiles,),
        in_specs=_halo_specs(tc, CONV_CH, t_len) + _halo_specs(tc, 1024, t_len) + [_res((32, CONV_CH))],
        out_specs=[_rows(tc, 1024), pl.BlockSpec((8 * 32, CONV_CH), lambda i: (0, 0))],
        out_shape=(jax.ShapeDtypeStruct((t_len, 1024), BF16), jax.ShapeDtypeStruct((8 * 32, CONV_CH), F32)),
        scratch_shapes=[pltpu.VMEM((tc + 32, CONV_CH), F32), pltpu.VMEM((tc + 32, CONV_CH), F32), shifts, shifts],
        compiler_params=_cp(1),
    )(dpre, dpre, dpre, ug, ug, ug, w_dw)


def _in_proj_bwd(du_glu, dq, dk, dv, dx1, x, g_mix, w_t, tm, rides=()):
    t_len = x.shape[0]
    n_ext = w_t.shape[0]

    def body(dg_ref, dq_ref, dk_ref, dv_ref, dx1_ref, x_ref, g_ref, w_ref, dx_ref, du_ref, db_ref, dgm_ref):
        i = pl.program_id(0)
        du = jnp.concatenate([dg_ref[...], dq_ref[...], dk_ref[...], dv_ref[...]], axis=1)
        du_ref[...] = du
        dh = _dot(du, w_ref[...])
        g = g_ref[...]
        _, xh, r = _rms_fwd(x_ref[...], g)
        dx_ref[...] = dx1_ref[...] + _rms_bwd(dh, xh, r, g)

        @pl.when(i == 0)
        def _():
            db_ref[...] = jnp.zeros_like(db_ref)
            dgm_ref[...] = jnp.zeros_like(dgm_ref)

        db_ref[...] += _colsum(du.astype(F32))
        dgm_ref[...] += _colsum(dh * xh)

    return _call(
        body, rides=rides, name="in_proj_bwd", grid=(t_len // tm,),
        in_specs=[_rows(tm, 1024), _rows(tm, 512), _rows(tm, 128), _rows(tm, 128), _rows(tm, D_MODEL), _rows(tm, D_MODEL),
                  _res((1, D_MODEL)), _res(w_t.shape)],
        out_specs=[_rows(tm, D_MODEL), _rows(tm, n_ext), pl.BlockSpec((1, n_ext), lambda i: (0, 0)),
                   pl.BlockSpec((1, D_MODEL), lambda i: (0, 0))],
        out_shape=(jax.ShapeDtypeStruct((t_len, D_MODEL), F32), jax.ShapeDtypeStruct((t_len, n_ext), BF16),
                   jax.ShapeDtypeStruct((1, n_ext), F32), jax.ShapeDtypeStruct((1, D_MODEL), F32)),
        compiler_params=_cp(1),
    )(du_glu, dq, dk, dv, dx1, x, g_mix, w_t)


def _weight_grad(a, d, name, tt, rides=()):
    t_len, k_dim = a.shape
    n_dim = d.shape[1]
    tk = k_dim if k_dim <= 1792 else k_dim // 2
    assert k_dim % tk == 0 and tk % 128 == 0 and n_dim % 128 == 0
    tt = min(tt, t_len)
    n_t = t_len // tt

    def body(a_ref, d_ref, o_ref, acc):
        t = pl.program_id(1)

        @pl.when(t == 0)
        def _():
            acc[...] = jnp.zeros_like(acc)

        acc[...] += _dot_tn(_bf(a_ref[...]), _bf(d_ref[...]))

        @pl.when(t == n_t - 1)
        def _():
            o_ref[...] = _bf(acc[...])

    res = _call(
        body, rides=rides, name=name, grid=(k_dim // tk, n_t),
        in_specs=[pl.BlockSpec((tt, tk), lambda k, t: (t, k)), pl.BlockSpec((tt, n_dim), lambda k, t: (t, 0))],
        out_specs=[pl.BlockSpec((tk, n_dim), lambda k, t: (k, 0))],
        out_shape=[jax.ShapeDtypeStruct((k_dim, n_dim), BF16)],
        scratch_shapes=[pltpu.VMEM((tk, n_dim), F32)],
        compiler_params=_cp(2),
    )(a, d)
    return res if rides else res[0]


ANY = pl.BlockSpec(memory_space=pl.ANY)


def _place():
    x, y, c = lax.axis_index("x"), lax.axis_index("y"), lax.axis_index("c")
    chips = [(1 - x, y), (x, 1 - y), (1 - x, 1 - y)]
    return x, y, c, chips


def _remote(src, dst, send_sems, recv_sems, k, to):
    return pltpu.make_async_remote_copy(src_ref=src, dst_ref=dst, send_sem=send_sems.at[k], recv_sem=recv_sems.at[k],
                                        device_id=to, device_id_type=MESH)


def _by_shape(names, arrays):
    groups = {}
    for k in names:
        groups.setdefault(arrays[k].shape, []).append(k)
    return list(groups.values())


def _cast_place(ws, chip_idx, tr):
    n = len(ws)
    rows, cols = ws[0].shape
    h = rows // 2
    tr = _div_tile(h, tr)
    per = h // tr

    def body(s_ref, *refs):
        for w_ref, o_ref in zip(refs[:n], refs[n:]):
            o_ref[0, 0] = _bf(w_ref[...])

    return _call(
        body, name="cast_place",
        grid_spec=pltpu.PrefetchScalarGridSpec(
            num_scalar_prefetch=1, grid=(2, per),
            in_specs=[pl.BlockSpec((tr, cols), lambda hh, r, s_ref: (hh * per + r, 0))] * n,
            out_specs=[pl.BlockSpec((1, 1, tr, cols), lambda hh, r, s_ref: (s_ref[0], hh, r, 0))] * n),
        out_shape=[jax.ShapeDtypeStruct((N_CHIPS, 2, h, cols), BF16)] * n,
        compiler_params=_cp(2),
    )(chip_idx, *ws)


def _same(arrays):
    return [jax.ShapeDtypeStruct(a.shape, a.dtype) for a in arrays]


def _gather_ride(bufs):
    n = len(bufs)

    def first_hop(outs, send, recv):
        x, y, c, chips = _place()
        mine = [outs[i].at[2 * x + y, c] for i in range(n)]
        return [_remote(mine[i], mine[i], send, recv, 3 * i + j, (cx, cy, c)) for i in range(n) for j, (cx, cy) in enumerate(chips)]

    def start(ins, outs, send, recv):
        for cp in first_hop(outs, send, recv):
            cp.start()

    def finish(ins, outs, send, recv):
        x, y, c, chips = _place()
        sib = (x, y, 1 - c)
        onward = []
        for i in range(n):
            for j, (cx, cy) in enumerate(chips):
                slab = outs[i].at[2 * cx + cy, c]
                _remote(slab, slab, send, recv, 3 * i + j, sib).wait_recv()
                onward.append(_remote(slab, slab, send, recv, 3 * n + 3 * i + j, sib))
                onward[-1].start()
        for i in range(n):
            for j, (cx, cy) in enumerate(chips):
                other = outs[i].at[2 * cx + cy, 1 - c]
                _remote(other, other, send, recv, 3 * n + 3 * i + j, sib).wait_recv()
        for cp in first_hop(outs, send, recv) + onward:
            cp.wait_send()

    return _Ride(bufs, _same(bufs), 6 * n, start, finish, aliases={i: i for i in range(n)})


def _spread_ride(buf):
    def sends(outs, send, recv):
        x, y, c, chips = _place()
        mine = outs[0].at[2 * x + y]
        return [_remote(mine, mine, send, recv, j, (cx, cy, c)) for j, (cx, cy) in enumerate(chips)]

    def start(ins, outs, send, recv):
        for cp in sends(outs, send, recv):
            cp.start()

    def finish(ins, outs, send, recv):
        _, _, c, chips = _place()
        for j, (cx, cy) in enumerate(chips):
            slab = outs[0].at[2 * cx + cy]
            _remote(slab, slab, send, recv, j, (cx, cy, c)).wait_recv()
        for cp in sends(outs, send, recv):
            cp.wait_send()

    return _Ride([buf], _same([buf]), 3, start, finish, aliases={0: 0})


def _allgather_ride(buf):
    def peers():
        x, y, c, _ = _place()
        return [(x ^ ((k >> 2) & 1), y ^ ((k >> 1) & 1), c ^ (k & 1)) for k in range(1, N_DEV)], 4 * x + 2 * y + c

    def sends(outs, send, recv):
        to, me = peers()
        mine = outs[0].at[me]
        return [_remote(mine, mine, send, recv, k, p) for k, p in enumerate(to)]

    def start(ins, outs, send, recv):
        for cp in sends(outs, send, recv):
            cp.start()

    def finish(ins, outs, send, recv):
        for k, (px, py, pc) in enumerate(peers()[0]):
            slab = outs[0].at[4 * px + 2 * py + pc]
            _remote(slab, slab, send, recv, k, (px, py, pc)).wait_recv()
        for cp in sends(outs, send, recv):
            cp.wait_send()

    return _Ride([buf], _same([buf]), N_DEV - 1, start, finish, aliases={0: 0})


def _sum_slabs(buf):
    def body(b_ref, o_ref):
        acc = b_ref[0]
        for d in range(1, buf.shape[0]):
            acc = acc + b_ref[d]
        o_ref[...] = acc

    return _call(body, name="sum_slabs", out_shape=jax.ShapeDtypeStruct(buf.shape[1:], buf.dtype))(buf)


def _pairwise_ride(arrays, out_shape, n_sem, copies):
    def start(ins, outs, send, recv):
        for cp in copies(ins, outs, send, recv):
            cp.start()

    def finish(ins, outs, send, recv):
        for cp in copies(ins, outs, send, recv):
            cp.wait()

    return _Ride(arrays, out_shape, n_sem, start, finish)


def _run_rides(name, rides):
    k_in = [len(r.operands) for r in rides]
    k_out = [len(r.out_shape) for r in rides]

    def body(*refs):
        pos, r_in, r_out = 0, [], []
        for k in k_in:
            r_in.append(refs[pos:pos + k])
            pos += k
        for k in k_out:
            r_out.append(refs[pos:pos + k])
            pos += k
        sems = refs[pos:]
        for j, r in enumerate(rides):
            r.start(r_in[j], r_out[j], sems[2 * j], sems[2 * j + 1])
        for j, r in enumerate(rides):
            r.finish(r_in[j], r_out[j], sems[2 * j], sems[2 * j + 1])

    aliases, off_in, off_out = {}, 0, 0
    for r, ki, ko in zip(rides, k_in, k_out):
        aliases.update({off_in + a: off_out + b for a, b in r.aliases.items()})
        off_in, off_out = off_in + ki, off_out + ko
    res = _call(
        body, name=name, in_specs=[ANY] * sum(k_in), out_specs=[ANY] * sum(k_out),
        out_shape=[s for r in rides for s in r.out_shape], input_output_aliases=aliases,
        scratch_shapes=[pltpu.SemaphoreType.DMA((r.n_sem,)) for r in rides for _ in range(2)],
    )(*[op for r in rides for op in r.operands])
    out, pos = [], 0
    for k in k_out:
        out.append(list(res[pos:pos + k]))
        pos += k
    return out


def _swap_ride(grads):
    def copies(ins, outs, send, recv):
        x, y, c, _ = _place()
        return [_remote(ins[i].at[:, 1 - c], outs[i], send, recv, i, (x, y, 1 - c)) for i in range(len(grads))]

    out_shape = [jax.ShapeDtypeStruct((g.shape[0],) + g.shape[2:], g.dtype) for g in grads]
    return _pairwise_ride(grads, out_shape, len(grads), copies)


def _pair_sum(grads, others, c_idx, tr):
    n = len(grads)
    n_s, _, h, cols = grads[0].shape
    tr = _div_tile(h, tr)

    def body(c_ref, *refs):
        for a_ref, b_ref, o_ref in zip(refs[:n], refs[n:2 * n], refs[2 * n:]):
            o_ref[...] = _bf(a_ref[0].astype(F32) + b_ref[...].astype(F32))

    return _call(
        body, name="pair_sum",
        grid_spec=pltpu.PrefetchScalarGridSpec(
            num_scalar_prefetch=1, grid=(n_s, h // tr),
            in_specs=[pl.BlockSpec((1, 1, tr, cols), lambda s, r, c_ref: (s, c_ref[0], r, 0))] * n
            + [pl.BlockSpec((1, tr, cols), lambda s, r, c_ref: (s, r, 0))] * n,
            out_specs=[pl.BlockSpec((1, tr, cols), lambda s, r, c_ref: (s, r, 0))] * n),
        out_shape=[jax.ShapeDtypeStruct((n_s, h, cols), BF16)] * n,
        compiler_params=_cp(2),
    )(c_idx, *grads, *others)


def _exchange_ride(sums):
    def copies(ins, outs, send, recv):
        _, _, c, chips = _place()
        return [_remote(ins[i].at[2 * cx + cy], outs[i].at[j], send, recv, 3 * i + j, (cx, cy, c))
                for i in range(len(sums)) for j, (cx, cy) in enumerate(chips)]

    out_shape = [jax.ShapeDtypeStruct((3,) + s.shape[1:], s.dtype) for s in sums]
    return _pairwise_ride(sums, out_shape, 3 * len(sums), copies)


def _chip_sum(owns, others, chip_idx, tr):
    n = len(owns)
    _, h, cols = owns[0].shape
    tr = _div_tile(h, tr)

    def body(s_ref, *refs):
        for a_ref, p_ref, o_ref in zip(refs[:n], refs[n:2 * n], refs[2 * n:]):
            acc = a_ref[0].astype(F32)
            for j in range(N_CHIPS - 1):
                acc = acc + p_ref[j].astype(F32)
            o_ref[...] = acc

    return _call(
        body, name="chip_sum",
        grid_spec=pltpu.PrefetchScalarGridSpec(
            num_scalar_prefetch=1, grid=(h // tr,),
            in_specs=[pl.BlockSpec((1, tr, cols), lambda r, s_ref: (s_ref[0], r, 0))] * n
            + [pl.BlockSpec((N_CHIPS - 1, tr, cols), lambda r, s_ref: (0, r, 0))] * n,
            out_specs=[pl.BlockSpec((tr, cols), lambda r, s_ref: (r, 0))] * n),
        out_shape=[jax.ShapeDtypeStruct((h, cols), F32)] * n,
        compiler_params=_cp(1),
    )(chip_idx, *owns, *others)


def _share_ride(halves):
    def copies(ins, outs, send, recv):
        x, y, c, _ = _place()
        return [_remote(ins[i], outs[i], send, recv, i, (x, y, 1 - c)) for i in range(len(halves))]

    return _pairwise_ride(halves, _same(halves), len(halves), copies)


def _adamw_math(w, g, m, v):
    m_new = ADAM_B1 * m + (1.0 - ADAM_B1) * g
    v_new = ADAM_B2 * v + (1.0 - ADAM_B2) * (g * g)
    m_hat = m_new * (1.0 / (1.0 - ADAM_B1 ** ADAM_STEP))
    v_hat = v_new * (1.0 / (1.0 - ADAM_B2 ** ADAM_STEP))
    delta = -ADAM_LR * (m_hat / (jnp.sqrt(v_hat) + ADAM_EPS) + ADAM_WD * w)
    return delta, m_new, v_new


def _adamw(w, g_mine, g_other, m, v, core_idx, tr):
    rows, cols = w.shape
    h = rows // 2
    tr = _div_tile(h, tr)
    per = h // tr

    def body(c_ref, w_ref, ga_ref, gb_ref, m_ref, v_ref, g_ref, d_ref, mo_ref, vo_ref):
        g = jnp.where(pl.program_id(0) == c_ref[0], ga_ref[...], gb_ref[...])
        d, mn, vn = _adamw_math(w_ref[...], g, m_ref[...], v_ref[...])
        g_ref[...] = g
        d_ref[...] = d
        mo_ref[...] = mn
        vo_ref[...] = vn

    full = pl.BlockSpec((tr, cols), lambda hh, r, c_ref: (hh * per + r, 0))
    mine = pl.BlockSpec((tr, cols), lambda hh, r, c_ref: (jnp.where(hh == c_ref[0], r, 0), 0))
    other = pl.BlockSpec((tr, cols), lambda hh, r, c_ref: (jnp.where(hh == c_ref[0], 0, r), 0))
    shp = jax.ShapeDtypeStruct(w.shape, F32)
    return _call(
        body, name="adamw",
        grid_spec=pltpu.PrefetchScalarGridSpec(num_scalar_prefetch=1, grid=(2, per), in_specs=[full, mine, other, full, full],
                                               out_specs=[full] * 4),
        out_shape=(shp, shp, shp, shp), compiler_params=_cp(2))(core_idx, w, g_mine, g_other, m, v)


def _adamw_small(ws, gs, ms, vs):
    n = len(ws)

    def body(*refs):
        w_r, g_r, m_r, v_r = refs[0:n], refs[n:2 * n], refs[2 * n:3 * n], refs[3 * n:4 * n]
        d_o, m_o, v_o = refs[4 * n:5 * n], refs[5 * n:6 * n], refs[6 * n:7 * n]
        for i in range(n):
            d, mn, vn = _adamw_math(w_r[i][...], g_r[i][...], m_r[i][...], v_r[i][...])
            d_o[i][...] = d
            m_o[i][...] = mn
            v_o[i][...] = vn

    shp = [jax.ShapeDtypeStruct(w.shape, F32) for w in ws]
    outs = _call(body, name="adamw_small", out_shape=shp * 3)(*ws, *gs, *ms, *vs)
    return outs[0:n], outs[n:2 * n], outs[2 * n:3 * n]


def _rope_tables(t_len):
    pos = jnp.arange(t_len, dtype=F32)
    inv_freq = ROPE_THETA ** (-jnp.arange(0, HEAD_DIM, 2, dtype=F32) / HEAD_DIM)
    ang = pos[:, None] * inv_freq[None, :]
    cos, sin = jnp.cos(ang), jnp.sin(ang)
    return jnp.tile(jnp.concatenate([cos, cos], axis=1), (1, 2)), jnp.tile(jnp.concatenate([-sin, sin], axis=1), (1, 2))


def _dup_heads(a):
    h0, h1 = a[..., 0:64], a[..., 64:128]
    return jnp.concatenate([h0, h0, h1, h1], axis=-1)


def _local_step(x, mem, target, small, wg, comm, tm_a=512, tm_b=256, tc=512, tt=1024):
    def run(stage, fn, n_own, *operands):
        rides = comm.rides(stage)
        res = fn(*operands, rides=rides)
        res = list(res) if isinstance(res, (list, tuple)) else [res]
        brought, pos = [], n_own
        for r in rides:
            brought.append(res[pos:pos + len(r.out_shape)])
            pos += len(r.out_shape)
        comm.landed(stage, brought, wg)
        return res[:n_own]

    t_len = x.shape[0]
    cos_t, sin_t = _rope_tables(t_len)
    b_in = small["b_in"]
    b_ext = jnp.concatenate([b_in[:, 0:1536], _dup_heads(b_in[:, 1536:1664]), _dup_heads(b_in[:, 1664:1792])], axis=1)
    w_dw = jnp.concatenate([wg["w_dw"], jnp.zeros((1, CONV_CH), F32)], axis=0)
    sink_b = jnp.broadcast_to(small["attn_sink"].reshape(8, 1), (8, 128))
    bias_q, bias_k = _band_bias()

    ug, q, kd, vd, h1 = run("in_proj_fwd", _in_proj_fwd, 5, x, small["g_mix"], wg["w_in"], b_ext, cos_t, sin_t, tm_a)
    yc, pre = run("conv_fwd", _conv_fwd, 2, ug, w_dw, small["b_dw"], small["g_conv_ln"], small["b_conv_ln"], tc)
    memn, kv = _mem_kv_fwd(mem, small["g_mem_kv"], wg["w_mem_kv"])
    ya, lse = run("attn_fwd", _attn_fwd, 2, q, kd, vd, sink_b, bias_q)
    ymix, x1, hq, qm, om, x2 = run("mix_mem_fwd", _mix_mem_fwd, 6, x, yc, ya, wg["w_out"], small["b_out"], small["g_mem_q"],
                                   wg["w_mem_q"], kv, wg["w_mem_o"], tm_a)
    hf, gate, up, act = run("ffn_up", _ffn_up, 4, x2, small["g_ffn"], wg["w_gate"], wg["w_up"], tm_a)
    dx3, loss, d_g_final = _ffn_down_loss(x2, act, wg["w_down"], small["g_final"], target, tm_a)

    dx2, dgate, dup, d_g_ffn = _ffn_bwd(dx3, x2, gate, up, small["g_ffn"], wg["w_gate"], wg["w_up"], wg["w_down"], tm_b)
    comm.grad("w_gate", _weight_grad(dgate, hf, "dw_gate", tt))
    comm.grad("w_up", run("dw_up", _weight_grad, 1, dup, hf, "dw_up", tt)[0])
    comm.grad("w_down", run("dw_down", _weight_grad, 1, act, dx3, "dw_down", tt)[0])
    dx1, dqm, dyc, dya, dkv, d_g_mem_q, d_b_out = run("mix_mem_bwd", _mix_mem_bwd, 7, dx2, x1, qm, kv, small["g_mem_q"],
                                                      wg["w_mem_q"], wg["w_mem_o"], wg["w_out"], tm_a)
    d_w_mem_kv, d_g_mem_kv = _mem_kv_bwd(dkv, memn, mem, small["g_mem_kv"], wg["w_mem_kv"])
    comm.grad("w_mem_kv", d_w_mem_kv)
    comm.grad("w_out", _weight_grad(ymix, dx1, "dw_out", tt))
    comm.grad("w_mem_q", _weight_grad(hq, dqm, "dw_mem_q", tt))
    comm.grad("w_mem_o", _weight_grad(om, dx2, "dw_mem_o", tt))
    dq, dd, dsink = run("attn_bwd_q", _attn_bwd_q, 3, q, kd, vd, dya, lse, sink_b, bias_q, cos_t, sin_t)
    dk, dv = run("attn_bwd_kv", _attn_bwd_kv, 2, q, kd, vd, dya, lse, dd, bias_k, cos_t, sin_t)
    dpre, cstats = run("conv_norm_bwd", _conv_norm_bwd, 2, pre, dyc, small["g_conv_ln"], small["b_conv_ln"], tc)
    du_glu, d_w_dw = run("conv_bwd", _conv_bwd, 2, dpre, ug, w_dw, tc)
    grad_x, du, d_b_in, d_g_mix = run("in_proj_bwd", _in_proj_bwd, 4, du_glu, dq, dk, dv, dx1, x, small["g_mix"], wg["w_in"],
                                      tm_a)
    grads = {
        "w_dw": jnp.sum(d_w_dw.reshape(32, 8, CONV_CH), axis=1)[0:CONV_W],
        "g_mix": d_g_mix, "b_in": d_b_in, "b_dw": cstats[2:3], "g_conv_ln": cstats[0:1],
        "b_conv_ln": cstats[1:2], "attn_sink": jnp.sum(dsink[:, :, 0], axis=0)[None, :], "b_out": d_b_out,
        "g_mem_q": d_g_mem_q, "g_mem_kv": d_g_mem_kv, "g_ffn": d_g_ffn, "g_final": d_g_final,
    }
    comm.small(loss[0:1, 0:1], grads)
    (d_w_in,) = run("dw_in", _weight_grad, 1, du, h1, "dw_in", tt)
    comm.grad("w_in", d_w_in)
    return loss[0:1, 0:1], grad_x, grads


BIG = ["w_in", "w_out", "w_mem_q", "w_mem_kv", "w_mem_o", "w_gate", "w_up", "w_down"]
KEEP_SLABS = ("w_mem_kv",)
TRANSPOSED = ("w_in", "w_gate", "w_up")
SMALL = ["g_mix", "b_in", "b_dw", "g_conv_ln", "b_conv_ln", "attn_sink", "b_out", "g_mem_q", "g_mem_kv", "g_ffn", "g_final"]
PACK_ROWS = 32
ROWS_PER_STEP = 512

GATHER_ON = {"in_proj_fwd": ("w_out", "w_mem_q"), "conv_fwd": ("w_mem_kv", "w_mem_o"), "attn_fwd": ("w_gate",),
             "mix_mem_fwd": ("w_up",), "ffn_up": ("w_down",)}
MID_GROUP = ("w_mem_kv", "w_out", "w_mem_q", "w_mem_o")
SWAP_ON = {"dw_up": ("w_gate",), "dw_down": ("w_up",), "mix_mem_bwd": ("w_down",), "attn_bwd_q": MID_GROUP}
EXCHANGE_ON = {"dw_down": ("w_gate",), "mix_mem_bwd": ("w_up",), "attn_bwd_q": ("w_down",),
               "attn_bwd_kv": ("w_mem_kv", "w_out"), "conv_norm_bwd": ("w_mem_q",), "conv_bwd": ("w_mem_o",)}
SMALL_ON = "dw_in"


def _as_weight(name, gathered):
    g = gathered.reshape(N_CHIPS, gathered.shape[2] * 2, gathered.shape[3])
    return g if name in KEEP_SLABS else g.reshape(-1, g.shape[2])


class _Overlap:
    def __init__(self, bufs, chip_idx, core_idx):
        self.bufs, self.chip_idx, self.core_idx = bufs, chip_idx, core_idx
        self.parts, self.sums, self.others = {}, {}, {}

    def rides(self, stage):
        rides = []
        if stage in GATHER_ON:
            rides.append(_gather_ride([self.bufs[k] for k in GATHER_ON[stage]]))
        if stage in EXCHANGE_ON:
            rides.append(_exchange_ride([self.sums[k] for k in EXCHANGE_ON[stage]]))
        if stage in SWAP_ON:
            rides.append(_swap_ride([self.parts[k] for k in SWAP_ON[stage]]))
        if stage == SMALL_ON:
            rides.append(_allgather_ride(self.packs))
        return rides

    def landed(self, stage, brought, wg):
        brought = list(brought)
        if stage in GATHER_ON:
            for k, g in zip(GATHER_ON[stage], brought.pop(0)):
                wg[k] = _as_weight(k, g)
        if stage in EXCHANGE_ON:
            self.others.update(zip(EXCHANGE_ON[stage], brought.pop(0)))
        if stage in SWAP_ON:
            self._pair(SWAP_ON[stage], brought.pop(0))
        if stage == SMALL_ON:
            (self.packs,) = brought.pop(0)

    def small(self, loss, grads):
        x, y, c = lax.axis_index("x"), lax.axis_index("y"), lax.axis_index("c")
        pack = _pack_small(loss, grads)
        self.packs = lax.dynamic_update_slice(jnp.zeros((N_DEV,) + pack.shape, F32), pack[None], (4 * x + 2 * y + c, 0, 0))

    def grad(self, name, g):
        if g.ndim == 2:
            g = g.reshape(N_CHIPS, g.shape[0] // N_CHIPS, g.shape[1])
        self.parts[name] = g.reshape(N_CHIPS, 2, g.shape[1] // 2, g.shape[2])

    def _pair(self, names, from_sibling):
        came = dict(zip(names, from_sibling))
        for group in _by_shape(names, self.parts):
            sums = _pair_sum([self.parts[k] for k in group], [came[k] for k in group], self.core_idx, ROWS_PER_STEP)
            self.sums.update(zip(group, sums))

    def finish(self):
        (from_sibling,) = _run_rides("swap_last", [_swap_ride([self.parts["w_in"]])])
        self._pair(("w_in",), from_sibling)
        ((self.others["w_in"],),) = _run_rides("exchange_last", [_exchange_ride([self.sums["w_in"]])])
        halves = {}
        for group in _by_shape(BIG, self.sums):
            res = _chip_sum([self.sums[k] for k in group], [self.others[k] for k in group], self.chip_idx, ROWS_PER_STEP)
            halves.update(zip(group, res))
        mine = [halves[k] for k in BIG]
        (theirs,) = _run_rides("sibling_share", [_share_ride(mine)])
        return mine, theirs, _sum_slabs(self.packs)


def _pack_small(loss, grads):
    def row(a):
        a = a.reshape(1, -1)
        return jnp.pad(a, ((0, 0), (0, 1024 - a.shape[1])))

    rows = [row(grads[k]) for k in ("g_mix", "b_out", "g_mem_q", "g_mem_kv", "g_ffn", "g_final")]
    rows += [grads["b_in"][:, 0:1024], row(grads["b_in"][:, 1024:1792])]
    rows += [jnp.concatenate([grads["b_dw"], grads["g_conv_ln"]], axis=1), row(grads["b_conv_ln"]), row(grads["attn_sink"]),
             row(loss)]
    dw = jnp.pad(grads["w_dw"], ((0, 1), (0, 0))).reshape(16, 1024)
    pack = jnp.concatenate(rows + [dw], axis=0)
    return jnp.pad(pack, ((0, PACK_ROWS - pack.shape[0]), (0, 0)))


def _unpack_small(pack):
    out = {k: pack[i:i + 1] for i, k in enumerate(("g_mix", "b_out", "g_mem_q", "g_mem_kv", "g_ffn", "g_final"))}
    out["b_in"] = jnp.concatenate([pack[6:7], pack[7:8, 0:768]], axis=1)
    out["b_dw"], out["g_conv_ln"] = pack[8:9, 0:512], pack[8:9, 512:1024]
    out["b_conv_ln"] = pack[9:10, 0:512]
    out["attn_sink"] = pack[10:11, 0:8]
    loss = pack[11, 0]
    dw = pack[12:28].reshape(32, 512)[0:CONV_W]
    return loss, out, dw


def kernel(x, mem, g_mix, w_in, b_in, w_dw, b_dw, g_conv_ln, b_conv_ln, attn_sink, w_out, b_out, g_mem_q, g_mem_kv, w_mem_q, w_mem_kv, w_mem_o, g_ffn, w_gate, w_up, w_down, g_final, loss_target, m_g_mix, m_w_in, m_b_in, m_w_dw, m_b_dw, m_g_conv_ln, m_b_conv_ln, m_attn_sink, m_w_out, m_b_out, m_g_mem_q, m_g_mem_kv, m_w_mem_q, m_w_mem_kv, m_w_mem_o, m_g_ffn, m_w_gate, m_w_up, m_w_down, m_g_final, v_g_mix, v_w_in, v_b_in, v_w_dw, v_b_dw, v_g_conv_ln, v_b_conv_ln, v_attn_sink, v_w_out, v_b_out, v_g_mem_q, v_g_mem_kv, v_w_mem_q, v_w_mem_kv, v_w_mem_o, v_g_ffn, v_w_gate, v_w_up, v_w_down, v_g_final):
    args = dict(locals())
    weight_names = ["g_mix", "w_in", "b_in", "w_dw", "b_dw", "g_conv_ln", "b_conv_ln", "attn_sink", "w_out", "b_out", "g_mem_q",
                    "g_mem_kv", "w_mem_q", "w_mem_kv", "w_mem_o", "g_ffn", "w_gate", "w_up", "w_down", "g_final"]
    chip = 2 * lax.axis_index("x") + lax.axis_index("y")
    core = lax.axis_index("c")

    chip_idx = chip.astype(jnp.int32).reshape(1)
    core_idx = core.astype(jnp.int32).reshape(1)

    def block(name):
        a = args[name][0]
        weight = name[2:] if name[:2] in ("m_", "v_") else name
        return a.T if weight in TRANSPOSED else a

    blocks = {k: block(k) for k in BIG}
    bufs = {}
    for group in _by_shape(BIG, blocks):
        bufs.update(zip(group, _cast_place([blocks[k] for k in group], chip_idx, ROWS_PER_STEP)))
    comm = _Overlap(bufs, chip_idx, core_idx)
    dw_buf = lax.dynamic_update_slice(jnp.zeros((N_CHIPS, CONV_W, 128), F32), w_dw, (chip, 0, 0))
    (first,), (dw_all,) = _run_rides("gather_first", [_gather_ride([comm.bufs["w_in"]]), _spread_ride(dw_buf)])
    wg = {"w_in": _as_weight("w_in", first), "w_dw": jnp.transpose(dw_all, (1, 0, 2)).reshape(CONV_W, CONV_CH)}
    small = {k: args[k].reshape(1, -1) for k in SMALL}

    loss, grad_x, grads = _local_step(x[0], mem[0], loss_target[0], small, wg, comm)

    halves, other_halves, pack_sum = comm.finish()

    loss_sum, small_grads, dw_full = _unpack_small(pack_sum)
    dw_cols = jnp.transpose(dw_full.reshape(CONV_W, N_CHIPS, 128), (1, 0, 2))
    small_grads["w_dw"] = lax.dynamic_index_in_dim(dw_cols, chip, axis=0, keepdims=False)

    out_g, out_d, out_m, out_v = {}, {}, {}, {}
    for k, g_mine, g_other in zip(BIG, halves, other_halves):
        res = _adamw(block(k), g_mine, g_other, block("m_" + k), block("v_" + k), core_idx, ROWS_PER_STEP)
        out_g[k], out_d[k], out_m[k], out_v[k] = [(r.T if k in TRANSPOSED else r)[None] for r in res]
    names = SMALL + ["w_dw"]

    def flat(a):
        return a[0] if a.ndim == 3 else a.reshape(1, -1)

    def pad_lanes(a):
        return jnp.pad(a, ((0, 0), (0, 128 - a.shape[1]))) if a.shape[1] < 128 else a

    ws = [flat(args[k]) for k in names]
    gs = [small_grads[k] for k in names]
    ms = [flat(args["m_" + k]) for k in names]
    vs = [flat(args["v_" + k]) for k in names]
    ds, mns, vns = _adamw_small([pad_lanes(a) for a in ws], [pad_lanes(a) for a in gs], [pad_lanes(a) for a in ms],
                                [pad_lanes(a) for a in vs])
    for i, k in enumerate(names):
        n_lanes = ws[i].shape[1]
        for out, val in ((out_g, gs[i]), (out_d, ds[i]), (out_m, mns[i]), (out_v, vns[i])):
            out[k] = val[:, 0:n_lanes].reshape(args[k].shape)

    return (loss_sum, grad_x[None], *[out_g[k] for k in weight_names], *[out_d[k] for k in weight_names],
            *[out_m[k] for k in weight_names], *[out_v[k] for k in weight_names])
```

```python
import jax
import jax.numpy as jnp
import numpy as np
from jax import lax
from jax.experimental import pallas as pl
from jax.experimental.pallas import tpu as pltpu

F32 = jnp.float32
BF16 = jnp.bfloat16
EPS = 1e-6
NEG = -1e30

D_MODEL = 1024
CONV_CH = 512
CONV_W = 31
HEAD_DIM = 64
BLK = 128
MEM_HEADS = 4
MEM_HD = 256
N_CHIPS = 4
N_DEV = 8
ATT_SCALE = HEAD_DIM ** -0.5
MEM_SCALE = MEM_HD ** -0.5
ROPE_THETA = 10000.0

ADAM_LR = 0.001
ADAM_B1 = 0.9
ADAM_B2 = 0.999
ADAM_EPS = 1e-08
ADAM_WD = 0.01
ADAM_STEP = 10

VMEM_LIMIT_BYTES = 56 * 1024 * 1024
MESH = pl.DeviceIdType.MESH


class _Ride:
    def __init__(self, operands, out_shape, n_sem, start, finish, aliases=None):
        self.operands, self.out_shape, self.n_sem = list(operands), list(out_shape), n_sem
        self.start, self.finish, self.aliases = start, finish, dict(aliases or {})


def _call(body, rides=(), **kw):
    if not rides:
        return pl.pallas_call(body, **kw)
    grid = kw["grid"]
    n_in, n_out = len(kw["in_specs"]), len(kw["out_specs"])
    scratch = list(kw.get("scratch_shapes", ()))
    k_in = [len(r.operands) for r in rides]
    k_out = [len(r.out_shape) for r in rides]

    def carried(*refs):
        pos = n_in
        r_in, r_out = [], []
        for k in k_in:
            r_in.append(refs[pos:pos + k])
            pos += k
        own_out = refs[pos:pos + n_out]
        pos += n_out
        for k in k_out:
            r_out.append(refs[pos:pos + k])
            pos += k
        own_scratch = refs[pos:pos + len(scratch)]
        sems = refs[pos + len(scratch):]
        first = last = None
        for axis, n_steps in enumerate(grid):
            step = pl.program_id(axis)
            first = (step == 0) if first is None else first & (step == 0)
            last = (step == n_steps - 1) if last is None else last & (step == n_steps - 1)

        @pl.when(first)
        def _():
            for j, r in enumerate(rides):
                r.start(r_in[j], r_out[j], sems[2 * j], sems[2 * j + 1])

        body(*refs[:n_in], *own_out, *own_scratch)

        @pl.when(last)
        def _():
            for j, r in enumerate(rides):
                r.finish(r_in[j], r_out[j], sems[2 * j], sems[2 * j + 1])

    kw = dict(kw)
    kw["in_specs"] = list(kw["in_specs"]) + [ANY] * sum(k_in)
    kw["out_specs"] = list(kw["out_specs"]) + [ANY] * sum(k_out)
    kw["out_shape"] = list(kw["out_shape"]) + [s for r in rides for s in r.out_shape]
    kw["scratch_shapes"] = scratch + [pltpu.SemaphoreType.DMA((r.n_sem,)) for r in rides for _ in range(2)]
    aliases, off_in, off_out = {}, n_in, n_out
    for r, ki, ko in zip(rides, k_in, k_out):
        aliases.update({off_in + a: off_out + b for a, b in r.aliases.items()})
        off_in, off_out = off_in + ki, off_out + ko
    if aliases:
        kw["input_output_aliases"] = aliases
    call = pl.pallas_call(carried, **kw)
    return lambda *args: call(*args, *[op for r in rides for op in r.operands])


def _cp(n_grid):
    return pltpu.CompilerParams(dimension_semantics=("arbitrary",) * n_grid, vmem_limit_bytes=VMEM_LIMIT_BYTES)


def _res(shape):
    nd = len(shape)
    return pl.BlockSpec(shape, lambda *_: (0,) * nd, pipeline_mode=pl.Buffered(1))


def _rows(tm, n):
    return pl.BlockSpec((tm, n), lambda i: (i, 0))


def _div_tile(n, target):
    best = None
    for d in range(16, min(n, target) + 1, 16):
        if n % d == 0:
            best = d
    assert best is not None, (n, target)
    return best


def _dot(a, b):
    return jnp.dot(a, b, preferred_element_type=F32)


def _dot_nt(a, b):
    return lax.dot_general(a, b, (((1,), (1,)), ((), ())), preferred_element_type=F32)


def _dot_tn(a, b):
    return lax.dot_general(a, b, (((0,), (0,)), ((), ())), preferred_element_type=F32)


def _bf(x):
    return x.astype(BF16)


def _sigmoid(x):
    return 1.0 / (1.0 + jnp.exp(-x))


def _rms_fwd(x, g):
    r = lax.rsqrt(jnp.mean(x * x, axis=-1, keepdims=True) + EPS)
    xh = x * r
    return xh * g, xh, r


def _rms_bwd(dh, xh, r, g):
    dxh = dh * g
    return r * (dxh - xh * jnp.mean(dxh * xh, axis=-1, keepdims=True))


def _colsum(x):
    return jnp.sum(x, axis=0, keepdims=True)


def _rope(x, cos, sin, sign):
    n = x.shape[1] // 128
    c = jnp.tile(cos, (1, n)) if n > 1 else cos
    s = jnp.tile(sin, (1, n)) if n > 1 else sin
    lane = lax.broadcasted_iota(jnp.int32, x.shape, 1)
    first = (lane & 63) < 32
    partner = jnp.where(first, pltpu.roll(x, x.shape[1] - 32, 1), pltpu.roll(x, 32, 1))
    return x * c + sign * (partner * s)


def _lo_lanes(shape):
    return lax.broadcasted_iota(jnp.int32, shape, 1) < 64


def _stack_heads(t):
    t0, t1 = t[:, 0:128], t[:, 128:256]
    lo = _lo_lanes(t0.shape)
    z = jnp.zeros_like(t0)
    return jnp.concatenate([jnp.where(lo, t0, z), jnp.where(lo, z, t0), jnp.where(lo, t1, z), jnp.where(lo, z, t1)], axis=0)


def _unstack_heads(o):
    lo = _lo_lanes((BLK, 128))
    return jnp.concatenate([jnp.where(lo, o[0:128], o[128:256]), jnp.where(lo, o[256:384], o[384:512])], axis=1)


def _fold_heads(parts):
    a, b = (p + pltpu.roll(p, 64, 1) for p in parts)
    return jnp.where(_lo_lanes(a.shape), a, b)


def _sink_col(sk_ref, g):
    return jnp.concatenate([jnp.broadcast_to(sk_ref[4 * g + h:4 * g + h + 1, :], (BLK, 128)) for h in range(4)], axis=0)


def _tile3(x):
    return jnp.concatenate([x, x, x], axis=1)


def _mem_kv_fwd(mem, g_kv, w_kv):
    m_len = mem.shape[0]
    cols = w_kv.shape[2]

    def body(mem_ref, g_ref, w_ref, memn_ref, kv_ref):
        h, _, _ = _rms_fwd(mem_ref[...], g_ref[...])
        hb = _bf(h)
        memn_ref[...] = hb
        for s in range(N_CHIPS):
            kv_ref[s] = _bf(_dot(hb, w_ref[s]))

    return _call(
        body, name="mem_kv_fwd",
        out_shape=(jax.ShapeDtypeStruct((m_len, D_MODEL), BF16), jax.ShapeDtypeStruct((N_CHIPS, m_len, cols), BF16)),
        compiler_params=pltpu.CompilerParams(vmem_limit_bytes=VMEM_LIMIT_BYTES),
    )(mem, g_kv, w_kv)


def _dup_head_rows(w_ref, lo):
    h0, h1 = w_ref[lo:lo + 64, :], w_ref[lo + 64:lo + 128, :]
    return jnp.concatenate([h0, h0, h1, h1], axis=0)


def _in_proj_fwd(x, g_mix, w_t, b_ext, cos_t, sin_t, tm, rides=()):
    t_len = x.shape[0]

    def body(x_ref, g_ref, w_ref, b_ref, c_ref, s_ref, ug_ref, q_ref, k_ref, v_ref, h_ref):
        h, _, _ = _rms_fwd(x_ref[...], g_ref[...])
        hb = _bf(h)
        h_ref[...] = hb
        ug_ref[...] = _dot_nt(hb, w_ref[0:1024, :]) + b_ref[:, 0:1024]
        c, s = c_ref[...], s_ref[...]
        q_ref[...] = _bf(_rope(_dot_nt(hb, w_ref[1024:1536, :]) + b_ref[:, 1024:1536], c, s, 1.0))
        k_ref[...] = _bf(_rope(_dot_nt(hb, _dup_head_rows(w_ref, 1536)) + b_ref[:, 1536:1792], c, s, 1.0))
        v_ref[...] = _bf(_dot_nt(hb, _dup_head_rows(w_ref, 1664)) + b_ref[:, 1792:2048])

    return _call(
        body, rides=rides, name="in_proj_fwd", grid=(t_len // tm,),
        in_specs=[_rows(tm, D_MODEL), _res((1, D_MODEL)), _res(w_t.shape), _res(b_ext.shape), _rows(tm, 128), _rows(tm, 128)],
        out_specs=[_rows(tm, 1024), _rows(tm, 512), _rows(tm, 256), _rows(tm, 256), _rows(tm, D_MODEL)],
        out_shape=(jax.ShapeDtypeStruct((t_len, 1024), F32), jax.ShapeDtypeStruct((t_len, 512), BF16),
                   jax.ShapeDtypeStruct((t_len, 256), BF16), jax.ShapeDtypeStruct((t_len, 256), BF16),
                   jax.ShapeDtypeStruct((t_len, D_MODEL), BF16)),
        compiler_params=_cp(1),
    )(x, g_mix, w_t, b_ext, cos_t, sin_t)


def _halo_specs(tc, n, t_len):
    per = tc // 16
    last = t_len // 16 - 1
    return [pl.BlockSpec((16, n), lambda i: (jnp.maximum(i * per - 1, 0), 0)),
            pl.BlockSpec((tc, n), lambda i: (i, 0)),
            pl.BlockSpec((16, n), lambda i: (jnp.minimum((i + 1) * per, last), 0))]


def _glu(z):
    return z[:, 0:CONV_CH] * _sigmoid(z[:, CONV_CH:2 * CONV_CH])


def _fill_halo_buf(buf, prev, main, nxt, i, n_tiles, tc):
    buf[0:16, :] = jnp.where(i > 0, prev, jnp.zeros_like(prev))
    buf[16:16 + tc, :] = main
    buf[16 + tc:32 + tc, :] = jnp.where(i < n_tiles - 1, nxt, jnp.zeros_like(nxt))


CONV_ROWS = 64


def _shift_copies(buf, shifted, tc):
    for r in range(1, 8):
        shifted[r - 1, :, :] = buf[r:r + tc + 24, :]


def _shifted_rows(buf, shifted, offset, base):
    src = buf if offset % 8 == 0 else shifted.at[offset % 8 - 1]
    return src[pl.ds(pl.multiple_of(base + 8 * (offset // 8), 8), CONV_ROWS), :]


def _conv_fwd(ug, w_dw, b_dw, g_ln, b_ln, tc, rides=()):
    t_len = ug.shape[0]
    n_tiles = t_len // tc

    def body(up_ref, um_ref, un_ref, w_ref, bdw_ref, g_ref, b_ref, y_ref, pre_ref, buf, shifted):
        i = pl.program_id(0)
        _fill_halo_buf(buf, _glu(up_ref[...]), _glu(um_ref[...]), _glu(un_ref[...]), i, n_tiles, tc)
        _shift_copies(buf, shifted, tc)

        def chunk(c, carry):
            base = c * CONV_ROWS
            acc = jnp.zeros((CONV_ROWS, CONV_CH), F32)
            for k in range(CONV_W):
                acc = acc + w_ref[k:k + 1, :] * _shifted_rows(buf, shifted, k + 1, base)
            pre_ref[pl.ds(pl.multiple_of(base, CONV_ROWS), CONV_ROWS), :] = acc + bdw_ref[...]
            return carry

        lax.fori_loop(0, tc // CONV_ROWS, chunk, 0)
        pre = pre_ref[...]
        mu = jnp.mean(pre, axis=-1, keepdims=True)
        d = pre - mu
        rstd = lax.rsqrt(jnp.mean(d * d, axis=-1, keepdims=True) + EPS)
        ln = d * rstd * g_ref[...] + b_ref[...]
        y_ref[...] = _bf(ln * _sigmoid(ln))

    return _call(
        body, rides=rides, name="conv_fwd", grid=(n_tiles,),
        in_specs=_halo_specs(tc, 1024, t_len) + [_res((32, CONV_CH)), _res((1, CONV_CH)), _res((1, CONV_CH)), _res((1, CONV_CH))],
        out_specs=[_rows(tc, CONV_CH), _rows(tc, CONV_CH)],
        out_shape=(jax.ShapeDtypeStruct((t_len, CONV_CH), BF16), jax.ShapeDtypeStruct((t_len, CONV_CH), F32)),
        scratch_shapes=[pltpu.VMEM((tc + 32, CONV_CH), F32), pltpu.VMEM((7, tc + 24, CONV_CH), F32)],
        compiler_params=_cp(1),
    )(ug, ug, ug, w_dw, b_dw, g_ln, b_ln)


def _nbr_specs(n, nb):
    return [pl.BlockSpec((BLK, n), lambda i: (jnp.maximum(i - 1, 0), 0)),
            pl.BlockSpec((BLK, n), lambda i: (i, 0)),
            pl.BlockSpec((BLK, n), lambda i: (jnp.minimum(i + 1, nb - 1), 0))]


def _nbr_specs4(nb):
    return [pl.BlockSpec((1, 2, 4 * BLK, 128), lambda i: (jnp.maximum(i - 1, 0), 0, 0, 0)),
            pl.BlockSpec((1, 2, 4 * BLK, 128), lambda i: (i, 0, 0, 0)),
            pl.BlockSpec((1, 2, 4 * BLK, 128), lambda i: (jnp.minimum(i + 1, nb - 1), 0, 0, 0))]


def _band_bias():
    a = np.arange(4 * BLK)[:, None] % BLK
    c = np.arange(3 * BLK)[None, :]
    inside = np.abs(c - BLK - a) <= BLK
    q_side = np.stack([inside & (c >= BLK), inside, inside & (c < 2 * BLK)])
    blk = np.arange(12 * BLK)[:, None] // (4 * BLK)
    a = np.arange(12 * BLK)[:, None] % BLK
    c = np.arange(BLK)[None, :]
    inside = np.abs(c - a + (1 - blk) * BLK) <= BLK
    k_side = np.stack([inside & (blk >= 1), inside, inside & (blk <= 1)])
    return [jnp.asarray(np.where(m, 0.0, NEG).astype(np.float32)) for m in (q_side, k_side)]


def _edge_spec(shape, nb):
    return pl.BlockSpec((1,) + shape, lambda i: (jnp.where(i == 0, 0, jnp.where(i == nb - 1, 2, 1)),) + (0,) * len(shape))


def _attn_fwd(q, kd, vd, sink_b, bias, rides=()):
    t_len = q.shape[0]
    nb = t_len // BLK

    def body(q_ref, kp_ref, kc_ref, kn_ref, vp_ref, vc_ref, vn_ref, sk_ref, bias_ref, y_ref, lse_ref):
        kcat = jnp.concatenate([kp_ref[...], kc_ref[...], kn_ref[...]], axis=0)
        vcat = jnp.concatenate([vp_ref[...], vc_ref[...], vn_ref[...]], axis=0)
        ys = []
        for g in range(2):
            qs = _stack_heads(q_ref[:, 256 * g:256 * g + 256])
            s = _dot_nt(qs, kcat[:, 128 * g:128 * g + 128]) * ATT_SCALE + bias_ref[0]
            skc = _sink_col(sk_ref, g)
            m_b = jnp.maximum(jnp.max(s, axis=-1, keepdims=True), skc)
            p = jnp.exp(s - _tile3(m_b))
            den_b = jnp.sum(p, axis=-1, keepdims=True) + jnp.exp(skc - m_b)
            pn = p * _tile3(1.0 / den_b)
            o = _dot(_bf(pn), vcat[:, 128 * g:128 * g + 128])
            ys.append(_unstack_heads(o))
            lse_ref[0, g] = m_b + jnp.log(den_b)
        y_ref[...] = _bf(jnp.concatenate(ys, axis=1))

    return _call(
        body, rides=rides, name="attn_fwd", grid=(nb,),
        in_specs=[_rows(BLK, 512)] + _nbr_specs(256, nb) + _nbr_specs(256, nb) + [_res((8, 128)), _edge_spec((4 * BLK, 3 * BLK), nb)],
        out_specs=[_rows(BLK, 512), pl.BlockSpec((1, 2, 4 * BLK, 128), lambda i: (i, 0, 0, 0))],
        out_shape=(jax.ShapeDtypeStruct((t_len, 512), BF16), jax.ShapeDtypeStruct((nb, 2, 4 * BLK, 128), F32)),
        compiler_params=_cp(1),
    )(q, kd, kd, kd, vd, vd, vd, sink_b, bias)


def _mem_heads(kv_ref, h):
    lo = MEM_HD * (h % 2)
    return kv_ref[h // 2, :, lo:lo + MEM_HD], kv_ref[2 + h // 2, :, lo:lo + MEM_HD]


def _mix_mem_fwd(x, yc, ya, w_out, b_out, g_q, w_q, kv, w_o, tm, rides=()):
    t_len = x.shape[0]

    def body(x_ref, yc_ref, ya_ref, wout_ref, bout_ref, g_ref, wq_ref, kv_ref, wo_ref,
             ymix_ref, x1_ref, hq_ref, qm_ref, om_ref, x2_ref):
        ymix = jnp.concatenate([yc_ref[...], ya_ref[...]], axis=1)
        ymix_ref[...] = ymix
        x1 = x_ref[...] + _dot(ymix, wout_ref[...]) + bout_ref[...]
        x1_ref[...] = x1
        hq, _, _ = _rms_fwd(x1, g_ref[...])
        hqb = _bf(hq)
        hq_ref[...] = hqb
        qm = _bf(_dot(hqb, wq_ref[...]))
        qm_ref[...] = qm
        outs = []
        for h in range(MEM_HEADS):
            kh, vh = _mem_heads(kv_ref, h)
            s = _dot_nt(qm[:, MEM_HD * h:MEM_HD * (h + 1)], kh) * MEM_SCALE
            p = jnp.exp(s - jnp.max(s, axis=-1, keepdims=True))
            p = p * (1.0 / jnp.sum(p, axis=-1, keepdims=True))
            outs.append(_dot(_bf(p), vh))
        om = _bf(jnp.concatenate(outs, axis=1))
        om_ref[...] = om
        x2_ref[...] = x1 + _dot(om, wo_ref[...])

    act_b = jax.ShapeDtypeStruct((t_len, D_MODEL), BF16)
    act_f = jax.ShapeDtypeStruct((t_len, D_MODEL), F32)
    return _call(
        body, rides=rides, name="mix_mem_fwd", grid=(t_len // tm,),
        in_specs=[_rows(tm, D_MODEL), _rows(tm, 512), _rows(tm, 512), _res(w_out.shape), _res((1, D_MODEL)), _res((1, D_MODEL)),
                  _res(w_q.shape), _res(kv.shape), _res(w_o.shape)],
        out_specs=[_rows(tm, D_MODEL)] * 6,
        out_shape=(act_b, act_f, act_b, act_b, act_b, act_f),
        compiler_params=_cp(1),
    )(x, yc, ya, w_out, b_out, g_q, w_q, kv, w_o)


def _hidden_chunks(ff, width=1024):
    return [(lo, min(lo + width, ff)) for lo in range(0, ff, width)]


def _ffn_up(x2, g_ffn, w_gate, w_up, tm, rides=()):
    t_len = x2.shape[0]
    ff = w_gate.shape[0]

    def body(x2_ref, g_ref, wg_ref, wu_ref, hf_ref, gate_ref, up_ref, act_ref):
        hf, _, _ = _rms_fwd(x2_ref[...], g_ref[...])
        hfb = _bf(hf)
        hf_ref[...] = hfb
        for lo, hi in _hidden_chunks(ff):
            gate = _dot_nt(hfb, wg_ref[lo:hi, :])
            up = _dot_nt(hfb, wu_ref[lo:hi, :])
            gate_ref[:, lo:hi] = _bf(gate)
            up_ref[:, lo:hi] = _bf(up)
            act_ref[:, lo:hi] = _bf(gate * _sigmoid(gate) * up)

    hid = jax.ShapeDtypeStruct((t_len, ff), BF16)
    return _call(
        body, rides=rides, name="ffn_up", grid=(t_len // tm,),
        in_specs=[_rows(tm, D_MODEL), _res((1, D_MODEL)), _res(w_gate.shape), _res(w_up.shape)],
        out_specs=[_rows(tm, D_MODEL), _rows(tm, ff), _rows(tm, ff), _rows(tm, ff)],
        out_shape=[jax.ShapeDtypeStruct((t_len, D_MODEL), BF16), hid, hid, hid],
        compiler_params=_cp(1),
    )(x2, g_ffn, w_gate, w_up)


def _ffn_down_loss(x2, act, w_down, g_final, target, tm):
    t_len = x2.shape[0]
    ff = w_down.shape[0]

    def body(x2_ref, act_ref, wd_ref, gf_ref, tgt_ref, dx3_ref, loss_ref, dgf_ref):
        i = pl.program_id(0)
        x3 = x2_ref[...]
        for lo, hi in _hidden_chunks(ff):
            x3 = x3 + _dot(act_ref[:, lo:hi], wd_ref[lo:hi, :])
        gf = gf_ref[...]
        y, xh, r = _rms_fwd(x3, gf)
        err = y - tgt_ref[...]
        part = 0.5 * jnp.sum(jnp.mean(err * err, axis=-1, keepdims=True), axis=0, keepdims=True)
        dy = err * (1.0 / D_MODEL)
        dx3_ref[...] = _rms_bwd(dy, xh, r, gf)

        @pl.when(i == 0)
        def _():
            loss_ref[...] = jnp.zeros_like(loss_ref)
            dgf_ref[...] = jnp.zeros_like(dgf_ref)

        loss_ref[...] += jnp.broadcast_to(part, loss_ref.shape)
        dgf_ref[...] += _colsum(dy * xh)

    vec = pl.BlockSpec((1, D_MODEL), lambda i: (0, 0))
    return _call(
        body, name="ffn_down_loss", grid=(t_len // tm,),
        in_specs=[_rows(tm, D_MODEL), _rows(tm, ff), _res(w_down.shape), _res((1, D_MODEL)), _rows(tm, D_MODEL)],
        out_specs=[_rows(tm, D_MODEL), vec, vec],
        out_shape=(jax.ShapeDtypeStruct((t_len, D_MODEL), F32), jax.ShapeDtypeStruct((1, D_MODEL), F32),
                   jax.ShapeDtypeStruct((1, D_MODEL), F32)),
        compiler_params=_cp(1),
    )(x2, act, w_down, g_final, target)


def _ffn_bwd(dx3, x2, gate, up, g_ffn, w_gate, w_up, w_down, tm):
    t_len = x2.shape[0]
    ff = w_gate.shape[0]

    def body(dx3_ref, x2_ref, gate_ref, up_ref, g_ref, wg_ref, wu_ref, wd_ref, dx2_ref, dgate_ref, dup_ref, dg_ref):
        i = pl.program_id(0)
        dx3 = dx3_ref[...]
        d3b = _bf(dx3)
        dh = jnp.zeros((tm, D_MODEL), F32)
        for lo, hi in _hidden_chunks(ff):
            dact = _dot_nt(d3b, wd_ref[lo:hi, :])
            gt = gate_ref[:, lo:hi].astype(F32)
            u = up_ref[:, lo:hi].astype(F32)
            sg = _sigmoid(gt)
            dup = _bf(dact * (gt * sg))
            dgate = _bf(dact * u * (sg * (1.0 + gt * (1.0 - sg))))
            dup_ref[:, lo:hi] = dup
            dgate_ref[:, lo:hi] = dgate
            dh = dh + _dot(dgate, wg_ref[lo:hi, :]) + _dot(dup, wu_ref[lo:hi, :])
        g = g_ref[...]
        _, xh, r = _rms_fwd(x2_ref[...], g)
        dx2_ref[...] = dx3 + _rms_bwd(dh, xh, r, g)

        @pl.when(i == 0)
        def _():
            dg_ref[...] = jnp.zeros_like(dg_ref)

        dg_ref[...] += _colsum(dh * xh)

    hid = jax.ShapeDtypeStruct((t_len, ff), BF16)
    hid_spec = _rows(tm, ff)
    return _call(
        body, name="ffn_bwd", grid=(t_len // tm,),
        in_specs=[_rows(tm, D_MODEL), _rows(tm, D_MODEL), hid_spec, hid_spec, _res((1, D_MODEL)),
                  _res(w_gate.shape), _res(w_up.shape), _res(w_down.shape)],
        out_specs=[_rows(tm, D_MODEL), hid_spec, hid_spec, pl.BlockSpec((1, D_MODEL), lambda i: (0, 0))],
        out_shape=(jax.ShapeDtypeStruct((t_len, D_MODEL), F32), hid, hid, jax.ShapeDtypeStruct((1, D_MODEL), F32)),
        compiler_params=_cp(1),
    )(dx3, x2, gate, up, g_ffn, w_gate, w_up, w_down)


def _mix_mem_bwd(dx2, x1, qm, kv, g_q, w_q, w_o, w_out, tm, rides=()):
    t_len = x1.shape[0]
    m_len = kv.shape[1]

    def body(dx2_ref, x1_ref, qm_ref, kv_ref, g_ref, wq_ref, wo_ref, wout_ref,
             dx1_ref, dqm_ref, dyc_ref, dya_ref, dkv_ref, dgq_ref, dbout_ref):
        i = pl.program_id(0)

        @pl.when(i == 0)
        def _():
            dkv_ref[...] = jnp.zeros_like(dkv_ref)
            dgq_ref[...] = jnp.zeros_like(dgq_ref)
            dbout_ref[...] = jnp.zeros_like(dbout_ref)

        dx2 = dx2_ref[...]
        dom = _dot_nt(_bf(dx2), wo_ref[...])
        dqs = []
        for h in range(MEM_HEADS):
            kh, vh = _mem_heads(kv_ref, h)
            qh = qm_ref[:, MEM_HD * h:MEM_HD * (h + 1)]
            s = _dot_nt(qh, kh) * MEM_SCALE
            p = jnp.exp(s - jnp.max(s, axis=-1, keepdims=True))
            p = p * (1.0 / jnp.sum(p, axis=-1, keepdims=True))
            domh = _bf(dom[:, MEM_HD * h:MEM_HD * (h + 1)])
            dp = _dot_nt(domh, vh)
            ds = _bf(p * (dp - jnp.sum(p * dp, axis=-1, keepdims=True)) * MEM_SCALE)
            dqs.append(_dot(ds, kh))
            lo = MEM_HD * (h % 2)
            dkv_ref[h // 2, :, lo:lo + MEM_HD] += _dot_tn(ds, qh)
            dkv_ref[2 + h // 2, :, lo:lo + MEM_HD] += _dot_tn(_bf(p), domh)
        dqm = _bf(jnp.concatenate(dqs, axis=1))
        dqm_ref[...] = dqm
        dhq = _dot_nt(dqm, wq_ref[...])
        g = g_ref[...]
        _, xh, r = _rms_fwd(x1_ref[...], g)
        dx1 = dx2 + _rms_bwd(dhq, xh, r, g)
        dx1_ref[...] = dx1
        dgq_ref[...] += _colsum(dhq * xh)
        dbout_ref[...] += _colsum(dx1)
        dymix = _dot_nt(_bf(dx1), wout_ref[...])
        dyc_ref[...] = dymix[:, 0:CONV_CH]
        dya_ref[...] = _bf(dymix[:, CONV_CH:2 * CONV_CH])

    vec = pl.BlockSpec((1, D_MODEL), lambda i: (0, 0))
    return _call(
        body, rides=rides, name="mix_mem_bwd", grid=(t_len // tm,),
        in_specs=[_rows(tm, D_MODEL), _rows(tm, D_MODEL), _rows(tm, D_MODEL), _res(kv.shape), _res((1, D_MODEL)),
                  _res(w_q.shape), _res(w_o.shape), _res(w_out.shape)],
        out_specs=[_rows(tm, D_MODEL), _rows(tm, D_MODEL), _rows(tm, CONV_CH), _rows(tm, CONV_CH),
                   pl.BlockSpec(kv.shape, lambda i: (0, 0, 0)), vec, vec],
        out_shape=(jax.ShapeDtypeStruct((t_len, D_MODEL), F32), jax.ShapeDtypeStruct((t_len, D_MODEL), BF16),
                   jax.ShapeDtypeStruct((t_len, CONV_CH), F32), jax.ShapeDtypeStruct((t_len, CONV_CH), BF16),
                   jax.ShapeDtypeStruct((N_CHIPS, m_len, kv.shape[2]), F32),
                   jax.ShapeDtypeStruct((1, D_MODEL), F32), jax.ShapeDtypeStruct((1, D_MODEL), F32)),
        compiler_params=_cp(1),
    )(dx2, x1, qm, kv, g_q, w_q, w_o, w_out)


def _mem_kv_bwd(dkv, memn, mem, g_kv, w_kv):
    m_len = mem.shape[0]

    def body(dkv_ref, memn_ref, mem_ref, g_ref, w_ref, dw_ref, dg_ref):
        hb = memn_ref[...]
        dmn = jnp.zeros((m_len, D_MODEL), F32)
        for s in range(N_CHIPS):
            d = _bf(dkv_ref[s])
            dw_ref[s] = _bf(_dot_tn(hb, d))
            dmn = dmn + _dot_nt(d, w_ref[s])
        _, xh, _ = _rms_fwd(mem_ref[...], g_ref[...])
        dg_ref[...] = _colsum(dmn * xh)

    return _call(
        body, name="mem_kv_bwd",
        out_shape=(jax.ShapeDtypeStruct(w_kv.shape, BF16), jax.ShapeDtypeStruct((1, D_MODEL), F32)),
        compiler_params=pltpu.CompilerParams(vmem_limit_bytes=VMEM_LIMIT_BYTES),
    )(dkv, memn, mem, g_kv, w_kv)


def _attn_bwd_q(q, kd, vd, dya, lse, sink_b, bias, cos_t, sin_t, rides=()):
    t_len = q.shape[0]
    nb = t_len // BLK

    def body(q_ref, kp_ref, kc_ref, kn_ref, vp_ref, vc_ref, vn_ref, do_ref, lse_ref, sk_ref, bias_ref, c_ref, s_ref,
             dq_ref, dd_ref, dsk_ref):
        kcat = jnp.concatenate([kp_ref[...], kc_ref[...], kn_ref[...]], axis=0)
        vcat = jnp.concatenate([vp_ref[...], vc_ref[...], vn_ref[...]], axis=0)
        dqs, dsks = [], []
        for g in range(2):
            qs = _stack_heads(q_ref[:, 256 * g:256 * g + 256])
            dos = _stack_heads(do_ref[:, 256 * g:256 * g + 256])
            kk = kcat[:, 128 * g:128 * g + 128]
            s = _dot_nt(qs, kk) * ATT_SCALE + bias_ref[0]
            lse_b = lse_ref[0, g]
            p = jnp.exp(s - _tile3(lse_b))
            dp = _dot_nt(dos, vcat[:, 128 * g:128 * g + 128])
            drow = jnp.sum(p * dp, axis=-1, keepdims=True)
            ds = _bf(p * (dp - drow) * ATT_SCALE)
            dqs.append(_unstack_heads(_dot(ds, kk)))
            d_b = jnp.broadcast_to(drow, (4 * BLK, 128))
            dd_ref[0, g] = d_b
            contrib = -(jnp.exp(_sink_col(sk_ref, g) - lse_b) * d_b)
            dsks.append(jnp.sum(contrib.reshape(4, BLK, 128), axis=1))
        dq = jnp.concatenate(dqs, axis=1)
        dq_ref[...] = _bf(_rope(dq, c_ref[...], s_ref[...], -1.0))
        dsk_ref[0] = jnp.concatenate(dsks, axis=0)

    stat = pl.BlockSpec((1, 2, 4 * BLK, 128), lambda i: (i, 0, 0, 0))
    return _call(
        body, rides=rides, name="attn_bwd_q", grid=(nb,),
        in_specs=[_rows(BLK, 512)] + _nbr_specs(256, nb) + _nbr_specs(256, nb)
        + [_rows(BLK, 512), stat, _res((8, 128)), _edge_spec((4 * BLK, 3 * BLK), nb), _rows(BLK, 128), _rows(BLK, 128)],
        out_specs=[_rows(BLK, 512), stat, pl.BlockSpec((1, 8, 128), lambda i: (i, 0, 0))],
        out_shape=(jax.ShapeDtypeStruct((t_len, 512), BF16), jax.ShapeDtypeStruct((nb, 2, 4 * BLK, 128), F32),
                   jax.ShapeDtypeStruct((nb, 8, 128), F32)),
        compiler_params=_cp(1),
    )(q, kd, kd, kd, vd, vd, vd, dya, lse, sink_b, bias, cos_t, sin_t)


def _attn_bwd_kv(q, kd, vd, dya, lse, dd, bias, cos_t, sin_t, rides=()):
    t_len = q.shape[0]
    nb = t_len // BLK

    def body(kc_ref, vc_ref, qp_ref, qc_ref, qn_ref, dop_ref, doc_ref, don_ref, lp_ref, lc_ref, ln_ref,
             dp_ref, dc_ref, dn_ref, bias_ref, c_ref, s_ref, dk_ref, dv_ref):
        dks, dvs = [], []
        for g in range(2):
            cols = slice(256 * g, 256 * g + 256)
            qs = jnp.concatenate([_stack_heads(r[:, cols]) for r in (qp_ref, qc_ref, qn_ref)], axis=0)
            dos = jnp.concatenate([_stack_heads(r[:, cols]) for r in (dop_ref, doc_ref, don_ref)], axis=0)
            lse_b = jnp.concatenate([r[0, g] for r in (lp_ref, lc_ref, ln_ref)], axis=0)
            d_b = jnp.concatenate([r[0, g] for r in (dp_ref, dc_ref, dn_ref)], axis=0)
            kk = kc_ref[:, 128 * g:128 * g + 128]
            s = _dot_nt(qs, kk) * ATT_SCALE + bias_ref[0]
            p = jnp.exp(s - lse_b)
            dp = _dot_nt(dos, vc_ref[:, 128 * g:128 * g + 128])
            ds = _bf(p * (dp - d_b) * ATT_SCALE)
            dvs.append(_dot_tn(_bf(p), dos))
            dks.append(_dot_tn(ds, qs))
        dk_ref[...] = _bf(_rope(_fold_heads(dks), c_ref[...], s_ref[...], -1.0))
        dv_ref[...] = _bf(_fold_heads(dvs))

    return _call(
        body, rides=rides, name="attn_bwd_kv", grid=(nb,),
        in_specs=[_rows(BLK, 256), _rows(BLK, 256)] + _nbr_specs(512, nb) + _nbr_specs(512, nb) + _nbr_specs4(nb) + _nbr_specs4(nb)
        + [_edge_spec((12 * BLK, BLK), nb), _rows(BLK, 128), _rows(BLK, 128)],
        out_specs=[_rows(BLK, 128), _rows(BLK, 128)],
        out_shape=(jax.ShapeDtypeStruct((t_len, 128), BF16), jax.ShapeDtypeStruct((t_len, 128), BF16)),
        compiler_params=_cp(1),
    )(kd, vd, q, q, q, dya, dya, dya, lse, lse, lse, dd, dd, dd, bias, cos_t, sin_t)


def _conv_norm_bwd(pre, dyc, g_ln, b_ln, tc, rides=()):
    t_len = pre.shape[0]

    def body(pre_ref, dy_ref, g_ref, b_ref, dpre_ref, stats_ref):
        i = pl.program_id(0)
        pre_v = pre_ref[...]
        mu = jnp.mean(pre_v, axis=-1, keepdims=True)
        d = pre_v - mu
        rstd = lax.rsqrt(jnp.mean(d * d, axis=-1, keepdims=True) + EPS)
        xh = d * rstd
        g = g_ref[...]
        ln = xh * g + b_ref[...]
        sg = _sigmoid(ln)
        dln = dy_ref[...] * (sg * (1.0 + ln * (1.0 - sg)))
        dxh = dln * g
        dpre = rstd * (dxh - jnp.mean(dxh, axis=-1, keepdims=True) - xh * jnp.mean(dxh * xh, axis=-1, keepdims=True))
        dpre_ref[...] = dpre

        @pl.when(i == 0)
        def _():
            stats_ref[...] = jnp.zeros_like(stats_ref)

        stats_ref[0:1, :] += _colsum(dln * xh)
        stats_ref[1:2, :] += _colsum(dln)
        stats_ref[2:3, :] += _colsum(dpre)

    return _call(
        body, rides=rides, name="conv_norm_bwd", grid=(t_len // tc,),
        in_specs=[_rows(tc, CONV_CH), _rows(tc, CONV_CH), _res((1, CONV_CH)), _res((1, CONV_CH))],
        out_specs=[_rows(tc, CONV_CH), pl.BlockSpec((8, CONV_CH), lambda i: (0, 0))],
        out_shape=(jax.ShapeDtypeStruct((t_len, CONV_CH), F32), jax.ShapeDtypeStruct((8, CONV_CH), F32)),
        compiler_params=_cp(1),
    )(pre, dyc, g_ln, b_ln)


def _conv_bwd(dpre, ug, w_dw, tc, rides=()):
    t_len = ug.shape[0]
    n_tiles = t_len // tc

    def body(dp_ref, dm_ref, dn_ref, up_ref, um_ref, un_ref, w_ref, du_ref, dw_ref, dbuf, vbuf, dshift, vshift):
        i = pl.program_id(0)
        _fill_halo_buf(dbuf, dp_ref[...], dm_ref[...], dn_ref[...], i, n_tiles, tc)
        _fill_halo_buf(vbuf, _glu(up_ref[...]), _glu(um_ref[...]), _glu(un_ref[...]), i, n_tiles, tc)
        _shift_copies(dbuf, dshift, tc)
        _shift_copies(vbuf, vshift, tc)

        @pl.when(i == 0)
        def _():
            dw_ref[...] = jnp.zeros_like(dw_ref)

        def chunk(c, carry):
            base = c * CONV_ROWS
            rows = pl.ds(pl.multiple_of(base, CONV_ROWS), CONV_ROWS)
            dmain = dm_ref[rows, :]
            dv = jnp.zeros((CONV_ROWS, CONV_CH), F32)
            for k in range(CONV_W):
                dv = dv + w_ref[k:k + 1, :] * _shifted_rows(dbuf, dshift, 31 - k, base)
                prod = dmain * _shifted_rows(vbuf, vshift, k + 1, base)
                dw_ref[8 * k:8 * k + 8, :] += jnp.sum(prod.reshape(CONV_ROWS // 8, 8, CONV_CH), axis=0)
            um = um_ref[rows, :]
            a, gt = um[:, 0:CONV_CH], um[:, CONV_CH:2 * CONV_CH]
            sg = _sigmoid(gt)
            du_ref[rows, :] = _bf(jnp.concatenate([dv * sg, dv * a * (sg * (1.0 - sg))], axis=1))
            return carry

        lax.fori_loop(0, tc // CONV_ROWS, chunk, 0)

    shifts = pltpu.VMEM((7, tc + 24, CONV_CH), F32)
    return _call(
        body, rides=rides, name="conv_bwd", grid=(n_tiles,),
        in_specs=_halo_specs(tc, CONV_CH, t_len) + _halo_specs(tc, 1024, t_len) + [_res((32, CONV_CH))],
        out_specs=[_rows(tc, 1024), pl.BlockSpec((8 * 32, CONV_CH), lambda i: (0, 0))],
        out_shape=(jax.ShapeDtypeStruct((t_len, 1024), BF16), jax.ShapeDtypeStruct((8 * 32, CONV_CH), F32)),
        scratch_shapes=[pltpu.VMEM((tc + 32, CONV_CH), F32), pltpu.VMEM((tc + 32, CONV_CH), F32), shifts, shifts],
        compiler_params=_cp(1),
    )(dpre, dpre, dpre, ug, ug, ug, w_dw)


def _in_proj_bwd(du_glu, dq, dk, dv, dx1, x, g_mix, w_t, tm, rides=()):
    t_len = x.shape[0]
    n_ext = w_t.shape[0]

    def body(dg_ref, dq_ref, dk_ref, dv_ref, dx1_ref, x_ref, g_ref, w_ref, dx_ref, du_ref, db_ref, dgm_ref):
        i = pl.program_id(0)
        du = jnp.concatenate([dg_ref[...], dq_ref[...], dk_ref[...], dv_ref[...]], axis=1)
        du_ref[...] = du
        dh = _dot(du, w_ref[...])
        g = g_ref[...]
        _, xh, r = _rms_fwd(x_ref[...], g)
        dx_ref[...] = dx1_ref[...] + _rms_bwd(dh, xh, r, g)

        @pl.when(i == 0)
        def _():
            db_ref[...] = jnp.zeros_like(db_ref)
            dgm_ref[...] = jnp.zeros_like(dgm_ref)

        db_ref[...] += _colsum(du.astype(F32))
        dgm_ref[...] += _colsum(dh * xh)

    return _call(
        body, rides=rides, name="in_proj_bwd", grid=(t_len // tm,),
        in_specs=[_rows(tm, 1024), _rows(tm, 512), _rows(tm, 128), _rows(tm, 128), _rows(tm, D_MODEL), _rows(tm, D_MODEL),
                  _res((1, D_MODEL)), _res(w_t.shape)],
        out_specs=[_rows(tm, D_MODEL), _rows(tm, n_ext), pl.BlockSpec((1, n_ext), lambda i: (0, 0)),
                   pl.BlockSpec((1, D_MODEL), lambda i: (0, 0))],
        out_shape=(jax.ShapeDtypeStruct((t_len, D_MODEL), F32), jax.ShapeDtypeStruct((t_len, n_ext), BF16),
                   jax.ShapeDtypeStruct((1, n_ext), F32), jax.ShapeDtypeStruct((1, D_MODEL), F32)),
        compiler_params=_cp(1),
    )(du_glu, dq, dk, dv, dx1, x, g_mix, w_t)


def _weight_grad(a, d, name, tt, rides=()):
    t_len, k_dim = a.shape
    n_dim = d.shape[1]
    tk = k_dim if k_dim <= 1792 else k_dim // 2
    assert k_dim % tk == 0 and tk % 128 == 0 and n_dim % 128 == 0
    tt = min(tt, t_len)
    n_t = t_len // tt

    def body(a_ref, d_ref, o_ref, acc):
        t = pl.program_id(1)

        @pl.when(t == 0)
        def _():
            acc[...] = jnp.zeros_like(acc)

        acc[...] += _dot_tn(_bf(a_ref[...]), _bf(d_ref[...]))

        @pl.when(t == n_t - 1)
        def _():
            o_ref[...] = _bf(acc[...])

    res = _call(
        body, rides=rides, name=name, grid=(k_dim // tk, n_t),
        in_specs=[pl.BlockSpec((tt, tk), lambda k, t: (t, k)), pl.BlockSpec((tt, n_dim), lambda k, t: (t, 0))],
        out_specs=[pl.BlockSpec((tk, n_dim), lambda k, t: (k, 0))],
        out_shape=[jax.ShapeDtypeStruct((k_dim, n_dim), BF16)],
        scratch_shapes=[pltpu.VMEM((tk, n_dim), F32)],
        compiler_params=_cp(2),
    )(a, d)
    return res if rides else res[0]


ANY = pl.BlockSpec(memory_space=pl.ANY)


def _place():
    x, y, c = lax.axis_index("x"), lax.axis_index("y"), lax.axis_index("c")
    chips = [(1 - x, y), (x, 1 - y), (1 - x, 1 - y)]
    return x, y, c, chips


def _remote(src, dst, send_sems, recv_sems, k, to):
    return pltpu.make_async_remote_copy(src_ref=src, dst_ref=dst, send_sem=send_sems.at[k], recv_sem=recv_sems.at[k],
                                        device_id=to, device_id_type=MESH)


def _by_shape(names, arrays):
    groups = {}
    for k in names:
        groups.setdefault(arrays[k].shape, []).append(k)
    return list(groups.values())


def _cast_place(ws, chip_idx, tr):
    n = len(ws)
    rows, cols = ws[0].shape
    h = rows // 2
    tr = _div_tile(h, tr)
    per = h // tr

    def body(s_ref, *refs):
        for w_ref, o_ref in zip(refs[:n], refs[n:]):
            o_ref[0, 0] = _bf(w_ref[...])

    return _call(
        body, name="cast_place",
        grid_spec=pltpu.PrefetchScalarGridSpec(
            num_scalar_prefetch=1, grid=(2, per),
            in_specs=[pl.BlockSpec((tr, cols), lambda hh, r, s_ref: (hh * per + r, 0))] * n,
            out_specs=[pl.BlockSpec((1, 1, tr, cols), lambda hh, r, s_ref: (s_ref[0], hh, r, 0))] * n),
        out_shape=[jax.ShapeDtypeStruct((N_CHIPS, 2, h, cols), BF16)] * n,
        compiler_params=_cp(2),
    )(chip_idx, *ws)


def _same(arrays):
    return [jax.ShapeDtypeStruct(a.shape, a.dtype) for a in arrays]


def _gather_ride(bufs):
    n = len(bufs)

    def first_hop(outs, send, recv):
        x, y, c, chips = _place()
        mine = [outs[i].at[2 * x + y, c] for i in range(n)]
        return [_remote(mine[i], mine[i], send, recv, 3 * i + j, (cx, cy, c)) for i in range(n) for j, (cx, cy) in enumerate(chips)]

    def start(ins, outs, send, recv):
        for cp in first_hop(outs, send, recv):
            cp.start()

    def finish(ins, outs, send, recv):
        x, y, c, chips = _place()
        sib = (x, y, 1 - c)
        onward = []
        for i in range(n):
            for j, (cx, cy) in enumerate(chips):
                slab = outs[i].at[2 * cx + cy, c]
                _remote(slab, slab, send, recv, 3 * i + j, sib).wait_recv()
                onward.append(_remote(slab, slab, send, recv, 3 * n + 3 * i + j, sib))
                onward[-1].start()
        for i in range(n):
            for j, (cx, cy) in enumerate(chips):
                other = outs[i].at[2 * cx + cy, 1 - c]
                _remote(other, other, send, recv, 3 * n + 3 * i + j, sib).wait_recv()
        for cp in first_hop(outs, send, recv) + onward:
            cp.wait_send()

    return _Ride(bufs, _same(bufs), 6 * n, start, finish, aliases={i: i for i in range(n)})


def _spread_ride(buf):
    def sends(outs, send, recv):
        x, y, c, chips = _place()
        mine = outs[0].at[2 * x + y]
        return [_remote(mine, mine, send, recv, j, (cx, cy, c)) for j, (cx, cy) in enumerate(chips)]

    def start(ins, outs, send, recv):
        for cp in sends(outs, send, recv):
            cp.start()

    def finish(ins, outs, send, recv):
        _, _, c, chips = _place()
        for j, (cx, cy) in enumerate(chips):
            slab = outs[0].at[2 * cx + cy]
            _remote(slab, slab, send, recv, j, (cx, cy, c)).wait_recv()
        for cp in sends(outs, send, recv):
            cp.wait_send()

    return _Ride([buf], _same([buf]), 3, start, finish, aliases={0: 0})


def _allgather_ride(buf):
    def peers():
        x, y, c, _ = _place()
        return [(x ^ ((k >> 2) & 1), y ^ ((k >> 1) & 1), c ^ (k & 1)) for k in range(1, N_DEV)], 4 * x + 2 * y + c

    def sends(outs, send, recv):
        to, me = peers()
        mine = outs[0].at[me]
        return [_remote(mine, mine, send, recv, k, p) for k, p in enumerate(to)]

    def start(ins, outs, send, recv):
        for cp in sends(outs, send, recv):
            cp.start()

    def finish(ins, outs, send, recv):
        for k, (px, py, pc) in enumerate(peers()[0]):
            slab = outs[0].at[4 * px + 2 * py + pc]
            _remote(slab, slab, send, recv, k, (px, py, pc)).wait_recv()
        for cp in sends(outs, send, recv):
            cp.wait_send()

    return _Ride([buf], _same([buf]), N_DEV - 1, start, finish, aliases={0: 0})


def _sum_slabs(buf):
    def body(b_ref, o_ref):
        acc = b_ref[0]
        for d in range(1, buf.shape[0]):
            acc = acc + b_ref[d]
        o_ref[...] = acc

    return _call(body, name="sum_slabs", out_shape=jax.ShapeDtypeStruct(buf.shape[1:], buf.dtype))(buf)


def _pairwise_ride(arrays, out_shape, n_sem, copies):
    def start(ins, outs, send, recv):
        for cp in copies(ins, outs, send, recv):
            cp.start()

    def finish(ins, outs, send, recv):
        for cp in copies(ins, outs, send, recv):
            cp.wait()

    return _Ride(arrays, out_shape, n_sem, start, finish)


def _run_rides(name, rides):
    k_in = [len(r.operands) for r in rides]
    k_out = [len(r.out_shape) for r in rides]

    def body(*refs):
        pos, r_in, r_out = 0, [], []
        for k in k_in:
            r_in.append(refs[pos:pos + k])
            pos += k
        for k in k_out:
            r_out.append(refs[pos:pos + k])
            pos += k
        sems = refs[pos:]
        for j, r in enumerate(rides):
            r.start(r_in[j], r_out[j], sems[2 * j], sems[2 * j + 1])
        for j, r in enumerate(rides):
            r.finish(r_in[j], r_out[j], sems[2 * j], sems[2 * j + 1])

    aliases, off_in, off_out = {}, 0, 0
    for r, ki, ko in zip(rides, k_in, k_out):
        aliases.update({off_in + a: off_out + b for a, b in r.aliases.items()})
        off_in, off_out = off_in + ki, off_out + ko
    res = _call(
        body, name=name, in_specs=[ANY] * sum(k_in), out_specs=[ANY] * sum(k_out),
        out_shape=[s for r in rides for s in r.out_shape], input_output_aliases=aliases,
        scratch_shapes=[pltpu.SemaphoreType.DMA((r.n_sem,)) for r in rides for _ in range(2)],
    )(*[op for r in rides for op in r.operands])
    out, pos = [], 0
    for k in k_out:
        out.append(list(res[pos:pos + k]))
        pos += k
    return out


def _swap_ride(grads):
    def copies(ins, outs, send, recv):
        x, y, c, _ = _place()
        return [_remote(ins[i].at[:, 1 - c], outs[i], send, recv, i, (x, y, 1 - c)) for i in range(len(grads))]

    out_shape = [jax.ShapeDtypeStruct((g.shape[0],) + g.shape[2:], g.dtype) for g in grads]
    return _pairwise_ride(grads, out_shape, len(grads), copies)


def _pair_sum(grads, others, c_idx, tr):
    n = len(grads)
    n_s, _, h, cols = grads[0].shape
    tr = _div_tile(h, tr)

    def body(c_ref, *refs):
        for a_ref, b_ref, o_ref in zip(refs[:n], refs[n:2 * n], refs[2 * n:]):
            o_ref[...] = _bf(a_ref[0].astype(F32) + b_ref[...].astype(F32))

    return _call(
        body, name="pair_sum",
        grid_spec=pltpu.PrefetchScalarGridSpec(
            num_scalar_prefetch=1, grid=(n_s, h // tr),
            in_specs=[pl.BlockSpec((1, 1, tr, cols), lambda s, r, c_ref: (s, c_ref[0], r, 0))] * n
            + [pl.BlockSpec((1, tr, cols), lambda s, r, c_ref: (s, r, 0))] * n,
            out_specs=[pl.BlockSpec((1, tr, cols), lambda s, r, c_ref: (s, r, 0))] * n),
        out_shape=[jax.ShapeDtypeStruct((n_s, h, cols), BF16)] * n,
        compiler_params=_cp(2),
    )(c_idx, *grads, *others)


def _exchange_ride(sums):
    def copies(ins, outs, send, recv):
        _, _, c, chips = _place()
        return [_remote(ins[i].at[2 * cx + cy], outs[i].at[j], send, recv, 3 * i + j, (cx, cy, c))
                for i in range(len(sums)) for j, (cx, cy) in enumerate(chips)]

    out_shape = [jax.ShapeDtypeStruct((3,) + s.shape[1:], s.dtype) for s in sums]
    return _pairwise_ride(sums, out_shape, 3 * len(sums), copies)


def _chip_sum(owns, others, chip_idx, tr):
    n = len(owns)
    _, h, cols = owns[0].shape
    tr = _div_tile(h, tr)

    def body(s_ref, *refs):
        for a_ref, p_ref, o_ref in zip(refs[:n], refs[n:2 * n], refs[2 * n:]):
            acc = a_ref[0].astype(F32)
            for j in range(N_CHIPS - 1):
                acc = acc + p_ref[j].astype(F32)
            o_ref[...] = acc

    return _call(
        body, name="chip_sum",
        grid_spec=pltpu.PrefetchScalarGridSpec(
            num_scalar_prefetch=1, grid=(h // tr,),
            in_specs=[pl.BlockSpec((1, tr, cols), lambda r, s_ref: (s_ref[0], r, 0))] * n
            + [pl.BlockSpec((N_CHIPS - 1, tr, cols), lambda r, s_ref: (0, r, 0))] * n,
            out_specs=[pl.BlockSpec((tr, cols), lambda r, s_ref: (r, 0))] * n),
        out_shape=[jax.ShapeDtypeStruct((h, cols), F32)] * n,
        compiler_params=_cp(1),
    )(chip_idx, *owns, *others)


def _share_ride(halves):
    def copies(ins, outs, send, recv):
        x, y, c, _ = _place()
        return [_remote(ins[i], outs[i], send, recv, i, (x, y, 1 - c)) for i in range(len(halves))]

    return _pairwise_ride(halves, _same(halves), len(halves), copies)


def _adamw_math(w, g, m, v):
    m_new = ADAM_B1 * m + (1.0 - ADAM_B1) * g
    v_new = ADAM_B2 * v + (1.0 - ADAM_B2) * (g * g)
    m_hat = m_new * (1.0 / (1.0 - ADAM_B1 ** ADAM_STEP))
    v_hat = v_new * (1.0 / (1.0 - ADAM_B2 ** ADAM_STEP))
    delta = -ADAM_LR * (m_hat / (jnp.sqrt(v_hat) + ADAM_EPS) + ADAM_WD * w)
    return delta, m_new, v_new


def _adamw(w, g_mine, g_other, m, v, core_idx, tr):
    rows, cols = w.shape
    h = rows // 2
    tr = _div_tile(h, tr)
    per = h // tr

    def body(c_ref, w_ref, ga_ref, gb_ref, m_ref, v_ref, g_ref, d_ref, mo_ref, vo_ref):
        g = jnp.where(pl.program_id(0) == c_ref[0], ga_ref[...], gb_ref[...])
        d, mn, vn = _adamw_math(w_ref[...], g, m_ref[...], v_ref[...])
        g_ref[...] = g
        d_ref[...] = d
        mo_ref[...] = mn
        vo_ref[...] = vn

    full = pl.BlockSpec((tr, cols), lambda hh, r, c_ref: (hh * per + r, 0))
    mine = pl.BlockSpec((tr, cols), lambda hh, r, c_ref: (jnp.where(hh == c_ref[0], r, 0), 0))
    other = pl.BlockSpec((tr, cols), lambda hh, r, c_ref: (jnp.where(hh == c_ref[0], 0, r), 0))
    shp = jax.ShapeDtypeStruct(w.shape, F32)
    return _call(
        body, name="adamw",
        grid_spec=pltpu.PrefetchScalarGridSpec(num_scalar_prefetch=1, grid=(2, per), in_specs=[full, mine, other, full, full],
                                               out_specs=[full] * 4),
        out_shape=(shp, shp, shp, shp), compiler_params=_cp(2))(core_idx, w, g_mine, g_other, m, v)


def _adamw_small(ws, gs, ms, vs):
    n = len(ws)

    def body(*refs):
        w_r, g_r, m_r, v_r = refs[0:n], refs[n:2 * n], refs[2 * n:3 * n], refs[3 * n:4 * n]
        d_o, m_o, v_o = refs[4 * n:5 * n], refs[5 * n:6 * n], refs[6 * n:7 * n]
        for i in range(n):
            d, mn, vn = _adamw_math(w_r[i][...], g_r[i][...], m_r[i][...], v_r[i][...])
            d_o[i][...] = d
            m_o[i][...] = mn
            v_o[i][...] = vn

    shp = [jax.ShapeDtypeStruct(w.shape, F32) for w in ws]
    outs = _call(body, name="adamw_small", out_shape=shp * 3)(*ws, *gs, *ms, *vs)
    return outs[0:n], outs[n:2 * n], outs[2 * n:3 * n]


def _rope_tables(t_len):
    pos = jnp.arange(t_len, dtype=F32)
    inv_freq = ROPE_THETA ** (-jnp.arange(0, HEAD_DIM, 2, dtype=F32) / HEAD_DIM)
    ang = pos[:, None] * inv_freq[None, :]
    cos, sin = jnp.cos(ang), jnp.sin(ang)
    return jnp.tile(jnp.concatenate([cos, cos], axis=1), (1, 2)), jnp.tile(jnp.concatenate([-sin, sin], axis=1), (1, 2))


def _dup_heads(a):
    h0, h1 = a[..., 0:64], a[..., 64:128]
    return jnp.concatenate([h0, h0, h1, h1], axis=-1)


def _local_step(x, mem, target, small, wg, comm, tm_a=512, tm_b=256, tc=512, tt=1024):
    def run(stage, fn, n_own, *operands):
        rides = comm.rides(stage)
        res = fn(*operands, rides=rides)
        res = list(res) if isinstance(res, (list, tuple)) else [res]
        brought, pos = [], n_own
        for r in rides:
            brought.append(res[pos:pos + len(r.out_shape)])
            pos += len(r.out_shape)
        comm.landed(stage, brought, wg)
        return res[:n_own]

    t_len = x.shape[0]
    cos_t, sin_t = _rope_tables(t_len)
    b_in = small["b_in"]
    b_ext = jnp.concatenate([b_in[:, 0:1536], _dup_heads(b_in[:, 1536:1664]), _dup_heads(b_in[:, 1664:1792])], axis=1)
    w_dw = jnp.concatenate([wg["w_dw"], jnp.zeros((1, CONV_CH), F32)], axis=0)
    sink_b = jnp.broadcast_to(small["attn_sink"].reshape(8, 1), (8, 128))
    bias_q, bias_k = _band_bias()

    ug, q, kd, vd, h1 = run("in_proj_fwd", _in_proj_fwd, 5, x, small["g_mix"], wg["w_in"], b_ext, cos_t, sin_t, tm_a)
    yc, pre = run("conv_fwd", _conv_fwd, 2, ug, w_dw, small["b_dw"], small["g_conv_ln"], small["b_conv_ln"], tc)
    memn, kv = _mem_kv_fwd(mem, small["g_mem_kv"], wg["w_mem_kv"])
    ya, lse = run("attn_fwd", _attn_fwd, 2, q, kd, vd, sink_b, bias_q)
    ymix, x1, hq, qm, om, x2 = run("mix_mem_fwd", _mix_mem_fwd, 6, x, yc, ya, wg["w_out"], small["b_out"], small["g_mem_q"],
                                   wg["w_mem_q"], kv, wg["w_mem_o"], tm_a)
    hf, gate, up, act = run("ffn_up", _ffn_up, 4, x2, small["g_ffn"], wg["w_gate"], wg["w_up"], tm_a)
    dx3, loss, d_g_final = _ffn_down_loss(x2, act, wg["w_down"], small["g_final"], target, tm_a)

    dx2, dgate, dup, d_g_ffn = _ffn_bwd(dx3, x2, gate, up, small["g_ffn"], wg["w_gate"], wg["w_up"], wg["w_down"], tm_b)
    comm.grad("w_gate", _weight_grad(dgate, hf, "dw_gate", tt))
    comm.grad("w_up", run("dw_up", _weight_grad, 1, dup, hf, "dw_up", tt)[0])
    comm.grad("w_down", run("dw_down", _weight_grad, 1, act, dx3, "dw_down", tt)[0])
    dx1, dqm, dyc, dya, dkv, d_g_mem_q, d_b_out = run("mix_mem_bwd", _mix_mem_bwd, 7, dx2, x1, qm, kv, small["g_mem_q"],
                                                      wg["w_mem_q"], wg["w_mem_o"], wg["w_out"], tm_a)
    d_w_mem_kv, d_g_mem_kv = _mem_kv_bwd(dkv, memn, mem, small["g_mem_kv"], wg["w_mem_kv"])
    comm.grad("w_mem_kv", d_w_mem_kv)
    comm.grad("w_out", _weight_grad(ymix, dx1, "dw_out", tt))
    comm.grad("w_mem_q", _weight_grad(hq, dqm, "dw_mem_q", tt))
    comm.grad("w_mem_o", _weight_grad(om, dx2, "dw_mem_o", tt))
    dq, dd, dsink = run("attn_bwd_q", _attn_bwd_q, 3, q, kd, vd, dya, lse, sink_b, bias_q, cos_t, sin_t)
    dk, dv = run("attn_bwd_kv", _attn_bwd_kv, 2, q, kd, vd, dya, lse, dd, bias_k, cos_t, sin_t)
    dpre, cstats = run("conv_norm_bwd", _conv_norm_bwd, 2, pre, dyc, small["g_conv_ln"], small["b_conv_ln"], tc)
    du_glu, d_w_dw = run("conv_bwd", _conv_bwd, 2, dpre, ug, w_dw, tc)
    grad_x, du, d_b_in, d_g_mix = run("in_proj_bwd", _in_proj_bwd, 4, du_glu, dq, dk, dv, dx1, x, small["g_mix"], wg["w_in"],
                                      tm_a)
    grads = {
        "w_dw": jnp.sum(d_w_dw.reshape(32, 8, CONV_CH), axis=1)[0:CONV_W],
        "g_mix": d_g_mix, "b_in": d_b_in, "b_dw": cstats[2:3], "g_conv_ln": cstats[0:1],
        "b_conv_ln": cstats[1:2], "attn_sink": jnp.sum(dsink[:, :, 0], axis=0)[None, :], "b_out": d_b_out,
        "g_mem_q": d_g_mem_q, "g_mem_kv": d_g_mem_kv, "g_ffn": d_g_ffn, "g_final": d_g_final,
    }
    comm.small(loss[0:1, 0:1], grads)
    (d_w_in,) = run("dw_in", _weight_grad, 1, du, h1, "dw_in", tt)
    comm.grad("w_in", d_w_in)
    return loss[0:1, 0:1], grad_x, grads


BIG = ["w_in", "w_out", "w_mem_q", "w_mem_kv", "w_mem_o", "w_gate", "w_up", "w_down"]
KEEP_SLABS = ("w_mem_kv",)
TRANSPOSED = ("w_in", "w_gate", "w_up")
SMALL = ["g_mix", "b_in", "b_dw", "g_conv_ln", "b_conv_ln", "attn_sink", "b_out", "g_mem_q", "g_mem_kv", "g_ffn", "g_final"]
PACK_ROWS = 32
ROWS_PER_STEP = 512

GATHER_ON = {"in_proj_fwd": ("w_out", "w_mem_q"), "conv_fwd": ("w_mem_kv", "w_mem_o"), "attn_fwd": ("w_gate",),
             "mix_mem_fwd": ("w_up",), "ffn_up": ("w_down",)}
MID_GROUP = ("w_mem_kv", "w_out", "w_mem_q", "w_mem_o")
SWAP_ON = {"dw_up": ("w_gate",), "dw_down": ("w_up",), "mix_mem_bwd": ("w_down",), "attn_bwd_q": MID_GROUP}
EXCHANGE_ON = {"mix_mem_bwd": ("w_gate",), "attn_bwd_q": ("w_up",), "attn_bwd_kv": ("w_down", "w_out"),
               "conv_bwd": ("w_mem_kv", "w_mem_q", "w_mem_o")}
SMALL_ON = "dw_in"


def _as_weight(name, gathered):
    g = gathered.reshape(N_CHIPS, gathered.shape[2] * 2, gathered.shape[3])
    return g if name in KEEP_SLABS else g.reshape(-1, g.shape[2])


class _Overlap:
    def __init__(self, bufs, chip_idx, core_idx):
        self.bufs, self.chip_idx, self.core_idx = bufs, chip_idx, core_idx
        self.parts, self.sums, self.others = {}, {}, {}

    def rides(self, stage):
        rides = []
        if stage in GATHER_ON:
            rides.append(_gather_ride([self.bufs[k] for k in GATHER_ON[stage]]))
        if stage in EXCHANGE_ON:
            rides.append(_exchange_ride([self.sums[k] for k in EXCHANGE_ON[stage]]))
        if stage in SWAP_ON:
            rides.append(_swap_ride([self.parts[k] for k in SWAP_ON[stage]]))
        if stage == SMALL_ON:
            rides.append(_allgather_ride(self.packs))
        return rides

    def landed(self, stage, brought, wg):
        brought = list(brought)
        if stage in GATHER_ON:
            for k, g in zip(GATHER_ON[stage], brought.pop(0)):
                wg[k] = _as_weight(k, g)
        if stage in EXCHANGE_ON:
            self.others.update(zip(EXCHANGE_ON[stage], brought.pop(0)))
        if stage in SWAP_ON:
            self._pair(SWAP_ON[stage], brought.pop(0))
        if stage == SMALL_ON:
            (self.packs,) = brought.pop(0)

    def small(self, loss, grads):
        x, y, c = lax.axis_index("x"), lax.axis_index("y"), lax.axis_index("c")
        pack = _pack_small(loss, grads)
        self.packs = lax.dynamic_update_slice(jnp.zeros((N_DEV,) + pack.shape, F32), pack[None], (4 * x + 2 * y + c, 0, 0))

    def grad(self, name, g):
        if g.ndim == 2:
            g = g.reshape(N_CHIPS, g.shape[0] // N_CHIPS, g.shape[1])
        self.parts[name] = g.reshape(N_CHIPS, 2, g.shape[1] // 2, g.shape[2])

    def _pair(self, names, from_sibling):
        came = dict(zip(names, from_sibling))
        for group in _by_shape(names, self.parts):
            sums = _pair_sum([self.parts[k] for k in group], [came[k] for k in group], self.core_idx, ROWS_PER_STEP)
            self.sums.update(zip(group, sums))

    def finish(self):
        (from_sibling,) = _run_rides("swap_last", [_swap_ride([self.parts["w_in"]])])
        self._pair(("w_in",), from_sibling)
        ((self.others["w_in"],),) = _run_rides("exchange_last", [_exchange_ride([self.sums["w_in"]])])
        halves = {}
        for group in _by_shape(BIG, self.sums):
            res = _chip_sum([self.sums[k] for k in group], [self.others[k] for k in group], self.chip_idx, ROWS_PER_STEP)
            halves.update(zip(group, res))
        mine = [halves[k] for k in BIG]
        (theirs,) = _run_rides("sibling_share", [_share_ride(mine)])
        return mine, theirs, _sum_slabs(self.packs)


def _pack_small(loss, grads):
    def row(a):
        a = a.reshape(1, -1)
        return jnp.pad(a, ((0, 0), (0, 1024 - a.shape[1])))

    rows = [row(grads[k]) for k in ("g_mix", "b_out", "g_mem_q", "g_mem_kv", "g_ffn", "g_final")]
    rows += [grads["b_in"][:, 0:1024], row(grads["b_in"][:, 1024:1792])]
    rows += [jnp.concatenate([grads["b_dw"], grads["g_conv_ln"]], axis=1), row(grads["b_conv_ln"]), row(grads["attn_sink"]),
             row(loss)]
    dw = jnp.pad(grads["w_dw"], ((0, 1), (0, 0))).reshape(16, 1024)
    pack = jnp.concatenate(rows + [dw], axis=0)
    return jnp.pad(pack, ((0, PACK_ROWS - pack.shape[0]), (0, 0)))


def _unpack_small(pack):
    out = {k: pack[i:i + 1] for i, k in enumerate(("g_mix", "b_out", "g_mem_q", "g_mem_kv", "g_ffn", "g_final"))}
    out["b_in"] = jnp.concatenate([pack[6:7], pack[7:8, 0:768]], axis=1)
    out["b_dw"], out["g_conv_ln"] = pack[8:9, 0:512], pack[8:9, 512:1024]
    out["b_conv_ln"] = pack[9:10, 0:512]
    out["attn_sink"] = pack[10:11, 0:8]
    loss = pack[11, 0]
    dw = pack[12:28].reshape(32, 512)[0:CONV_W]
    return loss, out, dw


def kernel(x, mem, g_mix, w_in, b_in, w_dw, b_dw, g_conv_ln, b_conv_ln, attn_sink, w_out, b_out, g_mem_q, g_mem_kv, w_mem_q, w_mem_kv, w_mem_o, g_ffn, w_gate, w_up, w_down, g_final, loss_target, m_g_mix, m_w_in, m_b_in, m_w_dw, m_b_dw, m_g_conv_ln, m_b_conv_ln, m_attn_sink, m_w_out, m_b_out, m_g_mem_q, m_g_mem_kv, m_w_mem_q, m_w_mem_kv, m_w_mem_o, m_g_ffn, m_w_gate, m_w_up, m_w_down, m_g_final, v_g_mix, v_w_in, v_b_in, v_w_dw, v_b_dw, v_g_conv_ln, v_b_conv_ln, v_attn_sink, v_w_out, v_b_out, v_g_mem_q, v_g_mem_kv, v_w_mem_q, v_w_mem_kv, v_w_mem_o, v_g_ffn, v_w_gate, v_w_up, v_w_down, v_g_final):
    args = dict(locals())
    weight_names = ["g_mix", "w_in", "b_in", "w_dw", "b_dw", "g_conv_ln", "b_conv_ln", "attn_sink", "w_out", "b_out", "g_mem_q",
                    "g_mem_kv", "w_mem_q", "w_mem_kv", "w_mem_o", "g_ffn", "w_gate", "w_up", "w_down", "g_final"]
    chip = 2 * lax.axis_index("x") + lax.axis_index("y")
    core = lax.axis_index("c")

    chip_idx = chip.astype(jnp.int32).reshape(1)
    core_idx = core.astype(jnp.int32).reshape(1)

    def block(name):
        a = args[name][0]
        weight = name[2:] if name[:2] in ("m_", "v_") else name
        return a.T if weight in TRANSPOSED else a

    blocks = {k: block(k) for k in BIG}
    bufs = {}
    for group in _by_shape(BIG, blocks):
        bufs.update(zip(group, _cast_place([blocks[k] for k in group], chip_idx, ROWS_PER_STEP)))
    comm = _Overlap(bufs, chip_idx, core_idx)
    dw_buf = lax.dynamic_update_slice(jnp.zeros((N_CHIPS, CONV_W, 128), F32), w_dw, (chip, 0, 0))
    (first,), (dw_all,) = _run_rides("gather_first", [_gather_ride([comm.bufs["w_in"]]), _spread_ride(dw_buf)])
    wg = {"w_in": _as_weight("w_in", first), "w_dw": jnp.transpose(dw_all, (1, 0, 2)).reshape(CONV_W, CONV_CH)}
    small = {k: args[k].reshape(1, -1) for k in SMALL}

    loss, grad_x, grads = _local_step(x[0], mem[0], loss_target[0], small, wg, comm)

    halves, other_halves, pack_sum = comm.finish()

    loss_sum, small_grads, dw_full = _unpack_small(pack_sum)
    dw_cols = jnp.transpose(dw_full.reshape(CONV_W, N_CHIPS, 128), (1, 0, 2))
    small_grads["w_dw"] = lax.dynamic_index_in_dim(dw_cols, chip, axis=0, keepdims=False)

    out_g, out_d, out_m, out_v = {}, {}, {}, {}
    for k, g_mine, g_other in zip(BIG, halves, other_halves):
        res = _adamw(block(k), g_mine, g_other, block("m_" + k), block("v_" + k), core_idx, ROWS_PER_STEP)
        out_g[k], out_d[k], out_m[k], out_v[k] = [(r.T if k in TRANSPOSED else r)[None] for r in res]
    names = SMALL + ["w_dw"]

    def flat(a):
        return a[0] if a.ndim == 3 else a.reshape(1, -1)

    def pad_lanes(a):
        return jnp.pad(a, ((0, 0), (0, 128 - a.shape[1]))) if a.shape[1] < 128 else a

    ws = [flat(args[k]) for k in names]
    gs = [small_grads[k] for k in names]
    ms = [flat(args["m_" + k]) for k in names]
    vs = [flat(args["v_" + k]) for k in names]
    ds, mns, vns = _adamw_small([pad_lanes(a) for a in ws], [pad_lanes(a) for a in gs], [pad_lanes(a) for a in ms],
                                [pad_lanes(a) for a in vs])
    for i, k in enumerate(names):
        n_lanes = ws[i].shape[1]
        for out, val in ((out_g, gs[i]), (out_d, ds[i]), (out_m, mns[i]), (out_v, vns[i])):
            out[k] = val[:, 0:n_lanes].reshape(args[k].shape)

    return (loss_sum, grad_x[None], *[out_g[k] for k in weight_names], *[out_d[k] for k in weight_names],
            *[out_m[k] for k in weight_names], *[out_v[k] for k in weight_names])
```

```python
import jax
import jax.numpy as jnp
import numpy as np
from jax import lax
from jax.experimental import pallas as pl
from jax.experimental.pallas import tpu as pltpu

F32 = jnp.float32
BF16 = jnp.bfloat16
EPS = 1e-6
NEG = -1e30

D_MODEL = 1024
CONV_CH = 512
CONV_W = 31
HEAD_DIM = 64
BLK = 128
MEM_HEADS = 4
MEM_HD = 256
N_CHIPS = 4
N_DEV = 8
ATT_SCALE = HEAD_DIM ** -0.5
MEM_SCALE = MEM_HD ** -0.5
ROPE_THETA = 10000.0

ADAM_LR = 0.001
ADAM_B1 = 0.9
ADAM_B2 = 0.999
ADAM_EPS = 1e-08
ADAM_WD = 0.01
ADAM_STEP = 10

VMEM_LIMIT_BYTES = 56 * 1024 * 1024
MESH = pl.DeviceIdType.MESH


class _Ride:
    def __init__(self, operands, out_shape, n_sem, start, finish, aliases=None):
        self.operands, self.out_shape, self.n_sem = list(operands), list(out_shape), n_sem
        self.start, self.finish, self.aliases = start, finish, dict(aliases or {})


def _call(body, rides=(), **kw):
    if not rides:
        return pl.pallas_call(body, **kw)
    grid = kw["grid"]
    n_in, n_out = len(kw["in_specs"]), len(kw["out_specs"])
    scratch = list(kw.get("scratch_shapes", ()))
    k_in = [len(r.operands) for r in rides]
    k_out = [len(r.out_shape) for r in rides]

    def carried(*refs):
        pos = n_in
        r_in, r_out = [], []
        for k in k_in:
            r_in.append(refs[pos:pos + k])
            pos += k
        own_out = refs[pos:pos + n_out]
        pos += n_out
        for k in k_out:
            r_out.append(refs[pos:pos + k])
            pos += k
        own_scratch = refs[pos:pos + len(scratch)]
        sems = refs[pos + len(scratch):]
        first = last = None
        for axis, n_steps in enumerate(grid):
            step = pl.program_id(axis)
            first = (step == 0) if first is None else first & (step == 0)
            last = (step == n_steps - 1) if last is None else last & (step == n_steps - 1)

        @pl.when(first)
        def _():
            for j, r in enumerate(rides):
                r.start(r_in[j], r_out[j], sems[2 * j], sems[2 * j + 1])

        body(*refs[:n_in], *own_out, *own_scratch)

        @pl.when(last)
        def _():
            for j, r in enumerate(rides):
                r.finish(r_in[j], r_out[j], sems[2 * j], sems[2 * j + 1])

    kw = dict(kw)
    kw["in_specs"] = list(kw["in_specs"]) + [ANY] * sum(k_in)
    kw["out_specs"] = list(kw["out_specs"]) + [ANY] * sum(k_out)
    kw["out_shape"] = list(kw["out_shape"]) + [s for r in rides for s in r.out_shape]
    kw["scratch_shapes"] = scratch + [pltpu.SemaphoreType.DMA((r.n_sem,)) for r in rides for _ in range(2)]
    aliases, off_in, off_out = {}, n_in, n_out
    for r, ki, ko in zip(rides, k_in, k_out):
        aliases.update({off_in + a: off_out + b for a, b in r.aliases.items()})
        off_in, off_out = off_in + ki, off_out + ko
    if aliases:
        kw["input_output_aliases"] = aliases
    call = pl.pallas_call(carried, **kw)
    return lambda *args: call(*args, *[op for r in rides for op in r.operands])


def _cp(n_grid):
    return pltpu.CompilerParams(dimension_semantics=("arbitrary",) * n_grid, vmem_limit_bytes=VMEM_LIMIT_BYTES)


def _res(shape):
    nd = len(shape)
    return pl.BlockSpec(shape, lambda *_: (0,) * nd, pipeline_mode=pl.Buffered(1))


def _rows(tm, n):
    return pl.BlockSpec((tm, n), lambda i: (i, 0))


def _div_tile(n, target):
    best = None
    for d in range(16, min(n, target) + 1, 16):
        if n % d == 0:
            best = d
    assert best is not None, (n, target)
    return best


def _dot(a, b):
    return jnp.dot(a, b, preferred_element_type=F32)


def _dot_nt(a, b):
    return lax.dot_general(a, b, (((1,), (1,)), ((), ())), preferred_element_type=F32)


def _dot_tn(a, b):
    return lax.dot_general(a, b, (((0,), (0,)), ((), ())), preferred_element_type=F32)


def _bf(x):
    return x.astype(BF16)


def _sigmoid(x):
    return 1.0 / (1.0 + jnp.exp(-x))


def _rms_fwd(x, g):
    r = lax.rsqrt(jnp.mean(x * x, axis=-1, keepdims=True) + EPS)
    xh = x * r
    return xh * g, xh, r


def _rms_bwd(dh, xh, r, g):
    dxh = dh * g
    return r * (dxh - xh * jnp.mean(dxh * xh, axis=-1, keepdims=True))


def _colsum(x):
    return jnp.sum(x, axis=0, keepdims=True)


def _rope(x, cos, sin, sign):
    n = x.shape[1] // 128
    c = jnp.tile(cos, (1, n)) if n > 1 else cos
    s = jnp.tile(sin, (1, n)) if n > 1 else sin
    lane = lax.broadcasted_iota(jnp.int32, x.shape, 1)
    first = (lane & 63) < 32
    partner = jnp.where(first, pltpu.roll(x, x.shape[1] - 32, 1), pltpu.roll(x, 32, 1))
    return x * c + sign * (partner * s)


def _lo_lanes(shape):
    return lax.broadcasted_iota(jnp.int32, shape, 1) < 64


def _stack_heads(t):
    t0, t1 = t[:, 0:128], t[:, 128:256]
    lo = _lo_lanes(t0.shape)
    z = jnp.zeros_like(t0)
    return jnp.concatenate([jnp.where(lo, t0, z), jnp.where(lo, z, t0), jnp.where(lo, t1, z), jnp.where(lo, z, t1)], axis=0)


def _unstack_heads(o):
    lo = _lo_lanes((BLK, 128))
    return jnp.concatenate([jnp.where(lo, o[0:128], o[128:256]), jnp.where(lo, o[256:384], o[384:512])], axis=1)


def _fold_heads(parts):
    a, b = (p + pltpu.roll(p, 64, 1) for p in parts)
    return jnp.where(_lo_lanes(a.shape), a, b)


def _sink_col(sk_ref, g):
    return jnp.concatenate([jnp.broadcast_to(sk_ref[4 * g + h:4 * g + h + 1, :], (BLK, 128)) for h in range(4)], axis=0)


def _tile3(x):
    return jnp.concatenate([x, x, x], axis=1)


def _mem_kv_fwd(mem, g_kv, w_kv):
    m_len = mem.shape[0]
    cols = w_kv.shape[2]

    def body(mem_ref, g_ref, w_ref, memn_ref, kv_ref):
        h, _, _ = _rms_fwd(mem_ref[...], g_ref[...])
        hb = _bf(h)
        memn_ref[...] = hb
        for s in range(N_CHIPS):
            kv_ref[s] = _bf(_dot(hb, w_ref[s]))

    return _call(
        body, name="mem_kv_fwd",
        out_shape=(jax.ShapeDtypeStruct((m_len, D_MODEL), BF16), jax.ShapeDtypeStruct((N_CHIPS, m_len, cols), BF16)),
        compiler_params=pltpu.CompilerParams(vmem_limit_bytes=VMEM_LIMIT_BYTES),
    )(mem, g_kv, w_kv)


def _dup_head_rows(w_ref, lo):
    h0, h1 = w_ref[lo:lo + 64, :], w_ref[lo + 64:lo + 128, :]
    return jnp.concatenate([h0, h0, h1, h1], axis=0)


def _in_proj_fwd(x, g_mix, w_t, b_ext, cos_t, sin_t, tm, rides=()):
    t_len = x.shape[0]

    def body(x_ref, g_ref, w_ref, b_ref, c_ref, s_ref, ug_ref, q_ref, k_ref, v_ref, h_ref):
        h, _, _ = _rms_fwd(x_ref[...], g_ref[...])
        hb = _bf(h)
        h_ref[...] = hb
        ug_ref[...] = _dot_nt(hb, w_ref[0:1024, :]) + b_ref[:, 0:1024]
        c, s = c_ref[...], s_ref[...]
        q_ref[...] = _bf(_rope(_dot_nt(hb, w_ref[1024:1536, :]) + b_ref[:, 1024:1536], c, s, 1.0))
        k_ref[...] = _bf(_rope(_dot_nt(hb, _dup_head_rows(w_ref, 1536)) + b_ref[:, 1536:1792], c, s, 1.0))
        v_ref[...] = _bf(_dot_nt(hb, _dup_head_rows(w_ref, 1664)) + b_ref[:, 1792:2048])

    return _call(
        body, rides=rides, name="in_proj_fwd", grid=(t_len // tm,),
        in_specs=[_rows(tm, D_MODEL), _res((1, D_MODEL)), _res(w_t.shape), _res(b_ext.shape), _rows(tm, 128), _rows(tm, 128)],
        out_specs=[_rows(tm, 1024), _rows(tm, 512), _rows(tm, 256), _rows(tm, 256), _rows(tm, D_MODEL)],
        out_shape=(jax.ShapeDtypeStruct((t_len, 1024), F32), jax.ShapeDtypeStruct((t_len, 512), BF16),
                   jax.ShapeDtypeStruct((t_len, 256), BF16), jax.ShapeDtypeStruct((t_len, 256), BF16),
                   jax.ShapeDtypeStruct((t_len, D_MODEL), BF16)),
        compiler_params=_cp(1),
    )(x, g_mix, w_t, b_ext, cos_t, sin_t)


def _halo_specs(tc, n, t_len):
    per = tc // 16
    last = t_len // 16 - 1
    return [pl.BlockSpec((16, n), lambda i: (jnp.maximum(i * per - 1, 0), 0)),
            pl.BlockSpec((tc, n), lambda i: (i, 0)),
            pl.BlockSpec((16, n), lambda i: (jnp.minimum((i + 1) * per, last), 0))]


def _glu(z):
    return z[:, 0:CONV_CH] * _sigmoid(z[:, CONV_CH:2 * CONV_CH])


def _fill_halo_buf(buf, prev, main, nxt, i, n_tiles, tc):
    buf[0:16, :] = jnp.where(i > 0, prev, jnp.zeros_like(prev))
    buf[16:16 + tc, :] = main
    buf[16 + tc:32 + tc, :] = jnp.where(i < n_tiles - 1, nxt, jnp.zeros_like(nxt))


CONV_ROWS = 64


def _shift_copies(buf, shifted, tc):
    for r in range(1, 8):
        shifted[r - 1, :, :] = buf[r:r + tc + 24, :]


def _shifted_rows(buf, shifted, offset, base):
    src = buf if offset % 8 == 0 else shifted.at[offset % 8 - 1]
    return src[pl.ds(pl.multiple_of(base + 8 * (offset // 8), 8), CONV_ROWS), :]


def _conv_fwd(ug, w_dw, b_dw, g_ln, b_ln, tc, rides=()):
    t_len = ug.shape[0]
    n_tiles = t_len // tc

    def body(up_ref, um_ref, un_ref, w_ref, bdw_ref, g_ref, b_ref, y_ref, pre_ref, buf, shifted):
        i = pl.program_id(0)
        _fill_halo_buf(buf, _glu(up_ref[...]), _glu(um_ref[...]), _glu(un_ref[...]), i, n_tiles, tc)
        _shift_copies(buf, shifted, tc)

        def chunk(c, carry):
            base = c * CONV_ROWS
            acc = jnp.zeros((CONV_ROWS, CONV_CH), F32)
            for k in range(CONV_W):
                acc = acc + w_ref[k:k + 1, :] * _shifted_rows(buf, shifted, k + 1, base)
            pre_ref[pl.ds(pl.multiple_of(base, CONV_ROWS), CONV_ROWS), :] = acc + bdw_ref[...]
            return carry

        lax.fori_loop(0, tc // CONV_ROWS, chunk, 0)
        pre = pre_ref[...]
        mu = jnp.mean(pre, axis=-1, keepdims=True)
        d = pre - mu
        rstd = lax.rsqrt(jnp.mean(d * d, axis=-1, keepdims=True) + EPS)
        ln = d * rstd * g_ref[...] + b_ref[...]
        y_ref[...] = _bf(ln * _sigmoid(ln))

    return _call(
        body, rides=rides, name="conv_fwd", grid=(n_tiles,),
        in_specs=_halo_specs(tc, 1024, t_len) + [_res((32, CONV_CH)), _res((1, CONV_CH)), _res((1, CONV_CH)), _res((1, CONV_CH))],
        out_specs=[_rows(tc, CONV_CH), _rows(tc, CONV_CH)],
        out_shape=(jax.ShapeDtypeStruct((t_len, CONV_CH), BF16), jax.ShapeDtypeStruct((t_len, CONV_CH), F32)),
        scratch_shapes=[pltpu.VMEM((tc + 32, CONV_CH), F32), pltpu.VMEM((7, tc + 24, CONV_CH), F32)],
        compiler_params=_cp(1),
    )(ug, ug, ug, w_dw, b_dw, g_ln, b_ln)


def _nbr_specs(n, nb):
    return [pl.BlockSpec((BLK, n), lambda i: (jnp.maximum(i - 1, 0), 0)),
            pl.BlockSpec((BLK, n), lambda i: (i, 0)),
            pl.BlockSpec((BLK, n), lambda i: (jnp.minimum(i + 1, nb - 1), 0))]


def _nbr_specs4(nb):
    return [pl.BlockSpec((1, 2, 4 * BLK, 128), lambda i: (jnp.maximum(i - 1, 0), 0, 0, 0)),
            pl.BlockSpec((1, 2, 4 * BLK, 128), lambda i: (i, 0, 0, 0)),
            pl.BlockSpec((1, 2, 4 * BLK, 128), lambda i: (jnp.minimum(i + 1, nb - 1), 0, 0, 0))]


def _band_bias():
    a = np.arange(4 * BLK)[:, None] % BLK
    c = np.arange(3 * BLK)[None, :]
    inside = np.abs(c - BLK - a) <= BLK
    q_side = np.stack([inside & (c >= BLK), inside, inside & (c < 2 * BLK)])
    blk = np.arange(12 * BLK)[:, None] // (4 * BLK)
    a = np.arange(12 * BLK)[:, None] % BLK
    c = np.arange(BLK)[None, :]
    inside = np.abs(c - a + (1 - blk) * BLK) <= BLK
    k_side = np.stack([inside & (blk >= 1), inside, inside & (blk <= 1)])
    return [jnp.asarray(np.where(m, 0.0, NEG).astype(np.float32)) for m in (q_side, k_side)]


def _edge_spec(shape, nb):
    return pl.BlockSpec((1,) + shape, lambda i: (jnp.where(i == 0, 0, jnp.where(i == nb - 1, 2, 1)),) + (0,) * len(shape))


def _attn_fwd(q, kd, vd, sink_b, bias, rides=()):
    t_len = q.shape[0]
    nb = t_len // BLK

    def body(q_ref, kp_ref, kc_ref, kn_ref, vp_ref, vc_ref, vn_ref, sk_ref, bias_ref, y_ref, lse_ref):
        kcat = jnp.concatenate([kp_ref[...], kc_ref[...], kn_ref[...]], axis=0)
        vcat = jnp.concatenate([vp_ref[...], vc_ref[...], vn_ref[...]], axis=0)
        ys = []
        for g in range(2):
            qs = _stack_heads(q_ref[:, 256 * g:256 * g + 256])
            s = _dot_nt(qs, kcat[:, 128 * g:128 * g + 128]) * ATT_SCALE + bias_ref[0]
            skc = _sink_col(sk_ref, g)
            m_b = jnp.maximum(jnp.max(s, axis=-1, keepdims=True), skc)
            p = jnp.exp(s - _tile3(m_b))
            den_b = jnp.sum(p, axis=-1, keepdims=True) + jnp.exp(skc - m_b)
            pn = p * _tile3(1.0 / den_b)
            o = _dot(_bf(pn), vcat[:, 128 * g:128 * g + 128])
            ys.append(_unstack_heads(o))
            lse_ref[0, g] = m_b + jnp.log(den_b)
        y_ref[...] = _bf(jnp.concatenate(ys, axis=1))

    return _call(
        body, rides=rides, name="attn_fwd", grid=(nb,),
        in_specs=[_rows(BLK, 512)] + _nbr_specs(256, nb) + _nbr_specs(256, nb) + [_res((8, 128)), _edge_spec((4 * BLK, 3 * BLK), nb)],
        out_specs=[_rows(BLK, 512), pl.BlockSpec((1, 2, 4 * BLK, 128), lambda i: (i, 0, 0, 0))],
        out_shape=(jax.ShapeDtypeStruct((t_len, 512), BF16), jax.ShapeDtypeStruct((nb, 2, 4 * BLK, 128), F32)),
        compiler_params=_cp(1),
    )(q, kd, kd, kd, vd, vd, vd, sink_b, bias)


def _mem_heads(kv_ref, h):
    lo = MEM_HD * (h % 2)
    return kv_ref[h // 2, :, lo:lo + MEM_HD], kv_ref[2 + h // 2, :, lo:lo + MEM_HD]


def _mix_mem_fwd(x, yc, ya, w_out, b_out, g_q, w_q, kv, w_o, tm, rides=()):
    t_len = x.shape[0]

    def body(x_ref, yc_ref, ya_ref, wout_ref, bout_ref, g_ref, wq_ref, kv_ref, wo_ref,
             ymix_ref, x1_ref, hq_ref, qm_ref, om_ref, x2_ref):
        ymix = jnp.concatenate([yc_ref[...], ya_ref[...]], axis=1)
        ymix_ref[...] = ymix
        x1 = x_ref[...] + _dot(ymix, wout_ref[...]) + bout_ref[...]
        x1_ref[...] = x1
        hq, _, _ = _rms_fwd(x1, g_ref[...])
        hqb = _bf(hq)
        hq_ref[...] = hqb
        qm = _bf(_dot(hqb, wq_ref[...]))
        qm_ref[...] = qm
        outs = []
        for h in range(MEM_HEADS):
            kh, vh = _mem_heads(kv_ref, h)
            s = _dot_nt(qm[:, MEM_HD * h:MEM_HD * (h + 1)], kh) * MEM_SCALE
            p = jnp.exp(s - jnp.max(s, axis=-1, keepdims=True))
            p = p * (1.0 / jnp.sum(p, axis=-1, keepdims=True))
            outs.append(_dot(_bf(p), vh))
        om = _bf(jnp.concatenate(outs, axis=1))
        om_ref[...] = om
        x2_ref[...] = x1 + _dot(om, wo_ref[...])

    act_b = jax.ShapeDtypeStruct((t_len, D_MODEL), BF16)
    act_f = jax.ShapeDtypeStruct((t_len, D_MODEL), F32)
    return _call(
        body, rides=rides, name="mix_mem_fwd", grid=(t_len // tm,),
        in_specs=[_rows(tm, D_MODEL), _rows(tm, 512), _rows(tm, 512), _res(w_out.shape), _res((1, D_MODEL)), _res((1, D_MODEL)),
                  _res(w_q.shape), _res(kv.shape), _res(w_o.shape)],
        out_specs=[_rows(tm, D_MODEL)] * 6,
        out_shape=(act_b, act_f, act_b, act_b, act_b, act_f),
        compiler_params=_cp(1),
    )(x, yc, ya, w_out, b_out, g_q, w_q, kv, w_o)


def _hidden_chunks(ff, width=1024):
    return [(lo, min(lo + width, ff)) for lo in range(0, ff, width)]


def _ffn_up(x2, g_ffn, w_gate, w_up, tm, rides=()):
    t_len = x2.shape[0]
    ff = w_gate.shape[0]

    def body(x2_ref, g_ref, wg_ref, wu_ref, hf_ref, gate_ref, up_ref, act_ref):
        hf, _, _ = _rms_fwd(x2_ref[...], g_ref[...])
        hfb = _bf(hf)
        hf_ref[...] = hfb
        for lo, hi in _hidden_chunks(ff):
            gate = _dot_nt(hfb, wg_ref[lo:hi, :])
            up = _dot_nt(hfb, wu_ref[lo:hi, :])
            gate_ref[:, lo:hi] = _bf(gate)
            up_ref[:, lo:hi] = _bf(up)
            act_ref[:, lo:hi] = _bf(gate * _sigmoid(gate) * up)

    hid = jax.ShapeDtypeStruct((t_len, ff), BF16)
    return _call(
        body, rides=rides, name="ffn_up", grid=(t_len // tm,),
        in_specs=[_rows(tm, D_MODEL), _res((1, D_MODEL)), _res(w_gate.shape), _res(w_up.shape)],
        out_specs=[_rows(tm, D_MODEL), _rows(tm, ff), _rows(tm, ff), _rows(tm, ff)],
        out_shape=[jax.ShapeDtypeStruct((t_len, D_MODEL), BF16), hid, hid, hid],
        compiler_params=_cp(1),
    )(x2, g_ffn, w_gate, w_up)


def _ffn_down_loss(x2, act, w_down, g_final, target, tm):
    t_len = x2.shape[0]
    ff = w_down.shape[0]

    def body(x2_ref, act_ref, wd_ref, gf_ref, tgt_ref, dx3_ref, loss_ref, dgf_ref):
        i = pl.program_id(0)
        x3 = x2_ref[...]
        for lo, hi in _hidden_chunks(ff):
            x3 = x3 + _dot(act_ref[:, lo:hi], wd_ref[lo:hi, :])
        gf = gf_ref[...]
        y, xh, r = _rms_fwd(x3, gf)
        err = y - tgt_ref[...]
        part = 0.5 * jnp.sum(jnp.mean(err * err, axis=-1, keepdims=True), axis=0, keepdims=True)
        dy = err * (1.0 / D_MODEL)
        dx3_ref[...] = _rms_bwd(dy, xh, r, gf)

        @pl.when(i == 0)
        def _():
            loss_ref[...] = jnp.zeros_like(loss_ref)
            dgf_ref[...] = jnp.zeros_like(dgf_ref)

        loss_ref[...] += jnp.broadcast_to(part, loss_ref.shape)
        dgf_ref[...] += _colsum(dy * xh)

    vec = pl.BlockSpec((1, D_MODEL), lambda i: (0, 0))
    return _call(
        body, name="ffn_down_loss", grid=(t_len // tm,),
        in_specs=[_rows(tm, D_MODEL), _rows(tm, ff), _res(w_down.shape), _res((1, D_MODEL)), _rows(tm, D_MODEL)],
        out_specs=[_rows(tm, D_MODEL), vec, vec],
        out_shape=(jax.ShapeDtypeStruct((t_len, D_MODEL), F32), jax.ShapeDtypeStruct((1, D_MODEL), F32),
                   jax.ShapeDtypeStruct((1, D_MODEL), F32)),
        compiler_params=_cp(1),
    )(x2, act, w_down, g_final, target)


def _ffn_bwd(dx3, x2, gate, up, g_ffn, w_gate, w_up, w_down, tm):
    t_len = x2.shape[0]
    ff = w_gate.shape[0]

    def body(dx3_ref, x2_ref, gate_ref, up_ref, g_ref, wg_ref, wu_ref, wd_ref, dx2_ref, dgate_ref, dup_ref, dg_ref):
        i = pl.program_id(0)
        dx3 = dx3_ref[...]
        d3b = _bf(dx3)
        dh = jnp.zeros((tm, D_MODEL), F32)
        for lo, hi in _hidden_chunks(ff):
            dact = _dot_nt(d3b, wd_ref[lo:hi, :])
            gt = gate_ref[:, lo:hi].astype(F32)
            u = up_ref[:, lo:hi].astype(F32)
            sg = _sigmoid(gt)
            dup = _bf(dact * (gt * sg))
            dgate = _bf(dact * u * (sg * (1.0 + gt * (1.0 - sg))))
            dup_ref[:, lo:hi] = dup
            dgate_ref[:, lo:hi] = dgate
            dh = dh + _dot(dgate, wg_ref[lo:hi, :]) + _dot(dup, wu_ref[lo:hi, :])
        g = g_ref[...]
        _, xh, r = _rms_fwd(x2_ref[...], g)
        dx2_ref[...] = dx3 + _rms_bwd(dh, xh, r, g)

        @pl.when(i == 0)
        def _():
            dg_ref[...] = jnp.zeros_like(dg_ref)

        dg_ref[...] += _colsum(dh * xh)

    hid = jax.ShapeDtypeStruct((t_len, ff), BF16)
    hid_spec = _rows(tm, ff)
    return _call(
        body, name="ffn_bwd", grid=(t_len // tm,),
        in_specs=[_rows(tm, D_MODEL), _rows(tm, D_MODEL), hid_spec, hid_spec, _res((1, D_MODEL)),
                  _res(w_gate.shape), _res(w_up.shape), _res(w_down.shape)],
        out_specs=[_rows(tm, D_MODEL), hid_spec, hid_spec, pl.BlockSpec((1, D_MODEL), lambda i: (0, 0))],
        out_shape=(jax.ShapeDtypeStruct((t_len, D_MODEL), F32), hid, hid, jax.ShapeDtypeStruct((1, D_MODEL), F32)),
        compiler_params=_cp(1),
    )(dx3, x2, gate, up, g_ffn, w_gate, w_up, w_down)


def _mix_mem_bwd(dx2, x1, qm, kv, g_q, w_q, w_o, w_out, tm, rides=()):
    t_len = x1.shape[0]
    m_len = kv.shape[1]

    def body(dx2_ref, x1_ref, qm_ref, kv_ref, g_ref, wq_ref, wo_ref, wout_ref,
             dx1_ref, dqm_ref, dyc_ref, dya_ref, dkv_ref, dgq_ref, dbout_ref):
        i = pl.program_id(0)

        @pl.when(i == 0)
        def _():
            dkv_ref[...] = jnp.zeros_like(dkv_ref)
            dgq_ref[...] = jnp.zeros_like(dgq_ref)
            dbout_ref[...] = jnp.zeros_like(dbout_ref)

        dx2 = dx2_ref[...]
        dom = _dot_nt(_bf(dx2), wo_ref[...])
        dqs = []
        for h in range(MEM_HEADS):
            kh, vh = _mem_heads(kv_ref, h)
            qh = qm_ref[:, MEM_HD * h:MEM_HD * (h + 1)]
            s = _dot_nt(qh, kh) * MEM_SCALE
            p = jnp.exp(s - jnp.max(s, axis=-1, keepdims=True))
            p = p * (1.0 / jnp.sum(p, axis=-1, keepdims=True))
            domh = _bf(dom[:, MEM_HD * h:MEM_HD * (h + 1)])
            dp = _dot_nt(domh, vh)
            ds = _bf(p * (dp - jnp.sum(p * dp, axis=-1, keepdims=True)) * MEM_SCALE)
            dqs.append(_dot(ds, kh))
            lo = MEM_HD * (h % 2)
            dkv_ref[h // 2, :, lo:lo + MEM_HD] += _dot_tn(ds, qh)
            dkv_ref[2 + h // 2, :, lo:lo + MEM_HD] += _dot_tn(_bf(p), domh)
        dqm = _bf(jnp.concatenate(dqs, axis=1))
        dqm_ref[...] = dqm
        dhq = _dot_nt(dqm, wq_ref[...])
        g = g_ref[...]
        _, xh, r = _rms_fwd(x1_ref[...], g)
        dx1 = dx2 + _rms_bwd(dhq, xh, r, g)
        dx1_ref[...] = dx1
        dgq_ref[...] += _colsum(dhq * xh)
        dbout_ref[...] += _colsum(dx1)
        dymix = _dot_nt(_bf(dx1), wout_ref[...])
        dyc_ref[...] = dymix[:, 0:CONV_CH]
        dya_ref[...] = _bf(dymix[:, CONV_CH:2 * CONV_CH])

    vec = pl.BlockSpec((1, D_MODEL), lambda i: (0, 0))
    return _call(
        body, rides=rides, name="mix_mem_bwd", grid=(t_len // tm,),
        in_specs=[_rows(tm, D_MODEL), _rows(tm, D_MODEL), _rows(tm, D_MODEL), _res(kv.shape), _res((1, D_MODEL)),
                  _res(w_q.shape), _res(w_o.shape), _res(w_out.shape)],
        out_specs=[_rows(tm, D_MODEL), _rows(tm, D_MODEL), _rows(tm, CONV_CH), _rows(tm, CONV_CH),
                   pl.BlockSpec(kv.shape, lambda i: (0, 0, 0)), vec, vec],
        out_shape=(jax.ShapeDtypeStruct((t_len, D_MODEL), F32), jax.ShapeDtypeStruct((t_len, D_MODEL), BF16),
                   jax.ShapeDtypeStruct((t_len, CONV_CH), F32), jax.ShapeDtypeStruct((t_len, CONV_CH), BF16),
                   jax.ShapeDtypeStruct((N_CHIPS, m_len, kv.shape[2]), F32),
                   jax.ShapeDtypeStruct((1, D_MODEL), F32), jax.ShapeDtypeStruct((1, D_MODEL), F32)),
        compiler_params=_cp(1),
    )(dx2, x1, qm, kv, g_q, w_q, w_o, w_out)


def _mem_kv_bwd(dkv, memn, mem, g_kv, w_kv):
    m_len = mem.shape[0]

    def body(dkv_ref, memn_ref, mem_ref, g_ref, w_ref, dw_ref, dg_ref):
        hb = memn_ref[...]
        dmn = jnp.zeros((m_len, D_MODEL), F32)
        for s in range(N_CHIPS):
            d = _bf(dkv_ref[s])
            dw_ref[s] = _bf(_dot_tn(hb, d))
            dmn = dmn + _dot_nt(d, w_ref[s])
        _, xh, _ = _rms_fwd(mem_ref[...], g_ref[...])
        dg_ref[...] = _colsum(dmn * xh)

    return _call(
        body, name="mem_kv_bwd",
        out_shape=(jax.ShapeDtypeStruct(w_kv.shape, BF16), jax.ShapeDtypeStruct((1, D_MODEL), F32)),
        compiler_params=pltpu.CompilerParams(vmem_limit_bytes=VMEM_LIMIT_BYTES),
    )(dkv, memn, mem, g_kv, w_kv)


def _attn_bwd_q(q, kd, vd, dya, lse, sink_b, bias, cos_t, sin_t, rides=()):
    t_len = q.shape[0]
    nb = t_len // BLK

    def body(q_ref, kp_ref, kc_ref, kn_ref, vp_ref, vc_ref, vn_ref, do_ref, lse_ref, sk_ref, bias_ref, c_ref, s_ref,
             dq_ref, dd_ref, dsk_ref):
        kcat = jnp.concatenate([kp_ref[...], kc_ref[...], kn_ref[...]], axis=0)
        vcat = jnp.concatenate([vp_ref[...], vc_ref[...], vn_ref[...]], axis=0)
        dqs, dsks = [], []
        for g in range(2):
            qs = _stack_heads(q_ref[:, 256 * g:256 * g + 256])
            dos = _stack_heads(do_ref[:, 256 * g:256 * g + 256])
            kk = kcat[:, 128 * g:128 * g + 128]
            s = _dot_nt(qs, kk) * ATT_SCALE + bias_ref[0]
            lse_b = lse_ref[0, g]
            p = jnp.exp(s - _tile3(lse_b))
            dp = _dot_nt(dos, vcat[:, 128 * g:128 * g + 128])
            drow = jnp.sum(p * dp, axis=-1, keepdims=True)
            ds = _bf(p * (dp - drow) * ATT_SCALE)
            dqs.append(_unstack_heads(_dot(ds, kk)))
            d_b = jnp.broadcast_to(drow, (4 * BLK, 128))
            dd_ref[0, g] = d_b
            contrib = -(jnp.exp(_sink_col(sk_ref, g) - lse_b) * d_b)
            dsks.append(jnp.sum(contrib.reshape(4, BLK, 128), axis=1))
        dq = jnp.concatenate(dqs, axis=1)
        dq_ref[...] = _bf(_rope(dq, c_ref[...], s_ref[...], -1.0))
        dsk_ref[0] = jnp.concatenate(dsks, axis=0)

    stat = pl.BlockSpec((1, 2, 4 * BLK, 128), lambda i: (i, 0, 0, 0))
    return _call(
        body, rides=rides, name="attn_bwd_q", grid=(nb,),
        in_specs=[_rows(BLK, 512)] + _nbr_specs(256, nb) + _nbr_specs(256, nb)
        + [_rows(BLK, 512), stat, _res((8, 128)), _edge_spec((4 * BLK, 3 * BLK), nb), _rows(BLK, 128), _rows(BLK, 128)],
        out_specs=[_rows(BLK, 512), stat, pl.BlockSpec((1, 8, 128), lambda i: (i, 0, 0))],
        out_shape=(jax.ShapeDtypeStruct((t_len, 512), BF16), jax.ShapeDtypeStruct((nb, 2, 4 * BLK, 128), F32),
                   jax.ShapeDtypeStruct((nb, 8, 128), F32)),
        compiler_params=_cp(1),
    )(q, kd, kd, kd, vd, vd, vd, dya, lse, sink_b, bias, cos_t, sin_t)


def _attn_bwd_kv(q, kd, vd, dya, lse, dd, bias, cos_t, sin_t, rides=()):
    t_len = q.shape[0]
    nb = t_len // BLK

    def body(kc_ref, vc_ref, qp_ref, qc_ref, qn_ref, dop_ref, doc_ref, don_ref, lp_ref, lc_ref, ln_ref,
             dp_ref, dc_ref, dn_ref, bias_ref, c_ref, s_ref, dk_ref, dv_ref):
        dks, dvs = [], []
        for g in range(2):
            cols = slice(256 * g, 256 * g + 256)
            qs = jnp.concatenate([_stack_heads(r[:, cols]) for r in (qp_ref, qc_ref, qn_ref)], axis=0)
            dos = jnp.concatenate([_stack_heads(r[:, cols]) for r in (dop_ref, doc_ref, don_ref)], axis=0)
            lse_b = jnp.concatenate([r[0, g] for r in (lp_ref, lc_ref, ln_ref)], axis=0)
            d_b = jnp.concatenate([r[0, g] for r in (dp_ref, dc_ref, dn_ref)], axis=0)
            kk = kc_ref[:, 128 * g:128 * g + 128]
            s = _dot_nt(qs, kk) * ATT_SCALE + bias_ref[0]
            p = jnp.exp(s - lse_b)
            dp = _dot_nt(dos, vc_ref[:, 128 * g:128 * g + 128])
            ds = _bf(p * (dp - d_b) * ATT_SCALE)
            dvs.append(_dot_tn(_bf(p), dos))
            dks.append(_dot_tn(ds, qs))
        dk_ref[...] = _bf(_rope(_fold_heads(dks), c_ref[...], s_ref[...], -1.0))
        dv_ref[...] = _bf(_fold_heads(dvs))

    return _call(
        body, rides=rides, name="attn_bwd_kv", grid=(nb,),
        in_specs=[_rows(BLK, 256), _rows(BLK, 256)] + _nbr_specs(512, nb) + _nbr_specs(512, nb) + _nbr_specs4(nb) + _nbr_specs4(nb)
        + [_edge_spec((12 * BLK, BLK), nb), _rows(BLK, 128), _rows(BLK, 128)],
        out_specs=[_rows(BLK, 128), _rows(BLK, 128)],
        out_shape=(jax.ShapeDtypeStruct((t_len, 128), BF16), jax.ShapeDtypeStruct((t_len, 128), BF16)),
        compiler_params=_cp(1),
    )(kd, vd, q, q, q, dya, dya, dya, lse, lse, lse, dd, dd, dd, bias, cos_t, sin_t)


def _conv_norm_bwd(pre, dyc, g_ln, b_ln, tc, rides=()):
    t_len = pre.shape[0]

    def body(pre_ref, dy_ref, g_ref, b_ref, dpre_ref, stats_ref):
        i = pl.program_id(0)
        pre_v = pre_ref[...]
        mu = jnp.mean(pre_v, axis=-1, keepdims=True)
        d = pre_v - mu
        rstd = lax.rsqrt(jnp.mean(d * d, axis=-1, keepdims=True) + EPS)
        xh = d * rstd
        g = g_ref[...]
        ln = xh * g + b_ref[...]
        sg = _sigmoid(ln)
        dln = dy_ref[...] * (sg * (1.0 + ln * (1.0 - sg)))
        dxh = dln * g
        dpre = rstd * (dxh - jnp.mean(dxh, axis=-1, keepdims=True) - xh * jnp.mean(dxh * xh, axis=-1, keepdims=True))
        dpre_ref[...] = dpre

        @pl.when(i == 0)
        def _():
            stats_ref[...] = jnp.zeros_like(stats_ref)

        stats_ref[0:1, :] += _colsum(dln * xh)
        stats_ref[1:2, :] += _colsum(dln)
        stats_ref[2:3, :] += _colsum(dpre)

    return _call(
        body, rides=rides, name="conv_norm_bwd", grid=(t_len // tc,),
        in_specs=[_rows(tc, CONV_CH), _rows(tc, CONV_CH), _res((1, CONV_CH)), _res((1, CONV_CH))],
        out_specs=[_rows(tc, CONV_CH), pl.BlockSpec((8, CONV_CH), lambda i: (0, 0))],
        out_shape=(jax.ShapeDtypeStruct((t_len, CONV_CH), F32), jax.ShapeDtypeStruct((8, CONV_CH), F32)),
        compiler_params=_cp(1),
    )(pre, dyc, g_ln, b_ln)


def _conv_bwd(dpre, ug, w_dw, tc, rides=()):
    t_len = ug.shape[0]
    n_tiles = t_len // tc

    def body(dp_ref, dm_ref, dn_ref, up_ref, um_ref, un_ref, w_ref, du_ref, dw_ref, dbuf, vbuf, dshift, vshift):
        i = pl.program_id(0)
        _fill_halo_buf(dbuf, dp_ref[...], dm_ref[...], dn_ref[...], i, n_tiles, tc)
        _fill_halo_buf(vbuf, _glu(up_ref[...]), _glu(um_ref[...]), _glu(un_ref[...]), i, n_tiles, tc)
        _shift_copies(dbuf, dshift, tc)
        _shift_copies(vbuf, vshift, tc)

        @pl.when(i == 0)
        def _():
            dw_ref[...] = jnp.zeros_like(dw_ref)

        def chunk(c, carry):
            base = c * CONV_ROWS
            rows = pl.ds(pl.multiple_of(base, CONV_ROWS), CONV_ROWS)
            dmain = dm_ref[rows, :]
            dv = jnp.zeros((CONV_ROWS, CONV_CH), F32)
            for k in range(CONV_W):
                dv = dv + w_ref[k:k + 1, :] * _shifted_rows(dbuf, dshift, 31 - k, base)
                prod = dmain * _shifted_rows(vbuf, vshift, k + 1, base)
                dw_ref[8 * k:8 * k + 8, :] += jnp.sum(prod.reshape(CONV_ROWS // 8, 8, CONV_CH), axis=0)
            um = um_ref[rows, :]
            a, gt = um[:, 0:CONV_CH], um[:, CONV_CH:2 * CONV_CH]
            sg = _sigmoid(gt)
            du_ref[rows, :] = _bf(jnp.concatenate([dv * sg, dv * a * (sg * (1.0 - sg))], axis=1))
            return carry

        lax.fori_loop(0, tc // CONV_ROWS, chunk, 0)

    shifts = pltpu.VMEM((7, tc + 24, CONV_CH), F32)
    return _call(
        body, rides=rides, name="conv_bwd", grid=(n_tiles,),
        in_specs=_halo_specs(tc, CONV_CH, t_len) + _halo_specs(tc, 1024, t_len) + [_res((32, CONV_CH))],
        out_specs=[_rows(tc, 1024), pl.BlockSpec((8 * 32, CONV_CH), lambda i: (0, 0))],
        out_shape=(jax.ShapeDtypeStruct((t_len, 1024), BF16), jax.ShapeDtypeStruct((8 * 32, CONV_CH), F32)),
        scratch_shapes=[pltpu.VMEM((tc + 32, CONV_CH), F32), pltpu.VMEM((tc + 32, CONV_CH), F32), shifts, shifts],
        compiler_params=_cp(1),
    )(dpre, dpre, dpre, ug, ug, ug, w_dw)


def _in_proj_bwd(du_glu, dq, dk, dv, dx1, x, g_mix, w_t, tm, rides=()):
    t_len = x.shape[0]
    n_ext = w_t.shape[0]

    def body(dg_ref, dq_ref, dk_ref, dv_ref, dx1_ref, x_ref, g_ref, w_ref, dx_ref, du_ref, db_ref, dgm_ref):
        i = pl.program_id(0)
        du = jnp.concatenate([dg_ref[...], dq_ref[...], dk_ref[...], dv_ref[...]], axis=1)
        du_ref[...] = du
        dh = _dot(du, w_ref[...])
        g = g_ref[...]
        _, xh, r = _rms_fwd(x_ref[...], g)
        dx_ref[...] = dx1_ref[...] + _rms_bwd(dh, xh, r, g)

        @pl.when(i == 0)
        def _():
            db_ref[...] = jnp.zeros_like(db_ref)
            dgm_ref[...] = jnp.zeros_like(dgm_ref)

        db_ref[...] += _colsum(du.astype(F32))
        dgm_ref[...] += _colsum(dh * xh)

    return _call(
        body, rides=rides, name="in_proj_bwd", grid=(t_len // tm,),
        in_specs=[_rows(tm, 1024), _rows(tm, 512), _rows(tm, 128), _rows(tm, 128), _rows(tm, D_MODEL), _rows(tm, D_MODEL),
                  _res((1, D_MODEL)), _res(w_t.shape)],
        out_specs=[_rows(tm, D_MODEL), _rows(tm, n_ext), pl.BlockSpec((1, n_ext), lambda i: (0, 0)),
                   pl.BlockSpec((1, D_MODEL), lambda i: (0, 0))],
        out_shape=(jax.ShapeDtypeStruct((t_len, D_MODEL), F32), jax.ShapeDtypeStruct((t_len, n_ext), BF16),
                   jax.ShapeDtypeStruct((1, n_ext), F32), jax.ShapeDtypeStruct((1, D_MODEL), F32)),
        compiler_params=_cp(1),
    )(du_glu, dq, dk, dv, dx1, x, g_mix, w_t)


def _weight_grad(a, d, name, tt, rides=()):
    t_len, k_dim = a.shape
    n_dim = d.shape[1]
    tk = k_dim if k_dim <= 1792 else k_dim // 2
    assert k_dim % tk == 0 and tk % 128 == 0 and n_dim % 128 == 0
    tt = min(tt, t_len)
    n_t = t_len // tt

    def body(a_ref, d_ref, o_ref, acc):
        t = pl.program_id(1)

        @pl.when(t == 0)
        def _():
            acc[...] = jnp.zeros_like(acc)

        acc[...] += _dot_tn(_bf(a_ref[...]), _bf(d_ref[...]))

        @pl.when(t == n_t - 1)
        def _():
            o_ref[...] = _bf(acc[...])

    res = _call(
        body, rides=rides, name=name, grid=(k_dim // tk, n_t),
        in_specs=[pl.BlockSpec((tt, tk), lambda k, t: (t, k)), pl.BlockSpec((tt, n_dim), lambda k, t: (t, 0))],
        out_specs=[pl.BlockSpec((tk, n_dim), lambda k, t: (k, 0))],
        out_shape=[jax.ShapeDtypeStruct((k_dim, n_dim), BF16)],
        scratch_shapes=[pltpu.VMEM((tk, n_dim), F32)],
        compiler_params=_cp(2),
    )(a, d)
    return res if rides else res[0]


ANY = pl.BlockSpec(memory_space=pl.ANY)


def _place():
    x, y, c = lax.axis_index("x"), lax.axis_index("y"), lax.axis_index("c")
    chips = [(1 - x, y), (x, 1 - y), (1 - x, 1 - y)]
    return x, y, c, chips


def _remote(src, dst, send_sems, recv_sems, k, to):
    return pltpu.make_async_remote_copy(src_ref=src, dst_ref=dst, send_sem=send_sems.at[k], recv_sem=recv_sems.at[k],
                                        device_id=to, device_id_type=MESH)


def _by_shape(names, arrays):
    groups = {}
    for k in names:
        groups.setdefault(arrays[k].shape, []).append(k)
    return list(groups.values())


def _cast_place(ws, chip_idx, tr):
    n = len(ws)
    rows, cols = ws[0].shape
    h = rows // 2
    tr = _div_tile(h, tr)
    per = h // tr

    def body(s_ref, *refs):
        for w_ref, o_ref in zip(refs[:n], refs[n:]):
            o_ref[0, 0] = _bf(w_ref[...])

    return _call(
        body, name="cast_place",
        grid_spec=pltpu.PrefetchScalarGridSpec(
            num_scalar_prefetch=1, grid=(2, per),
            in_specs=[pl.BlockSpec((tr, cols), lambda hh, r, s_ref: (hh * per + r, 0))] * n,
            out_specs=[pl.BlockSpec((1, 1, tr, cols), lambda hh, r, s_ref: (s_ref[0], hh, r, 0))] * n),
        out_shape=[jax.ShapeDtypeStruct((N_CHIPS, 2, h, cols), BF16)] * n,
        compiler_params=_cp(2),
    )(chip_idx, *ws)


def _same(arrays):
    return [jax.ShapeDtypeStruct(a.shape, a.dtype) for a in arrays]


def _gather_ride(bufs):
    n = len(bufs)

    def first_hop(outs, send, recv):
        x, y, c, chips = _place()
        mine = [outs[i].at[2 * x + y, c] for i in range(n)]
        return [_remote(mine[i], mine[i], send, recv, 3 * i + j, (cx, cy, c)) for i in range(n) for j, (cx, cy) in enumerate(chips)]

    def start(ins, outs, send, recv):
        for cp in first_hop(outs, send, recv):
            cp.start()

    def finish(ins, outs, send, recv):
        x, y, c, chips = _place()
        sib = (x, y, 1 - c)
        onward = []
        for i in range(n):
            for j, (cx, cy) in enumerate(chips):
                slab = outs[i].at[2 * cx + cy, c]
                _remote(slab, slab, send, recv, 3 * i + j, sib).wait_recv()
                onward.append(_remote(slab, slab, send, recv, 3 * n + 3 * i + j, sib))
                onward[-1].start()
        for i in range(n):
            for j, (cx, cy) in enumerate(chips):
                other = outs[i].at[2 * cx + cy, 1 - c]
                _remote(other, other, send, recv, 3 * n + 3 * i + j, sib).wait_recv()
        for cp in first_hop(outs, send, recv) + onward:
            cp.wait_send()

    return _Ride(bufs, _same(bufs), 6 * n, start, finish, aliases={i: i for i in range(n)})


def _spread_ride(buf):
    def sends(outs, send, recv):
        x, y, c, chips = _place()
        mine = outs[0].at[2 * x + y]
        return [_remote(mine, mine, send, recv, j, (cx, cy, c)) for j, (cx, cy) in enumerate(chips)]

    def start(ins, outs, send, recv):
        for cp in sends(outs, send, recv):
            cp.start()

    def finish(ins, outs, send, recv):
        _, _, c, chips = _place()
        for j, (cx, cy) in enumerate(chips):
            slab = outs[0].at[2 * cx + cy]
            _remote(slab, slab, send, recv, j, (cx, cy, c)).wait_recv()
        for cp in sends(outs, send, recv):
            cp.wait_send()

    return _Ride([buf], _same([buf]), 3, start, finish, aliases={0: 0})


def _allgather_ride(buf):
    def peers():
        x, y, c, _ = _place()
        return [(x ^ ((k >> 2) & 1), y ^ ((k >> 1) & 1), c ^ (k & 1)) for k in range(1, N_DEV)], 4 * x + 2 * y + c

    def sends(outs, send, recv):
        to, me = peers()
        mine = outs[0].at[me]
        return [_remote(mine, mine, send, recv, k, p) for k, p in enumerate(to)]

    def start(ins, outs, send, recv):
        for cp in sends(outs, send, recv):
            cp.start()

    def finish(ins, outs, send, recv):
        for k, (px, py, pc) in enumerate(peers()[0]):
            slab = outs[0].at[4 * px + 2 * py + pc]
            _remote(slab, slab, send, recv, k, (px, py, pc)).wait_recv()
        for cp in sends(outs, send, recv):
            cp.wait_send()

    return _Ride([buf], _same([buf]), N_DEV - 1, start, finish, aliases={0: 0})


def _sum_slabs(buf):
    def body(b_ref, o_ref):
        acc = b_ref[0]
        for d in range(1, buf.shape[0]):
            acc = acc + b_ref[d]
        o_ref[...] = acc

    return _call(body, name="sum_slabs", out_shape=jax.ShapeDtypeStruct(buf.shape[1:], buf.dtype))(buf)


def _pairwise_ride(arrays, out_shape, n_sem, copies):
    def start(ins, outs, send, recv):
        for cp in copies(ins, outs, send, recv):
            cp.start()

    def finish(ins, outs, send, recv):
        for cp in copies(ins, outs, send, recv):
            cp.wait()

    return _Ride(arrays, out_shape, n_sem, start, finish)


def _run_rides(name, rides):
    k_in = [len(r.operands) for r in rides]
    k_out = [len(r.out_shape) for r in rides]

    def body(*refs):
        pos, r_in, r_out = 0, [], []
        for k in k_in:
            r_in.append(refs[pos:pos + k])
            pos += k
        for k in k_out:
            r_out.append(refs[pos:pos + k])
            pos += k
        sems = refs[pos:]
        for j, r in enumerate(rides):
            r.start(r_in[j], r_out[j], sems[2 * j], sems[2 * j + 1])
        for j, r in enumerate(rides):
            r.finish(r_in[j], r_out[j], sems[2 * j], sems[2 * j + 1])

    aliases, off_in, off_out = {}, 0, 0
    for r, ki, ko in zip(rides, k_in, k_out):
        aliases.update({off_in + a: off_out + b for a, b in r.aliases.items()})
        off_in, off_out = off_in + ki, off_out + ko
    res = _call(
        body, name=name, in_specs=[ANY] * sum(k_in), out_specs=[ANY] * sum(k_out),
        out_shape=[s for r in rides for s in r.out_shape], input_output_aliases=aliases,
        scratch_shapes=[pltpu.SemaphoreType.DMA((r.n_sem,)) for r in rides for _ in range(2)],
    )(*[op for r in rides for op in r.operands])
    out, pos = [], 0
    for k in k_out:
        out.append(list(res[pos:pos + k]))
        pos += k
    return out


def _swap_ride(grads):
    def copies(ins, outs, send, recv):
        x, y, c, _ = _place()
        return [_remote(ins[i].at[:, 1 - c], outs[i], send, recv, i, (x, y, 1 - c)) for i in range(len(grads))]

    out_shape = [jax.ShapeDtypeStruct((g.shape[0],) + g.shape[2:], g.dtype) for g in grads]
    return _pairwise_ride(grads, out_shape, len(grads), copies)


def _pair_sum(grads, others, c_idx, tr):
    n = len(grads)
    n_s, _, h, cols = grads[0].shape
    tr = _div_tile(h, tr)

    def body(c_ref, *refs):
        for a_ref, b_ref, o_ref in zip(refs[:n], refs[n:2 * n], refs[2 * n:]):
            o_ref[...] = _bf(a_ref[0].astype(F32) + b_ref[...].astype(F32))

    return _call(
        body, name="pair_sum",
        grid_spec=pltpu.PrefetchScalarGridSpec(
            num_scalar_prefetch=1, grid=(n_s, h // tr),
            in_specs=[pl.BlockSpec((1, 1, tr, cols), lambda s, r, c_ref: (s, c_ref[0], r, 0))] * n
            + [pl.BlockSpec((1, tr, cols), lambda s, r, c_ref: (s, r, 0))] * n,
            out_specs=[pl.BlockSpec((1, tr, cols), lambda s, r, c_ref: (s, r, 0))] * n),
        out_shape=[jax.ShapeDtypeStruct((n_s, h, cols), BF16)] * n,
        compiler_params=_cp(2),
    )(c_idx, *grads, *others)


def _exchange_ride(sums):
    def copies(ins, outs, send, recv):
        _, _, c, chips = _place()
        return [_remote(ins[i].at[2 * cx + cy], outs[i].at[j], send, recv, 3 * i + j, (cx, cy, c))
                for i in range(len(sums)) for j, (cx, cy) in enumerate(chips)]

    out_shape = [jax.ShapeDtypeStruct((3,) + s.shape[1:], s.dtype) for s in sums]
    return _pairwise_ride(sums, out_shape, 3 * len(sums), copies)


def _chip_sum(owns, others, chip_idx, tr):
    n = len(owns)
    _, h, cols = owns[0].shape
    tr = _div_tile(h, tr)

    def body(s_ref, *refs):
        for a_ref, p_ref, o_ref in zip(refs[:n], refs[n:2 * n], refs[2 * n:]):
            acc = a_ref[0].astype(F32)
            for j in range(N_CHIPS - 1):
                acc = acc + p_ref[j].astype(F32)
            o_ref[...] = acc

    return _call(
        body, name="chip_sum",
        grid_spec=pltpu.PrefetchScalarGridSpec(
            num_scalar_prefetch=1, grid=(h // tr,),
            in_specs=[pl.BlockSpec((1, tr, cols), lambda r, s_ref: (s_ref[0], r, 0))] * n
            + [pl.BlockSpec((N_CHIPS - 1, tr, cols), lambda r, s_ref: (0, r, 0))] * n,
            out_specs=[pl.BlockSpec((tr, cols), lambda r, s_ref: (r, 0))] * n),
        out_shape=[jax.ShapeDtypeStruct((h, cols), F32)] * n,
        compiler_params=_cp(1),
    )(chip_idx, *owns, *others)


def _share_ride(halves):
    def copies(ins, outs, send, recv):
        x, y, c, _ = _place()
        return [_remote(ins[i], outs[i], send, recv, i, (x, y, 1 - c)) for i in range(len(halves))]

    return _pairwise_ride(halves, _same(halves), len(halves), copies)


def _adamw_math(w, g, m, v):
    m_new = ADAM_B1 * m + (1.0 - ADAM_B1) * g
    v_new = ADAM_B2 * v + (1.0 - ADAM_B2) * (g * g)
    m_hat = m_new * (1.0 / (1.0 - ADAM_B1 ** ADAM_STEP))
    v_hat = v_new * (1.0 / (1.0 - ADAM_B2 ** ADAM_STEP))
    delta = -ADAM_LR * (m_hat / (jnp.sqrt(v_hat) + ADAM_EPS) + ADAM_WD * w)
    return delta, m_new, v_new


def _adamw(w, g_mine, g_other, m, v, core_idx, tr):
    rows, cols = w.shape
    h = rows // 2
    tr = _div_tile(h, tr)
    per = h // tr

    def body(c_ref, w_ref, ga_ref, gb_ref, m_ref, v_ref, g_ref, d_ref, mo_ref, vo_ref):
        g = jnp.where(pl.program_id(0) == c_ref[0], ga_ref[...], gb_ref[...])
        d, mn, vn = _adamw_math(w_ref[...], g, m_ref[...], v_ref[...])
        g_ref[...] = g
        d_ref[...] = d
        mo_ref[...] = mn
        vo_ref[...] = vn

    full = pl.BlockSpec((tr, cols), lambda hh, r, c_ref: (hh * per + r, 0))
    mine = pl.BlockSpec((tr, cols), lambda hh, r, c_ref: (jnp.where(hh == c_ref[0], r, 0), 0))
    other = pl.BlockSpec((tr, cols), lambda hh, r, c_ref: (jnp.where(hh == c_ref[0], 0, r), 0))
    shp = jax.ShapeDtypeStruct(w.shape, F32)
    return _call(
        body, name="adamw",
        grid_spec=pltpu.PrefetchScalarGridSpec(num_scalar_prefetch=1, grid=(2, per), in_specs=[full, mine, other, full, full],
                                               out_specs=[full] * 4),
        out_shape=(shp, shp, shp, shp), compiler_params=_cp(2))(core_idx, w, g_mine, g_other, m, v)


def _adamw_small(ws, gs, ms, vs):
    n = len(ws)

    def body(*refs):
        w_r, g_r, m_r, v_r = refs[0:n], refs[n:2 * n], refs[2 * n:3 * n], refs[3 * n:4 * n]
        d_o, m_o, v_o = refs[4 * n:5 * n], refs[5 * n:6 * n], refs[6 * n:7 * n]
        for i in range(n):
            d, mn, vn = _adamw_math(w_r[i][...], g_r[i][...], m_r[i][...], v_r[i][...])
            d_o[i][...] = d
            m_o[i][...] = mn
            v_o[i][...] = vn

    shp = [jax.ShapeDtypeStruct(w.shape, F32) for w in ws]
    outs = _call(body, name="adamw_small", out_shape=shp * 3)(*ws, *gs, *ms, *vs)
    return outs[0:n], outs[n:2 * n], outs[2 * n:3 * n]


def _rope_tables(t_len):
    pos = jnp.arange(t_len, dtype=F32)
    inv_freq = ROPE_THETA ** (-jnp.arange(0, HEAD_DIM, 2, dtype=F32) / HEAD_DIM)
    ang = pos[:, None] * inv_freq[None, :]
    cos, sin = jnp.cos(ang), jnp.sin(ang)
    return jnp.tile(jnp.concatenate([cos, cos], axis=1), (1, 2)), jnp.tile(jnp.concatenate([-sin, sin], axis=1), (1, 2))


def _dup_heads(a):
    h0, h1 = a[..., 0:64], a[..., 64:128]
    return jnp.concatenate([h0, h0, h1, h1], axis=-1)


def _local_step(x, mem, target, small, wg, comm, tm_a=512, tm_b=256, tc=512, tt=1024):
    def run(stage, fn, n_own, *operands):
        rides = comm.rides(stage)
        res = fn(*operands, rides=rides)
        res = list(res) if isinstance(res, (list, tuple)) else [res]
        brought, pos = [], n_own
        for r in rides:
            brought.append(res[pos:pos + len(r.out_shape)])
            pos += len(r.out_shape)
        comm.landed(stage, brought, wg)
        return res[:n_own]

    t_len = x.shape[0]
    cos_t, sin_t = _rope_tables(t_len)
    b_in = small["b_in"]
    b_ext = jnp.concatenate([b_in[:, 0:1536], _dup_heads(b_in[:, 1536:1664]), _dup_heads(b_in[:, 1664:1792])], axis=1)
    w_dw = jnp.concatenate([wg["w_dw"], jnp.zeros((1, CONV_CH), F32)], axis=0)
    sink_b = jnp.broadcast_to(small["attn_sink"].reshape(8, 1), (8, 128))
    bias_q, bias_k = _band_bias()

    ug, q, kd, vd, h1 = run("in_proj_fwd", _in_proj_fwd, 5, x, small["g_mix"], wg["w_in"], b_ext, cos_t, sin_t, tm_a)
    yc, pre = run("conv_fwd", _conv_fwd, 2, ug, w_dw, small["b_dw"], small["g_conv_ln"], small["b_conv_ln"], tc)
    memn, kv = _mem_kv_fwd(mem, small["g_mem_kv"], wg["w_mem_kv"])
    ya, lse = run("attn_fwd", _attn_fwd, 2, q, kd, vd, sink_b, bias_q)
    ymix, x1, hq, qm, om, x2 = run("mix_mem_fwd", _mix_mem_fwd, 6, x, yc, ya, wg["w_out"], small["b_out"], small["g_mem_q"],
                                   wg["w_mem_q"], kv, wg["w_mem_o"], tm_a)
    hf, gate, up, act = run("ffn_up", _ffn_up, 4, x2, small["g_ffn"], wg["w_gate"], wg["w_up"], tm_a)
    dx3, loss, d_g_final = _ffn_down_loss(x2, act, wg["w_down"], small["g_final"], target, tm_a)

    dx2, dgate, dup, d_g_ffn = _ffn_bwd(dx3, x2, gate, up, small["g_ffn"], wg["w_gate"], wg["w_up"], wg["w_down"], tm_b)
    comm.grad("w_gate", _weight_grad(dgate, hf, "dw_gate", tt))
    comm.grad("w_up", run("dw_up", _weight_grad, 1, dup, hf, "dw_up", tt)[0])
    comm.grad("w_down", run("dw_down", _weight_grad, 1, act, dx3, "dw_down", tt)[0])
    dx1, dqm, dyc, dya, dkv, d_g_mem_q, d_b_out = run("mix_mem_bwd", _mix_mem_bwd, 7, dx2, x1, qm, kv, small["g_mem_q"],
                                                      wg["w_mem_q"], wg["w_mem_o"], wg["w_out"], tm_a)
    d_w_mem_kv, d_g_mem_kv = _mem_kv_bwd(dkv, memn, mem, small["g_mem_kv"], wg["w_mem_kv"])
    comm.grad("w_mem_kv", d_w_mem_kv)
    comm.grad("w_out", _weight_grad(ymix, dx1, "dw_out", tt))
    comm.grad("w_mem_q", _weight_grad(hq, dqm, "dw_mem_q", tt))
    comm.grad("w_mem_o", _weight_grad(om, dx2, "dw_mem_o", tt))
    dq, dd, dsink = run("attn_bwd_q", _attn_bwd_q, 3, q, kd, vd, dya, lse, sink_b, bias_q, cos_t, sin_t)
    dk, dv = run("attn_bwd_kv", _attn_bwd_kv, 2, q, kd, vd, dya, lse, dd, bias_k, cos_t, sin_t)
    dpre, cstats = run("conv_norm_bwd", _conv_norm_bwd, 2, pre, dyc, small["g_conv_ln"], small["b_conv_ln"], tc)
    du_glu, d_w_dw = run("conv_bwd", _conv_bwd, 2, dpre, ug, w_dw, tc)
    grad_x, du, d_b_in, d_g_mix = run("in_proj_bwd", _in_proj_bwd, 4, du_glu, dq, dk, dv, dx1, x, small["g_mix"], wg["w_in"],
                                      tm_a)
    grads = {
        "w_dw": jnp.sum(d_w_dw.reshape(32, 8, CONV_CH), axis=1)[0:CONV_W],
        "g_mix": d_g_mix, "b_in": d_b_in, "b_dw": cstats[2:3], "g_conv_ln": cstats[0:1],
        "b_conv_ln": cstats[1:2], "attn_sink": jnp.sum(dsink[:, :, 0], axis=0)[None, :], "b_out": d_b_out,
        "g_mem_q": d_g_mem_q, "g_mem_kv": d_g_mem_kv, "g_ffn": d_g_ffn, "g_final": d_g_final,
    }
    comm.small(loss[0:1, 0:1], grads)
    (d_w_in,) = run("dw_in", _weight_grad, 1, du, h1, "dw_in", tt)
    comm.grad("w_in", d_w_in)
    return loss[0:1, 0:1], grad_x, grads


BIG = ["w_in", "w_out", "w_mem_q", "w_mem_kv", "w_mem_o", "w_gate", "w_up", "w_down"]
KEEP_SLABS = ("w_mem_kv",)
TRANSPOSED = ("w_in", "w_gate", "w_up")
SMALL = ["g_mix", "b_in", "b_dw", "g_conv_ln", "b_conv_ln", "attn_sink", "b_out", "g_mem_q", "g_mem_kv", "g_ffn", "g_final"]
PACK_ROWS = 32
ROWS_PER_STEP = 512

GATHER_ON = {"in_proj_fwd": ("w_out", "w_mem_q"), "conv_fwd": ("w_mem_kv",), "attn_fwd": ("w_mem_o", "w_gate"),
             "mix_mem_fwd": ("w_up",), "ffn_up": ("w_down",)}
MID_GROUP = ("w_mem_kv", "w_out", "w_mem_q", "w_mem_o")
SWAP_ON = {"dw_up": ("w_gate",), "dw_down": ("w_up",), "mix_mem_bwd": ("w_down",), "attn_bwd_q": MID_GROUP}
EXCHANGE_ON = {"mix_mem_bwd": ("w_gate",), "attn_bwd_q": ("w_up",), "attn_bwd_kv": ("w_down", "w_out"),
               "conv_bwd": ("w_mem_kv", "w_mem_q", "w_mem_o")}
SMALL_ON = "dw_in"


def _as_weight(name, gathered):
    g = gathered.reshape(N_CHIPS, gathered.shape[2] * 2, gathered.shape[3])
    return g if name in KEEP_SLABS else g.reshape(-1, g.shape[2])


class _Overlap:
    def __init__(self, bufs, chip_idx, core_idx):
        self.bufs, self.chip_idx, self.core_idx = bufs, chip_idx, core_idx
        self.parts, self.sums, self.others = {}, {}, {}

    def rides(self, stage):
        rides = []
        if stage in GATHER_ON:
            rides.append(_gather_ride([self.bufs[k] for k in GATHER_ON[stage]]))
        if stage in EXCHANGE_ON:
            rides.append(_exchange_ride([self.sums[k] for k in EXCHANGE_ON[stage]]))
        if stage in SWAP_ON:
            rides.append(_swap_ride([self.parts[k] for k in SWAP_ON[stage]]))
        if stage == SMALL_ON:
            rides.append(_allgather_ride(self.packs))
        return rides

    def landed(self, stage, brought, wg):
        brought = list(brought)
        if stage in GATHER_ON:
            for k, g in zip(GATHER_ON[stage], brought.pop(0)):
                wg[k] = _as_weight(k, g)
        if stage in EXCHANGE_ON:
            self.others.update(zip(EXCHANGE_ON[stage], brought.pop(0)))
        if stage in SWAP_ON:
            self._pair(SWAP_ON[stage], brought.pop(0))
        if stage == SMALL_ON:
            (self.packs,) = brought.pop(0)

    def small(self, loss, grads):
        x, y, c = lax.axis_index("x"), lax.axis_index("y"), lax.axis_index("c")
        pack = _pack_small(loss, grads)
        self.packs = lax.dynamic_update_slice(jnp.zeros((N_DEV,) + pack.shape, F32), pack[None], (4 * x + 2 * y + c, 0, 0))

    def grad(self, name, g):
        if g.ndim == 2:
            g = g.reshape(N_CHIPS, g.shape[0] // N_CHIPS, g.shape[1])
        self.parts[name] = g.reshape(N_CHIPS, 2, g.shape[1] // 2, g.shape[2])

    def _pair(self, names, from_sibling):
        came = dict(zip(names, from_sibling))
        for group in _by_shape(names, self.parts):
            sums = _pair_sum([self.parts[k] for k in group], [came[k] for k in group], self.core_idx, ROWS_PER_STEP)
            self.sums.update(zip(group, sums))

    def finish(self):
        (from_sibling,) = _run_rides("swap_last", [_swap_ride([self.parts["w_in"]])])
        self._pair(("w_in",), from_sibling)
        ((self.others["w_in"],),) = _run_rides("exchange_last", [_exchange_ride([self.sums["w_in"]])])
        halves = {}
        for group in _by_shape(BIG, self.sums):
            res = _chip_sum([self.sums[k] for k in group], [self.others[k] for k in group], self.chip_idx, ROWS_PER_STEP)
            halves.update(zip(group, res))
        mine = [halves[k] for k in BIG]
        (theirs,) = _run_rides("sibling_share", [_share_ride(mine)])
        return mine, theirs, _sum_slabs(self.packs)


def _pack_small(loss, grads):
    def row(a):
        a = a.reshape(1, -1)
        return jnp.pad(a, ((0, 0), (0, 1024 - a.shape[1])))

    rows = [row(grads[k]) for k in ("g_mix", "b_out", "g_mem_q", "g_mem_kv", "g_ffn", "g_final")]
    rows += [grads["b_in"][:, 0:1024], row(grads["b_in"][:, 1024:1792])]
    rows += [jnp.concatenate([grads["b_dw"], grads["g_conv_ln"]], axis=1), row(grads["b_conv_ln"]), row(grads["attn_sink"]),
             row(loss)]
    dw = jnp.pad(grads["w_dw"], ((0, 1), (0, 0))).reshape(16, 1024)
    pack = jnp.concatenate(rows + [dw], axis=0)
    return jnp.pad(pack, ((0, PACK_ROWS - pack.shape[0]), (0, 0)))


def _unpack_small(pack):
    out = {k: pack[i:i + 1] for i, k in enumerate(("g_mix", "b_out", "g_mem_q", "g_mem_kv", "g_ffn", "g_final"))}
    out["b_in"] = jnp.concatenate([pack[6:7], pack[7:8, 0:768]], axis=1)
    out["b_dw"], out["g_conv_ln"] = pack[8:9, 0:512], pack[8:9, 512:1024]
    out["b_conv_ln"] = pack[9:10, 0:512]
    out["attn_sink"] = pack[10:11, 0:8]
    loss = pack[11, 0]
    dw = pack[12:28].reshape(32, 512)[0:CONV_W]
    return loss, out, dw


def kernel(x, mem, g_mix, w_in, b_in, w_dw, b_dw, g_conv_ln, b_conv_ln, attn_sink, w_out, b_out, g_mem_q, g_mem_kv, w_mem_q, w_mem_kv, w_mem_o, g_ffn, w_gate, w_up, w_down, g_final, loss_target, m_g_mix, m_w_in, m_b_in, m_w_dw, m_b_dw, m_g_conv_ln, m_b_conv_ln, m_attn_sink, m_w_out, m_b_out, m_g_mem_q, m_g_mem_kv, m_w_mem_q, m_w_mem_kv, m_w_mem_o, m_g_ffn, m_w_gate, m_w_up, m_w_down, m_g_final, v_g_mix, v_w_in, v_b_in, v_w_dw, v_b_dw, v_g_conv_ln, v_b_conv_ln, v_attn_sink, v_w_out, v_b_out, v_g_mem_q, v_g_mem_kv, v_w_mem_q, v_w_mem_kv, v_w_mem_o, v_g_ffn, v_w_gate, v_w_up, v_w_down, v_g_final):
    args = dict(locals())
    weight_names = ["g_mix", "w_in", "b_in", "w_dw", "b_dw", "g_conv_ln", "b_conv_ln", "attn_sink", "w_out", "b_out", "g_mem_q",
                    "g_mem_kv", "w_mem_q", "w_mem_kv", "w_mem_o", "g_ffn", "w_gate", "w_up", "w_down", "g_final"]
    chip = 2 * lax.axis_index("x") + lax.axis_index("y")
    core = lax.axis_index("c")

    chip_idx = chip.astype(jnp.int32).reshape(1)
    core_idx = core.astype(jnp.int32).reshape(1)

    def block(name):
        a = args[name][0]
        weight = name[2:] if name[:2] in ("m_", "v_") else name
        return a.T if weight in TRANSPOSED else a

    blocks = {k: block(k) for k in BIG}
    bufs = {}
    for group in _by_shape(BIG, blocks):
        bufs.update(zip(group, _cast_place([blocks[k] for k in group], chip_idx, ROWS_PER_STEP)))
    comm = _Overlap(bufs, chip_idx, core_idx)
    dw_buf = lax.dynamic_update_slice(jnp.zeros((N_CHIPS, CONV_W, 128), F32), w_dw, (chip, 0, 0))
    (first,), (dw_all,) = _run_rides("gather_first", [_gather_ride([comm.bufs["w_in"]]), _spread_ride(dw_buf)])
    wg = {"w_in": _as_weight("w_in", first), "w_dw": jnp.transpose(dw_all, (1, 0, 2)).reshape(CONV_W, CONV_CH)}
    small = {k: args[k].reshape(1, -1) for k in SMALL}

    loss, grad_x, grads = _local_step(x[0], mem[0], loss_target[0], small, wg, comm)

    halves, other_halves, pack_sum = comm.finish()

    loss_sum, small_grads, dw_full = _unpack_small(pack_sum)
    dw_cols = jnp.transpose(dw_full.reshape(CONV_W, N_CHIPS, 128), (1, 0, 2))
    small_grads["w_dw"] = lax.dynamic_index_in_dim(dw_cols, chip, axis=0, keepdims=False)

    out_g, out_d, out_m, out_v = {}, {}, {}, {}
    for k, g_mine, g_other in zip(BIG, halves, other_halves):
        res = _adamw(block(k), g_mine, g_other, block("m_" + k), block("v_" + k), core_idx, ROWS_PER_STEP)
        out_g[k], out_d[k], out_m[k], out_v[k] = [(r.T if k in TRANSPOSED else r)[None] for r in res]
    names = SMALL + ["w_dw"]

    def flat(a):
        return a[0] if a.ndim == 3 else a.reshape(1, -1)

    def pad_lanes(a):
        return jnp.pad(a, ((0, 0), (0, 128 - a.shape[1]))) if a.shape[1] < 128 else a

    ws = [flat(args[k]) for k in names]
    gs = [small_grads[k] for k in names]
    ms = [flat(args["m_" + k]) for k in names]
    vs = [flat(args["v_" + k]) for k in names]
    ds, mns, vns = _adamw_small([pad_lanes(a) for a in ws], [pad_lanes(a) for a in gs], [pad_lanes(a) for a in ms],
                                [pad_lanes(a) for a in vs])
    for i, k in enumerate(names):
        n_lanes = ws[i].shape[1]
        for out, val in ((out_g, gs[i]), (out_d, ds[i]), (out_m, mns[i]), (out_v, vns[i])):
            out[k] = val[:, 0:n_lanes].reshape(args[k].shape)

    return (loss_sum, grad_x[None], *[out_g[k] for k in weight_names], *[out_d[k] for k in weight_names],
            *[out_m[k] for k in weight_names], *[out_v[k] for k in weight_names])
```

```python
import jax
import jax.numpy as jnp
import numpy as np
from jax import lax
from jax.experimental import pallas as pl
from jax.experimental.pallas import tpu as pltpu

F32 = jnp.float32
BF16 = jnp.bfloat16
EPS = 1e-6
NEG = -1e30

D_MODEL = 1024
CONV_CH = 512
CONV_W = 31
HEAD_DIM = 64
BLK = 128
MEM_HEADS = 4
MEM_HD = 256
N_CHIPS = 4
N_DEV = 8
ATT_SCALE = HEAD_DIM ** -0.5
MEM_SCALE = MEM_HD ** -0.5
ROPE_THETA = 10000.0

ADAM_LR = 0.001
ADAM_B1 = 0.9
ADAM_B2 = 0.999
ADAM_EPS = 1e-08
ADAM_WD = 0.01
ADAM_STEP = 10

VMEM_LIMIT_BYTES = 56 * 1024 * 1024
MESH = pl.DeviceIdType.MESH


class _Ride:
    def __init__(self, operands, out_shape, n_sem, start, finish, aliases=None):
        self.operands, self.out_shape, self.n_sem = list(operands), list(out_shape), n_sem
        self.start, self.finish, self.aliases = start, finish, dict(aliases or {})


def _call(body, rides=(), **kw):
    if not rides:
        return pl.pallas_call(body, **kw)
    grid = kw["grid"]
    n_in, n_out = len(kw["in_specs"]), len(kw["out_specs"])
    scratch = list(kw.get("scratch_shapes", ()))
    k_in = [len(r.operands) for r in rides]
    k_out = [len(r.out_shape) for r in rides]

    def carried(*refs):
        pos = n_in
        r_in, r_out = [], []
        for k in k_in:
            r_in.append(refs[pos:pos + k])
            pos += k
        own_out = refs[pos:pos + n_out]
        pos += n_out
        for k in k_out:
            r_out.append(refs[pos:pos + k])
            pos += k
        own_scratch = refs[pos:pos + len(scratch)]
        sems = refs[pos + len(scratch):]
        first = last = None
        for axis, n_steps in enumerate(grid):
            step = pl.program_id(axis)
            first = (step == 0) if first is None else first & (step == 0)
            last = (step == n_steps - 1) if last is None else last & (step == n_steps - 1)

        @pl.when(first)
        def _():
            for j, r in enumerate(rides):
                r.start(r_in[j], r_out[j], sems[2 * j], sems[2 * j + 1])

        body(*refs[:n_in], *own_out, *own_scratch)

        @pl.when(last)
        def _():
            for j, r in enumerate(rides):
                r.finish(r_in[j], r_out[j], sems[2 * j], sems[2 * j + 1])

    kw = dict(kw)
    kw["in_specs"] = list(kw["in_specs"]) + [ANY] * sum(k_in)
    kw["out_specs"] = list(kw["out_specs"]) + [ANY] * sum(k_out)
    kw["out_shape"] = list(kw["out_shape"]) + [s for r in rides for s in r.out_shape]
    kw["scratch_shapes"] = scratch + [pltpu.SemaphoreType.DMA((r.n_sem,)) for r in rides for _ in range(2)]
    aliases, off_in, off_out = {}, n_in, n_out
    for r, ki, ko in zip(rides, k_in, k_out):
        aliases.update({off_in + a: off_out + b for a, b in r.aliases.items()})
        off_in, off_out = off_in + ki, off_out + ko
    if aliases:
        kw["input_output_aliases"] = aliases
    call = pl.pallas_call(carried, **kw)
    return lambda *args: call(*args, *[op for r in rides for op in r.operands])


def _cp(n_grid):
    return pltpu.CompilerParams(dimension_semantics=("arbitrary",) * n_grid, vmem_limit_bytes=VMEM_LIMIT_BYTES)


def _res(shape):
    nd = len(shape)
    return pl.BlockSpec(shape, lambda *_: (0,) * nd, pipeline_mode=pl.Buffered(1))


def _rows(tm, n):
    return pl.BlockSpec((tm, n), lambda i: (i, 0))


def _div_tile(n, target):
    best = None
    for d in range(16, min(n, target) + 1, 16):
        if n % d == 0:
            best = d
    assert best is not None, (n, target)
    return best


def _dot(a, b):
    return jnp.dot(a, b, preferred_element_type=F32)


def _dot_nt(a, b):
    return lax.dot_general(a, b, (((1,), (1,)), ((), ())), preferred_element_type=F32)


def _dot_tn(a, b):
    return lax.dot_general(a, b, (((0,), (0,)), ((), ())), preferred_element_type=F32)


def _bf(x):
    return x.astype(BF16)


def _sigmoid(x):
    return 1.0 / (1.0 + jnp.exp(-x))


def _rms_fwd(x, g):
    r = lax.rsqrt(jnp.mean(x * x, axis=-1, keepdims=True) + EPS)
    xh = x * r
    return xh * g, xh, r


def _rms_bwd(dh, xh, r, g):
    dxh = dh * g
    return r * (dxh - xh * jnp.mean(dxh * xh, axis=-1, keepdims=True))


def _colsum(x):
    return jnp.sum(x, axis=0, keepdims=True)


def _rope(x, cos, sin, sign):
    n = x.shape[1] // 128
    c = jnp.tile(cos, (1, n)) if n > 1 else cos
    s = jnp.tile(sin, (1, n)) if n > 1 else sin
    lane = lax.broadcasted_iota(jnp.int32, x.shape, 1)
    first = (lane & 63) < 32
    partner = jnp.where(first, pltpu.roll(x, x.shape[1] - 32, 1), pltpu.roll(x, 32, 1))
    return x * c + sign * (partner * s)


def _lo_lanes(shape):
    return lax.broadcasted_iota(jnp.int32, shape, 1) < 64


def _stack_heads(t):
    t0, t1 = t[:, 0:128], t[:, 128:256]
    lo = _lo_lanes(t0.shape)
    z = jnp.zeros_like(t0)
    return jnp.concatenate([jnp.where(lo, t0, z), jnp.where(lo, z, t0), jnp.where(lo, t1, z), jnp.where(lo, z, t1)], axis=0)


def _unstack_heads(o):
    lo = _lo_lanes((BLK, 128))
    return jnp.concatenate([jnp.where(lo, o[0:128], o[128:256]), jnp.where(lo, o[256:384], o[384:512])], axis=1)


def _fold_heads(parts):
    a, b = (p + pltpu.roll(p, 64, 1) for p in parts)
    return jnp.where(_lo_lanes(a.shape), a, b)


def _sink_col(sk_ref, g):
    return jnp.concatenate([jnp.broadcast_to(sk_ref[4 * g + h:4 * g + h + 1, :], (BLK, 128)) for h in range(4)], axis=0)


def _tile3(x):
    return jnp.concatenate([x, x, x], axis=1)


def _mem_kv_fwd(mem, g_kv, w_kv):
    m_len = mem.shape[0]
    cols = w_kv.shape[2]

    def body(mem_ref, g_ref, w_ref, memn_ref, kv_ref):
        h, _, _ = _rms_fwd(mem_ref[...], g_ref[...])
        hb = _bf(h)
        memn_ref[...] = hb
        for s in range(N_CHIPS):
            kv_ref[s] = _bf(_dot(hb, w_ref[s]))

    return _call(
        body, name="mem_kv_fwd",
        out_shape=(jax.ShapeDtypeStruct((m_len, D_MODEL), BF16), jax.ShapeDtypeStruct((N_CHIPS, m_len, cols), BF16)),
        compiler_params=pltpu.CompilerParams(vmem_limit_bytes=VMEM_LIMIT_BYTES),
    )(mem, g_kv, w_kv)


def _dup_head_rows(w_ref, lo):
    h0, h1 = w_ref[lo:lo + 64, :], w_ref[lo + 64:lo + 128, :]
    return jnp.concatenate([h0, h0, h1, h1], axis=0)


def _in_proj_fwd(x, g_mix, w_t, b_ext, cos_t, sin_t, tm, rides=()):
    t_len = x.shape[0]

    def body(x_ref, g_ref, w_ref, b_ref, c_ref, s_ref, ug_ref, q_ref, k_ref, v_ref, h_ref):
        h, _, _ = _rms_fwd(x_ref[...], g_ref[...])
        hb = _bf(h)
        h_ref[...] = hb
        ug_ref[...] = _dot_nt(hb, w_ref[0:1024, :]) + b_ref[:, 0:1024]
        c, s = c_ref[...], s_ref[...]
        q_ref[...] = _bf(_rope(_dot_nt(hb, w_ref[1024:1536, :]) + b_ref[:, 1024:1536], c, s, 1.0))
        k_ref[...] = _bf(_rope(_dot_nt(hb, _dup_head_rows(w_ref, 1536)) + b_ref[:, 1536:1792], c, s, 1.0))
        v_ref[...] = _bf(_dot_nt(hb, _dup_head_rows(w_ref, 1664)) + b_ref[:, 1792:2048])

    return _call(
        body, rides=rides, name="in_proj_fwd", grid=(t_len // tm,),
        in_specs=[_rows(tm, D_MODEL), _res((1, D_MODEL)), _res(w_t.shape), _res(b_ext.shape), _rows(tm, 128), _rows(tm, 128)],
        out_specs=[_rows(tm, 1024), _rows(tm, 512), _rows(tm, 256), _rows(tm, 256), _rows(tm, D_MODEL)],
        out_shape=(jax.ShapeDtypeStruct((t_len, 1024), F32), jax.ShapeDtypeStruct((t_len, 512), BF16),
                   jax.ShapeDtypeStruct((t_len, 256), BF16), jax.ShapeDtypeStruct((t_len, 256), BF16),
                   jax.ShapeDtypeStruct((t_len, D_MODEL), BF16)),
        compiler_params=_cp(1),
    )(x, g_mix, w_t, b_ext, cos_t, sin_t)


def _halo_specs(tc, n, t_len):
    per = tc // 16
    last = t_len // 16 - 1
    return [pl.BlockSpec((16, n), lambda i: (jnp.maximum(i * per - 1, 0), 0)),
            pl.BlockSpec((tc, n), lambda i: (i, 0)),
            pl.BlockSpec((16, n), lambda i: (jnp.minimum((i + 1) * per, last), 0))]


def _glu(z):
    return z[:, 0:CONV_CH] * _sigmoid(z[:, CONV_CH:2 * CONV_CH])


def _fill_halo_buf(buf, prev, main, nxt, i, n_tiles, tc):
    buf[0:16, :] = jnp.where(i > 0, prev, jnp.zeros_like(prev))
    buf[16:16 + tc, :] = main
    buf[16 + tc:32 + tc, :] = jnp.where(i < n_tiles - 1, nxt, jnp.zeros_like(nxt))


CONV_ROWS = 64


def _shift_copies(buf, shifted, tc):
    for r in range(1, 8):
        shifted[r - 1, :, :] = buf[r:r + tc + 24, :]


def _shifted_rows(buf, shifted, offset, base):
    src = buf if offset % 8 == 0 else shifted.at[offset % 8 - 1]
    return src[pl.ds(pl.multiple_of(base + 8 * (offset // 8), 8), CONV_ROWS), :]


def _conv_fwd(ug, w_dw, b_dw, g_ln, b_ln, tc, rides=()):
    t_len = ug.shape[0]
    n_tiles = t_len // tc

    def body(up_ref, um_ref, un_ref, w_ref, bdw_ref, g_ref, b_ref, y_ref, pre_ref, buf, shifted):
        i = pl.program_id(0)
        _fill_halo_buf(buf, _glu(up_ref[...]), _glu(um_ref[...]), _glu(un_ref[...]), i, n_tiles, tc)
        _shift_copies(buf, shifted, tc)

        def chunk(c, carry):
            base = c * CONV_ROWS
            acc = jnp.zeros((CONV_ROWS, CONV_CH), F32)
            for k in range(CONV_W):
                acc = acc + w_ref[k:k + 1, :] * _shifted_rows(buf, shifted, k + 1, base)
            pre_ref[pl.ds(pl.multiple_of(base, CONV_ROWS), CONV_ROWS), :] = acc + bdw_ref[...]
            return carry

        lax.fori_loop(0, tc // CONV_ROWS, chunk, 0)
        pre = pre_ref[...]
        mu = jnp.mean(pre, axis=-1, keepdims=True)
        d = pre - mu
        rstd = lax.rsqrt(jnp.mean(d * d, axis=-1, keepdims=True) + EPS)
        ln = d * rstd * g_ref[...] + b_ref[...]
        y_ref[...] = _bf(ln * _sigmoid(ln))

    return _call(
        body, rides=rides, name="conv_fwd", grid=(n_tiles,),
        in_specs=_halo_specs(tc, 1024, t_len) + [_res((32, CONV_CH)), _res((1, CONV_CH)), _res((1, CONV_CH)), _res((1, CONV_CH))],
        out_specs=[_rows(tc, CONV_CH), _rows(tc, CONV_CH)],
        out_shape=(jax.ShapeDtypeStruct((t_len, CONV_CH), BF16), jax.ShapeDtypeStruct((t_len, CONV_CH), F32)),
        scratch_shapes=[pltpu.VMEM((tc + 32, CONV_CH), F32), pltpu.VMEM((7, tc + 24, CONV_CH), F32)],
        compiler_params=_cp(1),
    )(ug, ug, ug, w_dw, b_dw, g_ln, b_ln)


def _nbr_specs(n, nb):
    return [pl.BlockSpec((BLK, n), lambda i: (jnp.maximum(i - 1, 0), 0)),
            pl.BlockSpec((BLK, n), lambda i: (i, 0)),
            pl.BlockSpec((BLK, n), lambda i: (jnp.minimum(i + 1, nb - 1), 0))]


def _nbr_specs4(nb):
    return [pl.BlockSpec((1, 2, 4 * BLK, 128), lambda i: (jnp.maximum(i - 1, 0), 0, 0, 0)),
            pl.BlockSpec((1, 2, 4 * BLK, 128), lambda i: (i, 0, 0, 0)),
            pl.BlockSpec((1, 2, 4 * BLK, 128), lambda i: (jnp.minimum(i + 1, nb - 1), 0, 0, 0))]


def _band_bias():
    a = np.arange(4 * BLK)[:, None] % BLK
    c = np.arange(3 * BLK)[None, :]
    inside = np.abs(c - BLK - a) <= BLK
    q_side = np.stack([inside & (c >= BLK), inside, inside & (c < 2 * BLK)])
    blk = np.arange(12 * BLK)[:, None] // (4 * BLK)
    a = np.arange(12 * BLK)[:, None] % BLK
    c = np.arange(BLK)[None, :]
    inside = np.abs(c - a + (1 - blk) * BLK) <= BLK
    k_side = np.stack([inside & (blk >= 1), inside, inside & (blk <= 1)])
    return [jnp.asarray(np.where(m, 0.0, NEG).astype(np.float32)) for m in (q_side, k_side)]


def _edge_spec(shape, nb):
    return pl.BlockSpec((1,) + shape, lambda i: (jnp.where(i == 0, 0, jnp.where(i == nb - 1, 2, 1)),) + (0,) * len(shape))


def _attn_fwd(q, kd, vd, sink_b, bias, rides=()):
    t_len = q.shape[0]
    nb = t_len // BLK

    def body(q_ref, kp_ref, kc_ref, kn_ref, vp_ref, vc_ref, vn_ref, sk_ref, bias_ref, y_ref, lse_ref):
        kcat = jnp.concatenate([kp_ref[...], kc_ref[...], kn_ref[...]], axis=0)
        vcat = jnp.concatenate([vp_ref[...], vc_ref[...], vn_ref[...]], axis=0)
        ys, lses = [], []
        for g in range(2):
            qs = _stack_heads(q_ref[:, 256 * g:256 * g + 256])
            s = _dot_nt(qs, kcat[:, 128 * g:128 * g + 128]) * ATT_SCALE + bias_ref[0]
            skc = _sink_col(sk_ref, g)
            m_b = jnp.maximum(jnp.max(s, axis=-1, keepdims=True), skc)
            p = jnp.exp(s - _tile3(m_b))
            den_b = jnp.sum(p, axis=-1, keepdims=True) + jnp.exp(skc - m_b)
            pn = p * _tile3(1.0 / den_b)
            o = _dot(_bf(pn), vcat[:, 128 * g:128 * g + 128])
            ys.append(_unstack_heads(o))
            lses.append(m_b + jnp.log(den_b))
        lse_ref[0] = jnp.stack(lses)
        y_ref[...] = _bf(jnp.concatenate(ys, axis=1))

    return _call(
        body, rides=rides, name="attn_fwd", grid=(nb,),
        in_specs=[_rows(BLK, 512)] + _nbr_specs(256, nb) + _nbr_specs(256, nb) + [_res((8, 128)), _edge_spec((4 * BLK, 3 * BLK), nb)],
        out_specs=[_rows(BLK, 512), pl.BlockSpec((1, 2, 4 * BLK, 128), lambda i: (i, 0, 0, 0))],
        out_shape=(jax.ShapeDtypeStruct((t_len, 512), BF16), jax.ShapeDtypeStruct((nb, 2, 4 * BLK, 128), F32)),
        compiler_params=_cp(1),
    )(q, kd, kd, kd, vd, vd, vd, sink_b, bias)


def _mem_heads(kv_ref, h):
    lo = MEM_HD * (h % 2)
    return kv_ref[h // 2, :, lo:lo + MEM_HD], kv_ref[2 + h // 2, :, lo:lo + MEM_HD]


def _mix_mem_fwd(x, yc, ya, w_out, b_out, g_q, w_q, kv, w_o, tm, rides=()):
    t_len = x.shape[0]

    def body(x_ref, yc_ref, ya_ref, wout_ref, bout_ref, g_ref, wq_ref, kv_ref, wo_ref,
             ymix_ref, x1_ref, hq_ref, qm_ref, om_ref, x2_ref):
        ymix = jnp.concatenate([yc_ref[...], ya_ref[...]], axis=1)
        ymix_ref[...] = ymix
        x1 = x_ref[...] + _dot(ymix, wout_ref[...]) + bout_ref[...]
        x1_ref[...] = x1
        hq, _, _ = _rms_fwd(x1, g_ref[...])
        hqb = _bf(hq)
        hq_ref[...] = hqb
        qm = _bf(_dot(hqb, wq_ref[...]))
        qm_ref[...] = qm
        outs = []
        for h in range(MEM_HEADS):
            kh, vh = _mem_heads(kv_ref, h)
            s = _dot_nt(qm[:, MEM_HD * h:MEM_HD * (h + 1)], kh) * MEM_SCALE
            p = jnp.exp(s - jnp.max(s, axis=-1, keepdims=True))
            p = p * (1.0 / jnp.sum(p, axis=-1, keepdims=True))
            outs.append(_dot(_bf(p), vh))
        om = _bf(jnp.concatenate(outs, axis=1))
        om_ref[...] = om
        x2_ref[...] = x1 + _dot(om, wo_ref[...])

    act_b = jax.ShapeDtypeStruct((t_len, D_MODEL), BF16)
    act_f = jax.ShapeDtypeStruct((t_len, D_MODEL), F32)
    return _call(
        body, rides=rides, name="mix_mem_fwd", grid=(t_len // tm,),
        in_specs=[_rows(tm, D_MODEL), _rows(tm, 512), _rows(tm, 512), _res(w_out.shape), _res((1, D_MODEL)), _res((1, D_MODEL)),
                  _res(w_q.shape), _res(kv.shape), _res(w_o.shape)],
        out_specs=[_rows(tm, D_MODEL)] * 6,
        out_shape=(act_b, act_f, act_b, act_b, act_b, act_f),
        compiler_params=_cp(1),
    )(x, yc, ya, w_out, b_out, g_q, w_q, kv, w_o)


def _hidden_chunks(ff, width=1024):
    return [(lo, min(lo + width, ff)) for lo in range(0, ff, width)]


def _ffn_up(x2, g_ffn, w_gate, w_up, tm, rides=()):
    t_len = x2.shape[0]
    ff = w_gate.shape[0]

    def body(x2_ref, g_ref, wg_ref, wu_ref, hf_ref, gate_ref, up_ref, act_ref):
        hf, _, _ = _rms_fwd(x2_ref[...], g_ref[...])
        hfb = _bf(hf)
        hf_ref[...] = hfb
        for lo, hi in _hidden_chunks(ff):
            gate = _dot_nt(hfb, wg_ref[lo:hi, :])
            up = _dot_nt(hfb, wu_ref[lo:hi, :])
            gate_ref[:, lo:hi] = _bf(gate)
            up_ref[:, lo:hi] = _bf(up)
            act_ref[:, lo:hi] = _bf(gate * _sigmoid(gate) * up)

    hid = jax.ShapeDtypeStruct((t_len, ff), BF16)
    return _call(
        body, rides=rides, name="ffn_up", grid=(t_len // tm,),
        in_specs=[_rows(tm, D_MODEL), _res((1, D_MODEL)), _res(w_gate.shape), _res(w_up.shape)],
        out_specs=[_rows(tm, D_MODEL), _rows(tm, ff), _rows(tm, ff), _rows(tm, ff)],
        out_shape=[jax.ShapeDtypeStruct((t_len, D_MODEL), BF16), hid, hid, hid],
        compiler_params=_cp(1),
    )(x2, g_ffn, w_gate, w_up)


def _ffn_down_loss(x2, act, w_down, g_final, target, tm):
    t_len = x2.shape[0]
    ff = w_down.shape[0]

    def body(x2_ref, act_ref, wd_ref, gf_ref, tgt_ref, dx3_ref, loss_ref, dgf_ref):
        i = pl.program_id(0)
        x3 = x2_ref[...]
        for lo, hi in _hidden_chunks(ff):
            x3 = x3 + _dot(act_ref[:, lo:hi], wd_ref[lo:hi, :])
        gf = gf_ref[...]
        y, xh, r = _rms_fwd(x3, gf)
        err = y - tgt_ref[...]
        part = 0.5 * jnp.sum(jnp.mean(err * err, axis=-1, keepdims=True), axis=0, keepdims=True)
        dy = err * (1.0 / D_MODEL)
        dx3_ref[...] = _rms_bwd(dy, xh, r, gf)

        @pl.when(i == 0)
        def _():
            loss_ref[...] = jnp.zeros_like(loss_ref)
            dgf_ref[...] = jnp.zeros_like(dgf_ref)

        loss_ref[...] += jnp.broadcast_to(part, loss_ref.shape)
        dgf_ref[...] += _colsum(dy * xh)

    vec = pl.BlockSpec((1, D_MODEL), lambda i: (0, 0))
    return _call(
        body, name="ffn_down_loss", grid=(t_len // tm,),
        in_specs=[_rows(tm, D_MODEL), _rows(tm, ff), _res(w_down.shape), _res((1, D_MODEL)), _rows(tm, D_MODEL)],
        out_specs=[_rows(tm, D_MODEL), vec, vec],
        out_shape=(jax.ShapeDtypeStruct((t_len, D_MODEL), F32), jax.ShapeDtypeStruct((1, D_MODEL), F32),
                   jax.ShapeDtypeStruct((1, D_MODEL), F32)),
        compiler_params=_cp(1),
    )(x2, act, w_down, g_final, target)


def _ffn_bwd(dx3, x2, gate, up, g_ffn, w_gate, w_up, w_down, tm):
    t_len = x2.shape[0]
    ff = w_gate.shape[0]

    def body(dx3_ref, x2_ref, gate_ref, up_ref, g_ref, wg_ref, wu_ref, wd_ref, dx2_ref, dgate_ref, dup_ref, dg_ref):
        i = pl.program_id(0)
        dx3 = dx3_ref[...]
        d3b = _bf(dx3)
        dh = jnp.zeros((tm, D_MODEL), F32)
        for lo, hi in _hidden_chunks(ff):
            dact = _dot_nt(d3b, wd_ref[lo:hi, :])
            gt = gate_ref[:, lo:hi].astype(F32)
            u = up_ref[:, lo:hi].astype(F32)
            sg = _sigmoid(gt)
            dup = _bf(dact * (gt * sg))
            dgate = _bf(dact * u * (sg * (1.0 + gt * (1.0 - sg))))
            dup_ref[:, lo:hi] = dup
            dgate_ref[:, lo:hi] = dgate
            dh = dh + _dot(dgate, wg_ref[lo:hi, :]) + _dot(dup, wu_ref[lo:hi, :])
        g = g_ref[...]
        _, xh, r = _rms_fwd(x2_ref[...], g)
        dx2_ref[...] = dx3 + _rms_bwd(dh, xh, r, g)

        @pl.when(i == 0)
        def _():
            dg_ref[...] = jnp.zeros_like(dg_ref)

        dg_ref[...] += _colsum(dh * xh)

    hid = jax.ShapeDtypeStruct((t_len, ff), BF16)
    hid_spec = _rows(tm, ff)
    return _call(
        body, name="ffn_bwd", grid=(t_len // tm,),
        in_specs=[_rows(tm, D_MODEL), _rows(tm, D_MODEL), hid_spec, hid_spec, _res((1, D_MODEL)),
                  _res(w_gate.shape), _res(w_up.shape), _res(w_down.shape)],
        out_specs=[_rows(tm, D_MODEL), hid_spec, hid_spec, pl.BlockSpec((1, D_MODEL), lambda i: (0, 0))],
        out_shape=(jax.ShapeDtypeStruct((t_len, D_MODEL), F32), hid, hid, jax.ShapeDtypeStruct((1, D_MODEL), F32)),
        compiler_params=_cp(1),
    )(dx3, x2, gate, up, g_ffn, w_gate, w_up, w_down)


def _mix_mem_bwd(dx2, x1, qm, kv, g_q, w_q, w_o, w_out, tm, rides=()):
    t_len = x1.shape[0]
    m_len = kv.shape[1]

    def body(dx2_ref, x1_ref, qm_ref, kv_ref, g_ref, wq_ref, wo_ref, wout_ref,
             dx1_ref, dqm_ref, dyc_ref, dya_ref, dkv_ref, dgq_ref, dbout_ref):
        i = pl.program_id(0)

        @pl.when(i == 0)
        def _():
            dkv_ref[...] = jnp.zeros_like(dkv_ref)
            dgq_ref[...] = jnp.zeros_like(dgq_ref)
            dbout_ref[...] = jnp.zeros_like(dbout_ref)

        dx2 = dx2_ref[...]
        dom = _dot_nt(_bf(dx2), wo_ref[...])
        dqs, dks, dvs = [], [], []
        for h in range(MEM_HEADS):
            kh, vh = _mem_heads(kv_ref, h)
            qh = qm_ref[:, MEM_HD * h:MEM_HD * (h + 1)]
            s = _dot_nt(qh, kh) * MEM_SCALE
            p = jnp.exp(s - jnp.max(s, axis=-1, keepdims=True))
            p = p * (1.0 / jnp.sum(p, axis=-1, keepdims=True))
            domh = _bf(dom[:, MEM_HD * h:MEM_HD * (h + 1)])
            dp = _dot_nt(domh, vh)
            ds = _bf(p * (dp - jnp.sum(p * dp, axis=-1, keepdims=True)) * MEM_SCALE)
            dqs.append(_dot(ds, kh))
            dks.append(_dot_tn(ds, qh))
            dvs.append(_dot_tn(_bf(p), domh))
        dkv_ref[...] += jnp.stack([jnp.concatenate(part[2 * s:2 * s + 2], axis=1) for part in (dks, dvs) for s in range(2)])
        dqm = _bf(jnp.concatenate(dqs, axis=1))
        dqm_ref[...] = dqm
        dhq = _dot_nt(dqm, wq_ref[...])
        g = g_ref[...]
        _, xh, r = _rms_fwd(x1_ref[...], g)
        dx1 = dx2 + _rms_bwd(dhq, xh, r, g)
        dx1_ref[...] = dx1
        dgq_ref[...] += _colsum(dhq * xh)
        dbout_ref[...] += _colsum(dx1)
        dymix = _dot_nt(_bf(dx1), wout_ref[...])
        dyc_ref[...] = dymix[:, 0:CONV_CH]
        dya_ref[...] = _bf(dymix[:, CONV_CH:2 * CONV_CH])

    vec = pl.BlockSpec((1, D_MODEL), lambda i: (0, 0))
    return _call(
        body, rides=rides, name="mix_mem_bwd", grid=(t_len // tm,),
        in_specs=[_rows(tm, D_MODEL), _rows(tm, D_MODEL), _rows(tm, D_MODEL), _res(kv.shape), _res((1, D_MODEL)),
                  _res(w_q.shape), _res(w_o.shape), _res(w_out.shape)],
        out_specs=[_rows(tm, D_MODEL), _rows(tm, D_MODEL), _rows(tm, CONV_CH), _rows(tm, CONV_CH),
                   pl.BlockSpec(kv.shape, lambda i: (0, 0, 0)), vec, vec],
        out_shape=(jax.ShapeDtypeStruct((t_len, D_MODEL), F32), jax.ShapeDtypeStruct((t_len, D_MODEL), BF16),
                   jax.ShapeDtypeStruct((t_len, CONV_CH), F32), jax.ShapeDtypeStruct((t_len, CONV_CH), BF16),
                   jax.ShapeDtypeStruct((N_CHIPS, m_len, kv.shape[2]), F32),
                   jax.ShapeDtypeStruct((1, D_MODEL), F32), jax.ShapeDtypeStruct((1, D_MODEL), F32)),
        compiler_params=_cp(1),
    )(dx2, x1, qm, kv, g_q, w_q, w_o, w_out)


def _mem_kv_bwd(dkv, memn, mem, g_kv, w_kv):
    m_len = mem.shape[0]

    def body(dkv_ref, memn_ref, mem_ref, g_ref, w_ref, dw_ref, dg_ref):
        hb = memn_ref[...]
        dmn = jnp.zeros((m_len, D_MODEL), F32)
        for s in range(N_CHIPS):
            d = _bf(dkv_ref[s])
            dw_ref[s] = _bf(_dot_tn(hb, d))
            dmn = dmn + _dot_nt(d, w_ref[s])
        _, xh, _ = _rms_fwd(mem_ref[...], g_ref[...])
        dg_ref[...] = _colsum(dmn * xh)

    return _call(
        body, name="mem_kv_bwd",
        out_shape=(jax.ShapeDtypeStruct(w_kv.shape, BF16), jax.ShapeDtypeStruct((1, D_MODEL), F32)),
        compiler_params=pltpu.CompilerParams(vmem_limit_bytes=VMEM_LIMIT_BYTES),
    )(dkv, memn, mem, g_kv, w_kv)


def _attn_bwd_q(q, kd, vd, dya, lse, sink_b, bias, cos_t, sin_t, rides=()):
    t_len = q.shape[0]
    nb = t_len // BLK

    def body(q_ref, kp_ref, kc_ref, kn_ref, vp_ref, vc_ref, vn_ref, do_ref, lse_ref, sk_ref, bias_ref, c_ref, s_ref,
             dq_ref, dd_ref, dsk_ref):
        kcat = jnp.concatenate([kp_ref[...], kc_ref[...], kn_ref[...]], axis=0)
        vcat = jnp.concatenate([vp_ref[...], vc_ref[...], vn_ref[...]], axis=0)
        dqs, dsks, dds = [], [], []
        for g in range(2):
            qs = _stack_heads(q_ref[:, 256 * g:256 * g + 256])
            dos = _stack_heads(do_ref[:, 256 * g:256 * g + 256])
            kk = kcat[:, 128 * g:128 * g + 128]
            s = _dot_nt(qs, kk) * ATT_SCALE + bias_ref[0]
            lse_b = lse_ref[0, g]
            p = jnp.exp(s - _tile3(lse_b))
            dp = _dot_nt(dos, vcat[:, 128 * g:128 * g + 128])
            drow = jnp.sum(p * dp, axis=-1, keepdims=True)
            ds = _bf(p * (dp - drow) * ATT_SCALE)
            dqs.append(_unstack_heads(_dot(ds, kk)))
            d_b = jnp.broadcast_to(drow, (4 * BLK, 128))
            dds.append(d_b)
            contrib = -(jnp.exp(_sink_col(sk_ref, g) - lse_b) * d_b)
            dsks.append(jnp.sum(contrib.reshape(4, BLK, 128), axis=1))
        dd_ref[0] = jnp.stack(dds)
        dq = jnp.concatenate(dqs, axis=1)
        dq_ref[...] = _bf(_rope(dq, c_ref[...], s_ref[...], -1.0))
        dsk_ref[0] = jnp.concatenate(dsks, axis=0)

    stat = pl.BlockSpec((1, 2, 4 * BLK, 128), lambda i: (i, 0, 0, 0))
    return _call(
        body, rides=rides, name="attn_bwd_q", grid=(nb,),
        in_specs=[_rows(BLK, 512)] + _nbr_specs(256, nb) + _nbr_specs(256, nb)
        + [_rows(BLK, 512), stat, _res((8, 128)), _edge_spec((4 * BLK, 3 * BLK), nb), _rows(BLK, 128), _rows(BLK, 128)],
        out_specs=[_rows(BLK, 512), stat, pl.BlockSpec((1, 8, 128), lambda i: (i, 0, 0))],
        out_shape=(jax.ShapeDtypeStruct((t_len, 512), BF16), jax.ShapeDtypeStruct((nb, 2, 4 * BLK, 128), F32),
                   jax.ShapeDtypeStruct((nb, 8, 128), F32)),
        compiler_params=_cp(1),
    )(q, kd, kd, kd, vd, vd, vd, dya, lse, sink_b, bias, cos_t, sin_t)


def _attn_bwd_kv(q, kd, vd, dya, lse, dd, bias, cos_t, sin_t, rides=()):
    t_len = q.shape[0]
    nb = t_len // BLK

    def body(kc_ref, vc_ref, qp_ref, qc_ref, qn_ref, dop_ref, doc_ref, don_ref, lp_ref, lc_ref, ln_ref,
             dp_ref, dc_ref, dn_ref, bias_ref, c_ref, s_ref, dk_ref, dv_ref):
        dks, dvs = [], []
        for g in range(2):
            cols = slice(256 * g, 256 * g + 256)
            qs = jnp.concatenate([_stack_heads(r[:, cols]) for r in (qp_ref, qc_ref, qn_ref)], axis=0)
            dos = jnp.concatenate([_stack_heads(r[:, cols]) for r in (dop_ref, doc_ref, don_ref)], axis=0)
            lse_b = jnp.concatenate([r[0, g] for r in (lp_ref, lc_ref, ln_ref)], axis=0)
            d_b = jnp.concatenate([r[0, g] for r in (dp_ref, dc_ref, dn_ref)], axis=0)
            kk = kc_ref[:, 128 * g:128 * g + 128]
            s = _dot_nt(qs, kk) * ATT_SCALE + bias_ref[0]
            p = jnp.exp(s - lse_b)
            dp = _dot_nt(dos, vc_ref[:, 128 * g:128 * g + 128])
            ds = _bf(p * (dp - d_b) * ATT_SCALE)
            dvs.append(_dot_tn(_bf(p), dos))
            dks.append(_dot_tn(ds, qs))
        dk_ref[...] = _bf(_rope(_fold_heads(dks), c_ref[...], s_ref[...], -1.0))
        dv_ref[...] = _bf(_fold_heads(dvs))

    return _call(
        body, rides=rides, name="attn_bwd_kv", grid=(nb,),
        in_specs=[_rows(BLK, 256), _rows(BLK, 256)] + _nbr_specs(512, nb) + _nbr_specs(512, nb) + _nbr_specs4(nb) + _nbr_specs4(nb)
        + [_edge_spec((12 * BLK, BLK), nb), _rows(BLK, 128), _rows(BLK, 128)],
        out_specs=[_rows(BLK, 128), _rows(BLK, 128)],
        out_shape=(jax.ShapeDtypeStruct((t_len, 128), BF16), jax.ShapeDtypeStruct((t_len, 128), BF16)),
        compiler_params=_cp(1),
    )(kd, vd, q, q, q, dya, dya, dya, lse, lse, lse, dd, dd, dd, bias, cos_t, sin_t)


def _conv_norm_bwd(pre, dyc, g_ln, b_ln, tc, rides=()):
    t_len = pre.shape[0]

    def body(pre_ref, dy_ref, g_ref, b_ref, dpre_ref, stats_ref):
        i = pl.program_id(0)
        pre_v = pre_ref[...]
        mu = jnp.mean(pre_v, axis=-1, keepdims=True)
        d = pre_v - mu
        rstd = lax.rsqrt(jnp.mean(d * d, axis=-1, keepdims=True) + EPS)
        xh = d * rstd
        g = g_ref[...]
        ln = xh * g + b_ref[...]
        sg = _sigmoid(ln)
        dln = dy_ref[...] * (sg * (1.0 + ln * (1.0 - sg)))
        dxh = dln * g
        dpre = rstd * (dxh - jnp.mean(dxh, axis=-1, keepdims=True) - xh * jnp.mean(dxh * xh, axis=-1, keepdims=True))
        dpre_ref[...] = dpre

        @pl.when(i == 0)
        def _():
            stats_ref[...] = jnp.zeros_like(stats_ref)

        stats_ref[0:1, :] += _colsum(dln * xh)
        stats_ref[1:2, :] += _colsum(dln)
        stats_ref[2:3, :] += _colsum(dpre)

    return _call(
        body, rides=rides, name="conv_norm_bwd", grid=(t_len // tc,),
        in_specs=[_rows(tc, CONV_CH), _rows(tc, CONV_CH), _res((1, CONV_CH)), _res((1, CONV_CH))],
        out_specs=[_rows(tc, CONV_CH), pl.BlockSpec((8, CONV_CH), lambda i: (0, 0))],
        out_shape=(jax.ShapeDtypeStruct((t_len, CONV_CH), F32), jax.ShapeDtypeStruct((8, CONV_CH), F32)),
        compiler_params=_cp(1),
    )(pre, dyc, g_ln, b_ln)


def _conv_bwd(dpre, ug, w_dw, tc, rides=()):
    t_len = ug.shape[0]
    n_tiles = t_len // tc

    def body(dp_ref, dm_ref, dn_ref, up_ref, um_ref, un_ref, w_ref, du_ref, dw_ref, dbuf, vbuf, dshift, vshift):
        i = pl.program_id(0)
        _fill_halo_buf(dbuf, dp_ref[...], dm_ref[...], dn_ref[...], i, n_tiles, tc)
        _fill_halo_buf(vbuf, _glu(up_ref[...]), _glu(um_ref[...]), _glu(un_ref[...]), i, n_tiles, tc)
        _shift_copies(dbuf, dshift, tc)
        _shift_copies(vbuf, vshift, tc)

        @pl.when(i == 0)
        def _():
            dw_ref[...] = jnp.zeros_like(dw_ref)

        def chunk(c, carry):
            base = c * CONV_ROWS
            rows = pl.ds(pl.multiple_of(base, CONV_ROWS), CONV_ROWS)
            dmain = dm_ref[rows, :]
            dv = jnp.zeros((CONV_ROWS, CONV_CH), F32)
            for k in range(CONV_W):
                dv = dv + w_ref[k:k + 1, :] * _shifted_rows(dbuf, dshift, 31 - k, base)
                prod = dmain * _shifted_rows(vbuf, vshift, k + 1, base)
                dw_ref[8 * k:8 * k + 8, :] += jnp.sum(prod.reshape(CONV_ROWS // 8, 8, CONV_CH), axis=0)
            um = um_ref[rows, :]
            a, gt = um[:, 0:CONV_CH], um[:, CONV_CH:2 * CONV_CH]
            sg = _sigmoid(gt)
            du_ref[rows, :] = _bf(jnp.concatenate([dv * sg, dv * a * (sg * (1.0 - sg))], axis=1))
            return carry

        lax.fori_loop(0, tc // CONV_ROWS, chunk, 0)

    shifts = pltpu.VMEM((7, tc + 24, CONV_CH), F32)
    return _call(
        body, rides=rides, name="conv_bwd", grid=(n_tiles,),
        in_specs=_halo_specs(tc, CONV_CH, t_len) + _halo_specs(tc, 1024, t_len) + [_res((32, CONV_CH))],
        out_specs=[_rows(tc, 1024), pl.BlockSpec((8 * 32, CONV_CH), lambda i: (0, 0))],
        out_shape=(jax.ShapeDtypeStruct((t_len, 1024), BF16), jax.ShapeDtypeStruct((8 * 32, CONV_CH), F32)),
        scratch_shapes=[pltpu.VMEM((tc + 32, CONV_CH), F32), pltpu.VMEM((tc + 32, CONV_CH), F32), shifts, shifts],
        compiler_params=_cp(1),
    )(dpre, dpre, dpre, ug, ug, ug, w_dw)


def _in_proj_bwd(du_glu, dq, dk, dv, dx1, x, g_mix, w_t, tm, rides=()):
    t_len = x.shape[0]
    n_ext = w_t.shape[0]

    def body(dg_ref, dq_ref, dk_ref, dv_ref, dx1_ref, x_ref, g_ref, w_ref, dx_ref, du_ref, db_ref, dgm_ref):
        i = pl.program_id(0)
        du = jnp.concatenate([dg_ref[...], dq_ref[...], dk_ref[...], dv_ref[...]], axis=1)
        du_ref[...] = du
        dh = _dot(du, w_ref[...])
        g = g_ref[...]
        _, xh, r = _rms_fwd(x_ref[...], g)
        dx_ref[...] = dx1_ref[...] + _rms_bwd(dh, xh, r, g)

        @pl.when(i == 0)
        def _():
            db_ref[...] = jnp.zeros_like(db_ref)
            dgm_ref[...] = jnp.zeros_like(dgm_ref)

        db_ref[...] += _colsum(du.astype(F32))
        dgm_ref[...] += _colsum(dh * xh)

    return _call(
        body, rides=rides, name="in_proj_bwd", grid=(t_len // tm,),
        in_specs=[_rows(tm, 1024), _rows(tm, 512), _rows(tm, 128), _rows(tm, 128), _rows(tm, D_MODEL), _rows(tm, D_MODEL),
                  _res((1, D_MODEL)), _res(w_t.shape)],
        out_specs=[_rows(tm, D_MODEL), _rows(tm, n_ext), pl.BlockSpec((1, n_ext), lambda i: (0, 0)),
                   pl.BlockSpec((1, D_MODEL), lambda i: (0, 0))],
        out_shape=(jax.ShapeDtypeStruct((t_len, D_MODEL), F32), jax.ShapeDtypeStruct((t_len, n_ext), BF16),
                   jax.ShapeDtypeStruct((1, n_ext), F32), jax.ShapeDtypeStruct((1, D_MODEL), F32)),
        compiler_params=_cp(1),
    )(du_glu, dq, dk, dv, dx1, x, g_mix, w_t)


def _weight_grad(a, d, name, tt, rides=()):
    t_len, k_dim = a.shape
    n_dim = d.shape[1]
    tk = k_dim if k_dim <= 1792 else k_dim // 2
    assert k_dim % tk == 0 and tk % 128 == 0 and n_dim % 128 == 0
    tt = min(tt, t_len)
    n_t = t_len // tt

    def body(a_ref, d_ref, o_ref, acc):
        t = pl.program_id(1)

        @pl.when(t == 0)
        def _():
            acc[...] = jnp.zeros_like(acc)

        acc[...] += _dot_tn(_bf(a_ref[...]), _bf(d_ref[...]))

        @pl.when(t == n_t - 1)
        def _():
            o_ref[...] = _bf(acc[...])

    res = _call(
        body, rides=rides, name=name, grid=(k_dim // tk, n_t),
        in_specs=[pl.BlockSpec((tt, tk), lambda k, t: (t, k)), pl.BlockSpec((tt, n_dim), lambda k, t: (t, 0))],
        out_specs=[pl.BlockSpec((tk, n_dim), lambda k, t: (k, 0))],
        out_shape=[jax.ShapeDtypeStruct((k_dim, n_dim), BF16)],
        scratch_shapes=[pltpu.VMEM((tk, n_dim), F32)],
        compiler_params=_cp(2),
    )(a, d)
    return res if rides else res[0]


ANY = pl.BlockSpec(memory_space=pl.ANY)


def _place():
    x, y, c = lax.axis_index("x"), lax.axis_index("y"), lax.axis_index("c")
    chips = [(1 - x, y), (x, 1 - y), (1 - x, 1 - y)]
    return x, y, c, chips


def _remote(src, dst, send_sems, recv_sems, k, to):
    return pltpu.make_async_remote_copy(src_ref=src, dst_ref=dst, send_sem=send_sems.at[k], recv_sem=recv_sems.at[k],
                                        device_id=to, device_id_type=MESH)


def _by_shape(names, arrays):
    groups = {}
    for k in names:
        groups.setdefault(arrays[k].shape, []).append(k)
    return list(groups.values())


def _cast_place(ws, chip_idx, tr):
    n = len(ws)
    rows, cols = ws[0].shape
    h = rows // 2
    tr = _div_tile(h, tr)
    per = h // tr

    def body(s_ref, *refs):
        for w_ref, o_ref in zip(refs[:n], refs[n:]):
            o_ref[0, 0] = _bf(w_ref[...])

    return _call(
        body, name="cast_place",
        grid_spec=pltpu.PrefetchScalarGridSpec(
            num_scalar_prefetch=1, grid=(2, per),
            in_specs=[pl.BlockSpec((tr, cols), lambda hh, r, s_ref: (hh * per + r, 0))] * n,
            out_specs=[pl.BlockSpec((1, 1, tr, cols), lambda hh, r, s_ref: (s_ref[0], hh, r, 0))] * n),
        out_shape=[jax.ShapeDtypeStruct((N_CHIPS, 2, h, cols), BF16)] * n,
        compiler_params=_cp(2),
    )(chip_idx, *ws)


def _same(arrays):
    return [jax.ShapeDtypeStruct(a.shape, a.dtype) for a in arrays]


def _gather_ride(bufs):
    n = len(bufs)

    def first_hop(outs, send, recv):
        x, y, c, chips = _place()
        mine = [outs[i].at[2 * x + y, c] for i in range(n)]
        return [_remote(mine[i], mine[i], send, recv, 3 * i + j, (cx, cy, c)) for i in range(n) for j, (cx, cy) in enumerate(chips)]

    def start(ins, outs, send, recv):
        for cp in first_hop(outs, send, recv):
            cp.start()

    def finish(ins, outs, send, recv):
        x, y, c, chips = _place()
        sib = (x, y, 1 - c)
        onward = []
        for i in range(n):
            for j, (cx, cy) in enumerate(chips):
                slab = outs[i].at[2 * cx + cy, c]
                _remote(slab, slab, send, recv, 3 * i + j, sib).wait_recv()
                onward.append(_remote(slab, slab, send, recv, 3 * n + 3 * i + j, sib))
                onward[-1].start()
        for i in range(n):
            for j, (cx, cy) in enumerate(chips):
                other = outs[i].at[2 * cx + cy, 1 - c]
                _remote(other, other, send, recv, 3 * n + 3 * i + j, sib).wait_recv()
        for cp in first_hop(outs, send, recv) + onward:
            cp.wait_send()

    return _Ride(bufs, _same(bufs), 6 * n, start, finish, aliases={i: i for i in range(n)})


def _spread_ride(buf):
    def sends(outs, send, recv):
        x, y, c, chips = _place()
        mine = outs[0].at[2 * x + y]
        return [_remote(mine, mine, send, recv, j, (cx, cy, c)) for j, (cx, cy) in enumerate(chips)]

    def start(ins, outs, send, recv):
        for cp in sends(outs, send, recv):
            cp.start()

    def finish(ins, outs, send, recv):
        _, _, c, chips = _place()
        for j, (cx, cy) in enumerate(chips):
            slab = outs[0].at[2 * cx + cy]
            _remote(slab, slab, send, recv, j, (cx, cy, c)).wait_recv()
        for cp in sends(outs, send, recv):
            cp.wait_send()

    return _Ride([buf], _same([buf]), 3, start, finish, aliases={0: 0})


def _allgather_ride(buf):
    def peers():
        x, y, c, _ = _place()
        return [(x ^ ((k >> 2) & 1), y ^ ((k >> 1) & 1), c ^ (k & 1)) for k in range(1, N_DEV)], 4 * x + 2 * y + c

    def sends(outs, send, recv):
        to, me = peers()
        mine = outs[0].at[me]
        return [_remote(mine, mine, send, recv, k, p) for k, p in enumerate(to)]

    def start(ins, outs, send, recv):
        for cp in sends(outs, send, recv):
            cp.start()

    def finish(ins, outs, send, recv):
        for k, (px, py, pc) in enumerate(peers()[0]):
            slab = outs[0].at[4 * px + 2 * py + pc]
            _remote(slab, slab, send, recv, k, (px, py, pc)).wait_recv()
        for cp in sends(outs, send, recv):
            cp.wait_send()

    return _Ride([buf], _same([buf]), N_DEV - 1, start, finish, aliases={0: 0})


def _sum_slabs(buf):
    def body(b_ref, o_ref):
        acc = b_ref[0]
        for d in range(1, buf.shape[0]):
            acc = acc + b_ref[d]
        o_ref[...] = acc

    return _call(body, name="sum_slabs", out_shape=jax.ShapeDtypeStruct(buf.shape[1:], buf.dtype))(buf)


def _pairwise_ride(arrays, out_shape, n_sem, copies):
    def start(ins, outs, send, recv):
        for cp in copies(ins, outs, send, recv):
            cp.start()

    def finish(ins, outs, send, recv):
        for cp in copies(ins, outs, send, recv):
            cp.wait()

    return _Ride(arrays, out_shape, n_sem, start, finish)


def _run_rides(name, rides):
    k_in = [len(r.operands) for r in rides]
    k_out = [len(r.out_shape) for r in rides]

    def body(*refs):
        pos, r_in, r_out = 0, [], []
        for k in k_in:
            r_in.append(refs[pos:pos + k])
            pos += k
        for k in k_out:
            r_out.append(refs[pos:pos + k])
            pos += k
        sems = refs[pos:]
        for j, r in enumerate(rides):
            r.start(r_in[j], r_out[j], sems[2 * j], sems[2 * j + 1])
        for j, r in enumerate(rides):
            r.finish(r_in[j], r_out[j], sems[2 * j], sems[2 * j + 1])

    aliases, off_in, off_out = {}, 0, 0
    for r, ki, ko in zip(rides, k_in, k_out):
        aliases.update({off_in + a: off_out + b for a, b in r.aliases.items()})
        off_in, off_out = off_in + ki, off_out + ko
    res = _call(
        body, name=name, in_specs=[ANY] * sum(k_in), out_specs=[ANY] * sum(k_out),
        out_shape=[s for r in rides for s in r.out_shape], input_output_aliases=aliases,
        scratch_shapes=[pltpu.SemaphoreType.DMA((r.n_sem,)) for r in rides for _ in range(2)],
    )(*[op for r in rides for op in r.operands])
    out, pos = [], 0
    for k in k_out:
        out.append(list(res[pos:pos + k]))
        pos += k
    return out


def _swap_ride(grads):
    def copies(ins, outs, send, recv):
        x, y, c, _ = _place()
        return [_remote(ins[i].at[:, 1 - c], outs[i], send, recv, i, (x, y, 1 - c)) for i in range(len(grads))]

    out_shape = [jax.ShapeDtypeStruct((g.shape[0],) + g.shape[2:], g.dtype) for g in grads]
    return _pairwise_ride(grads, out_shape, len(grads), copies)


def _pair_sum(grads, others, c_idx, tr):
    n = len(grads)
    n_s, _, h, cols = grads[0].shape
    tr = _div_tile(h, tr)

    def body(c_ref, *refs):
        for a_ref, b_ref, o_ref in zip(refs[:n], refs[n:2 * n], refs[2 * n:]):
            o_ref[...] = _bf(a_ref[0].astype(F32) + b_ref[...].astype(F32))

    return _call(
        body, name="pair_sum",
        grid_spec=pltpu.PrefetchScalarGridSpec(
            num_scalar_prefetch=1, grid=(n_s, h // tr),
            in_specs=[pl.BlockSpec((1, 1, tr, cols), lambda s, r, c_ref: (s, c_ref[0], r, 0))] * n
            + [pl.BlockSpec((1, tr, cols), lambda s, r, c_ref: (s, r, 0))] * n,
            out_specs=[pl.BlockSpec((1, tr, cols), lambda s, r, c_ref: (s, r, 0))] * n),
        out_shape=[jax.ShapeDtypeStruct((n_s, h, cols), BF16)] * n,
        compiler_params=_cp(2),
    )(c_idx, *grads, *others)


def _exchange_ride(sums):
    def copies(ins, outs, send, recv):
        _, _, c, chips = _place()
        return [_remote(ins[i].at[2 * cx + cy], outs[i].at[j], send, recv, 3 * i + j, (cx, cy, c))
                for i in range(len(sums)) for j, (cx, cy) in enumerate(chips)]

    out_shape = [jax.ShapeDtypeStruct((3,) + s.shape[1:], s.dtype) for s in sums]
    return _pairwise_ride(sums, out_shape, 3 * len(sums), copies)


def _chip_sum(owns, others, chip_idx, tr):
    n = len(owns)
    _, h, cols = owns[0].shape
    tr = _div_tile(h, tr)

    def body(s_ref, *refs):
        for a_ref, p_ref, o_ref in zip(refs[:n], refs[n:2 * n], refs[2 * n:]):
            acc = a_ref[0].astype(F32)
            for j in range(N_CHIPS - 1):
                acc = acc + p_ref[j].astype(F32)
            o_ref[...] = acc

    return _call(
        body, name="chip_sum",
        grid_spec=pltpu.PrefetchScalarGridSpec(
            num_scalar_prefetch=1, grid=(h // tr,),
            in_specs=[pl.BlockSpec((1, tr, cols), lambda r, s_ref: (s_ref[0], r, 0))] * n
            + [pl.BlockSpec((N_CHIPS - 1, tr, cols), lambda r, s_ref: (0, r, 0))] * n,
            out_specs=[pl.BlockSpec((tr, cols), lambda r, s_ref: (r, 0))] * n),
        out_shape=[jax.ShapeDtypeStruct((h, cols), F32)] * n,
        compiler_params=_cp(1),
    )(chip_idx, *owns, *others)


def _share_ride(halves):
    def copies(ins, outs, send, recv):
        x, y, c, _ = _place()
        return [_remote(ins[i], outs[i], send, recv, i, (x, y, 1 - c)) for i in range(len(halves))]

    return _pairwise_ride(halves, _same(halves), len(halves), copies)


def _adamw_math(w, g, m, v):
    m_new = ADAM_B1 * m + (1.0 - ADAM_B1) * g
    v_new = ADAM_B2 * v + (1.0 - ADAM_B2) * (g * g)
    m_hat = m_new * (1.0 / (1.0 - ADAM_B1 ** ADAM_STEP))
    v_hat = v_new * (1.0 / (1.0 - ADAM_B2 ** ADAM_STEP))
    delta = -ADAM_LR * (m_hat / (jnp.sqrt(v_hat) + ADAM_EPS) + ADAM_WD * w)
    return delta, m_new, v_new


def _adamw(w, g_mine, g_other, m, v, core_idx, tr):
    rows, cols = w.shape
    h = rows // 2
    tr = _div_tile(h, tr)
    per = h // tr

    def body(c_ref, w_ref, ga_ref, gb_ref, m_ref, v_ref, g_ref, d_ref, mo_ref, vo_ref):
        g = jnp.where(pl.program_id(0) == c_ref[0], ga_ref[...], gb_ref[...])
        d, mn, vn = _adamw_math(w_ref[...], g, m_ref[...], v_ref[...])
        g_ref[...] = g
        d_ref[...] = d
        mo_ref[...] = mn
        vo_ref[...] = vn

    full = pl.BlockSpec((tr, cols), lambda hh, r, c_ref: (hh * per + r, 0))
    mine = pl.BlockSpec((tr, cols), lambda hh, r, c_ref: (jnp.where(hh == c_ref[0], r, 0), 0))
    other = pl.BlockSpec((tr, cols), lambda hh, r, c_ref: (jnp.where(hh == c_ref[0], 0, r), 0))
    shp = jax.ShapeDtypeStruct(w.shape, F32)
    return _call(
        body, name="adamw",
        grid_spec=pltpu.PrefetchScalarGridSpec(num_scalar_prefetch=1, grid=(2, per), in_specs=[full, mine, other, full, full],
                                               out_specs=[full] * 4),
        out_shape=(shp, shp, shp, shp), compiler_params=_cp(2))(core_idx, w, g_mine, g_other, m, v)


def _adamw_small(ws, gs, ms, vs):
    n = len(ws)

    def body(*refs):
        w_r, g_r, m_r, v_r = refs[0:n], refs[n:2 * n], refs[2 * n:3 * n], refs[3 * n:4 * n]
        d_o, m_o, v_o = refs[4 * n:5 * n], refs[5 * n:6 * n], refs[6 * n:7 * n]
        for i in range(n):
            d, mn, vn = _adamw_math(w_r[i][...], g_r[i][...], m_r[i][...], v_r[i][...])
            d_o[i][...] = d
            m_o[i][...] = mn
            v_o[i][...] = vn

    shp = [jax.ShapeDtypeStruct(w.shape, F32) for w in ws]
    outs = _call(body, name="adamw_small", out_shape=shp * 3)(*ws, *gs, *ms, *vs)
    return outs[0:n], outs[n:2 * n], outs[2 * n:3 * n]


def _rope_tables(t_len):
    pos = jnp.arange(t_len, dtype=F32)
    inv_freq = ROPE_THETA ** (-jnp.arange(0, HEAD_DIM, 2, dtype=F32) / HEAD_DIM)
    ang = pos[:, None] * inv_freq[None, :]
    cos, sin = jnp.cos(ang), jnp.sin(ang)
    return jnp.tile(jnp.concatenate([cos, cos], axis=1), (1, 2)), jnp.tile(jnp.concatenate([-sin, sin], axis=1), (1, 2))


def _dup_heads(a):
    h0, h1 = a[..., 0:64], a[..., 64:128]
    return jnp.concatenate([h0, h0, h1, h1], axis=-1)


def _local_step(x, mem, target, small, wg, comm, tm_a=512, tm_b=256, tc=512, tt=1024):
    def run(stage, fn, n_own, *operands):
        rides = comm.rides(stage)
        res = fn(*operands, rides=rides)
        res = list(res) if isinstance(res, (list, tuple)) else [res]
        brought, pos = [], n_own
        for r in rides:
            brought.append(res[pos:pos + len(r.out_shape)])
            pos += len(r.out_shape)
        comm.landed(stage, brought, wg)
        return res[:n_own]

    t_len = x.shape[0]
    cos_t, sin_t = _rope_tables(t_len)
    b_in = small["b_in"]
    b_ext = jnp.concatenate([b_in[:, 0:1536], _dup_heads(b_in[:, 1536:1664]), _dup_heads(b_in[:, 1664:1792])], axis=1)
    w_dw = jnp.concatenate([wg["w_dw"], jnp.zeros((1, CONV_CH), F32)], axis=0)
    sink_b = jnp.broadcast_to(small["attn_sink"].reshape(8, 1), (8, 128))
    bias_q, bias_k = _band_bias()

    ug, q, kd, vd, h1 = run("in_proj_fwd", _in_proj_fwd, 5, x, small["g_mix"], wg["w_in"], b_ext, cos_t, sin_t, tm_a)
    yc, pre = run("conv_fwd", _conv_fwd, 2, ug, w_dw, small["b_dw"], small["g_conv_ln"], small["b_conv_ln"], tc)
    memn, kv = _mem_kv_fwd(mem, small["g_mem_kv"], wg["w_mem_kv"])
    ya, lse = run("attn_fwd", _attn_fwd, 2, q, kd, vd, sink_b, bias_q)
    ymix, x1, hq, qm, om, x2 = run("mix_mem_fwd", _mix_mem_fwd, 6, x, yc, ya, wg["w_out"], small["b_out"], small["g_mem_q"],
                                   wg["w_mem_q"], kv, wg["w_mem_o"], tm_a)
    hf, gate, up, act = run("ffn_up", _ffn_up, 4, x2, small["g_ffn"], wg["w_gate"], wg["w_up"], tm_a)
    dx3, loss, d_g_final = _ffn_down_loss(x2, act, wg["w_down"], small["g_final"], target, tm_a)

    dx2, dgate, dup, d_g_ffn = _ffn_bwd(dx3, x2, gate, up, small["g_ffn"], wg["w_gate"], wg["w_up"], wg["w_down"], tm_b)
    comm.grad("w_gate", _weight_grad(dgate, hf, "dw_gate", tt))
    comm.grad("w_up", run("dw_up", _weight_grad, 1, dup, hf, "dw_up", tt)[0])
    comm.grad("w_down", run("dw_down", _weight_grad, 1, act, dx3, "dw_down", tt)[0])
    dx1, dqm, dyc, dya, dkv, d_g_mem_q, d_b_out = run("mix_mem_bwd", _mix_mem_bwd, 7, dx2, x1, qm, kv, small["g_mem_q"],
                                                      wg["w_mem_q"], wg["w_mem_o"], wg["w_out"], tm_a)
    d_w_mem_kv, d_g_mem_kv = _mem_kv_bwd(dkv, memn, mem, small["g_mem_kv"], wg["w_mem_kv"])
    comm.grad("w_mem_kv", d_w_mem_kv)
    comm.grad("w_out", _weight_grad(ymix, dx1, "dw_out", tt))
    comm.grad("w_mem_q", _weight_grad(hq, dqm, "dw_mem_q", tt))
    comm.grad("w_mem_o", _weight_grad(om, dx2, "dw_mem_o", tt))
    dq, dd, dsink = run("attn_bwd_q", _attn_bwd_q, 3, q, kd, vd, dya, lse, sink_b, bias_q, cos_t, sin_t)
    dk, dv = run("attn_bwd_kv", _attn_bwd_kv, 2, q, kd, vd, dya, lse, dd, bias_k, cos_t, sin_t)
    dpre, cstats = run("conv_norm_bwd", _conv_norm_bwd, 2, pre, dyc, small["g_conv_ln"], small["b_conv_ln"], tc)
    du_glu, d_w_dw = run("conv_bwd", _conv_bwd, 2, dpre, ug, w_dw, tc)
    grad_x, du, d_b_in, d_g_mix = run("in_proj_bwd", _in_proj_bwd, 4, du_glu, dq, dk, dv, dx1, x, small["g_mix"], wg["w_in"],
                                      tm_a)
    grads = {
        "w_dw": jnp.sum(d_w_dw.reshape(32, 8, CONV_CH), axis=1)[0:CONV_W],
        "g_mix": d_g_mix, "b_in": d_b_in, "b_dw": cstats[2:3], "g_conv_ln": cstats[0:1],
        "b_conv_ln": cstats[1:2], "attn_sink": jnp.sum(dsink[:, :, 0], axis=0)[None, :], "b_out": d_b_out,
        "g_mem_q": d_g_mem_q, "g_mem_kv": d_g_mem_kv, "g_ffn": d_g_ffn, "g_final": d_g_final,
    }
    comm.small(loss[0:1, 0:1], grads)
    (d_w_in,) = run("dw_in", _weight_grad, 1, du, h1, "dw_in", tt)
    comm.grad("w_in", d_w_in)
    return loss[0:1, 0:1], grad_x, grads


BIG = ["w_in", "w_out", "w_mem_q", "w_mem_kv", "w_mem_o", "w_gate", "w_up", "w_down"]
KEEP_SLABS = ("w_mem_kv",)
TRANSPOSED = ("w_in", "w_gate", "w_up")
SMALL = ["g_mix", "b_in", "b_dw", "g_conv_ln", "b_conv_ln", "attn_sink", "b_out", "g_mem_q", "g_mem_kv", "g_ffn", "g_final"]
PACK_ROWS = 32
ROWS_PER_STEP = 512

GATHER_ON = {"in_proj_fwd": ("w_out", "w_mem_q"), "conv_fwd": ("w_mem_kv",), "attn_fwd": ("w_mem_o", "w_gate"),
             "mix_mem_fwd": ("w_up",), "ffn_up": ("w_down",)}
MID_GROUP = ("w_mem_kv", "w_out", "w_mem_q", "w_mem_o")
SWAP_ON = {"dw_up": ("w_gate",), "dw_down": ("w_up",), "mix_mem_bwd": ("w_down",), "attn_bwd_q": MID_GROUP}
EXCHANGE_ON = {"mix_mem_bwd": ("w_gate",), "attn_bwd_q": ("w_up",), "attn_bwd_kv": ("w_down", "w_out"),
               "conv_bwd": ("w_mem_kv", "w_mem_q", "w_mem_o")}
SMALL_ON = "dw_in"


def _as_weight(name, gathered):
    g = gathered.reshape(N_CHIPS, gathered.shape[2] * 2, gathered.shape[3])
    return g if name in KEEP_SLABS else g.reshape(-1, g.shape[2])


class _Overlap:
    def __init__(self, bufs, chip_idx, core_idx):
        self.bufs, self.chip_idx, self.core_idx = bufs, chip_idx, core_idx
        self.parts, self.sums, self.others = {}, {}, {}

    def rides(self, stage):
        rides = []
        if stage in GATHER_ON:
            rides.append(_gather_ride([self.bufs[k] for k in GATHER_ON[stage]]))
        if stage in EXCHANGE_ON:
            rides.append(_exchange_ride([self.sums[k] for k in EXCHANGE_ON[stage]]))
        if stage in SWAP_ON:
            rides.append(_swap_ride([self.parts[k] for k in SWAP_ON[stage]]))
        if stage == SMALL_ON:
            rides.append(_allgather_ride(self.packs))
        return rides

    def landed(self, stage, brought, wg):
        brought = list(brought)
        if stage in GATHER_ON:
            for k, g in zip(GATHER_ON[stage], brought.pop(0)):
                wg[k] = _as_weight(k, g)
        if stage in EXCHANGE_ON:
            self.others.update(zip(EXCHANGE_ON[stage], brought.pop(0)))
        if stage in SWAP_ON:
            self._pair(SWAP_ON[stage], brought.pop(0))
        if stage == SMALL_ON:
            (self.packs,) = brought.pop(0)

    def small(self, loss, grads):
        x, y, c = lax.axis_index("x"), lax.axis_index("y"), lax.axis_index("c")
        pack = _pack_small(loss, grads)
        self.packs = lax.dynamic_update_slice(jnp.zeros((N_DEV,) + pack.shape, F32), pack[None], (4 * x + 2 * y + c, 0, 0))

    def grad(self, name, g):
        if g.ndim == 2:
            g = g.reshape(N_CHIPS, g.shape[0] // N_CHIPS, g.shape[1])
        self.parts[name] = g.reshape(N_CHIPS, 2, g.shape[1] // 2, g.shape[2])

    def _pair(self, names, from_sibling):
        came = dict(zip(names, from_sibling))
        for group in _by_shape(names, self.parts):
            sums = _pair_sum([self.parts[k] for k in group], [came[k] for k in group], self.core_idx, ROWS_PER_STEP)
            self.sums.update(zip(group, sums))

    def finish(self):
        (from_sibling,) = _run_rides("swap_last", [_swap_ride([self.parts["w_in"]])])
        self._pair(("w_in",), from_sibling)
        ((self.others["w_in"],),) = _run_rides("exchange_last", [_exchange_ride([self.sums["w_in"]])])
        halves = {}
        for group in _by_shape(BIG, self.sums):
            res = _chip_sum([self.sums[k] for k in group], [self.others[k] for k in group], self.chip_idx, ROWS_PER_STEP)
            halves.update(zip(group, res))
        mine = [halves[k] for k in BIG]
        (theirs,) = _run_rides("sibling_share", [_share_ride(mine)])
        return mine, theirs, _sum_slabs(self.packs)


def _pack_small(loss, grads):
    def row(a):
        a = a.reshape(1, -1)
        return jnp.pad(a, ((0, 0), (0, 1024 - a.shape[1])))

    rows = [row(grads[k]) for k in ("g_mix", "b_out", "g_mem_q", "g_mem_kv", "g_ffn", "g_final")]
    rows += [grads["b_in"][:, 0:1024], row(grads["b_in"][:, 1024:1792])]
    rows += [jnp.concatenate([grads["b_dw"], grads["g_conv_ln"]], axis=1), row(grads["b_conv_ln"]), row(grads["attn_sink"]),
             row(loss)]
    dw = jnp.pad(grads["w_dw"], ((0, 1), (0, 0))).reshape(16, 1024)
    pack = jnp.concatenate(rows + [dw], axis=0)
    return jnp.pad(pack, ((0, PACK_ROWS - pack.shape[0]), (0, 0)))


def _unpack_small(pack):
    out = {k: pack[i:i + 1] for i, k in enumerate(("g_mix", "b_out", "g_mem_q", "g_mem_kv", "g_ffn", "g_final"))}
    out["b_in"] = jnp.concatenate([pack[6:7], pack[7:8, 0:768]], axis=1)
    out["b_dw"], out["g_conv_ln"] = pack[8:9, 0:512], pack[8:9, 512:1024]
    out["b_conv_ln"] = pack[9:10, 0:512]
    out["attn_sink"] = pack[10:11, 0:8]
    loss = pack[11, 0]
    dw = pack[12:28].reshape(32, 512)[0:CONV_W]
    return loss, out, dw


def kernel(x, mem, g_mix, w_in, b_in, w_dw, b_dw, g_conv_ln, b_conv_ln, attn_sink, w_out, b_out, g_mem_q, g_mem_kv, w_mem_q, w_mem_kv, w_mem_o, g_ffn, w_gate, w_up, w_down, g_final, loss_target, m_g_mix, m_w_in, m_b_in, m_w_dw, m_b_dw, m_g_conv_ln, m_b_conv_ln, m_attn_sink, m_w_out, m_b_out, m_g_mem_q, m_g_mem_kv, m_w_mem_q, m_w_mem_kv, m_w_mem_o, m_g_ffn, m_w_gate, m_w_up, m_w_down, m_g_final, v_g_mix, v_w_in, v_b_in, v_w_dw, v_b_dw, v_g_conv_ln, v_b_conv_ln, v_attn_sink, v_w_out, v_b_out, v_g_mem_q, v_g_mem_kv, v_w_mem_q, v_w_mem_kv, v_w_mem_o, v_g_ffn, v_w_gate, v_w_up, v_w_down, v_g_final):
    args = dict(locals())
    weight_names = ["g_mix", "w_in", "b_in", "w_dw", "b_dw", "g_conv_ln", "b_conv_ln", "attn_sink", "w_out", "b_out", "g_mem_q",
                    "g_mem_kv", "w_mem_q", "w_mem_kv", "w_mem_o", "g_ffn", "w_gate", "w_up", "w_down", "g_final"]
    chip = 2 * lax.axis_index("x") + lax.axis_index("y")
    core = lax.axis_index("c")

    chip_idx = chip.astype(jnp.int32).reshape(1)
    core_idx = core.astype(jnp.int32).reshape(1)

    def block(name):
        a = args[name][0]
        weight = name[2:] if name[:2] in ("m_", "v_") else name
        return a.T if weight in TRANSPOSED else a

    blocks = {k: block(k) for k in BIG}
    bufs = {}
    for group in _by_shape(BIG, blocks):
        bufs.update(zip(group, _cast_place([blocks[k] for k in group], chip_idx, ROWS_PER_STEP)))
    comm = _Overlap(bufs, chip_idx, core_idx)
    dw_buf = lax.dynamic_update_slice(jnp.zeros((N_CHIPS, CONV_W, 128), F32), w_dw, (chip, 0, 0))
    (first,), (dw_all,) = _run_rides("gather_first", [_gather_ride([comm.bufs["w_in"]]), _spread_ride(dw_buf)])
    wg = {"w_in": _as_weight("w_in", first), "w_dw": jnp.transpose(dw_all, (1, 0, 2)).reshape(CONV_W, CONV_CH)}
    small = {k: args[k].reshape(1, -1) for k in SMALL}

    loss, grad_x, grads = _local_step(x[0], mem[0], loss_target[0], small, wg, comm)

    halves, other_halves, pack_sum = comm.finish()

    loss_sum, small_grads, dw_full = _unpack_small(pack_sum)
    dw_cols = jnp.transpose(dw_full.reshape(CONV_W, N_CHIPS, 128), (1, 0, 2))
    small_grads["w_dw"] = lax.dynamic_index_in_dim(dw_cols, chip, axis=0, keepdims=False)

    out_g, out_d, out_m, out_v = {}, {}, {}, {}
    for k, g_mine, g_other in zip(BIG, halves, other_halves):
        res = _adamw(block(k), g_mine, g_other, block("m_" + k), block("v_" + k), core_idx, ROWS_PER_STEP)
        out_g[k], out_d[k], out_m[k], out_v[k] = [(r.T if k in TRANSPOSED else r)[None] for r in res]
    names = SMALL + ["w_dw"]

    def flat(a):
        return a[0] if a.ndim == 3 else a.reshape(1, -1)

    def pad_lanes(a):
        return jnp.pad(a, ((0, 0), (0, 128 - a.shape[1]))) if a.shape[1] < 128 else a

    ws = [flat(args[k]) for k in names]
    gs = [small_grads[k] for k in names]
    ms = [flat(args["m_" + k]) for k in names]
    vs = [flat(args["v_" + k]) for k in names]
    ds, mns, vns = _adamw_small([pad_lanes(a) for a in ws], [pad_lanes(a) for a in gs], [pad_lanes(a) for a in ms],
                                [pad_lanes(a) for a in vs])
    for i, k in enumerate(names):
        n_lanes = ws[i].shape[1]
        for out, val in ((out_g, gs[i]), (out_d, ds[i]), (out_m, mns[i]), (out_v, vns[i])):
            out[k] = val[:, 0:n_lanes].reshape(args[k].shape)

    return (loss_sum, grad_x[None], *[out_g[k] for k in weight_names], *[out_d[k] for k in weight_names],
            *[out_m[k] for k in weight_names], *[out_v[k] for k in weight_names])
```
